```python
import math
import jax, jax.numpy as jnp
from jax import lax
import numpy as np

D_MODEL = 1024
BATCH = 8
SEQ = 2048
DEPTH = 1
DEC_BATCH = 128
DEC_SEQ = 1
PAST_LEN = 16384
PAGE_SIZE = 128

MIX_DIM = D_MODEL
CONV_DIM = MIX_DIM // 2
CONV_W = 3
MLSTM_DIM = MIX_DIM - CONV_DIM
MLSTM_HEADS = 4
MLSTM_DQK = MLSTM_DIM // MLSTM_HEADS
MLSTM_DV = MLSTM_DIM // MLSTM_HEADS
MLSTM_CHUNK = 64
N_MEM = 256
X_HEADS = 4
X_HEAD_DIM = D_MODEL // X_HEADS
D_FF = -(-8 * D_MODEL // (3 * 256)) * 256
IN_DIM = 3 * CONV_DIM + 4 * MLSTM_DIM + 2 * MLSTM_HEADS
EPS = 1e-6

kernel_name = 'hymba_conv_mlstm_memxattn_step'


def rmsnorm(x, g):
    x32 = x.astype(jnp.float32)
    y = x32 * lax.rsqrt(jnp.mean(x32 * x32, axis=-1, keepdims=True) + EPS) * g.astype(jnp.float32)
    return y.astype(x.dtype)


def short_conv(u, conv_state, conv_w):
    T = u.shape[1]
    up = jnp.concatenate([conv_state.astype(u.dtype), u], axis=1)
    y = conv_w[0] * up[:, 0:T]
    for j in range(1, CONV_W):
        y = y + conv_w[j] * up[:, j:j + T]
    return y, up[:, T:]


def mlstm_chunk(carry, inp):
    C, n, m = carry
    q, k, v, ig, lf = inp
    L = q.shape[2]
    b = jnp.cumsum(lf, axis=-1)
    causal = jnp.tril(jnp.ones((L, L), dtype=bool))
    dmat = jnp.where(causal, b[..., :, None] - b[..., None, :] + ig[..., None, :], -jnp.inf)
    inter = b + m[..., None]
    m_row = jnp.maximum(inter, jnp.max(dmat, axis=-1))
    w = jnp.einsum('bhld,bhsd->bhls', q, k) * jnp.exp(dmat - m_row[..., None])
    g = jnp.exp(inter - m_row)
    num = g[..., None] * jnp.einsum('bhvd,bhld->bhlv', C, q) + jnp.einsum('bhls,bhsv->bhlv', w, v)
    den = g * jnp.einsum('bhd,bhld->bhl', n, q) + jnp.sum(w, axis=-1)
    h = num / jnp.maximum(jnp.abs(den), jnp.exp(-m_row))[..., None]
    b_last = b[..., -1]
    a = b_last[..., None] - b + ig
    m_new = jnp.maximum(b_last + m, jnp.max(a, axis=-1))
    decay = jnp.exp(b_last + m - m_new)
    s = jnp.exp(a - m_new[..., None])
    C_new = decay[..., None, None] * C + jnp.einsum('bhs,bhsv,bhsd->bhvd', s, v, k)
    n_new = decay[..., None] * n + jnp.einsum('bhs,bhsd->bhd', s, k)
    return (C_new, n_new, m_new), h


def mlstm(q, k, v, ig, lf, C0, n0, m0, chunk):
    B, T = q.shape[0], q.shape[1]
    nc = T // chunk

    def to_chunks(a):
        a = a.reshape((B, nc, chunk) + a.shape[2:])
        return jnp.moveaxis(jnp.moveaxis(a, 1, 0), 2, 3)

    carry, h = lax.scan(mlstm_chunk, (C0, n0, m0),
                        (to_chunks(q), to_chunks(k), to_chunks(v), to_chunks(ig), to_chunks(lf)))
    h = jnp.moveaxis(jnp.moveaxis(h, 3, 2), 0, 1).reshape(B, T, MLSTM_HEADS, MLSTM_DV)
    return h, carry


def hybrid_mixer(xn, conv_state, C0, n0, m0, w_in, b_in, conv_w, g_mh, w_out, chunk):
    B, T, _ = xn.shape
    dt = xn.dtype
    proj = xn @ w_in + b_in
    splits = np.cumsum([CONV_DIM] * 3 + [MLSTM_DIM] * 4).tolist()
    gb, gc, hc, q, k, v, o, gates = jnp.split(proj, splits, axis=-1)
    y_conv, new_conv = short_conv(gc * hc, conv_state, conv_w)
    y_conv = gb * y_conv
    f32 = jnp.float32
    qh = q.reshape(B, T, MLSTM_HEADS, MLSTM_DQK).astype(f32)
    kh = k.reshape(B, T, MLSTM_HEADS, MLSTM_DQK).astype(f32) * (MLSTM_DQK ** -0.5)
    vh = v.reshape(B, T, MLSTM_HEADS, MLSTM_DV).astype(f32)
    ig = gates[..., :MLSTM_HEADS].astype(f32)
    lf = jax.nn.log_sigmoid(gates[..., MLSTM_HEADS:].astype(f32))
    h, (C, n, m) = mlstm(qh, kh, vh, ig, lf, C0.astype(f32), n0.astype(f32), m0.astype(f32), chunk)
    h = h * lax.rsqrt(jnp.mean(h * h, axis=-1, keepdims=True) + EPS)
    h = (h.reshape(B, T, MLSTM_DIM) * g_mh.astype(f32)).astype(dt)
    y_ml = jax.nn.sigmoid(o) * h
    out = jnp.concatenate([y_conv, y_ml], axis=-1) @ w_out
    return out, new_conv, C.astype(C0.dtype), n.astype(n0.dtype), m.astype(m0.dtype)


def mem_kv(mem, g_mem, w_xkv):
    B, M, _ = mem.shape
    kk, vv = jnp.split(rmsnorm(mem, g_mem) @ w_xkv, 2, axis=-1)
    return kk.reshape(B, M, X_HEADS, X_HEAD_DIM), vv.reshape(B, M, X_HEADS, X_HEAD_DIM)


def cross_attn(xn, mk, mv, w_xq, w_xo):
    B, T, _ = xn.shape
    q = (xn @ w_xq).reshape(B, T, X_HEADS, X_HEAD_DIM)
    s = jnp.einsum('bthd,bmhd->bhtm', q, mk.astype(q.dtype)).astype(jnp.float32) * (X_HEAD_DIM ** -0.5)
    p = jax.nn.softmax(s, axis=-1).astype(xn.dtype)
    o = jnp.einsum('bhtm,bmhd->bthd', p, mv.astype(xn.dtype)).reshape(B, T, X_HEADS * X_HEAD_DIM)
    return o @ w_xo


def swiglu(xn, w_gu, w_down):
    g, u = jnp.split(xn @ w_gu, 2, axis=-1)
    return (jax.nn.silu(g) * u) @ w_down


def setup_inputs(seed: int = 0) -> dict:
    key = jax.random.key(seed)
    ks = jax.random.split(key, 32)
    nrm = jax.random.normal
    f = jnp.float32
    b_in = 0.02 * nrm(ks[10], (DEPTH, IN_DIM), f)
    b_in = b_in.at[:, -MLSTM_HEADS:].add(jnp.linspace(3.0, 6.0, MLSTM_HEADS, dtype=f))
    return {
        'x_prompt': nrm(ks[0], (BATCH, SEQ, D_MODEL), f),
        'x_sample': nrm(ks[1], (DEC_BATCH, DEC_SEQ, D_MODEL), f),
        'mem_prompt': nrm(ks[2], (BATCH, N_MEM, D_MODEL), f),
        'state_conv': nrm(ks[3], (DEPTH, DEC_BATCH, CONV_W - 1, CONV_DIM), f),
        'state_mlstm_C': 0.3 * nrm(ks[4], (DEPTH, DEC_BATCH, MLSTM_HEADS, MLSTM_DV, MLSTM_DQK), f),
        'state_mlstm_n': 0.3 * nrm(ks[5], (DEPTH, DEC_BATCH, MLSTM_HEADS, MLSTM_DQK), f),
        'state_mlstm_m': nrm(ks[6], (DEPTH, DEC_BATCH, MLSTM_HEADS), f),
        'cache_mem_k': nrm(ks[7], (DEPTH, DEC_BATCH, N_MEM, X_HEADS, X_HEAD_DIM), f),
        'cache_mem_v': nrm(ks[8], (DEPTH, DEC_BATCH, N_MEM, X_HEADS, X_HEAD_DIM), f),
        'g_mix': 1.0 + 0.02 * nrm(ks[9], (DEPTH, D_MODEL), f),
        'w_in': nrm(ks[11], (DEPTH, D_MODEL, IN_DIM), f) * D_MODEL ** -0.5,
        'b_in': b_in,
        'conv_w': nrm(ks[12], (DEPTH, CONV_W, CONV_DIM), f) * CONV_W ** -0.5,
        'g_mh': 1.0 + 0.02 * nrm(ks[13], (DEPTH, MLSTM_DIM), f),
        'w_out': nrm(ks[14], (DEPTH, MIX_DIM, D_MODEL), f) * MIX_DIM ** -0.5,
        'g_cross': 1.0 + 0.02 * nrm(ks[15], (DEPTH, D_MODEL), f),
        'g_mem': 1.0 + 0.02 * nrm(ks[16], (DEPTH, D_MODEL), f),
        'w_xq': nrm(ks[17], (DEPTH, D_MODEL, X_HEADS * X_HEAD_DIM), f) * D_MODEL ** -0.5,
        'w_xkv': nrm(ks[18], (DEPTH, D_MODEL, 2 * X_HEADS * X_HEAD_DIM), f) * D_MODEL ** -0.5,
        'w_xo': nrm(ks[19], (DEPTH, X_HEADS * X_HEAD_DIM, D_MODEL), f) * (X_HEADS * X_HEAD_DIM) ** -0.5,
        'g_ffn': 1.0 + 0.02 * nrm(ks[20], (DEPTH, D_MODEL), f),
        'w_gu': nrm(ks[21], (DEPTH, D_MODEL, 2 * D_FF), f) * D_MODEL ** -0.5,
        'w_down': nrm(ks[22], (DEPTH, D_FF, D_MODEL), f) * D_FF ** -0.5,
        'g_final': 1.0 + 0.02 * nrm(ks[23], (D_MODEL,), f),
    }


def reference(x_prompt, x_sample, mem_prompt, state_conv, state_mlstm_C, state_mlstm_n, state_mlstm_m,
              cache_mem_k, cache_mem_v, g_mix, w_in, b_in, conv_w, g_mh, w_out, g_cross, g_mem,
              w_xq, w_xkv, w_xo, g_ffn, w_gu, w_down, g_final):
    Bp, Tp = x_prompt.shape[0], x_prompt.shape[1]
    dt = x_prompt.dtype
    chunk_p = math.gcd(Tp, MLSTM_CHUNK)
    chunk_s = x_sample.shape[1]
    hp, hs = x_prompt, x_sample
    pcv, pCs, pns, pms, pks, pvs = [], [], [], [], [], []
    scv, sCs, sns, sms = [], [], [], []
    for l in range(DEPTH):
        zc = jnp.zeros((Bp, CONV_W - 1, CONV_DIM), dt)
        zC = jnp.zeros((Bp, MLSTM_HEADS, MLSTM_DV, MLSTM_DQK), dt)
        zn = jnp.zeros((Bp, MLSTM_HEADS, MLSTM_DQK), dt)
        zm = jnp.zeros((Bp, MLSTM_HEADS), dt)
        mp, pc, pC, pn, pm = hybrid_mixer(rmsnorm(hp, g_mix[l]), zc, zC, zn, zm, w_in[l], b_in[l],
                                          conv_w[l], g_mh[l], w_out[l], chunk_p)
        ms, sc, sC, sn, sm = hybrid_mixer(rmsnorm(hs, g_mix[l]), state_conv[l], state_mlstm_C[l],
                                          state_mlstm_n[l], state_mlstm_m[l], w_in[l], b_in[l],
                                          conv_w[l], g_mh[l], w_out[l], chunk_s)
        hp = hp + mp
        hs = hs + ms
        pk, pv = mem_kv(mem_prompt, g_mem[l], w_xkv[l])
        hp = hp + cross_attn(rmsnorm(hp, g_cross[l]), pk, pv, w_xq[l], w_xo[l])
        hs = hs + cross_attn(rmsnorm(hs, g_cross[l]), cache_mem_k[l], cache_mem_v[l], w_xq[l], w_xo[l])
        hp = hp + swiglu(rmsnorm(hp, g_ffn[l]), w_gu[l], w_down[l])
        hs = hs + swiglu(rmsnorm(hs, g_ffn[l]), w_gu[l], w_down[l])
        pcv.append(pc); pCs.append(pC); pns.append(pn); pms.append(pm); pks.append(pk); pvs.append(pv)
        scv.append(sc); sCs.append(sC); sns.append(sn); sms.append(sm)
    y_prompt = rmsnorm(hp, g_final)
    y_sample = rmsnorm(hs, g_final)
    p_conv, p_C, p_n, p_m = jnp.stack(pcv), jnp.stack(pCs), jnp.stack(pns), jnp.stack(pms)
    p_mem_k, p_mem_v = jnp.stack(pks), jnp.stack(pvs)
    s_conv, s_C, s_n, s_m = jnp.stack(scv), jnp.stack(sCs), jnp.stack(sns), jnp.stack(sms)
    return (y_prompt, y_sample, p_conv, p_C, p_n, p_m, p_mem_k, p_mem_v, s_conv, s_C, s_n, s_m)
```

```python
import functools

import jax
import jax.numpy as jnp
from jax import lax
from jax.experimental import pallas as pl
from jax.experimental.pallas import tpu as pltpu

F32 = jnp.float32
BF16 = jnp.bfloat16

D_MODEL = 1024
CONV_DIM = 512
CONV_W = 3
MLSTM_DIM = 512
HEADS = 4
DQK = 128
DV = 128
N_MEM = 256
X_HEADS = 4
X_HEAD_DIM = 256
D_FF = 2816
MAIN_DIM = 3 * CONV_DIM + 4 * MLSTM_DIM
GATE_PAD = 128
EPS = 1e-6

MLSTM_CHUNK = 256
ROW_TILE = 512
FF_CHUNK = 256
S2_ROWS = 16
S4_ROWS = 8
VMEM_LIMIT = 56 * 1024 * 1024


def _dot(a, b):
    return jnp.dot(a, b, preferred_element_type=F32)


def _dot_nt(a, b):
    return lax.dot_general(a, b, (((1,), (1,)), ((), ())), preferred_element_type=F32)


def _dot_tn(a, b):
    return lax.dot_general(a, b, (((0,), (0,)), ((), ())), preferred_element_type=F32)


def _rmsnorm(x, g):
    return x * lax.rsqrt(jnp.mean(x * x, axis=-1, keepdims=True) + EPS) * g


def _const_spec(shape):
    zeros = (0,) * len(shape)
    return pl.BlockSpec(shape, lambda *_: zeros, pipeline_mode=pl.Buffered(1))


def _params(sem):
    return pltpu.CompilerParams(dimension_semantics=sem, vmem_limit_bytes=VMEM_LIMIT)


def _gate_transform(gt):
    lane = lax.broadcasted_iota(jnp.int32, gt.shape, 1)
    return jnp.where(lane < HEADS, gt, jax.nn.log_sigmoid(gt))


def _p1_kernel(tiles_per_batch, x_ref, g_ref, w_ref, b_ref, wg_ref, bg_ref, cw_ref,
               yconv_ref, q_ref, k_ref, v_ref, so_ref, gate_ref, pconv_ref, ubuf):
    tm = x_ref.shape[0]
    i = pl.program_id(0)
    xn = _rmsnorm(x_ref[...], g_ref[...]).astype(BF16)

    def seg(j):
        sl = slice(j * CONV_DIM, (j + 1) * CONV_DIM)
        return _dot(xn, w_ref[:, sl]) + b_ref[:, sl]

    prev = ubuf[tm:tm + 8, :]
    ubuf[0:8, :] = jnp.where(i % tiles_per_batch == 0, jnp.zeros_like(prev), prev)
    ubuf[8:8 + tm, :] = seg(1) * seg(2)
    conv = (cw_ref[0:1, :] * ubuf[6:6 + tm, :] + cw_ref[1:2, :] * ubuf[7:7 + tm, :]
            + cw_ref[2:3, :] * ubuf[8:8 + tm, :])
    yconv_ref[...] = (seg(0) * conv).astype(BF16)
    pconv_ref[0] = ubuf[tm + 6:tm + 8, :]

    q_ref[...] = seg(3).astype(BF16)
    k_ref[...] = (seg(4) * (DQK ** -0.5)).astype(BF16)
    v_ref[...] = seg(5).astype(BF16)
    so_ref[...] = jax.nn.sigmoid(seg(6)).astype(BF16)
    gate_ref[...] = _gate_transform(_dot(xn, wg_ref[...]) + bg_ref[...])


def _p1_call(x, g, w, b, wg, bg, cw, seq_len):
    rows = x.shape[0]
    tm = ROW_TILE
    tiles_per_batch = seq_len // tm
    n_batch = rows // seq_len
    row = lambda width: pl.BlockSpec((tm, width), lambda i: (i, 0))
    return pl.pallas_call(
        functools.partial(_p1_kernel, tiles_per_batch),
        grid=(rows // tm,),
        in_specs=[row(D_MODEL), _const_spec((1, D_MODEL)), _const_spec((D_MODEL, MAIN_DIM)),
                  _const_spec((1, MAIN_DIM)), _const_spec((D_MODEL, GATE_PAD)),
                  _const_spec((1, GATE_PAD)), _const_spec((CONV_W, CONV_DIM))],
        out_specs=[row(CONV_DIM), row(MLSTM_DIM), row(MLSTM_DIM), row(MLSTM_DIM), row(MLSTM_DIM),
                   row(GATE_PAD),
                   pl.BlockSpec((1, CONV_W - 1, CONV_DIM), lambda i: (i // tiles_per_batch, 0, 0))],
        out_shape=[jax.ShapeDtypeStruct((rows, CONV_DIM), BF16)]
        + [jax.ShapeDtypeStruct((rows, MLSTM_DIM), BF16)] * 4
        + [jax.ShapeDtypeStruct((rows, GATE_PAD), F32),
           jax.ShapeDtypeStruct((n_batch, CONV_W - 1, CONV_DIM), F32)],
        scratch_shapes=[pltpu.VMEM((tm + 8, CONV_DIM), F32)],
        compiler_params=_params(("arbitrary",)),
        name="p1_inproj_conv",
    )(x, g, w, b, wg, bg, cw)


def _p2_kernel(q_ref, k_ref, v_ref, so_ref, yconv_ref, gate_ref, x_ref, wout_ref, gmh_ref,
               hp_ref, pc_ref, pn_ref, pm_ref, c_s, n_s, m_s, y_s):
    L = q_ref.shape[0]
    c = pl.program_id(1)

    @pl.when(c == 0)
    def _():
        c_s[...] = jnp.zeros_like(c_s)
        n_s[...] = jnp.zeros_like(n_s)
        m_s[...] = jnp.zeros_like(m_s)

    gt = gate_ref[...]
    row = lax.broadcasted_iota(jnp.int32, (L, L), 0)
    col = lax.broadcasted_iota(jnp.int32, (L, L), 1)
    causal = row >= col
    bcum = jnp.dot(causal.astype(F32), gt, preferred_element_type=F32,
                   precision=lax.Precision.HIGHEST)
    gt_t = gt.T
    bcum_t = bcum.T

    y_s[:, 0:CONV_DIM] = yconv_ref[...]
    for h in range(HEADS):
        sl = slice(h * DQK, (h + 1) * DQK)
        q = q_ref[:, sl]
        k = k_ref[:, sl]
        v = v_ref[:, sl]
        b_c = bcum[:, HEADS + h:HEADS + h + 1]
        b_r = bcum_t[HEADS + h:HEADS + h + 1, :]
        ig_c = gt[:, h:h + 1]
        ig_r = gt_t[h:h + 1, :]
        m_prev = m_s[h:h + 1, 0:1]
        c_prev = c_s[h]
        n_prev = n_s[h:h + 1, :]

        dmat = jnp.where(causal, b_c - b_r + ig_r, -jnp.inf)
        inter = b_c + m_prev
        m_row = jnp.maximum(inter, jnp.max(dmat, axis=1, keepdims=True))
        w = _dot_nt(q, k) * jnp.exp(dmat - m_row)
        g = jnp.exp(inter - m_row)
        num = g * _dot_nt(q, c_prev.astype(BF16)) + _dot(w.astype(BF16), v)
        qn = jnp.sum(q.astype(F32) * n_prev, axis=1, keepdims=True)
        den = g * qn + jnp.sum(w, axis=1, keepdims=True)
        hh = num / jnp.maximum(jnp.abs(den), jnp.exp(-m_row))
        hh = hh * lax.rsqrt(jnp.mean(hh * hh, axis=1, keepdims=True) + EPS) * gmh_ref[:, sl]
        y_s[:, CONV_DIM + h * DV:CONV_DIM + (h + 1) * DV] = (so_ref[:, sl].astype(F32) * hh).astype(BF16)

        b_last = b_c[L - 1:L, :]
        a_r = b_last - b_r + ig_r
        a_c = b_last - b_c + ig_c
        m_new = jnp.maximum(b_last + m_prev, jnp.max(a_r, axis=1, keepdims=True))
        decay = jnp.exp(b_last + m_prev - m_new)
        s_c = jnp.exp(a_c - m_new)
        sv = (v.astype(F32) * s_c).astype(BF16)
        c_s[h] = decay * c_prev + _dot_tn(sv, k)
        n_s[h:h + 1, :] = decay * n_prev + jnp.sum(k.astype(F32) * s_c, axis=0, keepdims=True)
        m_s[h:h + 1, :] = jnp.broadcast_to(m_new, (1, m_s.shape[1]))

    hp_ref[...] = x_ref[...] + _dot(y_s[...], wout_ref[...])

    @pl.when(c == pl.num_programs(1) - 1)
    def _():
        pc_ref[0] = c_s[...]
        pn_ref[0] = n_s[0:HEADS, :]
        lane = lax.broadcasted_iota(jnp.int32, (1, m_s.shape[1]), 1)
        acc = jnp.zeros((1, m_s.shape[1]), F32)
        for h in range(HEADS):
            acc = jnp.where(lane == h, m_s[h:h + 1, :], acc)
        pm_ref[0] = acc


def _p2_call(q, k, v, so, yconv, gates, x, wout, gmh, n_batch, seq_len):
    L = MLSTM_CHUNK
    nc = seq_len // L
    row = lambda width: pl.BlockSpec((L, width), lambda b, c: (b * nc + c, 0))
    rows = n_batch * seq_len
    return pl.pallas_call(
        _p2_kernel,
        grid=(n_batch, nc),
        in_specs=[row(MLSTM_DIM), row(MLSTM_DIM), row(MLSTM_DIM), row(MLSTM_DIM), row(CONV_DIM),
                  row(GATE_PAD), row(D_MODEL), _const_spec((D_MODEL, D_MODEL)),
                  _const_spec((1, MLSTM_DIM))],
        out_specs=[row(D_MODEL),
                   pl.BlockSpec((1, HEADS, DV, DQK), lambda b, c: (b, 0, 0, 0)),
                   pl.BlockSpec((1, HEADS, DQK), lambda b, c: (b, 0, 0)),
                   pl.BlockSpec((1, 1, 128), lambda b, c: (b, 0, 0))],
        out_shape=[jax.ShapeDtypeStruct((rows, D_MODEL), F32),
                   jax.ShapeDtypeStruct((n_batch, HEADS, DV, DQK), F32),
                   jax.ShapeDtypeStruct((n_batch, HEADS, DQK), F32),
                   jax.ShapeDtypeStruct((n_batch, 1, 128), F32)],
        scratch_shapes=[pltpu.VMEM((HEADS, DV, DQK), F32), pltpu.VMEM((8, DQK), F32),
                        pltpu.VMEM((8, 128), F32), pltpu.VMEM((L, D_MODEL), BF16)],
        compiler_params=_params(("arbitrary", "arbitrary")),
        name="p2_mlstm_outproj",
    )(q, k, v, so, yconv, gates, x, wout, gmh)


def _pm_kernel(mem_ref, g_ref, w_ref, k_ref, v_ref, kb_ref, vb_ref):
    xn = _rmsnorm(mem_ref[...], g_ref[...]).astype(BF16)
    kk = _dot(xn, w_ref[:, 0:D_MODEL])
    vv = _dot(xn, w_ref[:, D_MODEL:2 * D_MODEL])
    k_ref[...] = kk
    v_ref[...] = vv
    kb_ref[...] = kk.astype(BF16)
    vb_ref[...] = vv.astype(BF16)


def _pm_call(mem, g, w):
    rows = mem.shape[0]
    tm = ROW_TILE
    row = pl.BlockSpec((tm, D_MODEL), lambda i: (i, 0))
    return pl.pallas_call(
        _pm_kernel,
        grid=(rows // tm,),
        in_specs=[row, _const_spec((1, D_MODEL)), _const_spec((D_MODEL, 2 * D_MODEL))],
        out_specs=[row, row, row, row],
        out_shape=[jax.ShapeDtypeStruct((rows, D_MODEL), F32)] * 2
        + [jax.ShapeDtypeStruct((rows, D_MODEL), BF16)] * 2,
        compiler_params=_params(("arbitrary",)),
        name="pm_mem_kv",
    )(mem, g, w)


def _p3_kernel(hp_ref, g_ref, wq_ref, k_ref, v_ref, o_ref):
    xn = _rmsnorm(hp_ref[...], g_ref[...]).astype(BF16)
    q = _dot(xn, wq_ref[...])
    for h in range(X_HEADS):
        sl = slice(h * X_HEAD_DIM, (h + 1) * X_HEAD_DIM)
        s = _dot_nt(q[:, sl].astype(BF16), k_ref[0, :, sl]) * (X_HEAD_DIM ** -0.5)
        e = jnp.exp(s - jnp.max(s, axis=1, keepdims=True))
        p = e * (1.0 / jnp.sum(e, axis=1, keepdims=True))
        o_ref[:, sl] = _dot(p.astype(BF16), v_ref[0, :, sl]).astype(BF16)


def _p3_call(hp, g, wq, kb, vb, seq_len):
    rows = hp.shape[0]
    tm = ROW_TILE
    tiles_per_batch = seq_len // tm
    row = pl.BlockSpec((tm, D_MODEL), lambda i: (i, 0))
    mem = pl.BlockSpec((1, N_MEM, D_MODEL), lambda i: (i // tiles_per_batch, 0, 0))
    return pl.pallas_call(
        _p3_kernel,
        grid=(rows // tm,),
        in_specs=[row, _const_spec((1, D_MODEL)), _const_spec((D_MODEL, D_MODEL)), mem, mem],
        out_specs=row,
        out_shape=jax.ShapeDtypeStruct((rows, D_MODEL), BF16),
        compiler_params=_params(("arbitrary",)),
        name="p3_cross_attn",
    )(hp, g, wq, kb, vb)


def _p4_kernel(hp_ref, o_ref, wxo_ref, gf_ref, wgu_ref, wd_ref, gfin_ref, y_ref, act_s):
    hp = hp_ref[...] + _dot(o_ref[...].astype(BF16), wxo_ref[...])
    xn = _rmsnorm(hp, gf_ref[...]).astype(BF16)
    for j in range(D_FF // FF_CHUNK):
        gu = _dot(xn, wgu_ref[:, 2 * FF_CHUNK * j:2 * FF_CHUNK * (j + 1)])
        g = gu[:, 0:FF_CHUNK]
        u = gu[:, FF_CHUNK:2 * FF_CHUNK]
        act_s[:, FF_CHUNK * j:FF_CHUNK * (j + 1)] = (g * jax.nn.sigmoid(g) * u).astype(BF16)
    hp = hp + _dot(act_s[...], wd_ref[...])
    y_ref[...] = _rmsnorm(hp, gfin_ref[...])


def _p4_call(hp, o, wxo, gf, wgu, wd, gfin, tm):
    rows = hp.shape[0]
    row = pl.BlockSpec((tm, D_MODEL), lambda i: (i, 0))
    return pl.pallas_call(
        _p4_kernel,
        grid=(rows // tm,),
        in_specs=[row, row, _const_spec((D_MODEL, D_MODEL)), _const_spec((1, D_MODEL)),
                  _const_spec((D_MODEL, 2 * D_FF)), _const_spec((D_FF, D_MODEL)),
                  _const_spec((1, D_MODEL))],
        out_specs=row,
        out_shape=jax.ShapeDtypeStruct((rows, D_MODEL), F32),
        scratch_shapes=[pltpu.VMEM((tm, D_FF), BF16)],
        compiler_params=_params(("arbitrary",)),
        name="p4_ffn_final",
    )(hp, o, wxo, gf, wgu, wd, gfin)


def _s1_kernel(x_ref, g_ref, w_ref, b_ref, wg_ref, bg_ref, cw_ref, st_ref, wvt_ref, bvt_ref,
               wgt_ref, bgt_ref,
               yconv_ref, sconv_ref, q_ref, k_ref, v_ref, so_ref, gate_ref, vt_ref, gatet_ref):
    xn = _rmsnorm(x_ref[...], g_ref[...]).astype(BF16)

    def seg(j):
        sl = slice(j * CONV_DIM, (j + 1) * CONV_DIM)
        return _dot(xn, w_ref[:, sl]) + b_ref[:, sl]

    u = seg(1) * seg(2)
    st0 = st_ref[:, 0:CONV_DIM]
    st1 = st_ref[:, CONV_DIM:2 * CONV_DIM]
    conv = cw_ref[0:1, :] * st0 + cw_ref[1:2, :] * st1 + cw_ref[2:3, :] * u
    yconv_ref[...] = seg(0) * conv
    sconv_ref[:, 0:CONV_DIM] = st1
    sconv_ref[:, CONV_DIM:2 * CONV_DIM] = u
    q_ref[...] = seg(3)
    k_ref[...] = seg(4) * (DQK ** -0.5)
    v_ref[...] = seg(5)
    so_ref[...] = jax.nn.sigmoid(seg(6))
    gate_ref[...] = _gate_transform(_dot(xn, wg_ref[...]) + bg_ref[...])
    vt_ref[...] = _dot_nt(wvt_ref[...], xn) + bvt_ref[...]
    gt_t = _dot_nt(wgt_ref[...], xn) + bgt_ref[...]
    sub = lax.broadcasted_iota(jnp.int32, gt_t.shape, 0)
    gatet_ref[...] = jnp.where(sub < HEADS, gt_t, jax.nn.log_sigmoid(gt_t))


def _s1_call(x, g, w, b, wg, bg, cw, st, wvt, bvt, wgt, bgt):
    n = x.shape[0]
    full = lambda *shape: pl.BlockSpec(shape, lambda i: (0,) * len(shape))
    ins = [x, g, w, b, wg, bg, cw, st, wvt, bvt, wgt, bgt]
    return pl.pallas_call(
        _s1_kernel,
        grid=(1,),
        in_specs=[full(*a.shape) for a in ins],
        out_specs=[full(n, CONV_DIM), full(n, 2 * CONV_DIM), full(n, MLSTM_DIM), full(n, MLSTM_DIM),
                   full(n, MLSTM_DIM), full(n, MLSTM_DIM), full(n, GATE_PAD), full(MLSTM_DIM, n),
                   full(GATE_PAD, n)],
        out_shape=[jax.ShapeDtypeStruct((n, CONV_DIM), F32),
                   jax.ShapeDtypeStruct((n, 2 * CONV_DIM), F32)]
        + [jax.ShapeDtypeStruct((n, MLSTM_DIM), F32)] * 4
        + [jax.ShapeDtypeStruct((n, GATE_PAD), F32),
           jax.ShapeDtypeStruct((MLSTM_DIM, n), F32),
           jax.ShapeDtypeStruct((GATE_PAD, n), F32)],
        compiler_params=_params(("arbitrary",)),
        name="s1_inproj_conv",
    )(*ins)


def _pick_row(tile, r):
    sub = lax.broadcasted_iota(jnp.int32, tile.shape, 0)
    return jnp.sum(jnp.where(sub == r, tile, 0.0), axis=0, keepdims=True)


def _s2_kernel(c_ref, q_ref, k_ref, vt_ref, gate_ref, gatet_ref, m_ref, mt_ref,
               cn_ref, cqt_ref, dec_s, svt_s):
    n = vt_ref.shape[1]
    bb = c_ref.shape[0]
    i = pl.program_id(0)

    @pl.when(i == 0)
    def _():
        cqt_ref[...] = jnp.zeros_like(cqt_ref)

    lane = lax.broadcasted_iota(jnp.int32, (DV, n), 1)
    for h in range(HEADS):
        sl = slice(h * DQK, (h + 1) * DQK)
        ig_c = gate_ref[:, h:h + 1]
        lf_c = gate_ref[:, HEADS + h:HEADS + h + 1]
        m_c = m_ref[:, h:h + 1]
        m_new_c = jnp.maximum(lf_c + m_c, ig_c)
        dec_s[...] = jnp.broadcast_to(jnp.exp(lf_c + m_c - m_new_c), (bb, DQK))
        ig_r = gatet_ref[h:h + 1, :]
        lf_r = gatet_ref[HEADS + h:HEADS + h + 1, :]
        m_r = mt_ref[h:h + 1, :]
        s_r = jnp.exp(ig_r - jnp.maximum(lf_r + m_r, ig_r))
        svt_s[...] = vt_ref[sl, :] * s_r

        def body(bl, cqt):
            onehot = lane == i * bb + bl
            c = c_ref[bl, h]
            q_row = _pick_row(q_ref[:, sl], bl)
            k_row = _pick_row(k_ref[:, sl], bl)
            dec_row = _pick_row(dec_s[...], bl)
            cq_col = jnp.sum(c * q_row, axis=1, keepdims=True)
            sv_col = jnp.sum(jnp.where(onehot, svt_s[...], 0.0), axis=1, keepdims=True)
            cn_ref[bl, h] = dec_row * c + sv_col * k_row
            return jnp.where(onehot, cq_col, cqt)

        cqt_ref[sl, :] = lax.fori_loop(0, bb, body, cqt_ref[sl, :])


def _s2_call(c, q, k, vt, gates, gatest, m, mt):
    n = q.shape[0]
    bb = S2_ROWS
    full = lambda *shape: pl.BlockSpec(shape, lambda i: (0,) * len(shape))
    rows = lambda width: pl.BlockSpec((bb, width), lambda i: (i, 0))
    cblk = pl.BlockSpec((bb, HEADS, DV, DQK), lambda i: (i, 0, 0, 0))
    return pl.pallas_call(
        _s2_kernel,
        grid=(n // bb,),
        in_specs=[cblk, rows(MLSTM_DIM), rows(MLSTM_DIM), full(MLSTM_DIM, n),
                  rows(GATE_PAD), full(GATE_PAD, n), rows(HEADS), full(HEADS, n)],
        out_specs=[cblk, full(MLSTM_DIM, n)],
        out_shape=[jax.ShapeDtypeStruct(c.shape, F32), jax.ShapeDtypeStruct((MLSTM_DIM, n), F32)],
        scratch_shapes=[pltpu.VMEM((bb, DQK), F32), pltpu.VMEM((DV, n), F32)],
        compiler_params=_params(("arbitrary",)),
        name="s2_memory_update",
    )(c, q, k, vt, gates, gatest, m, mt)


def _s3_kernel(cqt_ref, q_ref, k_ref, v_ref, so_ref, gate_ref, n_ref, m_ref, yconv_ref, x_ref,
               wout_ref, gmh_ref, gx_ref, wq_ref,
               hs_ref, qx_ref, nn_ref, mn_ref, y_s):
    n_rows = q_ref.shape[0]
    y_s[:, 0:CONV_DIM] = yconv_ref[...].astype(BF16)
    lane = lax.broadcasted_iota(jnp.int32, (n_rows, mn_ref.shape[1]), 1)
    m_out = jnp.zeros((n_rows, mn_ref.shape[1]), F32)
    for h in range(HEADS):
        sl = slice(h * DQK, (h + 1) * DQK)
        q = q_ref[:, sl]
        k = k_ref[:, sl]
        v = v_ref[:, sl]
        n_prev = n_ref[:, sl]
        cq = cqt_ref[sl, :].T
        ig = gate_ref[:, h:h + 1]
        lf = gate_ref[:, HEADS + h:HEADS + h + 1]
        m_prev = m_ref[:, h:h + 1]
        inter = lf + m_prev
        m_row = jnp.maximum(inter, ig)
        wgt = jnp.sum(q * k, axis=1, keepdims=True) * jnp.exp(ig - m_row)
        g = jnp.exp(inter - m_row)
        num = g * cq + wgt * v
        den = g * jnp.sum(n_prev * q, axis=1, keepdims=True) + wgt
        hh = num / jnp.maximum(jnp.abs(den), jnp.exp(-m_row))
        hh = hh * lax.rsqrt(jnp.mean(hh * hh, axis=1, keepdims=True) + EPS) * gmh_ref[:, sl]
        y_s[:, CONV_DIM + h * DV:CONV_DIM + (h + 1) * DV] = (so_ref[:, sl] * hh).astype(BF16)
        nn_ref[:, sl] = g * n_prev + jnp.exp(ig - m_row) * k
        m_out = jnp.where(lane == h, m_row, m_out)
    mn_ref[...] = m_out
    hs = x_ref[...] + _dot(y_s[...], wout_ref[...])
    hs_ref[...] = hs
    qx_ref[...] = _dot(_rmsnorm(hs, gx_ref[...]).astype(BF16), wq_ref[...])


def _s3_call(cqt, q, k, v, so, gates, nst, m, yconv, x, wout, gmh, gx, wq):
    n = q.shape[0]
    full = lambda *shape: pl.BlockSpec(shape, lambda i: (0,) * len(shape))
    ins = [cqt, q, k, v, so, gates, nst, m, yconv, x, wout, gmh, gx, wq]
    return pl.pallas_call(
        _s3_kernel,
        grid=(1,),
        in_specs=[full(*a.shape) for a in ins],
        out_specs=[full(n, D_MODEL), full(n, D_MODEL), full(n, MLSTM_DIM), full(n, 128)],
        out_shape=[jax.ShapeDtypeStruct((n, D_MODEL), F32), jax.ShapeDtypeStruct((n, D_MODEL), F32),
                   jax.ShapeDtypeStruct((n, MLSTM_DIM), F32), jax.ShapeDtypeStruct((n, 128), F32)],
        scratch_shapes=[pltpu.VMEM((n, D_MODEL), BF16)],
        compiler_params=_params(("arbitrary",)),
        name="s3_mlstm_finish",
    )(*ins)


def _s4_kernel(qx_ref, kc_ref, vc_ref, o_ref):
    bb = kc_ref.shape[0]
    sub = lax.broadcasted_iota(jnp.int32, (bb, X_HEAD_DIM), 0)

    def body(bl, outs):
        q_row = _pick_row(qx_ref[...], bl)
        new = []
        for h in range(X_HEADS):
            sl = slice(h * X_HEAD_DIM, (h + 1) * X_HEAD_DIM)
            s = jnp.sum(kc_ref[bl, :, sl] * q_row[:, sl], axis=1, keepdims=True)
            s = s * (X_HEAD_DIM ** -0.5)
            e = jnp.exp(s - jnp.max(s, axis=0, keepdims=True))
            p = e * (1.0 / jnp.sum(e, axis=0, keepdims=True))
            o_row = jnp.sum(p * vc_ref[bl, :, sl], axis=0, keepdims=True)
            new.append(jnp.where(sub == bl, o_row, outs[h]))
        return tuple(new)

    outs = lax.fori_loop(0, bb, body, tuple(jnp.zeros((bb, X_HEAD_DIM), F32) for _ in range(X_HEADS)))
    for h in range(X_HEADS):
        o_ref[:, h * X_HEAD_DIM:(h + 1) * X_HEAD_DIM] = outs[h]


def _s4_call(qx, kc, vc):
    n = qx.shape[0]
    bb = S4_ROWS
    cache = pl.BlockSpec((bb, N_MEM, D_MODEL), lambda i: (i, 0, 0))
    return pl.pallas_call(
        _s4_kernel,
        grid=(n // bb,),
        in_specs=[pl.BlockSpec((bb, D_MODEL), lambda i: (i, 0)), cache, cache],
        out_specs=pl.BlockSpec((bb, D_MODEL), lambda i: (i, 0)),
        out_shape=jax.ShapeDtypeStruct((n, D_MODEL), F32),
        compiler_params=_params(("arbitrary",)),
        name="s4_cross_attn",
    )(qx, kc, vc)


def kernel(x_prompt, x_sample, mem_prompt, state_conv, state_mlstm_C, state_mlstm_n, state_mlstm_m,
           cache_mem_k, cache_mem_v, g_mix, w_in, b_in, conv_w, g_mh, w_out, g_cross, g_mem,
           w_xq, w_xkv, w_xo, g_ffn, w_gu, w_down, g_final):
    n_batch, seq_len, _ = x_prompt.shape
    n_dec = x_sample.shape[0]
    depth = w_in.shape[0]
    assert depth == 1 and x_sample.shape[1] == 1
    assert seq_len % ROW_TILE == 0 and seq_len % MLSTM_CHUNK == 0

    w_main = w_in[0, :, :MAIN_DIM].astype(BF16)
    b_main = b_in[0, :MAIN_DIM].reshape(1, MAIN_DIM)
    n_gate = 2 * HEADS
    w_gate = jnp.pad(w_in[0, :, MAIN_DIM:], ((0, 0), (0, GATE_PAD - n_gate))).astype(BF16)
    b_gate = jnp.pad(b_in[0, MAIN_DIM:], (0, GATE_PAD - n_gate)).reshape(1, GATE_PAD)
    v_cols = slice(3 * CONV_DIM + 2 * MLSTM_DIM, 3 * CONV_DIM + 3 * MLSTM_DIM)
    w_vt = w_in[0, :, v_cols].T.astype(BF16)
    b_vt = b_in[0, v_cols].reshape(MLSTM_DIM, 1)
    w_gate_t = w_gate.T
    b_gate_t = b_gate.reshape(GATE_PAD, 1)
    w_out_b = w_out[0].astype(BF16)
    w_xq_b = w_xq[0].astype(BF16)
    w_xkv_b = w_xkv[0].astype(BF16)
    w_xo_b = w_xo[0].astype(BF16)
    n_ff = D_FF // FF_CHUNK
    w_gu_b = jnp.concatenate(
        [w_gu[0, :, :D_FF].reshape(D_MODEL, n_ff, FF_CHUNK),
         w_gu[0, :, D_FF:].reshape(D_MODEL, n_ff, FF_CHUNK)], axis=2
    ).reshape(D_MODEL, 2 * D_FF).astype(BF16)
    w_down_b = w_down[0].astype(BF16)
    g_mix_r = g_mix[0].reshape(1, D_MODEL)
    g_cross_r = g_cross[0].reshape(1, D_MODEL)
    g_mem_r = g_mem[0].reshape(1, D_MODEL)
    g_ffn_r = g_ffn[0].reshape(1, D_MODEL)
    g_final_r = g_final.reshape(1, D_MODEL)
    g_mh_r = g_mh[0].reshape(1, MLSTM_DIM)
    cw = conv_w[0]

    xp = x_prompt.reshape(n_batch * seq_len, D_MODEL)
    yconv, q, k, v, so, gates, p_conv = _p1_call(xp, g_mix_r, w_main, b_main, w_gate, b_gate, cw, seq_len)
    hp1, p_c, p_n, p_m = _p2_call(q, k, v, so, yconv, gates, xp, w_out_b, g_mh_r, n_batch, seq_len)
    pk, pv, pkb, pvb = _pm_call(mem_prompt.reshape(n_batch * N_MEM, D_MODEL), g_mem_r, w_xkv_b)
    o_p = _p3_call(hp1, g_cross_r, w_xq_b, pkb.reshape(n_batch, N_MEM, D_MODEL),
                   pvb.reshape(n_batch, N_MEM, D_MODEL), seq_len)
    y_p = _p4_call(hp1, o_p, w_xo_b, g_ffn_r, w_gu_b, w_down_b, g_final_r, ROW_TILE)

    xs = x_sample.reshape(n_dec, D_MODEL)
    st = state_conv[0].reshape(n_dec, (CONV_W - 1) * CONV_DIM)
    m0 = state_mlstm_m[0]
    s_yconv, s_conv, sq, sk, sv, sso, sgates, svt, sgates_t = _s1_call(
        xs, g_mix_r, w_main, b_main, w_gate, b_gate, cw, st, w_vt, b_vt, w_gate_t, b_gate_t)
    s_c, cqt = _s2_call(state_mlstm_C[0], sq, sk, svt, sgates, sgates_t, m0, m0.T)
    hs1, qx, s_n, s_m = _s3_call(cqt, sq, sk, sv, sso, sgates,
                                 state_mlstm_n[0].reshape(n_dec, MLSTM_DIM), m0, s_yconv, xs,
                                 w_out_b, g_mh_r, g_cross_r, w_xq_b)
    o_s = _s4_call(qx, cache_mem_k[0].reshape(n_dec, N_MEM, D_MODEL),
                   cache_mem_v[0].reshape(n_dec, N_MEM, D_MODEL))
    y_s = _p4_call(hs1, o_s, w_xo_b, g_ffn_r, w_gu_b, w_down_b, g_final_r, n_dec)

    mem_shape = (1, n_batch, N_MEM, X_HEADS, X_HEAD_DIM)
    return (y_p.reshape(n_batch, seq_len, D_MODEL),
            y_s.reshape(n_dec, 1, D_MODEL),
            p_conv.reshape(1, n_batch, CONV_W - 1, CONV_DIM),
            p_c.reshape(1, n_batch, HEADS, DV, DQK),
            p_n.reshape(1, n_batch, HEADS, DQK),
            p_m[:, 0, :HEADS].reshape(1, n_batch, HEADS),
            pk.reshape(mem_shape),
            pv.reshape(mem_shape),
            s_conv.reshape(1, n_dec, CONV_W - 1, CONV_DIM),
            s_c.reshape(1, n_dec, HEADS, DV, DQK),
            s_n.reshape(1, n_dec, HEADS, DQK),
            s_m[:, :HEADS].reshape(1, n_dec, HEADS))
```

```python
import functools

import jax
import jax.numpy as jnp
from jax import lax
from jax.experimental import pallas as pl
from jax.experimental.pallas import tpu as pltpu

F32 = jnp.float32
BF16 = jnp.bfloat16

D_MODEL = 1024
CONV_DIM = 512
CONV_W = 3
MLSTM_DIM = 512
HEADS = 4
DQK = 128
DV = 128
N_MEM = 256
X_HEADS = 4
X_HEAD_DIM = 256
D_FF = 2816
MAIN_DIM = 3 * CONV_DIM + 4 * MLSTM_DIM
GATE_PAD = 128
EPS = 1e-6

MLSTM_CHUNK = 256
ROW_TILE = 512
FF_CHUNK = 256
S2_ROWS = 16
S4_ROWS = 8
VMEM_LIMIT = 56 * 1024 * 1024


def _dot(a, b):
    return jnp.dot(a, b, preferred_element_type=F32)


def _dot_nt(a, b):
    return lax.dot_general(a, b, (((1,), (1,)), ((), ())), preferred_element_type=F32)


def _dot_tn(a, b):
    return lax.dot_general(a, b, (((0,), (0,)), ((), ())), preferred_element_type=F32)


def _rmsnorm(x, g):
    return x * lax.rsqrt(jnp.mean(x * x, axis=-1, keepdims=True) + EPS) * g


def _const_spec(shape):
    zeros = (0,) * len(shape)
    return pl.BlockSpec(shape, lambda *_: zeros, pipeline_mode=pl.Buffered(1))


def _params(sem):
    return pltpu.CompilerParams(dimension_semantics=sem, vmem_limit_bytes=VMEM_LIMIT)


def _gate_transform(gt):
    lane = lax.broadcasted_iota(jnp.int32, gt.shape, 1)
    return jnp.where(lane < HEADS, gt, jax.nn.log_sigmoid(gt))


def _gate_transform_rows(gt):
    sub = lax.broadcasted_iota(jnp.int32, gt.shape, 0)
    return jnp.where(sub < HEADS, gt, jax.nn.log_sigmoid(gt))


def _p1_kernel(tiles_per_batch, x_ref, g_ref, w_ref, b_ref, wg_ref, bg_ref, cw_ref,
               yconv_ref, q_ref, k_ref, v_ref, so_ref, gate_ref, pconv_ref, ubuf):
    tm = x_ref.shape[0]
    i = pl.program_id(0)
    xn = _rmsnorm(x_ref[...], g_ref[...]).astype(BF16)

    def seg(j):
        sl = slice(j * CONV_DIM, (j + 1) * CONV_DIM)
        return _dot(xn, w_ref[:, sl]) + b_ref[:, sl]

    prev = ubuf[tm:tm + 8, :]
    ubuf[0:8, :] = jnp.where(i % tiles_per_batch == 0, jnp.zeros_like(prev), prev)
    ubuf[8:8 + tm, :] = seg(1) * seg(2)
    conv = (cw_ref[0:1, :] * ubuf[6:6 + tm, :] + cw_ref[1:2, :] * ubuf[7:7 + tm, :]
            + cw_ref[2:3, :] * ubuf[8:8 + tm, :])
    yconv_ref[...] = (seg(0) * conv).astype(BF16)
    pconv_ref[0] = ubuf[tm + 6:tm + 8, :]

    q_ref[...] = seg(3).astype(BF16)
    k_ref[...] = (seg(4) * (DQK ** -0.5)).astype(BF16)
    v_ref[...] = seg(5).astype(BF16)
    so_ref[...] = jax.nn.sigmoid(seg(6)).astype(BF16)
    gate_ref[...] = _gate_transform_rows(_dot_nt(wg_ref[...], xn) + bg_ref[...])


def _p1_call(x, g, w, b, wg, bg, cw, seq_len):
    rows = x.shape[0]
    tm = ROW_TILE
    tiles_per_batch = seq_len // tm
    n_batch = rows // seq_len
    row = lambda width: pl.BlockSpec((tm, width), lambda i: (i, 0))
    return pl.pallas_call(
        functools.partial(_p1_kernel, tiles_per_batch),
        grid=(rows // tm,),
        in_specs=[row(D_MODEL), _const_spec((1, D_MODEL)), _const_spec((D_MODEL, MAIN_DIM)),
                  _const_spec((1, MAIN_DIM)), _const_spec((2 * HEADS, D_MODEL)),
                  _const_spec((2 * HEADS, 1)), _const_spec((CONV_W, CONV_DIM))],
        out_specs=[row(CONV_DIM), row(MLSTM_DIM), row(MLSTM_DIM), row(MLSTM_DIM), row(MLSTM_DIM),
                   pl.BlockSpec((2 * HEADS, tm), lambda i: (0, i)),
                   pl.BlockSpec((1, CONV_W - 1, CONV_DIM), lambda i: (i // tiles_per_batch, 0, 0))],
        out_shape=[jax.ShapeDtypeStruct((rows, CONV_DIM), BF16)]
        + [jax.ShapeDtypeStruct((rows, MLSTM_DIM), BF16)] * 4
        + [jax.ShapeDtypeStruct((2 * HEADS, rows), F32),
           jax.ShapeDtypeStruct((n_batch, CONV_W - 1, CONV_DIM), F32)],
        scratch_shapes=[pltpu.VMEM((tm + 8, CONV_DIM), F32)],
        compiler_params=_params(("arbitrary",)),
        name="p1_inproj_conv",
    )(x, g, w, b, wg, bg, cw)


def _p2_kernel(q_ref, k_ref, v_ref, so_ref, yconv_ref, gate_ref, x_ref, wout_ref, gmh_ref,
               hp_ref, pc_ref, pn_ref, pm_ref, c_s, m_s, y_s):
    L = q_ref.shape[0]
    c = pl.program_id(1)

    @pl.when(c == 0)
    def _():
        c_s[...] = jnp.zeros_like(c_s)
        m_s[...] = jnp.zeros_like(m_s)

    gt = gate_ref[...]
    row = lax.broadcasted_iota(jnp.int32, (L, L), 0)
    col = lax.broadcasted_iota(jnp.int32, (L, L), 1)
    causal = row >= col
    bsum = jnp.dot(gt, (row <= col).astype(F32), preferred_element_type=F32,
                   precision=lax.Precision.HIGHEST)

    y_s[:, 0:CONV_DIM] = yconv_ref[...]
    for h in range(HEADS):
        sl = slice(h * DQK, (h + 1) * DQK)
        q = q_ref[:, sl]
        k = k_ref[:, sl]
        v = v_ref[:, sl]
        lf_r = gt[HEADS + h:HEADS + h + 1, :]
        a_r = gt[h:h + 1, :] - bsum[HEADS + h:HEADS + h + 1, :]
        m_prev = jnp.max(m_s[h:h + 1, :], axis=1, keepdims=True)
        c_prev = c_s[h]

        m_c = jnp.maximum(m_prev, jnp.max(jnp.where(causal, a_r, -jnp.inf), axis=1, keepdims=True))
        b_c = jnp.sum(jnp.where(causal, lf_r, 0.0), axis=1, keepdims=True)
        w = _dot_nt(q, k) * jnp.exp(jnp.where(causal, a_r - m_c, -jnp.inf))
        g = jnp.exp(m_prev - m_c)
        qc = _dot_nt(q, c_prev.astype(BF16))
        num = g * qc[:, 0:DV] + _dot(w.astype(BF16), v)
        den = g * qc[:, DV:2 * DV] + jnp.sum(w, axis=1, keepdims=True)
        hh = num / jnp.maximum(jnp.abs(den), jnp.exp(-(b_c + m_c)))
        hh = hh * lax.rsqrt(jnp.mean(hh * hh, axis=1, keepdims=True) + EPS) * gmh_ref[:, sl]
        y_s[:, CONV_DIM + h * DV:CONV_DIM + (h + 1) * DV] = (so_ref[:, sl].astype(F32) * hh).astype(BF16)

        m_last = jnp.maximum(m_prev, jnp.max(a_r, axis=1, keepdims=True))
        b_last = jnp.sum(lf_r, axis=1, keepdims=True)
        s_r = jnp.exp(a_r - m_last)
        sv_t = jnp.concatenate([v.T.astype(F32) * s_r, jnp.broadcast_to(s_r, (DV, L))], axis=0)
        c_s[h] = jnp.exp(m_prev - m_last) * c_prev + _dot(sv_t.astype(BF16), k)
        m_s[h:h + 1, :] = jnp.broadcast_to(b_last + m_last, (1, m_s.shape[1]))

    hp_ref[...] = x_ref[...] + _dot(y_s[...], wout_ref[...])

    @pl.when(c == pl.num_programs(1) - 1)
    def _():
        lane = lax.broadcasted_iota(jnp.int32, (1, m_s.shape[1]), 1)
        acc = jnp.zeros((1, m_s.shape[1]), F32)
        for h in range(HEADS):
            pc_ref[0, h] = c_s[h, 0:DV, :]
            pn_ref[0, h:h + 1, :] = c_s[h, DV:DV + 1, :]
            acc = jnp.where(lane == h, m_s[h:h + 1, :], acc)
        pm_ref[0] = acc


def _p2_call(q, k, v, so, yconv, gates, x, wout, gmh, n_batch, seq_len):
    L = MLSTM_CHUNK
    nc = seq_len // L
    row = lambda width: pl.BlockSpec((L, width), lambda b, c: (b * nc + c, 0))
    rows = n_batch * seq_len
    return pl.pallas_call(
        _p2_kernel,
        grid=(n_batch, nc),
        in_specs=[row(MLSTM_DIM), row(MLSTM_DIM), row(MLSTM_DIM), row(MLSTM_DIM), row(CONV_DIM),
                  pl.BlockSpec((2 * HEADS, L), lambda b, c: (0, b * nc + c)), row(D_MODEL),
                  _const_spec((D_MODEL, D_MODEL)), _const_spec((1, MLSTM_DIM))],
        out_specs=[row(D_MODEL),
                   pl.BlockSpec((1, HEADS, DV, DQK), lambda b, c: (b, 0, 0, 0)),
                   pl.BlockSpec((1, HEADS, DQK), lambda b, c: (b, 0, 0)),
                   pl.BlockSpec((1, 1, 128), lambda b, c: (b, 0, 0))],
        out_shape=[jax.ShapeDtypeStruct((rows, D_MODEL), F32),
                   jax.ShapeDtypeStruct((n_batch, HEADS, DV, DQK), F32),
                   jax.ShapeDtypeStruct((n_batch, HEADS, DQK), F32),
                   jax.ShapeDtypeStruct((n_batch, 1, 128), F32)],
        scratch_shapes=[pltpu.VMEM((HEADS, 2 * DV, DQK), F32), pltpu.VMEM((8, 128), F32),
                        pltpu.VMEM((L, D_MODEL), BF16)],
        compiler_params=_params(("arbitrary", "arbitrary")),
        name="p2_mlstm_outproj",
    )(q, k, v, so, yconv, gates, x, wout, gmh)


def _pm_kernel(mem_ref, g_ref, w_ref, k_ref, v_ref, kb_ref, vb_ref):
    xn = _rmsnorm(mem_ref[...], g_ref[...]).astype(BF16)
    kk = _dot(xn, w_ref[:, 0:D_MODEL])
    vv = _dot(xn, w_ref[:, D_MODEL:2 * D_MODEL])
    for h in range(X_HEADS):
        sl = slice(h * X_HEAD_DIM, (h + 1) * X_HEAD_DIM)
        k_ref[:, h, :] = kk[:, sl]
        v_ref[:, h, :] = vv[:, sl]
    kb_ref[...] = kk.astype(BF16)
    vb_ref[...] = vv.astype(BF16)


def _pm_call(mem, g, w):
    rows = mem.shape[0]
    tm = ROW_TILE
    row = pl.BlockSpec((tm, D_MODEL), lambda i: (i, 0))
    row4 = pl.BlockSpec((tm, X_HEADS, X_HEAD_DIM), lambda i: (i, 0, 0))
    return pl.pallas_call(
        _pm_kernel,
        grid=(rows // tm,),
        in_specs=[row, _const_spec((1, D_MODEL)), _const_spec((D_MODEL, 2 * D_MODEL))],
        out_specs=[row4, row4, row, row],
        out_shape=[jax.ShapeDtypeStruct((rows, X_HEADS, X_HEAD_DIM), F32)] * 2
        + [jax.ShapeDtypeStruct((rows, D_MODEL), BF16)] * 2,
        compiler_params=_params(("arbitrary",)),
        name="pm_mem_kv",
    )(mem, g, w)


def _p3_kernel(hp_ref, g_ref, wq_ref, k_ref, v_ref, o_ref):
    xn = _rmsnorm(hp_ref[...], g_ref[...]).astype(BF16)
    q = _dot(xn, wq_ref[...])
    for h in range(X_HEADS):
        sl = slice(h * X_HEAD_DIM, (h + 1) * X_HEAD_DIM)
        s = _dot_nt(q[:, sl].astype(BF16), k_ref[0, :, sl]) * (X_HEAD_DIM ** -0.5)
        e = jnp.exp(s - jnp.max(s, axis=1, keepdims=True))
        p = e * (1.0 / jnp.sum(e, axis=1, keepdims=True))
        o_ref[:, sl] = _dot(p.astype(BF16), v_ref[0, :, sl]).astype(BF16)


def _p3_call(hp, g, wq, kb, vb, seq_len):
    rows = hp.shape[0]
    tm = ROW_TILE
    tiles_per_batch = seq_len // tm
    row = pl.BlockSpec((tm, D_MODEL), lambda i: (i, 0))
    mem = pl.BlockSpec((1, N_MEM, D_MODEL), lambda i: (i // tiles_per_batch, 0, 0))
    return pl.pallas_call(
        _p3_kernel,
        grid=(rows // tm,),
        in_specs=[row, _const_spec((1, D_MODEL)), _const_spec((D_MODEL, D_MODEL)), mem, mem],
        out_specs=row,
        out_shape=jax.ShapeDtypeStruct((rows, D_MODEL), BF16),
        compiler_params=_params(("arbitrary",)),
        name="p3_cross_attn",
    )(hp, g, wq, kb, vb)


def _p4_kernel(hp_ref, o_ref, wxo_ref, gf_ref, wgu_ref, wd_ref, gfin_ref, y_ref, act_s):
    hp = hp_ref[...] + _dot(o_ref[...].astype(BF16), wxo_ref[...])
    xn = _rmsnorm(hp, gf_ref[...]).astype(BF16)
    for j in range(D_FF // FF_CHUNK):
        g = _dot(xn, wgu_ref[:, FF_CHUNK * j:FF_CHUNK * (j + 1)])
        u = _dot(xn, wgu_ref[:, D_FF + FF_CHUNK * j:D_FF + FF_CHUNK * (j + 1)])
        act_s[:, FF_CHUNK * j:FF_CHUNK * (j + 1)] = (g * jax.nn.sigmoid(g) * u).astype(BF16)
    hp = hp + _dot(act_s[...], wd_ref[...])
    y_ref[...] = _rmsnorm(hp, gfin_ref[...])


def _p4_call(hp, o, wxo, gf, wgu, wd, gfin, tm):
    rows = hp.shape[0]
    row = pl.BlockSpec((tm, D_MODEL), lambda i: (i, 0))
    return pl.pallas_call(
        _p4_kernel,
        grid=(rows // tm,),
        in_specs=[row, row, _const_spec((D_MODEL, D_MODEL)), _const_spec((1, D_MODEL)),
                  _const_spec((D_MODEL, 2 * D_FF)), _const_spec((D_FF, D_MODEL)),
                  _const_spec((1, D_MODEL))],
        out_specs=row,
        out_shape=jax.ShapeDtypeStruct((rows, D_MODEL), F32),
        scratch_shapes=[pltpu.VMEM((tm, D_FF), BF16)],
        compiler_params=_params(("arbitrary",)),
        name="p4_ffn_final",
    )(hp, o, wxo, gf, wgu, wd, gfin)


def _s1_kernel(x_ref, g_ref, w_ref, b_ref, wg_ref, bg_ref, cw_ref, st_ref, wvt_ref, bvt_ref,
               wgt_ref, bgt_ref,
               yconv_ref, sconv_ref, q_ref, k_ref, v_ref, so_ref, gate_ref, vt_ref, gatet_ref):
    xn = _rmsnorm(x_ref[...], g_ref[...]).astype(BF16)

    def seg(j):
        sl = slice(j * CONV_DIM, (j + 1) * CONV_DIM)
        return _dot(xn, w_ref[:, sl]) + b_ref[:, sl]

    u = seg(1) * seg(2)
    st0 = st_ref[:, 0:CONV_DIM]
    st1 = st_ref[:, CONV_DIM:2 * CONV_DIM]
    conv = cw_ref[0:1, :] * st0 + cw_ref[1:2, :] * st1 + cw_ref[2:3, :] * u
    yconv_ref[...] = seg(0) * conv
    sconv_ref[:, 0:CONV_DIM] = st1
    sconv_ref[:, CONV_DIM:2 * CONV_DIM] = u
    q_ref[...] = seg(3)
    k_ref[...] = seg(4) * (DQK ** -0.5)
    v_ref[...] = seg(5)
    so_ref[...] = jax.nn.sigmoid(seg(6))
    gate_ref[...] = _gate_transform(_dot(xn, wg_ref[...]) + bg_ref[...])
    vt_ref[...] = _dot_nt(wvt_ref[...], xn) + bvt_ref[...]
    gt_t = _dot_nt(wgt_ref[...], xn) + bgt_ref[...]
    sub = lax.broadcasted_iota(jnp.int32, gt_t.shape, 0)
    gatet_ref[...] = jnp.where(sub < HEADS, gt_t, jax.nn.log_sigmoid(gt_t))


def _s1_call(x, g, w, b, wg, bg, cw, st, wvt, bvt, wgt, bgt):
    n = x.shape[0]
    full = lambda *shape: pl.BlockSpec(shape, lambda i: (0,) * len(shape))
    ins = [x, g, w, b, wg, bg, cw, st, wvt, bvt, wgt, bgt]
    return pl.pallas_call(
        _s1_kernel,
        grid=(1,),
        in_specs=[full(*a.shape) for a in ins],
        out_specs=[full(n, CONV_DIM), full(n, 2 * CONV_DIM), full(n, MLSTM_DIM), full(n, MLSTM_DIM),
                   full(n, MLSTM_DIM), full(n, MLSTM_DIM), full(n, GATE_PAD), full(MLSTM_DIM, n),
                   full(GATE_PAD, n)],
        out_shape=[jax.ShapeDtypeStruct((n, CONV_DIM), F32),
                   jax.ShapeDtypeStruct((n, 2 * CONV_DIM), F32)]
        + [jax.ShapeDtypeStruct((n, MLSTM_DIM), F32)] * 4
        + [jax.ShapeDtypeStruct((n, GATE_PAD), F32),
           jax.ShapeDtypeStruct((MLSTM_DIM, n), F32),
           jax.ShapeDtypeStruct((GATE_PAD, n), F32)],
        compiler_params=_params(("arbitrary",)),
        name="s1_inproj_conv",
    )(*ins)


def _pick_row(tile, r):
    sub = lax.broadcasted_iota(jnp.int32, tile.shape, 0)
    return jnp.sum(jnp.where(sub == r, tile, 0.0), axis=0, keepdims=True)


def _s2_kernel(c_ref, q_ref, k_ref, vt_ref, gate_ref, gatet_ref, m_ref, mt_ref,
               cn_ref, cqt_ref, dec_s, svt_s):
    n = vt_ref.shape[1]
    bb = c_ref.shape[0]
    i = pl.program_id(0)

    @pl.when(i == 0)
    def _():
        cqt_ref[...] = jnp.zeros_like(cqt_ref)

    lane = lax.broadcasted_iota(jnp.int32, (DV, n), 1)
    for h in range(HEADS):
        sl = slice(h * DQK, (h + 1) * DQK)
        ig_c = gate_ref[:, h:h + 1]
        lf_c = gate_ref[:, HEADS + h:HEADS + h + 1]
        m_c = m_ref[:, h:h + 1]
        m_new_c = jnp.maximum(lf_c + m_c, ig_c)
        dec_s[...] = jnp.broadcast_to(jnp.exp(lf_c + m_c - m_new_c), (bb, DQK))
        ig_r = gatet_ref[h:h + 1, :]
        lf_r = gatet_ref[HEADS + h:HEADS + h + 1, :]
        m_r = mt_ref[h:h + 1, :]
        s_r = jnp.exp(ig_r - jnp.maximum(lf_r + m_r, ig_r))
        svt_s[...] = vt_ref[sl, :] * s_r

        def body(bl, cqt):
            onehot = lane == i * bb + bl
            c = c_ref[bl, h]
            q_row = _pick_row(q_ref[:, sl], bl)
            k_row = _pick_row(k_ref[:, sl], bl)
            dec_row = _pick_row(dec_s[...], bl)
            cq_col = jnp.sum(c * q_row, axis=1, keepdims=True)
            sv_col = jnp.sum(jnp.where(onehot, svt_s[...], 0.0), axis=1, keepdims=True)
            cn_ref[bl, h] = dec_row * c + sv_col * k_row
            return jnp.where(onehot, cq_col, cqt)

        cqt_ref[sl, :] = lax.fori_loop(0, bb, body, cqt_ref[sl, :], unroll=True)


def _s2_call(c, q, k, vt, gates, gatest, m, mt):
    n = q.shape[0]
    bb = S2_ROWS
    full = lambda *shape: pl.BlockSpec(shape, lambda i: (0,) * len(shape))
    rows = lambda width: pl.BlockSpec((bb, width), lambda i: (i, 0))
    cblk = pl.BlockSpec((bb, HEADS, DV, DQK), lambda i: (i, 0, 0, 0))
    return pl.pallas_call(
        _s2_kernel,
        grid=(n // bb,),
        in_specs=[cblk, rows(MLSTM_DIM), rows(MLSTM_DIM), full(MLSTM_DIM, n),
                  rows(GATE_PAD), full(GATE_PAD, n), rows(HEADS), full(HEADS, n)],
        out_specs=[cblk, full(MLSTM_DIM, n)],
        out_shape=[jax.ShapeDtypeStruct(c.shape, F32), jax.ShapeDtypeStruct((MLSTM_DIM, n), F32)],
        scratch_shapes=[pltpu.VMEM((bb, DQK), F32), pltpu.VMEM((DV, n), F32)],
        compiler_params=_params(("arbitrary",)),
        name="s2_memory_update",
    )(c, q, k, vt, gates, gatest, m, mt)


def _s3_kernel(cqt_ref, q_ref, k_ref, v_ref, so_ref, gate_ref, n_ref, m_ref, yconv_ref, x_ref,
               wout_ref, gmh_ref, gx_ref, wq_ref,
               hs_ref, qx_ref, nn_ref, mn_ref, y_s):
    n_rows = q_ref.shape[0]
    y_s[:, 0:CONV_DIM] = yconv_ref[...].astype(BF16)
    lane = lax.broadcasted_iota(jnp.int32, (n_rows, mn_ref.shape[1]), 1)
    m_out = jnp.zeros((n_rows, mn_ref.shape[1]), F32)
    for h in range(HEADS):
        sl = slice(h * DQK, (h + 1) * DQK)
        q = q_ref[:, sl]
        k = k_ref[:, sl]
        v = v_ref[:, sl]
        n_prev = n_ref[:, sl]
        cq = cqt_ref[sl, :].T
        ig = gate_ref[:, h:h + 1]
        lf = gate_ref[:, HEADS + h:HEADS + h + 1]
        m_prev = m_ref[:, h:h + 1]
        inter = lf + m_prev
        m_row = jnp.maximum(inter, ig)
        wgt = jnp.sum(q * k, axis=1, keepdims=True) * jnp.exp(ig - m_row)
        g = jnp.exp(inter - m_row)
        num = g * cq + wgt * v
        den = g * jnp.sum(n_prev * q, axis=1, keepdims=True) + wgt
        hh = num / jnp.maximum(jnp.abs(den), jnp.exp(-m_row))
        hh = hh * lax.rsqrt(jnp.mean(hh * hh, axis=1, keepdims=True) + EPS) * gmh_ref[:, sl]
        y_s[:, CONV_DIM + h * DV:CONV_DIM + (h + 1) * DV] = (so_ref[:, sl] * hh).astype(BF16)
        nn_ref[:, sl] = g * n_prev + jnp.exp(ig - m_row) * k
        m_out = jnp.where(lane == h, m_row, m_out)
    mn_ref[...] = m_out
    hs = x_ref[...] + _dot(y_s[...], wout_ref[...])
    hs_ref[...] = hs
    qx_ref[...] = _dot(_rmsnorm(hs, gx_ref[...]).astype(BF16), wq_ref[...])


def _s3_call(cqt, q, k, v, so, gates, nst, m, yconv, x, wout, gmh, gx, wq):
    n = q.shape[0]
    full = lambda *shape: pl.BlockSpec(shape, lambda i: (0,) * len(shape))
    ins = [cqt, q, k, v, so, gates, nst, m, yconv, x, wout, gmh, gx, wq]
    return pl.pallas_call(
        _s3_kernel,
        grid=(1,),
        in_specs=[full(*a.shape) for a in ins],
        out_specs=[full(n, D_MODEL), full(n, D_MODEL), full(n, MLSTM_DIM), full(n, 128)],
        out_shape=[jax.ShapeDtypeStruct((n, D_MODEL), F32), jax.ShapeDtypeStruct((n, D_MODEL), F32),
                   jax.ShapeDtypeStruct((n, MLSTM_DIM), F32), jax.ShapeDtypeStruct((n, 128), F32)],
        scratch_shapes=[pltpu.VMEM((n, D_MODEL), BF16)],
        compiler_params=_params(("arbitrary",)),
        name="s3_mlstm_finish",
    )(*ins)


def _s4_kernel(q_ref, kc_ref, vc_ref, o_ref):
    def body(bl, carry):
        qs = q_ref[bl] * (X_HEAD_DIM ** -0.5)
        s = jnp.sum(kc_ref[bl] * qs, axis=-1, keepdims=True)
        e = jnp.exp(s - jnp.max(s, axis=0, keepdims=True))
        o_ref[bl] = jnp.sum(e * vc_ref[bl], axis=0) * (1.0 / jnp.sum(e, axis=0))
        return carry

    lax.fori_loop(0, kc_ref.shape[0], body, 0)


def _s4_call(q4, kc, vc):
    n = q4.shape[0]
    bb = S4_ROWS
    cache = pl.BlockSpec((bb, N_MEM, X_HEADS, X_HEAD_DIM), lambda i: (i, 0, 0, 0))
    rows = pl.BlockSpec((bb, X_HEADS, X_HEAD_DIM), lambda i: (i, 0, 0))
    return pl.pallas_call(
        _s4_kernel,
        grid=(n // bb,),
        in_specs=[rows, cache, cache],
        out_specs=rows,
        out_shape=jax.ShapeDtypeStruct((n, X_HEADS, X_HEAD_DIM), F32),
        compiler_params=_params(("arbitrary",)),
        name="s4_cross_attn",
    )(q4, kc, vc)


def kernel(x_prompt, x_sample, mem_prompt, state_conv, state_mlstm_C, state_mlstm_n, state_mlstm_m,
           cache_mem_k, cache_mem_v, g_mix, w_in, b_in, conv_w, g_mh, w_out, g_cross, g_mem,
           w_xq, w_xkv, w_xo, g_ffn, w_gu, w_down, g_final):
    n_batch, seq_len, _ = x_prompt.shape
    n_dec = x_sample.shape[0]
    depth = w_in.shape[0]
    assert depth == 1 and x_sample.shape[1] == 1
    assert seq_len % ROW_TILE == 0 and seq_len % MLSTM_CHUNK == 0

    w_main = w_in[0, :, :MAIN_DIM].astype(BF16)
    b_main = b_in[0, :MAIN_DIM].reshape(1, MAIN_DIM)
    n_gate = 2 * HEADS
    w_gate = jnp.pad(w_in[0, :, MAIN_DIM:], ((0, 0), (0, GATE_PAD - n_gate))).astype(BF16)
    b_gate = jnp.pad(b_in[0, MAIN_DIM:], (0, GATE_PAD - n_gate)).reshape(1, GATE_PAD)
    v_cols = slice(3 * CONV_DIM + 2 * MLSTM_DIM, 3 * CONV_DIM + 3 * MLSTM_DIM)
    w_vt = w_in[0, :, v_cols].T.astype(BF16)
    b_vt = b_in[0, v_cols].reshape(MLSTM_DIM, 1)
    w_gate_t = w_gate.T
    b_gate_t = b_gate.reshape(GATE_PAD, 1)
    w_gate_r = w_gate_t[:n_gate]
    b_gate_r = b_gate_t[:n_gate]
    w_out_b = w_out[0].astype(BF16)
    w_xq_b = w_xq[0].astype(BF16)
    w_xkv_b = w_xkv[0].astype(BF16)
    w_xo_b = w_xo[0].astype(BF16)
    w_gu_b = w_gu[0].astype(BF16)
    w_down_b = w_down[0].astype(BF16)
    g_mix_r = g_mix[0].reshape(1, D_MODEL)
    g_cross_r = g_cross[0].reshape(1, D_MODEL)
    g_mem_r = g_mem[0].reshape(1, D_MODEL)
    g_ffn_r = g_ffn[0].reshape(1, D_MODEL)
    g_final_r = g_final.reshape(1, D_MODEL)
    g_mh_r = g_mh[0].reshape(1, MLSTM_DIM)
    cw = conv_w[0]

    xp = x_prompt.reshape(n_batch * seq_len, D_MODEL)
    yconv, q, k, v, so, gates, p_conv = _p1_call(xp, g_mix_r, w_main, b_main, w_gate_r, b_gate_r, cw, seq_len)
    hp1, p_c, p_n, p_m = _p2_call(q, k, v, so, yconv, gates, xp, w_out_b, g_mh_r, n_batch, seq_len)
    pk, pv, pkb, pvb = _pm_call(mem_prompt.reshape(n_batch * N_MEM, D_MODEL), g_mem_r, w_xkv_b)
    o_p = _p3_call(hp1, g_cross_r, w_xq_b, pkb.reshape(n_batch, N_MEM, D_MODEL),
                   pvb.reshape(n_batch, N_MEM, D_MODEL), seq_len)
    y_p = _p4_call(hp1, o_p, w_xo_b, g_ffn_r, w_gu_b, w_down_b, g_final_r, ROW_TILE)

    xs = x_sample.reshape(n_dec, D_MODEL)
    st = state_conv[0].reshape(n_dec, (CONV_W - 1) * CONV_DIM)
    m0 = state_mlstm_m[0]
    s_yconv, s_conv, sq, sk, sv, sso, sgates, svt, sgates_t = _s1_call(
        xs, g_mix_r, w_main, b_main, w_gate, b_gate, cw, st, w_vt, b_vt, w_gate_t, b_gate_t)
    s_c, cqt = _s2_call(state_mlstm_C[0], sq, sk, svt, sgates, sgates_t, m0, m0.T)
    hs1, qx, s_n, s_m = _s3_call(cqt, sq, sk, sv, sso, sgates,
                                 state_mlstm_n[0].reshape(n_dec, MLSTM_DIM), m0, s_yconv, xs,
                                 w_out_b, g_mh_r, g_cross_r, w_xq_b)
    o_s = _s4_call(qx.reshape(n_dec, X_HEADS, X_HEAD_DIM), cache_mem_k[0], cache_mem_v[0])
    y_s = _p4_call(hs1, o_s.reshape(n_dec, D_MODEL), w_xo_b, g_ffn_r, w_gu_b, w_down_b, g_final_r, n_dec)

    mem_shape = (1, n_batch, N_MEM, X_HEADS, X_HEAD_DIM)
    return (y_p.reshape(n_batch, seq_len, D_MODEL),
            y_s.reshape(n_dec, 1, D_MODEL),
            p_conv.reshape(1, n_batch, CONV_W - 1, CONV_DIM),
            p_c.reshape(1, n_batch, HEADS, DV, DQK),
            p_n.reshape(1, n_batch, HEADS, DQK),
            p_m[:, 0, :HEADS].reshape(1, n_batch, HEADS),
            pk.reshape(mem_shape),
            pv.reshape(mem_shape),
            s_conv.reshape(1, n_dec, CONV_W - 1, CONV_DIM),
            s_c.reshape(1, n_dec, HEADS, DV, DQK),
            s_n.reshape(1, n_dec, HEADS, DQK),
            s_m[:, :HEADS].reshape(1, n_dec, HEADS))
```

```python
import functools

import jax
import jax.numpy as jnp
from jax import lax
from jax.experimental import pallas as pl
from jax.experimental.pallas import tpu as pltpu

F32 = jnp.float32
BF16 = jnp.bfloat16

D_MODEL = 1024
CONV_DIM = 512
CONV_W = 3
MLSTM_DIM = 512
HEADS = 4
DQK = 128
DV = 128
N_MEM = 256
X_HEADS = 4
X_HEAD_DIM = 256
D_FF = 2816
MAIN_DIM = 3 * CONV_DIM + 4 * MLSTM_DIM
GATE_PAD = 128
EPS = 1e-6

MLSTM_CHUNK = 256
P2_SEQS = 2
ROW_TILE = 512
FF_CHUNK = 256
S2_ROWS = 16
VMEM_LIMIT = 56 * 1024 * 1024


def _dot(a, b):
    return jnp.dot(a, b, preferred_element_type=F32)


def _dot_nt(a, b):
    return lax.dot_general(a, b, (((1,), (1,)), ((), ())), preferred_element_type=F32)


def _dot_tn(a, b):
    return lax.dot_general(a, b, (((0,), (0,)), ((), ())), preferred_element_type=F32)


def _rmsnorm(x, g):
    return x * lax.rsqrt(jnp.mean(x * x, axis=-1, keepdims=True) + EPS) * g


def _const_spec(shape):
    zeros = (0,) * len(shape)
    return pl.BlockSpec(shape, lambda *_: zeros, pipeline_mode=pl.Buffered(1))


def _params(sem):
    return pltpu.CompilerParams(dimension_semantics=sem, vmem_limit_bytes=VMEM_LIMIT)


def _gate_transform(gt):
    lane = lax.broadcasted_iota(jnp.int32, gt.shape, 1)
    return jnp.where(lane < HEADS, gt, jax.nn.log_sigmoid(gt))


def _gate_transform_rows(gt):
    sub = lax.broadcasted_iota(jnp.int32, gt.shape, 0)
    return jnp.where(sub < HEADS, gt, jax.nn.log_sigmoid(gt))


def _p1_kernel(tiles_per_batch, x_ref, g_ref, w_ref, b_ref, wg_ref, bg_ref, cw_ref,
               yconv_ref, q_ref, k_ref, v_ref, so_ref, gate_ref, pconv_ref, ubuf):
    tm = x_ref.shape[0]
    i = pl.program_id(0)
    xn = _rmsnorm(x_ref[...], g_ref[...]).astype(BF16)

    def seg(j):
        sl = slice(j * CONV_DIM, (j + 1) * CONV_DIM)
        return _dot(xn, w_ref[:, sl]) + b_ref[:, sl]

    prev = ubuf[tm:tm + 8, :]
    ubuf[0:8, :] = jnp.where(i % tiles_per_batch == 0, jnp.zeros_like(prev), prev)
    ubuf[8:8 + tm, :] = seg(1) * seg(2)
    conv = (cw_ref[0:1, :] * ubuf[6:6 + tm, :] + cw_ref[1:2, :] * ubuf[7:7 + tm, :]
            + cw_ref[2:3, :] * ubuf[8:8 + tm, :])
    yconv_ref[...] = (seg(0) * conv).astype(BF16)
    pconv_ref[0] = ubuf[tm + 6:tm + 8, :]

    q_ref[...] = seg(3).astype(BF16)
    k_ref[...] = (seg(4) * (DQK ** -0.5)).astype(BF16)
    v_ref[...] = seg(5).astype(BF16)
    so_ref[...] = jax.nn.sigmoid(seg(6)).astype(BF16)
    gt = _gate_transform_rows(_dot_nt(wg_ref[...], xn) + bg_ref[...])
    n_gate = gt.shape[0]
    gate_ref[0, 0:n_gate, :] = gt
    L = MLSTM_CHUNK
    n_blk = tm // L
    hi = gt.astype(BF16).astype(F32)
    r1 = gt - hi
    mid = r1.astype(BF16).astype(F32)
    lo = r1 - mid
    terms = jnp.concatenate([t[:, j * L:(j + 1) * L] for t in (hi, mid, lo) for j in range(n_blk)], axis=0)
    tri = (lax.broadcasted_iota(jnp.int32, (L, L), 0) <= lax.broadcasted_iota(jnp.int32, (L, L), 1)).astype(BF16)
    parts = _dot(terms.astype(BF16), tri)
    for j in range(n_blk):
        rows = [parts[(t * n_blk + j) * n_gate:(t * n_blk + j + 1) * n_gate, :] for t in range(3)]
        gate_ref[0, n_gate:2 * n_gate, j * L:(j + 1) * L] = (rows[0] + rows[1]) + rows[2]


def _p1_call(x, g, w, b, wg, bg, cw, seq_len):
    rows = x.shape[0]
    tm = ROW_TILE
    tiles_per_batch = seq_len // tm
    n_batch = rows // seq_len
    row = lambda width: pl.BlockSpec((tm, width), lambda i: (i, 0))
    return pl.pallas_call(
        functools.partial(_p1_kernel, tiles_per_batch),
        grid=(rows // tm,),
        in_specs=[row(D_MODEL), _const_spec((1, D_MODEL)), _const_spec((D_MODEL, MAIN_DIM)),
                  _const_spec((1, MAIN_DIM)), _const_spec((2 * HEADS, D_MODEL)),
                  _const_spec((2 * HEADS, 1)), _const_spec((CONV_W, CONV_DIM))],
        out_specs=[row(CONV_DIM), row(MLSTM_DIM), row(MLSTM_DIM), row(MLSTM_DIM), row(MLSTM_DIM),
                   pl.BlockSpec((1, 4 * HEADS, tm), lambda i: (i // tiles_per_batch, 0, i % tiles_per_batch)),
                   pl.BlockSpec((1, CONV_W - 1, CONV_DIM), lambda i: (i // tiles_per_batch, 0, 0))],
        out_shape=[jax.ShapeDtypeStruct((rows, CONV_DIM), BF16)]
        + [jax.ShapeDtypeStruct((rows, MLSTM_DIM), BF16)] * 4
        + [jax.ShapeDtypeStruct((n_batch, 4 * HEADS, seq_len), F32),
           jax.ShapeDtypeStruct((n_batch, CONV_W - 1, CONV_DIM), F32)],
        scratch_shapes=[pltpu.VMEM((tm + 8, CONV_DIM), F32)],
        compiler_params=_params(("arbitrary",)),
        name="p1_inproj_conv",
    )(x, g, w, b, wg, bg, cw)


def _p2_kernel(q_ref, k_ref, v_ref, so_ref, yconv_ref, gate_ref, x_ref, wout_ref, gmh_ref,
               hp_ref, pc_ref, pn_ref, pm_ref, c_s, m_s, y_s):
    nb, L = q_ref.shape[0], q_ref.shape[1]
    c = pl.program_id(1)

    @pl.when(c == 0)
    def _():
        c_s[...] = jnp.zeros_like(c_s)
        m_s[...] = jnp.zeros_like(m_s)

    row = lax.broadcasted_iota(jnp.int32, (L, L), 0)
    col = lax.broadcasted_iota(jnp.int32, (L, L), 1)
    causal = row >= col

    for bi in range(nb):
        gt = gate_ref[bi]
        for h in range(HEADS):
            sl = slice(h * DQK, (h + 1) * DQK)
            q = q_ref[bi, :, sl]
            k = k_ref[bi, :, sl]
            v = v_ref[bi, :, sl]
            lf_r = gt[HEADS + h:HEADS + h + 1, :]
            a_r = gt[h:h + 1, :] - gt[3 * HEADS + h:3 * HEADS + h + 1, :]
            m_prev = jnp.max(m_s[bi, h:h + 1, :], axis=1, keepdims=True)
            c_prev = c_s[bi, h]

            m_c = jnp.maximum(m_prev, jnp.max(jnp.where(causal, a_r, -jnp.inf), axis=1, keepdims=True))
            b_c = jnp.sum(jnp.where(causal, lf_r, 0.0), axis=1, keepdims=True)
            w = _dot_nt(q, k) * jnp.exp(jnp.where(causal, a_r - m_c, -jnp.inf))
            g = jnp.exp(m_prev - m_c)
            qc = _dot_nt(q, c_prev.astype(BF16))
            num = g * qc[:, 0:DV] + _dot(w.astype(BF16), v)
            den = g * qc[:, DV:2 * DV] + jnp.sum(w, axis=1, keepdims=True)
            hh = num / jnp.maximum(jnp.abs(den), jnp.exp(-(b_c + m_c)))
            hh = hh * lax.rsqrt(jnp.mean(hh * hh, axis=1, keepdims=True) + EPS) * gmh_ref[:, sl]
            y_s[bi * L:(bi + 1) * L, h * DV:(h + 1) * DV] = (so_ref[bi, :, sl].astype(F32) * hh).astype(BF16)

            m_last = jnp.maximum(m_prev, jnp.max(a_r, axis=1, keepdims=True))
            b_last = jnp.sum(lf_r, axis=1, keepdims=True)
            s_r = jnp.exp(a_r - m_last)
            sv_t = jnp.concatenate([v.T.astype(F32) * s_r, jnp.broadcast_to(s_r, (DV, L))], axis=0)
            c_s[bi, h] = jnp.exp(m_prev - m_last) * c_prev + _dot(sv_t.astype(BF16), k)
            m_s[bi, h:h + 1, :] = jnp.broadcast_to(b_last + m_last, (1, m_s.shape[2]))

    for bi in range(nb):
        out = (_dot(yconv_ref[bi], wout_ref[0:CONV_DIM, :])
               + _dot(y_s[bi * L:(bi + 1) * L, :], wout_ref[CONV_DIM:CONV_DIM + MLSTM_DIM, :]))
        hp_ref[bi] = x_ref[bi] + out

    @pl.when(c == pl.num_programs(1) - 1)
    def _():
        lane = lax.broadcasted_iota(jnp.int32, (1, m_s.shape[2]), 1)
        for bi in range(nb):
            acc = jnp.zeros((1, m_s.shape[2]), F32)
            for h in range(HEADS):
                pc_ref[bi, h] = c_s[bi, h, 0:DV, :]
                pn_ref[bi, h:h + 1, :] = c_s[bi, h, DV:DV + 1, :]
                acc = jnp.where(lane == h, m_s[bi, h:h + 1, :], acc)
            pm_ref[bi] = acc


def _p2_call(q, k, v, so, yconv, gates, x, wout, gmh, n_batch, seq_len):
    L = MLSTM_CHUNK
    nb = P2_SEQS
    nc = seq_len // L
    seq = lambda width: pl.BlockSpec((nb, L, width), lambda b, c: (b, c, 0))
    as_seq = lambda a: a.reshape(n_batch, seq_len, a.shape[-1])
    return pl.pallas_call(
        _p2_kernel,
        grid=(n_batch // nb, nc),
        in_specs=[seq(MLSTM_DIM), seq(MLSTM_DIM), seq(MLSTM_DIM), seq(MLSTM_DIM), seq(CONV_DIM),
                  pl.BlockSpec((nb, 4 * HEADS, L), lambda b, c: (b, 0, c)), seq(D_MODEL),
                  _const_spec((D_MODEL, D_MODEL)), _const_spec((1, MLSTM_DIM))],
        out_specs=[seq(D_MODEL),
                   pl.BlockSpec((nb, HEADS, DV, DQK), lambda b, c: (b, 0, 0, 0)),
                   pl.BlockSpec((nb, HEADS, DQK), lambda b, c: (b, 0, 0)),
                   pl.BlockSpec((nb, 1, 128), lambda b, c: (b, 0, 0))],
        out_shape=[jax.ShapeDtypeStruct((n_batch, seq_len, D_MODEL), F32),
                   jax.ShapeDtypeStruct((n_batch, HEADS, DV, DQK), F32),
                   jax.ShapeDtypeStruct((n_batch, HEADS, DQK), F32),
                   jax.ShapeDtypeStruct((n_batch, 1, 128), F32)],
        scratch_shapes=[pltpu.VMEM((nb, HEADS, 2 * DV, DQK), F32), pltpu.VMEM((nb, 8, 128), F32),
                        pltpu.VMEM((nb * L, MLSTM_DIM), BF16)],
        compiler_params=_params(("arbitrary", "arbitrary")),
        name="p2_mlstm_outproj",
    )(as_seq(q), as_seq(k), as_seq(v), as_seq(so), as_seq(yconv), gates, as_seq(x), wout, gmh)


def _pm_kernel(mem_ref, g_ref, w_ref, k_ref, v_ref, kb_ref, vb_ref):
    xn = _rmsnorm(mem_ref[...], g_ref[...]).astype(BF16)
    kk = _dot(xn, w_ref[:, 0:D_MODEL])
    vv = _dot(xn, w_ref[:, D_MODEL:2 * D_MODEL])
    for h in range(X_HEADS):
        sl = slice(h * X_HEAD_DIM, (h + 1) * X_HEAD_DIM)
        k_ref[:, h, :] = kk[:, sl]
        v_ref[:, h, :] = vv[:, sl]
    kb_ref[...] = kk.astype(BF16)
    vb_ref[...] = vv.astype(BF16)


def _pm_call(mem, g, w):
    rows = mem.shape[0]
    tm = ROW_TILE
    row = pl.BlockSpec((tm, D_MODEL), lambda i: (i, 0))
    row4 = pl.BlockSpec((tm, X_HEADS, X_HEAD_DIM), lambda i: (i, 0, 0))
    return pl.pallas_call(
        _pm_kernel,
        grid=(rows // tm,),
        in_specs=[row, _const_spec((1, D_MODEL)), _const_spec((D_MODEL, 2 * D_MODEL))],
        out_specs=[row4, row4, row, row],
        out_shape=[jax.ShapeDtypeStruct((rows, X_HEADS, X_HEAD_DIM), F32)] * 2
        + [jax.ShapeDtypeStruct((rows, D_MODEL), BF16)] * 2,
        compiler_params=_params(("arbitrary",)),
        name="pm_mem_kv",
    )(mem, g, w)


def _p3_kernel(hp_ref, g_ref, wq_ref, k_ref, v_ref, o_ref):
    xn = _rmsnorm(hp_ref[...], g_ref[...]).astype(BF16)
    q = _dot(xn, wq_ref[...])
    for h in range(X_HEADS):
        sl = slice(h * X_HEAD_DIM, (h + 1) * X_HEAD_DIM)
        s = _dot_nt(q[:, sl].astype(BF16), k_ref[0, :, sl]) * (X_HEAD_DIM ** -0.5)
        e = jnp.exp(s - jnp.max(s, axis=1, keepdims=True))
        p = e * (1.0 / jnp.sum(e, axis=1, keepdims=True))
        o_ref[:, sl] = _dot(p.astype(BF16), v_ref[0, :, sl]).astype(BF16)


def _p3_call(hp, g, wq, kb, vb, seq_len):
    rows = hp.shape[0]
    tm = ROW_TILE
    tiles_per_batch = seq_len // tm
    row = pl.BlockSpec((tm, D_MODEL), lambda i: (i, 0))
    mem = pl.BlockSpec((1, N_MEM, D_MODEL), lambda i: (i // tiles_per_batch, 0, 0))
    return pl.pallas_call(
        _p3_kernel,
        grid=(rows // tm,),
        in_specs=[row, _const_spec((1, D_MODEL)), _const_spec((D_MODEL, D_MODEL)), mem, mem],
        out_specs=row,
        out_shape=jax.ShapeDtypeStruct((rows, D_MODEL), BF16),
        compiler_params=_params(("arbitrary",)),
        name="p3_cross_attn",
    )(hp, g, wq, kb, vb)


def _cache_attention_row(q4, kc, vc):
    qs = q4 * (X_HEAD_DIM ** -0.5)
    s = jnp.sum(kc * qs, axis=-1, keepdims=True)
    e = jnp.exp(s - jnp.max(s, axis=0, keepdims=True))
    return jnp.sum(e * vc, axis=0) * (1.0 / jnp.sum(e, axis=0))


def _p4_kernel(side_rows, hp_ref, o_ref, wxo_ref, gf_ref, wgu_ref, wd_ref, gfin_ref, *rest):
    if side_rows:
        q4_ref, kc_ref, vc_ref, y_ref, os_ref, act_s = rest
        for bl in range(side_rows):
            os_ref[bl] = _cache_attention_row(q4_ref[bl], kc_ref[bl], vc_ref[bl])
    else:
        y_ref, act_s = rest
    hp = hp_ref[...] + _dot(o_ref[...].astype(BF16), wxo_ref[...])
    xn = _rmsnorm(hp, gf_ref[...]).astype(BF16)
    for j in range(D_FF // FF_CHUNK):
        g = _dot(xn, wgu_ref[:, FF_CHUNK * j:FF_CHUNK * (j + 1)])
        u = _dot(xn, wgu_ref[:, D_FF + FF_CHUNK * j:D_FF + FF_CHUNK * (j + 1)])
        act_s[:, FF_CHUNK * j:FF_CHUNK * (j + 1)] = (g * jax.nn.sigmoid(g) * u).astype(BF16)
    hp = hp + _dot(act_s[...], wd_ref[...])
    y_ref[...] = _rmsnorm(hp, gfin_ref[...])


def _p4_call(hp, o, wxo, gf, wgu, wd, gfin, tm, side=None):
    rows = hp.shape[0]
    steps = rows // tm
    row = pl.BlockSpec((tm, D_MODEL), lambda i: (i, 0))
    in_specs = [row, row, _const_spec((D_MODEL, D_MODEL)), _const_spec((1, D_MODEL)),
                _const_spec((D_MODEL, 2 * D_FF)), _const_spec((D_FF, D_MODEL)),
                _const_spec((1, D_MODEL))]
    out_specs = [row]
    out_shape = [jax.ShapeDtypeStruct((rows, D_MODEL), F32)]
    args = [hp, o, wxo, gf, wgu, wd, gfin]
    side_rows = 0
    if side is not None:
        q4, kc, vc = side
        side_rows = q4.shape[0] // steps
        assert side_rows * steps == q4.shape[0]
        srow = pl.BlockSpec((side_rows, X_HEADS, X_HEAD_DIM), lambda i: (i, 0, 0))
        cache = pl.BlockSpec((side_rows, N_MEM, X_HEADS, X_HEAD_DIM), lambda i: (i, 0, 0, 0))
        in_specs += [srow, cache, cache]
        out_specs += [srow]
        out_shape += [jax.ShapeDtypeStruct(q4.shape, F32)]
        args += [q4, kc, vc]
    return pl.pallas_call(
        functools.partial(_p4_kernel, side_rows),
        grid=(steps,),
        in_specs=in_specs,
        out_specs=out_specs,
        out_shape=out_shape,
        scratch_shapes=[pltpu.VMEM((tm, D_FF), BF16)],
        compiler_params=_params(("arbitrary",)),
        name="p4_ffn_final",
    )(*args)


def _s1_kernel(x_ref, g_ref, w_ref, b_ref, wg_ref, bg_ref, cw_ref, st_ref, wvt_ref, bvt_ref,
               wgt_ref, bgt_ref,
               yconv_ref, sconv_ref, q_ref, k_ref, v_ref, so_ref, gate_ref, vt_ref, gatet_ref):
    xn = _rmsnorm(x_ref[...], g_ref[...]).astype(BF16)

    def seg(j):
        sl = slice(j * CONV_DIM, (j + 1) * CONV_DIM)
        return _dot(xn, w_ref[:, sl]) + b_ref[:, sl]

    u = seg(1) * seg(2)
    st0 = st_ref[:, 0:CONV_DIM]
    st1 = st_ref[:, CONV_DIM:2 * CONV_DIM]
    conv = cw_ref[0:1, :] * st0 + cw_ref[1:2, :] * st1 + cw_ref[2:3, :] * u
    yconv_ref[...] = seg(0) * conv
    sconv_ref[:, 0:CONV_DIM] = st1
    sconv_ref[:, CONV_DIM:2 * CONV_DIM] = u
    q_ref[...] = seg(3)
    k_ref[...] = seg(4) * (DQK ** -0.5)
    v_ref[...] = seg(5)
    so_ref[...] = jax.nn.sigmoid(seg(6))
    gate_ref[...] = _gate_transform(_dot(xn, wg_ref[...]) + bg_ref[...])
    vt_ref[...] = _dot_nt(wvt_ref[...], xn) + bvt_ref[...]
    gt_t = _dot_nt(wgt_ref[...], xn) + bgt_ref[...]
    sub = lax.broadcasted_iota(jnp.int32, gt_t.shape, 0)
    gatet_ref[...] = jnp.where(sub < HEADS, gt_t, jax.nn.log_sigmoid(gt_t))


def _s1_call(x, g, w, b, wg, bg, cw, st, wvt, bvt, wgt, bgt):
    n = x.shape[0]
    full = lambda *shape: pl.BlockSpec(shape, lambda i: (0,) * len(shape))
    ins = [x, g, w, b, wg, bg, cw, st, wvt, bvt, wgt, bgt]
    return pl.pallas_call(
        _s1_kernel,
        grid=(1,),
        in_specs=[full(*a.shape) for a in ins],
        out_specs=[full(n, CONV_DIM), full(n, 2 * CONV_DIM), full(n, MLSTM_DIM), full(n, MLSTM_DIM),
                   full(n, MLSTM_DIM), full(n, MLSTM_DIM), full(n, GATE_PAD), full(MLSTM_DIM, n),
                   full(GATE_PAD, n)],
        out_shape=[jax.ShapeDtypeStruct((n, CONV_DIM), F32),
                   jax.ShapeDtypeStruct((n, 2 * CONV_DIM), F32)]
        + [jax.ShapeDtypeStruct((n, MLSTM_DIM), F32)] * 4
        + [jax.ShapeDtypeStruct((n, GATE_PAD), F32),
           jax.ShapeDtypeStruct((MLSTM_DIM, n), F32),
           jax.ShapeDtypeStruct((GATE_PAD, n), F32)],
        compiler_params=_params(("arbitrary",)),
        name="s1_inproj_conv",
    )(*ins)


def _pick_row(tile, r):
    sub = lax.broadcasted_iota(jnp.int32, tile.shape, 0)
    return jnp.sum(jnp.where(sub == r, tile, 0.0), axis=0, keepdims=True)


def _s2_kernel(c_ref, q_ref, k_ref, vt_ref, gate_ref, gatet_ref, m_ref, mt_ref,
               cn_ref, cqt_ref, dec_s, svt_s):
    n = vt_ref.shape[1]
    bb = c_ref.shape[0]
    i = pl.program_id(0)

    @pl.when(i == 0)
    def _():
        cqt_ref[...] = jnp.zeros_like(cqt_ref)

    lane = lax.broadcasted_iota(jnp.int32, (DV, n), 1)
    for h in range(HEADS):
        sl = slice(h * DQK, (h + 1) * DQK)
        ig_c = gate_ref[:, h:h + 1]
        lf_c = gate_ref[:, HEADS + h:HEADS + h + 1]
        m_c = m_ref[:, h:h + 1]
        m_new_c = jnp.maximum(lf_c + m_c, ig_c)
        dec_s[...] = jnp.broadcast_to(jnp.exp(lf_c + m_c - m_new_c), (bb, DQK))
        ig_r = gatet_ref[h:h + 1, :]
        lf_r = gatet_ref[HEADS + h:HEADS + h + 1, :]
        m_r = mt_ref[h:h + 1, :]
        s_r = jnp.exp(ig_r - jnp.maximum(lf_r + m_r, ig_r))
        svt_s[...] = vt_ref[sl, :] * s_r

        def body(bl, cqt):
            onehot = lane == i * bb + bl
            c = c_ref[bl, h]
            q_row = _pick_row(q_ref[:, sl], bl)
            k_row = _pick_row(k_ref[:, sl], bl)
            dec_row = _pick_row(dec_s[...], bl)
            cq_col = jnp.sum(c * q_row, axis=1, keepdims=True)
            sv_col = jnp.sum(jnp.where(onehot, svt_s[...], 0.0), axis=1, keepdims=True)
            cn_ref[bl, h] = dec_row * c + sv_col * k_row
            return jnp.where(onehot, cq_col, cqt)

        cqt_ref[sl, :] = lax.fori_loop(0, bb, body, cqt_ref[sl, :], unroll=True)


def _s2_call(c, q, k, vt, gates, gatest, m, mt):
    n = q.shape[0]
    bb = S2_ROWS
    full = lambda *shape: pl.BlockSpec(shape, lambda i: (0,) * len(shape))
    rows = lambda width: pl.BlockSpec((bb, width), lambda i: (i, 0))
    cblk = pl.BlockSpec((bb, HEADS, DV, DQK), lambda i: (i, 0, 0, 0))
    return pl.pallas_call(
        _s2_kernel,
        grid=(n // bb,),
        in_specs=[cblk, rows(MLSTM_DIM), rows(MLSTM_DIM), full(MLSTM_DIM, n),
                  rows(GATE_PAD), full(GATE_PAD, n), rows(HEADS), full(HEADS, n)],
        out_specs=[cblk, full(MLSTM_DIM, n)],
        out_shape=[jax.ShapeDtypeStruct(c.shape, F32), jax.ShapeDtypeStruct((MLSTM_DIM, n), F32)],
        scratch_shapes=[pltpu.VMEM((bb, DQK), F32), pltpu.VMEM((DV, n), F32)],
        compiler_params=_params(("arbitrary",)),
        name="s2_memory_update",
    )(c, q, k, vt, gates, gatest, m, mt)


def _s3_kernel(cqt_ref, q_ref, k_ref, v_ref, so_ref, gate_ref, n_ref, m_ref, yconv_ref, x_ref,
               wout_ref, gmh_ref, gx_ref, wq_ref,
               hs_ref, qx_ref, nn_ref, mn_ref, y_s):
    n_rows = q_ref.shape[0]
    y_s[:, 0:CONV_DIM] = yconv_ref[...].astype(BF16)
    lane = lax.broadcasted_iota(jnp.int32, (n_rows, mn_ref.shape[1]), 1)
    m_out = jnp.zeros((n_rows, mn_ref.shape[1]), F32)
    for h in range(HEADS):
        sl = slice(h * DQK, (h + 1) * DQK)
        q = q_ref[:, sl]
        k = k_ref[:, sl]
        v = v_ref[:, sl]
        n_prev = n_ref[:, sl]
        cq = cqt_ref[sl, :].T
        ig = gate_ref[:, h:h + 1]
        lf = gate_ref[:, HEADS + h:HEADS + h + 1]
        m_prev = m_ref[:, h:h + 1]
        inter = lf + m_prev
        m_row = jnp.maximum(inter, ig)
        wgt = jnp.sum(q * k, axis=1, keepdims=True) * jnp.exp(ig - m_row)
        g = jnp.exp(inter - m_row)
        num = g * cq + wgt * v
        den = g * jnp.sum(n_prev * q, axis=1, keepdims=True) + wgt
        hh = num / jnp.maximum(jnp.abs(den), jnp.exp(-m_row))
        hh = hh * lax.rsqrt(jnp.mean(hh * hh, axis=1, keepdims=True) + EPS) * gmh_ref[:, sl]
        y_s[:, CONV_DIM + h * DV:CONV_DIM + (h + 1) * DV] = (so_ref[:, sl] * hh).astype(BF16)
        nn_ref[:, sl] = g * n_prev + jnp.exp(ig - m_row) * k
        m_out = jnp.where(lane == h, m_row, m_out)
    mn_ref[...] = m_out
    hs = x_ref[...] + _dot(y_s[...], wout_ref[...])
    hs_ref[...] = hs
    qx_ref[...] = _dot(_rmsnorm(hs, gx_ref[...]).astype(BF16), wq_ref[...])


def _s3_call(cqt, q, k, v, so, gates, nst, m, yconv, x, wout, gmh, gx, wq):
    n = q.shape[0]
    full = lambda *shape: pl.BlockSpec(shape, lambda i: (0,) * len(shape))
    ins = [cqt, q, k, v, so, gates, nst, m, yconv, x, wout, gmh, gx, wq]
    return pl.pallas_call(
        _s3_kernel,
        grid=(1,),
        in_specs=[full(*a.shape) for a in ins],
        out_specs=[full(n, D_MODEL), full(n, D_MODEL), full(n, MLSTM_DIM), full(n, 128)],
        out_shape=[jax.ShapeDtypeStruct((n, D_MODEL), F32), jax.ShapeDtypeStruct((n, D_MODEL), F32),
                   jax.ShapeDtypeStruct((n, MLSTM_DIM), F32), jax.ShapeDtypeStruct((n, 128), F32)],
        scratch_shapes=[pltpu.VMEM((n, D_MODEL), BF16)],
        compiler_params=_params(("arbitrary",)),
        name="s3_mlstm_finish",
    )(*ins)


def kernel(x_prompt, x_sample, mem_prompt, state_conv, state_mlstm_C, state_mlstm_n, state_mlstm_m,
           cache_mem_k, cache_mem_v, g_mix, w_in, b_in, conv_w, g_mh, w_out, g_cross, g_mem,
           w_xq, w_xkv, w_xo, g_ffn, w_gu, w_down, g_final):
    n_batch, seq_len, _ = x_prompt.shape
    n_dec = x_sample.shape[0]
    depth = w_in.shape[0]
    assert depth == 1 and x_sample.shape[1] == 1
    assert seq_len % ROW_TILE == 0 and ROW_TILE % MLSTM_CHUNK == 0 and n_batch % P2_SEQS == 0

    w_main = w_in[0, :, :MAIN_DIM].astype(BF16)
    b_main = b_in[0, :MAIN_DIM].reshape(1, MAIN_DIM)
    n_gate = 2 * HEADS
    w_gate = jnp.pad(w_in[0, :, MAIN_DIM:], ((0, 0), (0, GATE_PAD - n_gate))).astype(BF16)
    b_gate = jnp.pad(b_in[0, MAIN_DIM:], (0, GATE_PAD - n_gate)).reshape(1, GATE_PAD)
    v_cols = slice(3 * CONV_DIM + 2 * MLSTM_DIM, 3 * CONV_DIM + 3 * MLSTM_DIM)
    w_vt = w_in[0, :, v_cols].T.astype(BF16)
    b_vt = b_in[0, v_cols].reshape(MLSTM_DIM, 1)
    w_gate_t = w_gate.T
    b_gate_t = b_gate.reshape(GATE_PAD, 1)
    w_gate_r = w_gate_t[:n_gate]
    b_gate_r = b_gate_t[:n_gate]
    w_out_b = w_out[0].astype(BF16)
    w_xq_b = w_xq[0].astype(BF16)
    w_xkv_b = w_xkv[0].astype(BF16)
    w_xo_b = w_xo[0].astype(BF16)
    w_gu_b = w_gu[0].astype(BF16)
    w_down_b = w_down[0].astype(BF16)
    g_mix_r = g_mix[0].reshape(1, D_MODEL)
    g_cross_r = g_cross[0].reshape(1, D_MODEL)
    g_mem_r = g_mem[0].reshape(1, D_MODEL)
    g_ffn_r = g_ffn[0].reshape(1, D_MODEL)
    g_final_r = g_final.reshape(1, D_MODEL)
    g_mh_r = g_mh[0].reshape(1, MLSTM_DIM)
    cw = conv_w[0]

    xs = x_sample.reshape(n_dec, D_MODEL)
    st = state_conv[0].reshape(n_dec, (CONV_W - 1) * CONV_DIM)
    m0 = state_mlstm_m[0]
    s_yconv, s_conv, sq, sk, sv, sso, sgates, svt, sgates_t = _s1_call(
        xs, g_mix_r, w_main, b_main, w_gate, b_gate, cw, st, w_vt, b_vt, w_gate_t, b_gate_t)
    s_c, cqt = _s2_call(state_mlstm_C[0], sq, sk, svt, sgates, sgates_t, m0, m0.T)
    hs1, qx, s_n, s_m = _s3_call(cqt, sq, sk, sv, sso, sgates,
                                 state_mlstm_n[0].reshape(n_dec, MLSTM_DIM), m0, s_yconv, xs,
                                 w_out_b, g_mh_r, g_cross_r, w_xq_b)

    xp = x_prompt.reshape(n_batch * seq_len, D_MODEL)
    yconv, q, k, v, so, gates, p_conv = _p1_call(xp, g_mix_r, w_main, b_main, w_gate_r, b_gate_r, cw, seq_len)
    hp1, p_c, p_n, p_m = _p2_call(q, k, v, so, yconv, gates, xp, w_out_b, g_mh_r, n_batch, seq_len)
    hp1 = hp1.reshape(n_batch * seq_len, D_MODEL)
    pk, pv, pkb, pvb = _pm_call(mem_prompt.reshape(n_batch * N_MEM, D_MODEL), g_mem_r, w_xkv_b)
    o_p = _p3_call(hp1, g_cross_r, w_xq_b, pkb.reshape(n_batch, N_MEM, D_MODEL),
                   pvb.reshape(n_batch, N_MEM, D_MODEL), seq_len)
    y_p, o_s = _p4_call(hp1, o_p, w_xo_b, g_ffn_r, w_gu_b, w_down_b, g_final_r, ROW_TILE,
                        side=(qx.reshape(n_dec, X_HEADS, X_HEAD_DIM), cache_mem_k[0], cache_mem_v[0]))

    y_s, = _p4_call(hs1, o_s.reshape(n_dec, D_MODEL), w_xo_b, g_ffn_r, w_gu_b, w_down_b, g_final_r, n_dec)

    mem_shape = (1, n_batch, N_MEM, X_HEADS, X_HEAD_DIM)
    return (y_p.reshape(n_batch, seq_len, D_MODEL),
            y_s.reshape(n_dec, 1, D_MODEL),
            p_conv.reshape(1, n_batch, CONV_W - 1, CONV_DIM),
            p_c.reshape(1, n_batch, HEADS, DV, DQK),
            p_n.reshape(1, n_batch, HEADS, DQK),
            p_m[:, 0, :HEADS].reshape(1, n_batch, HEADS),
            pk.reshape(mem_shape),
            pv.reshape(mem_shape),
            s_conv.reshape(1, n_dec, CONV_W - 1, CONV_DIM),
            s_c.reshape(1, n_dec, HEADS, DV, DQK),
            s_n.reshape(1, n_dec, HEADS, DQK),
            s_m[:, :HEADS].reshape(1, n_dec, HEADS))
```

```python
import functools

import jax
import jax.numpy as jnp
from jax import lax
from jax.experimental import pallas as pl
from jax.experimental.pallas import tpu as pltpu

F32 = jnp.float32
BF16 = jnp.bfloat16

D_MODEL = 1024
CONV_DIM = 512
CONV_W = 3
MLSTM_DIM = 512
HEADS = 4
DQK = 128
DV = 128
N_MEM = 256
X_HEADS = 4
X_HEAD_DIM = 256
D_FF = 2816
MAIN_DIM = 3 * CONV_DIM + 4 * MLSTM_DIM
GATE_PAD = 128
EPS = 1e-6

MLSTM_CHUNK = 256
P2_SEQS = 2
ROW_TILE = 512
P1_TILE = 1024
P3_TILE = 1024
FF_CHUNK = 256
S2_ROWS = 16
MEM_CHUNK = 32
VMEM_LIMIT = 56 * 1024 * 1024


def _dot(a, b):
    return jnp.dot(a, b, preferred_element_type=F32)


def _dot_nt(a, b):
    return lax.dot_general(a, b, (((1,), (1,)), ((), ())), preferred_element_type=F32)


def _dot_tn(a, b):
    return lax.dot_general(a, b, (((0,), (0,)), ((), ())), preferred_element_type=F32)


def _rmsnorm(x, g):
    return x * lax.rsqrt(jnp.mean(x * x, axis=-1, keepdims=True) + EPS) * g


def _const_spec(shape):
    zeros = (0,) * len(shape)
    return pl.BlockSpec(shape, lambda *_: zeros, pipeline_mode=pl.Buffered(1))


def _params(sem, flags=None):
    return pltpu.CompilerParams(dimension_semantics=sem, vmem_limit_bytes=VMEM_LIMIT, flags=flags)


def _gate_transform(gt):
    lane = lax.broadcasted_iota(jnp.int32, gt.shape, 1)
    return jnp.where(lane < HEADS, gt, jax.nn.log_sigmoid(gt))


def _gate_transform_rows(gt):
    sub = lax.broadcasted_iota(jnp.int32, gt.shape, 0)
    return jnp.where(sub < HEADS, gt, jax.nn.log_sigmoid(gt))


def _p1_kernel(tiles_per_batch, x_ref, g_ref, w_ref, b_ref, wg_ref, bg_ref, cw_ref,
               yconv_ref, q_ref, k_ref, v_ref, so_ref, gate_ref, pconv_ref, ubuf):
    tm = x_ref.shape[0]
    i = pl.program_id(0)
    xn = _rmsnorm(x_ref[...], g_ref[...]).astype(BF16)

    def seg(j):
        sl = slice(j * CONV_DIM, (j + 1) * CONV_DIM)
        return _dot(xn, w_ref[:, sl]) + b_ref[:, sl]

    prev = ubuf[tm:tm + 8, :]
    ubuf[0:8, :] = jnp.where(i % tiles_per_batch == 0, jnp.zeros_like(prev), prev)
    ubuf[8:8 + tm, :] = seg(1) * seg(2)
    conv = (cw_ref[0:1, :] * ubuf[6:6 + tm, :] + cw_ref[1:2, :] * ubuf[7:7 + tm, :]
            + cw_ref[2:3, :] * ubuf[8:8 + tm, :])
    yconv_ref[...] = (seg(0) * conv).astype(BF16)
    pconv_ref[0] = ubuf[tm + 6:tm + 8, :]

    q_ref[...] = seg(3).astype(BF16)
    k_ref[...] = (seg(4) * (DQK ** -0.5)).astype(BF16)
    v_ref[...] = seg(5).astype(BF16)
    so_ref[...] = jax.nn.sigmoid(seg(6)).astype(BF16)
    gt = _gate_transform_rows(_dot_nt(wg_ref[...], xn) + bg_ref[...])
    n_gate = gt.shape[0]
    gate_ref[0, 0:n_gate, :] = gt
    L = MLSTM_CHUNK
    n_blk = tm // L
    hi = gt.astype(BF16).astype(F32)
    r1 = gt - hi
    mid = r1.astype(BF16).astype(F32)
    lo = r1 - mid
    terms = jnp.concatenate([t[:, j * L:(j + 1) * L] for t in (hi, mid, lo) for j in range(n_blk)], axis=0)
    tri = (lax.broadcasted_iota(jnp.int32, (L, L), 0) <= lax.broadcasted_iota(jnp.int32, (L, L), 1)).astype(BF16)
    parts = _dot(terms.astype(BF16), tri)
    for j in range(n_blk):
        rows = [parts[(t * n_blk + j) * n_gate:(t * n_blk + j + 1) * n_gate, :] for t in range(3)]
        gate_ref[0, n_gate:2 * n_gate, j * L:(j + 1) * L] = (rows[0] + rows[1]) + rows[2]


def _p1_call(x, g, w, b, wg, bg, cw, seq_len):
    rows = x.shape[0]
    tm = P1_TILE
    tiles_per_batch = seq_len // tm
    n_batch = rows // seq_len
    row = lambda width: pl.BlockSpec((tm, width), lambda i: (i, 0))
    return pl.pallas_call(
        functools.partial(_p1_kernel, tiles_per_batch),
        grid=(rows // tm,),
        in_specs=[row(D_MODEL), _const_spec((1, D_MODEL)), _const_spec((D_MODEL, MAIN_DIM)),
                  _const_spec((1, MAIN_DIM)), _const_spec((2 * HEADS, D_MODEL)),
                  _const_spec((2 * HEADS, 1)), _const_spec((CONV_W, CONV_DIM))],
        out_specs=[row(CONV_DIM), row(MLSTM_DIM), row(MLSTM_DIM), row(MLSTM_DIM), row(MLSTM_DIM),
                   pl.BlockSpec((1, 4 * HEADS, tm), lambda i: (i // tiles_per_batch, 0, i % tiles_per_batch)),
                   pl.BlockSpec((1, CONV_W - 1, CONV_DIM), lambda i: (i // tiles_per_batch, 0, 0))],
        out_shape=[jax.ShapeDtypeStruct((rows, CONV_DIM), BF16)]
        + [jax.ShapeDtypeStruct((rows, MLSTM_DIM), BF16)] * 4
        + [jax.ShapeDtypeStruct((n_batch, 4 * HEADS, seq_len), F32),
           jax.ShapeDtypeStruct((n_batch, CONV_W - 1, CONV_DIM), F32)],
        scratch_shapes=[pltpu.VMEM((tm + 8, CONV_DIM), F32)],
        compiler_params=_params(("arbitrary",)),
        name="p1_inproj_conv",
    )(x, g, w, b, wg, bg, cw)


def _p2_kernel(q_ref, k_ref, v_ref, so_ref, yconv_ref, gate_ref, x_ref, wout_ref, gmh_ref,
               hp_ref, pc_ref, pn_ref, pm_ref, c_s, m_s, y_s):
    nb, L = q_ref.shape[0], q_ref.shape[1]
    c = pl.program_id(1)

    @pl.when(c == 0)
    def _():
        c_s[...] = jnp.zeros_like(c_s)
        m_s[...] = jnp.zeros_like(m_s)

    row = lax.broadcasted_iota(jnp.int32, (L, L), 0)
    col = lax.broadcasted_iota(jnp.int32, (L, L), 1)
    causal = row >= col

    for bi in range(nb):
        gt = gate_ref[bi]
        for h in range(HEADS):
            sl = slice(h * DQK, (h + 1) * DQK)
            q = q_ref[bi, :, sl]
            k = k_ref[bi, :, sl]
            v = v_ref[bi, :, sl]
            lf_r = gt[HEADS + h:HEADS + h + 1, :]
            a_r = gt[h:h + 1, :] - gt[3 * HEADS + h:3 * HEADS + h + 1, :]
            m_prev = jnp.max(m_s[bi, h:h + 1, :], axis=1, keepdims=True)
            c_prev = c_s[bi, h]

            m_c = jnp.maximum(m_prev, jnp.max(jnp.where(causal, a_r, -jnp.inf), axis=1, keepdims=True))
            b_c = jnp.sum(jnp.where(causal, lf_r, 0.0), axis=1, keepdims=True)
            w = _dot_nt(q, k) * jnp.exp(jnp.where(causal, a_r - m_c, -jnp.inf))
            g = jnp.exp(m_prev - m_c)
            qc = _dot_nt(q, c_prev.astype(BF16))
            num = g * qc[:, 0:DV] + _dot(w.astype(BF16), v)
            den = g * qc[:, DV:2 * DV] + jnp.sum(w, axis=1, keepdims=True)
            hh = num / jnp.maximum(jnp.abs(den), jnp.exp(-(b_c + m_c)))
            hh = hh * lax.rsqrt(jnp.mean(hh * hh, axis=1, keepdims=True) + EPS) * gmh_ref[:, sl]
            y_s[bi * L:(bi + 1) * L, h * DV:(h + 1) * DV] = (so_ref[bi, :, sl].astype(F32) * hh).astype(BF16)

            m_last = jnp.maximum(m_prev, jnp.max(a_r, axis=1, keepdims=True))
            b_last = jnp.sum(lf_r, axis=1, keepdims=True)
            s_r = jnp.exp(a_r - m_last)
            sv_t = jnp.concatenate([v.T.astype(F32) * s_r, jnp.broadcast_to(s_r, (DV, L))], axis=0)
            c_s[bi, h] = jnp.exp(m_prev - m_last) * c_prev + _dot(sv_t.astype(BF16), k)
            m_s[bi, h:h + 1, :] = jnp.broadcast_to(b_last + m_last, (1, m_s.shape[2]))

    for bi in range(nb):
        out = (_dot(yconv_ref[bi], wout_ref[0:CONV_DIM, :])
               + _dot(y_s[bi * L:(bi + 1) * L, :], wout_ref[CONV_DIM:CONV_DIM + MLSTM_DIM, :]))
        hp_ref[bi] = x_ref[bi] + out

    @pl.when(c == pl.num_programs(1) - 1)
    def _():
        lane = lax.broadcasted_iota(jnp.int32, (1, m_s.shape[2]), 1)
        for bi in range(nb):
            acc = jnp.zeros((1, m_s.shape[2]), F32)
            for h in range(HEADS):
                pc_ref[bi, h] = c_s[bi, h, 0:DV, :]
                pn_ref[bi, h:h + 1, :] = c_s[bi, h, DV:DV + 1, :]
                acc = jnp.where(lane == h, m_s[bi, h:h + 1, :], acc)
            pm_ref[bi] = acc


def _p2_call(q, k, v, so, yconv, gates, x, wout, gmh, n_batch, seq_len):
    L = MLSTM_CHUNK
    nb = P2_SEQS
    nc = seq_len // L
    seq = lambda width: pl.BlockSpec((nb, L, width), lambda b, c: (b, c, 0))
    as_seq = lambda a: a.reshape(n_batch, seq_len, a.shape[-1])
    return pl.pallas_call(
        _p2_kernel,
        grid=(n_batch // nb, nc),
        in_specs=[seq(MLSTM_DIM), seq(MLSTM_DIM), seq(MLSTM_DIM), seq(MLSTM_DIM), seq(CONV_DIM),
                  pl.BlockSpec((nb, 4 * HEADS, L), lambda b, c: (b, 0, c)), seq(D_MODEL),
                  _const_spec((D_MODEL, D_MODEL)), _const_spec((1, MLSTM_DIM))],
        out_specs=[seq(D_MODEL),
                   pl.BlockSpec((nb, HEADS, DV, DQK), lambda b, c: (b, 0, 0, 0)),
                   pl.BlockSpec((nb, HEADS, DQK), lambda b, c: (b, 0, 0)),
                   pl.BlockSpec((nb, 1, 128), lambda b, c: (b, 0, 0))],
        out_shape=[jax.ShapeDtypeStruct((n_batch, seq_len, D_MODEL), F32),
                   jax.ShapeDtypeStruct((n_batch, HEADS, DV, DQK), F32),
                   jax.ShapeDtypeStruct((n_batch, HEADS, DQK), F32),
                   jax.ShapeDtypeStruct((n_batch, 1, 128), F32)],
        scratch_shapes=[pltpu.VMEM((nb, HEADS, 2 * DV, DQK), F32), pltpu.VMEM((nb, 8, 128), F32),
                        pltpu.VMEM((nb * L, MLSTM_DIM), BF16)],
        compiler_params=_params(("arbitrary", "arbitrary")),
        name="p2_mlstm_outproj",
    )(as_seq(q), as_seq(k), as_seq(v), as_seq(so), as_seq(yconv), gates, as_seq(x), wout, gmh)


def _pm_kernel(mem_ref, g_ref, w_ref, k_ref, v_ref, kb_ref, vb_ref):
    xn = _rmsnorm(mem_ref[...], g_ref[...]).astype(BF16)
    kk = _dot(xn, w_ref[:, 0:D_MODEL])
    vv = _dot(xn, w_ref[:, D_MODEL:2 * D_MODEL])
    for h in range(X_HEADS):
        sl = slice(h * X_HEAD_DIM, (h + 1) * X_HEAD_DIM)
        k_ref[:, h, :] = kk[:, sl]
        v_ref[:, h, :] = vv[:, sl]
    kb_ref[...] = kk.astype(BF16)
    vb_ref[...] = vv.astype(BF16)


def _pm_call(mem, g, w):
    rows = mem.shape[0]
    tm = ROW_TILE
    row = pl.BlockSpec((tm, D_MODEL), lambda i: (i, 0))
    row4 = pl.BlockSpec((tm, X_HEADS, X_HEAD_DIM), lambda i: (i, 0, 0))
    return pl.pallas_call(
        _pm_kernel,
        grid=(rows // tm,),
        in_specs=[row, _const_spec((1, D_MODEL)), _const_spec((D_MODEL, 2 * D_MODEL))],
        out_specs=[row4, row4, row, row],
        out_shape=[jax.ShapeDtypeStruct((rows, X_HEADS, X_HEAD_DIM), F32)] * 2
        + [jax.ShapeDtypeStruct((rows, D_MODEL), BF16)] * 2,
        compiler_params=_params(("arbitrary",)),
        name="pm_mem_kv",
    )(mem, g, w)


def _p3_kernel(hp_ref, g_ref, wq_ref, k_ref, v_ref, o_ref):
    xn = _rmsnorm(hp_ref[...], g_ref[...]).astype(BF16)
    q = _dot(xn, wq_ref[...])
    for h in range(X_HEADS):
        sl = slice(h * X_HEAD_DIM, (h + 1) * X_HEAD_DIM)
        s = _dot_nt(q[:, sl].astype(BF16), k_ref[0, :, sl]) * (X_HEAD_DIM ** -0.5)
        e = jnp.exp(s - jnp.max(s, axis=1, keepdims=True))
        p = e * (1.0 / jnp.sum(e, axis=1, keepdims=True))
        o_ref[:, sl] = _dot(p.astype(BF16), v_ref[0, :, sl]).astype(BF16)


def _p3_call(hp, g, wq, kb, vb, seq_len):
    rows = hp.shape[0]
    tm = P3_TILE
    tiles_per_batch = seq_len // tm
    row = pl.BlockSpec((tm, D_MODEL), lambda i: (i, 0))
    mem = pl.BlockSpec((1, N_MEM, D_MODEL), lambda i: (i // tiles_per_batch, 0, 0))
    return pl.pallas_call(
        _p3_kernel,
        grid=(rows // tm,),
        in_specs=[row, _const_spec((1, D_MODEL)), _const_spec((D_MODEL, D_MODEL)), mem, mem],
        out_specs=row,
        out_shape=jax.ShapeDtypeStruct((rows, D_MODEL), BF16),
        compiler_params=_params(("arbitrary",)),
        name="p3_cross_attn",
    )(hp, g, wq, kb, vb)


def _cache_attention_row(q4, kc_ref, vc_ref, bl):
    qs = q4 * (X_HEAD_DIM ** -0.5)
    m_run = jnp.full((1, X_HEADS, 1), -jnp.inf, F32)
    l_run = jnp.zeros((1, X_HEADS, 1), F32)
    acc = jnp.zeros((X_HEADS, X_HEAD_DIM), F32)
    for c in range(N_MEM // MEM_CHUNK):
        blk = slice(c * MEM_CHUNK, (c + 1) * MEM_CHUNK)
        s = jnp.sum(kc_ref[bl, blk] * qs, axis=-1, keepdims=True)
        m_new = jnp.maximum(m_run, jnp.max(s, axis=0, keepdims=True))
        alpha = jnp.exp(m_run - m_new)
        e = jnp.exp(s - m_new)
        l_run = alpha * l_run + jnp.sum(e, axis=0, keepdims=True)
        acc = alpha[0] * acc + jnp.sum(e * vc_ref[bl, blk], axis=0)
        m_run = m_new
    return acc * (1.0 / l_run[0])


def _p4_kernel(side_rows, hp_ref, o_ref, wxo_ref, gf_ref, wgu_ref, wd_ref, gfin_ref, *rest):
    if side_rows:
        q4_ref, kc_ref, vc_ref, y_ref, os_ref, act_s = rest
        for bl in range(side_rows):
            os_ref[bl] = _cache_attention_row(q4_ref[bl], kc_ref, vc_ref, bl)
    else:
        y_ref, act_s = rest
    hp = hp_ref[...] + _dot(o_ref[...].astype(BF16), wxo_ref[...])
    xn = _rmsnorm(hp, gf_ref[...]).astype(BF16)
    for j in range(D_FF // FF_CHUNK):
        g = _dot(xn, wgu_ref[:, FF_CHUNK * j:FF_CHUNK * (j + 1)])
        u = _dot(xn, wgu_ref[:, D_FF + FF_CHUNK * j:D_FF + FF_CHUNK * (j + 1)])
        act_s[:, FF_CHUNK * j:FF_CHUNK * (j + 1)] = (g * jax.nn.sigmoid(g) * u).astype(BF16)
    hp = hp + _dot(act_s[...], wd_ref[...])
    y_ref[...] = _rmsnorm(hp, gfin_ref[...])


def _p4_call(hp, o, wxo, gf, wgu, wd, gfin, tm, side=None):
    rows = hp.shape[0]
    steps = rows // tm
    row = pl.BlockSpec((tm, D_MODEL), lambda i: (i, 0))
    in_specs = [row, row, _const_spec((D_MODEL, D_MODEL)), _const_spec((1, D_MODEL)),
                _const_spec((D_MODEL, 2 * D_FF)), _const_spec((D_FF, D_MODEL)),
                _const_spec((1, D_MODEL))]
    out_specs = [row]
    out_shape = [jax.ShapeDtypeStruct((rows, D_MODEL), F32)]
    args = [hp, o, wxo, gf, wgu, wd, gfin]
    side_rows = 0
    if side is not None:
        q4, kc, vc = side
        side_rows = q4.shape[0] // steps
        assert side_rows * steps == q4.shape[0]
        srow = pl.BlockSpec((side_rows, X_HEADS, X_HEAD_DIM), lambda i: (i, 0, 0))
        cache = pl.BlockSpec((side_rows, N_MEM, X_HEADS, X_HEAD_DIM), lambda i: (i, 0, 0, 0))
        in_specs += [srow, cache, cache]
        out_specs += [srow]
        out_shape += [jax.ShapeDtypeStruct(q4.shape, F32)]
        args += [q4, kc, vc]
    return pl.pallas_call(
        functools.partial(_p4_kernel, side_rows),
        grid=(steps,),
        in_specs=in_specs,
        out_specs=out_specs,
        out_shape=out_shape,
        scratch_shapes=[pltpu.VMEM((tm, D_FF), BF16)],
        compiler_params=_params(("arbitrary",)),
        name="p4_ffn_final",
    )(*args)


def _s1_kernel(x_ref, g_ref, w_ref, b_ref, wg_ref, bg_ref, cw_ref, st_ref, wvt_ref, bvt_ref,
               wgt_ref, bgt_ref,
               yconv_ref, sconv_ref, q_ref, k_ref, v_ref, so_ref, gate_ref, vt_ref, gatet_ref):
    xn = _rmsnorm(x_ref[...], g_ref[...]).astype(BF16)

    def seg(j):
        sl = slice(j * CONV_DIM, (j + 1) * CONV_DIM)
        return _dot(xn, w_ref[:, sl]) + b_ref[:, sl]

    u = seg(1) * seg(2)
    st0 = st_ref[:, 0:CONV_DIM]
    st1 = st_ref[:, CONV_DIM:2 * CONV_DIM]
    conv = cw_ref[0:1, :] * st0 + cw_ref[1:2, :] * st1 + cw_ref[2:3, :] * u
    yconv_ref[...] = seg(0) * conv
    sconv_ref[:, 0:CONV_DIM] = st1
    sconv_ref[:, CONV_DIM:2 * CONV_DIM] = u
    q_ref[...] = seg(3)
    k_ref[...] = seg(4) * (DQK ** -0.5)
    v_ref[...] = seg(5)
    so_ref[...] = jax.nn.sigmoid(seg(6))
    gate_ref[...] = _gate_transform(_dot(xn, wg_ref[...]) + bg_ref[...])
    vt_ref[...] = _dot_nt(wvt_ref[...], xn) + bvt_ref[...]
    gt_t = _dot_nt(wgt_ref[...], xn) + bgt_ref[...]
    sub = lax.broadcasted_iota(jnp.int32, gt_t.shape, 0)
    gatet_ref[...] = jnp.where(sub < HEADS, gt_t, jax.nn.log_sigmoid(gt_t))


def _s1_call(x, g, w, b, wg, bg, cw, st, wvt, bvt, wgt, bgt):
    n = x.shape[0]
    full = lambda *shape: pl.BlockSpec(shape, lambda i: (0,) * len(shape))
    ins = [x, g, w, b, wg, bg, cw, st, wvt, bvt, wgt, bgt]
    return pl.pallas_call(
        _s1_kernel,
        grid=(1,),
        in_specs=[full(*a.shape) for a in ins],
        out_specs=[full(n, CONV_DIM), full(n, 2 * CONV_DIM), full(n, MLSTM_DIM), full(n, MLSTM_DIM),
                   full(n, MLSTM_DIM), full(n, MLSTM_DIM), full(n, GATE_PAD), full(MLSTM_DIM, n),
                   full(GATE_PAD, n)],
        out_shape=[jax.ShapeDtypeStruct((n, CONV_DIM), F32),
                   jax.ShapeDtypeStruct((n, 2 * CONV_DIM), F32)]
        + [jax.ShapeDtypeStruct((n, MLSTM_DIM), F32)] * 4
        + [jax.ShapeDtypeStruct((n, GATE_PAD), F32),
           jax.ShapeDtypeStruct((MLSTM_DIM, n), F32),
           jax.ShapeDtypeStruct((GATE_PAD, n), F32)],
        compiler_params=_params(("arbitrary",)),
        name="s1_inproj_conv",
    )(*ins)


def _pick_row(tile, r):
    sub = lax.broadcasted_iota(jnp.int32, tile.shape, 0)
    return jnp.sum(jnp.where(sub == r, tile, 0.0), axis=0, keepdims=True)


def _s2_kernel(c_ref, q_ref, k_ref, vt_ref, gate_ref, gatet_ref, m_ref, mt_ref,
               cn_ref, cqt_ref, dec_s, svt_s):
    n = vt_ref.shape[1]
    bb = c_ref.shape[0]
    i = pl.program_id(0)

    @pl.when(i == 0)
    def _():
        cqt_ref[...] = jnp.zeros_like(cqt_ref)

    lane = lax.broadcasted_iota(jnp.int32, (DV, n), 1)
    for h in range(HEADS):
        sl = slice(h * DQK, (h + 1) * DQK)
        ig_c = gate_ref[:, h:h + 1]
        lf_c = gate_ref[:, HEADS + h:HEADS + h + 1]
        m_c = m_ref[:, h:h + 1]
        m_new_c = jnp.maximum(lf_c + m_c, ig_c)
        dec_s[...] = jnp.broadcast_to(jnp.exp(lf_c + m_c - m_new_c), (bb, DQK))
        ig_r = gatet_ref[h:h + 1, :]
        lf_r = gatet_ref[HEADS + h:HEADS + h + 1, :]
        m_r = mt_ref[h:h + 1, :]
        s_r = jnp.exp(ig_r - jnp.maximum(lf_r + m_r, ig_r))
        svt_s[...] = vt_ref[sl, :] * s_r

        def body(bl, cqt):
            onehot = lane == i * bb + bl
            c = c_ref[bl, h]
            q_row = _pick_row(q_ref[:, sl], bl)
            k_row = _pick_row(k_ref[:, sl], bl)
            dec_row = _pick_row(dec_s[...], bl)
            cq_col = jnp.sum(c * q_row, axis=1, keepdims=True)
            sv_col = jnp.sum(jnp.where(onehot, svt_s[...], 0.0), axis=1, keepdims=True)
            cn_ref[bl, h] = dec_row * c + sv_col * k_row
            return jnp.where(onehot, cq_col, cqt)

        cqt_ref[sl, :] = lax.fori_loop(0, bb, body, cqt_ref[sl, :], unroll=True)


def _s2_call(c, q, k, vt, gates, gatest, m, mt):
    n = q.shape[0]
    bb = S2_ROWS
    full = lambda *shape: pl.BlockSpec(shape, lambda i: (0,) * len(shape))
    rows = lambda width: pl.BlockSpec((bb, width), lambda i: (i, 0))
    cblk = pl.BlockSpec((bb, HEADS, DV, DQK), lambda i: (i, 0, 0, 0))
    return pl.pallas_call(
        _s2_kernel,
        grid=(n // bb,),
        in_specs=[cblk, rows(MLSTM_DIM), rows(MLSTM_DIM), full(MLSTM_DIM, n),
                  rows(GATE_PAD), full(GATE_PAD, n), rows(HEADS), full(HEADS, n)],
        out_specs=[cblk, full(MLSTM_DIM, n)],
        out_shape=[jax.ShapeDtypeStruct(c.shape, F32), jax.ShapeDtypeStruct((MLSTM_DIM, n), F32)],
        scratch_shapes=[pltpu.VMEM((bb, DQK), F32), pltpu.VMEM((DV, n), F32)],
        compiler_params=_params(("arbitrary",)),
        name="s2_memory_update",
    )(c, q, k, vt, gates, gatest, m, mt)


def _s3_kernel(cqt_ref, q_ref, k_ref, v_ref, so_ref, gate_ref, n_ref, m_ref, yconv_ref, x_ref,
               wout_ref, gmh_ref, gx_ref, wq_ref,
               hs_ref, qx_ref, nn_ref, mn_ref, y_s):
    n_rows = q_ref.shape[0]
    y_s[:, 0:CONV_DIM] = yconv_ref[...].astype(BF16)
    lane = lax.broadcasted_iota(jnp.int32, (n_rows, mn_ref.shape[1]), 1)
    m_out = jnp.zeros((n_rows, mn_ref.shape[1]), F32)
    for h in range(HEADS):
        sl = slice(h * DQK, (h + 1) * DQK)
        q = q_ref[:, sl]
        k = k_ref[:, sl]
        v = v_ref[:, sl]
        n_prev = n_ref[:, sl]
        cq = cqt_ref[sl, :].T
        ig = gate_ref[:, h:h + 1]
        lf = gate_ref[:, HEADS + h:HEADS + h + 1]
        m_prev = m_ref[:, h:h + 1]
        inter = lf + m_prev
        m_row = jnp.maximum(inter, ig)
        wgt = jnp.sum(q * k, axis=1, keepdims=True) * jnp.exp(ig - m_row)
        g = jnp.exp(inter - m_row)
        num = g * cq + wgt * v
        den = g * jnp.sum(n_prev * q, axis=1, keepdims=True) + wgt
        hh = num / jnp.maximum(jnp.abs(den), jnp.exp(-m_row))
        hh = hh * lax.rsqrt(jnp.mean(hh * hh, axis=1, keepdims=True) + EPS) * gmh_ref[:, sl]
        y_s[:, CONV_DIM + h * DV:CONV_DIM + (h + 1) * DV] = (so_ref[:, sl] * hh).astype(BF16)
        nn_ref[:, sl] = g * n_prev + jnp.exp(ig - m_row) * k
        m_out = jnp.where(lane == h, m_row, m_out)
    mn_ref[...] = m_out
    hs = x_ref[...] + _dot(y_s[...], wout_ref[...])
    hs_ref[...] = hs
    qx_ref[...] = _dot(_rmsnorm(hs, gx_ref[...]).astype(BF16), wq_ref[...])


def _s3_call(cqt, q, k, v, so, gates, nst, m, yconv, x, wout, gmh, gx, wq):
    n = q.shape[0]
    full = lambda *shape: pl.BlockSpec(shape, lambda i: (0,) * len(shape))
    ins = [cqt, q, k, v, so, gates, nst, m, yconv, x, wout, gmh, gx, wq]
    return pl.pallas_call(
        _s3_kernel,
        grid=(1,),
        in_specs=[full(*a.shape) for a in ins],
        out_specs=[full(n, D_MODEL), full(n, D_MODEL), full(n, MLSTM_DIM), full(n, 128)],
        out_shape=[jax.ShapeDtypeStruct((n, D_MODEL), F32), jax.ShapeDtypeStruct((n, D_MODEL), F32),
                   jax.ShapeDtypeStruct((n, MLSTM_DIM), F32), jax.ShapeDtypeStruct((n, 128), F32)],
        scratch_shapes=[pltpu.VMEM((n, D_MODEL), BF16)],
        compiler_params=_params(("arbitrary",)),
        name="s3_mlstm_finish",
    )(*ins)


def kernel(x_prompt, x_sample, mem_prompt, state_conv, state_mlstm_C, state_mlstm_n, state_mlstm_m,
           cache_mem_k, cache_mem_v, g_mix, w_in, b_in, conv_w, g_mh, w_out, g_cross, g_mem,
           w_xq, w_xkv, w_xo, g_ffn, w_gu, w_down, g_final):
    n_batch, seq_len, _ = x_prompt.shape
    n_dec = x_sample.shape[0]
    depth = w_in.shape[0]
    assert depth == 1 and x_sample.shape[1] == 1
    assert all(seq_len % t == 0 for t in (ROW_TILE, P1_TILE, P3_TILE))
    assert P1_TILE % MLSTM_CHUNK == 0 and n_batch % P2_SEQS == 0

    w_main = w_in[0, :, :MAIN_DIM].astype(BF16)
    b_main = b_in[0, :MAIN_DIM].reshape(1, MAIN_DIM)
    n_gate = 2 * HEADS
    w_gate = jnp.pad(w_in[0, :, MAIN_DIM:], ((0, 0), (0, GATE_PAD - n_gate))).astype(BF16)
    b_gate = jnp.pad(b_in[0, MAIN_DIM:], (0, GATE_PAD - n_gate)).reshape(1, GATE_PAD)
    v_cols = slice(3 * CONV_DIM + 2 * MLSTM_DIM, 3 * CONV_DIM + 3 * MLSTM_DIM)
    w_vt = w_in[0, :, v_cols].T.astype(BF16)
    b_vt = b_in[0, v_cols].reshape(MLSTM_DIM, 1)
    w_gate_t = w_gate.T
    b_gate_t = b_gate.reshape(GATE_PAD, 1)
    w_gate_r = w_gate_t[:n_gate]
    b_gate_r = b_gate_t[:n_gate]
    w_out_b = w_out[0].astype(BF16)
    w_xq_b = w_xq[0].astype(BF16)
    w_xkv_b = w_xkv[0].astype(BF16)
    w_xo_b = w_xo[0].astype(BF16)
    w_gu_b = w_gu[0].astype(BF16)
    w_down_b = w_down[0].astype(BF16)
    g_mix_r = g_mix[0].reshape(1, D_MODEL)
    g_cross_r = g_cross[0].reshape(1, D_MODEL)
    g_mem_r = g_mem[0].reshape(1, D_MODEL)
    g_ffn_r = g_ffn[0].reshape(1, D_MODEL)
    g_final_r = g_final.reshape(1, D_MODEL)
    g_mh_r = g_mh[0].reshape(1, MLSTM_DIM)
    cw = conv_w[0]

    xs = x_sample.reshape(n_dec, D_MODEL)
    st = state_conv[0].reshape(n_dec, (CONV_W - 1) * CONV_DIM)
    m0 = state_mlstm_m[0]
    s_yconv, s_conv, sq, sk, sv, sso, sgates, svt, sgates_t = _s1_call(
        xs, g_mix_r, w_main, b_main, w_gate, b_gate, cw, st, w_vt, b_vt, w_gate_t, b_gate_t)
    s_c, cqt = _s2_call(state_mlstm_C[0], sq, sk, svt, sgates, sgates_t, m0, m0.T)
    hs1, qx, s_n, s_m = _s3_call(cqt, sq, sk, sv, sso, sgates,
                                 state_mlstm_n[0].reshape(n_dec, MLSTM_DIM), m0, s_yconv, xs,
                                 w_out_b, g_mh_r, g_cross_r, w_xq_b)

    xp = x_prompt.reshape(n_batch * seq_len, D_MODEL)
    yconv, q, k, v, so, gates, p_conv = _p1_call(xp, g_mix_r, w_main, b_main, w_gate_r, b_gate_r, cw, seq_len)
    hp1, p_c, p_n, p_m = _p2_call(q, k, v, so, yconv, gates, xp, w_out_b, g_mh_r, n_batch, seq_len)
    hp1 = hp1.reshape(n_batch * seq_len, D_MODEL)
    pk, pv, pkb, pvb = _pm_call(mem_prompt.reshape(n_batch * N_MEM, D_MODEL), g_mem_r, w_xkv_b)
    o_p = _p3_call(hp1, g_cross_r, w_xq_b, pkb.reshape(n_batch, N_MEM, D_MODEL),
                   pvb.reshape(n_batch, N_MEM, D_MODEL), seq_len)
    y_p, o_s = _p4_call(hp1, o_p, w_xo_b, g_ffn_r, w_gu_b, w_down_b, g_final_r, ROW_TILE,
                        side=(qx.reshape(n_dec, X_HEADS, X_HEAD_DIM), cache_mem_k[0], cache_mem_v[0]))

    y_s, = _p4_call(hs1, o_s.reshape(n_dec, D_MODEL), w_xo_b, g_ffn_r, w_gu_b, w_down_b, g_final_r, n_dec)

    mem_shape = (1, n_batch, N_MEM, X_HEADS, X_HEAD_DIM)
    return (y_p.reshape(n_batch, seq_len, D_MODEL),
            y_s.reshape(n_dec, 1, D_MODEL),
            p_conv.reshape(1, n_batch, CONV_W - 1, CONV_DIM),
            p_c.reshape(1, n_batch, HEADS, DV, DQK),
            p_n.reshape(1, n_batch, HEADS, DQK),
            p_m[:, 0, :HEADS].reshape(1, n_batch, HEADS),
            pk.reshape(mem_shape),
            pv.reshape(mem_shape),
            s_conv.reshape(1, n_dec, CONV_W - 1, CONV_DIM),
            s_c.reshape(1, n_dec, HEADS, DV, DQK),
            s_n.reshape(1, n_dec, HEADS, DQK),
            s_m[:, :HEADS].reshape(1, n_dec, HEADS))
```

```python
import functools

import jax
import jax.numpy as jnp
from jax import lax
from jax.experimental import pallas as pl
from jax.experimental.pallas import tpu as pltpu

F32 = jnp.float32
BF16 = jnp.bfloat16

D_MODEL = 1024
CONV_DIM = 512
CONV_W = 3
MLSTM_DIM = 512
HEADS = 4
DQK = 128
DV = 128
N_MEM = 256
X_HEADS = 4
X_HEAD_DIM = 256
D_FF = 2816
MAIN_DIM = 3 * CONV_DIM + 4 * MLSTM_DIM
EPS = 1e-6

MLSTM_CHUNK = 256
P2_SEQS = 2
ROW_TILE = 512
P1_TILE = 1024
P3_TILE = 1024
FF_CHUNK = 256
MEM_CHUNK = 32
VMEM_LIMIT = 56 * 1024 * 1024


def _dot(a, b):
    return jnp.dot(a, b, preferred_element_type=F32)


def _dot_nt(a, b):
    return lax.dot_general(a, b, (((1,), (1,)), ((), ())), preferred_element_type=F32)


def _rmsnorm(x, g):
    return x * lax.rsqrt(jnp.mean(x * x, axis=-1, keepdims=True) + EPS) * g


def _const_spec(shape):
    zeros = (0,) * len(shape)
    return pl.BlockSpec(shape, lambda *_: zeros, pipeline_mode=pl.Buffered(1))


def _params(sem, flags=None):
    return pltpu.CompilerParams(dimension_semantics=sem, vmem_limit_bytes=VMEM_LIMIT, flags=flags)


def _gate_transform(gt):
    lane = lax.broadcasted_iota(jnp.int32, gt.shape, 1)
    return jnp.where(lane < HEADS, gt, jax.nn.log_sigmoid(gt))


def _gate_transform_rows(gt):
    sub = lax.broadcasted_iota(jnp.int32, gt.shape, 0)
    return jnp.where(sub < HEADS, gt, jax.nn.log_sigmoid(gt))


def _memory_update_rows(i, c_ref, q_ref, k_ref, vt_ref, gate_ref, gatet_ref, m_ref, mt_ref, cn_ref, cqt_ref):
    n = vt_ref.shape[1]
    bb = c_ref.shape[0]

    @pl.when(i == 0)
    def _():
        cqt_ref[...] = jnp.zeros_like(cqt_ref)

    lane = lax.broadcasted_iota(jnp.int32, (DV, n), 1)
    for h in range(HEADS):
        sl = slice(h * DQK, (h + 1) * DQK)
        ig_c = gate_ref[:, h:h + 1]
        lf_c = gate_ref[:, HEADS + h:HEADS + h + 1]
        m_c = m_ref[:, h:h + 1]
        dec = jnp.broadcast_to(jnp.exp(lf_c + m_c - jnp.maximum(lf_c + m_c, ig_c)), (bb, DQK))
        ig_r = gatet_ref[h:h + 1, :]
        lf_r = gatet_ref[HEADS + h:HEADS + h + 1, :]
        m_r = mt_ref[h:h + 1, :]
        svt = vt_ref[sl, :] * jnp.exp(ig_r - jnp.maximum(lf_r + m_r, ig_r))
        q_t = q_ref[:, sl]
        k_t = k_ref[:, sl]
        cqt = cqt_ref[sl, :]
        for bl in range(bb):
            onehot = lane == i * bb + bl
            c = c_ref[bl, h]
            cq_col = jnp.sum(c * q_t[bl:bl + 1, :], axis=1, keepdims=True)
            sv_col = jnp.sum(jnp.where(onehot, svt, 0.0), axis=1, keepdims=True)
            cn_ref[bl, h] = dec[bl:bl + 1, :] * c + sv_col * k_t[bl:bl + 1, :]
            cqt = jnp.where(onehot, cq_col, cqt)
        cqt_ref[sl, :] = cqt


def _p1_kernel(tiles_per_batch, x_ref, g_ref, w_ref, b_ref, wg_ref, bg_ref, cw_ref,
               c_ref, sq_ref, sk_ref, svt_ref, sgate_ref, sgatet_ref, sm_ref, smt_ref,
               yconv_ref, q_ref, k_ref, v_ref, so_ref, gate_ref, pconv_ref, cn_ref, cqt_ref, ubuf):
    tm = x_ref.shape[0]
    i = pl.program_id(0)
    _memory_update_rows(i, c_ref, sq_ref, sk_ref, svt_ref, sgate_ref, sgatet_ref, sm_ref, smt_ref,
                        cn_ref, cqt_ref)
    xn = _rmsnorm(x_ref[...], g_ref[...]).astype(BF16)

    def seg(j):
        sl = slice(j * CONV_DIM, (j + 1) * CONV_DIM)
        return _dot(xn, w_ref[:, sl]) + b_ref[:, sl]

    prev = ubuf[tm:tm + 8, :]
    ubuf[0:8, :] = jnp.where(i % tiles_per_batch == 0, jnp.zeros_like(prev), prev)
    ubuf[8:8 + tm, :] = seg(1) * seg(2)
    conv = (cw_ref[0:1, :] * ubuf[6:6 + tm, :] + cw_ref[1:2, :] * ubuf[7:7 + tm, :]
            + cw_ref[2:3, :] * ubuf[8:8 + tm, :])
    yconv_ref[...] = (seg(0) * conv).astype(BF16)
    pconv_ref[0] = ubuf[tm + 6:tm + 8, :]

    q_ref[...] = seg(3).astype(BF16)
    k_ref[...] = (seg(4) * (DQK ** -0.5)).astype(BF16)
    v_ref[...] = seg(5).astype(BF16)
    so_ref[...] = jax.nn.sigmoid(seg(6)).astype(BF16)
    gt = _gate_transform_rows(_dot_nt(wg_ref[...], xn) + bg_ref[...])
    n_gate = gt.shape[0]
    gate_ref[0, 0:n_gate, :] = gt
    L = MLSTM_CHUNK
    n_blk = tm // L
    hi = gt.astype(BF16).astype(F32)
    r1 = gt - hi
    mid = r1.astype(BF16).astype(F32)
    lo = r1 - mid
    terms = jnp.concatenate([t[:, j * L:(j + 1) * L] for t in (hi, mid, lo) for j in range(n_blk)], axis=0)
    tri = (lax.broadcasted_iota(jnp.int32, (L, L), 0) <= lax.broadcasted_iota(jnp.int32, (L, L), 1)).astype(BF16)
    parts = _dot(terms.astype(BF16), tri)
    for j in range(n_blk):
        rows = [parts[(t * n_blk + j) * n_gate:(t * n_blk + j + 1) * n_gate, :] for t in range(3)]
        gate_ref[0, n_gate:2 * n_gate, j * L:(j + 1) * L] = (rows[0] + rows[1]) + rows[2]


def _p1_call(x, g, w, b, wg, bg, cw, seq_len, side):
    rows = x.shape[0]
    tm = P1_TILE
    steps = rows // tm
    tiles_per_batch = seq_len // tm
    n_batch = rows // seq_len
    row = lambda width: pl.BlockSpec((tm, width), lambda i: (i, 0))
    c, sq, sk, svt, sgate, sgatet, sm, smt = side
    n = sq.shape[0]
    sr = n // steps
    assert sr * steps == n and sr % 8 == 0
    full = lambda a: pl.BlockSpec(a.shape, lambda i: (0,) * a.ndim)
    srow = lambda a: pl.BlockSpec((sr,) + a.shape[1:], lambda i: (i,) + (0,) * (a.ndim - 1))
    return pl.pallas_call(
        functools.partial(_p1_kernel, tiles_per_batch),
        grid=(steps,),
        in_specs=[row(D_MODEL), _const_spec((1, D_MODEL)), _const_spec(w.shape),
                  _const_spec(b.shape), _const_spec((2 * HEADS, D_MODEL)),
                  _const_spec((2 * HEADS, 1)), _const_spec((CONV_W, CONV_DIM)),
                  srow(c), srow(sq), srow(sk), full(svt), srow(sgate), full(sgatet), srow(sm), full(smt)],
        out_specs=[row(CONV_DIM), row(MLSTM_DIM), row(MLSTM_DIM), row(MLSTM_DIM), row(MLSTM_DIM),
                   pl.BlockSpec((1, 4 * HEADS, tm), lambda i: (i // tiles_per_batch, 0, i % tiles_per_batch)),
                   pl.BlockSpec((1, CONV_W - 1, CONV_DIM), lambda i: (i // tiles_per_batch, 0, 0)),
                   srow(c), full(svt)],
        out_shape=[jax.ShapeDtypeStruct((rows, CONV_DIM), BF16)]
        + [jax.ShapeDtypeStruct((rows, MLSTM_DIM), BF16)] * 4
        + [jax.ShapeDtypeStruct((n_batch, 4 * HEADS, seq_len), F32),
           jax.ShapeDtypeStruct((n_batch, CONV_W - 1, CONV_DIM), F32),
           jax.ShapeDtypeStruct(c.shape, F32), jax.ShapeDtypeStruct(svt.shape, F32)],
        scratch_shapes=[pltpu.VMEM((tm + 8, CONV_DIM), F32)],
        compiler_params=_params(("arbitrary",)),
        name="p1_inproj_conv",
    )(x, g, w, b, wg, bg, cw, c, sq, sk, svt, sgate, sgatet, sm, smt)


def _p2_kernel(q_ref, k_ref, v_ref, so_ref, yconv_ref, gate_ref, x_ref, wout_ref, gmh_ref,
               hp_ref, pc_ref, pn_ref, pm_ref, c_s, m_s, y_s):
    nb, L = q_ref.shape[0], q_ref.shape[1]
    c = pl.program_id(1)

    @pl.when(c == 0)
    def _():
        c_s[...] = jnp.zeros_like(c_s)
        m_s[...] = jnp.zeros_like(m_s)

    row = lax.broadcasted_iota(jnp.int32, (L, L), 0)
    col = lax.broadcasted_iota(jnp.int32, (L, L), 1)
    causal = row >= col

    for bi in range(nb):
        gt = gate_ref[bi]
        for h in range(HEADS):
            sl = slice(h * DQK, (h + 1) * DQK)
            q = q_ref[bi, :, sl]
            k = k_ref[bi, :, sl]
            v = v_ref[bi, :, sl]
            lf_r = gt[HEADS + h:HEADS + h + 1, :]
            a_r = gt[h:h + 1, :] - gt[3 * HEADS + h:3 * HEADS + h + 1, :]
            m_prev = jnp.max(m_s[bi, h:h + 1, :], axis=1, keepdims=True)
            c_prev = c_s[bi, h]

            m_c = jnp.maximum(m_prev, jnp.max(jnp.where(causal, a_r, -jnp.inf), axis=1, keepdims=True))
            b_c = jnp.sum(jnp.where(causal, lf_r, 0.0), axis=1, keepdims=True)
            w = _dot_nt(q, k) * jnp.exp(jnp.where(causal, a_r - m_c, -jnp.inf))
            g = jnp.exp(m_prev - m_c)
            qc = _dot_nt(q, c_prev.astype(BF16))
            num = g * qc[:, 0:DV] + _dot(w.astype(BF16), v)
            den = g * qc[:, DV:2 * DV] + jnp.sum(w, axis=1, keepdims=True)
            hh = num / jnp.maximum(jnp.abs(den), jnp.exp(-(b_c + m_c)))
            hh = hh * lax.rsqrt(jnp.mean(hh * hh, axis=1, keepdims=True) + EPS) * gmh_ref[:, sl]
            y_s[bi * L:(bi + 1) * L, h * DV:(h + 1) * DV] = (so_ref[bi, :, sl].astype(F32) * hh).astype(BF16)

            m_last = jnp.maximum(m_prev, jnp.max(a_r, axis=1, keepdims=True))
            b_last = jnp.sum(lf_r, axis=1, keepdims=True)
            s_r = jnp.exp(a_r - m_last)
            sv_t = jnp.concatenate([v.T.astype(F32) * s_r, jnp.broadcast_to(s_r, (DV, L))], axis=0)
            c_s[bi, h] = jnp.exp(m_prev - m_last) * c_prev + _dot(sv_t.astype(BF16), k)
            m_s[bi, h:h + 1, :] = jnp.broadcast_to(b_last + m_last, (1, m_s.shape[2]))

    for bi in range(nb):
        out = (_dot(yconv_ref[bi], wout_ref[0:CONV_DIM, :])
               + _dot(y_s[bi * L:(bi + 1) * L, :], wout_ref[CONV_DIM:CONV_DIM + MLSTM_DIM, :]))
        hp_ref[bi] = x_ref[bi] + out

    @pl.when(c == pl.num_programs(1) - 1)
    def _():
        lane = lax.broadcasted_iota(jnp.int32, (1, m_s.shape[2]), 1)
        for bi in range(nb):
            acc = jnp.zeros((1, m_s.shape[2]), F32)
            for h in range(HEADS):
                pc_ref[bi, h] = c_s[bi, h, 0:DV, :]
                pn_ref[bi, h:h + 1, :] = c_s[bi, h, DV:DV + 1, :]
                acc = jnp.where(lane == h, m_s[bi, h:h + 1, :], acc)
            pm_ref[bi] = acc


def _p2_call(q, k, v, so, yconv, gates, x, wout, gmh, n_batch, seq_len):
    L = MLSTM_CHUNK
    nb = P2_SEQS
    nc = seq_len // L
    seq = lambda width: pl.BlockSpec((nb, L, width), lambda b, c: (b, c, 0))
    as_seq = lambda a: a.reshape(n_batch, seq_len, a.shape[-1])
    return pl.pallas_call(
        _p2_kernel,
        grid=(n_batch // nb, nc),
        in_specs=[seq(MLSTM_DIM), seq(MLSTM_DIM), seq(MLSTM_DIM), seq(MLSTM_DIM), seq(CONV_DIM),
                  pl.BlockSpec((nb, 4 * HEADS, L), lambda b, c: (b, 0, c)), seq(D_MODEL),
                  _const_spec((D_MODEL, D_MODEL)), _const_spec((1, MLSTM_DIM))],
        out_specs=[seq(D_MODEL),
                   pl.BlockSpec((nb, HEADS, DV, DQK), lambda b, c: (b, 0, 0, 0)),
                   pl.BlockSpec((nb, HEADS, DQK), lambda b, c: (b, 0, 0)),
                   pl.BlockSpec((nb, 1, 128), lambda b, c: (b, 0, 0))],
        out_shape=[jax.ShapeDtypeStruct((n_batch, seq_len, D_MODEL), F32),
                   jax.ShapeDtypeStruct((n_batch, HEADS, DV, DQK), F32),
                   jax.ShapeDtypeStruct((n_batch, HEADS, DQK), F32),
                   jax.ShapeDtypeStruct((n_batch, 1, 128), F32)],
        scratch_shapes=[pltpu.VMEM((nb, HEADS, 2 * DV, DQK), F32), pltpu.VMEM((nb, 8, 128), F32),
                        pltpu.VMEM((nb * L, MLSTM_DIM), BF16)],
        compiler_params=_params(("arbitrary", "arbitrary")),
        name="p2_mlstm_outproj",
    )(as_seq(q), as_seq(k), as_seq(v), as_seq(so), as_seq(yconv), gates, as_seq(x), wout, gmh)


def _pm_kernel(mem_ref, g_ref, w_ref, k_ref, v_ref, kb_ref, vb_ref):
    xn = _rmsnorm(mem_ref[...], g_ref[...]).astype(BF16)
    kk = _dot(xn, w_ref[:, 0:D_MODEL])
    vv = _dot(xn, w_ref[:, D_MODEL:2 * D_MODEL])
    for h in range(X_HEADS):
        sl = slice(h * X_HEAD_DIM, (h + 1) * X_HEAD_DIM)
        k_ref[:, h, :] = kk[:, sl]
        v_ref[:, h, :] = vv[:, sl]
    kb_ref[...] = kk.astype(BF16)
    vb_ref[...] = vv.astype(BF16)


def _pm_call(mem, g, w):
    rows = mem.shape[0]
    tm = ROW_TILE
    row = pl.BlockSpec((tm, D_MODEL), lambda i: (i, 0))
    row4 = pl.BlockSpec((tm, X_HEADS, X_HEAD_DIM), lambda i: (i, 0, 0))
    return pl.pallas_call(
        _pm_kernel,
        grid=(rows // tm,),
        in_specs=[row, _const_spec((1, D_MODEL)), _const_spec((D_MODEL, 2 * D_MODEL))],
        out_specs=[row4, row4, row, row],
        out_shape=[jax.ShapeDtypeStruct((rows, X_HEADS, X_HEAD_DIM), F32)] * 2
        + [jax.ShapeDtypeStruct((rows, D_MODEL), BF16)] * 2,
        compiler_params=_params(("arbitrary",)),
        name="pm_mem_kv",
    )(mem, g, w)


def _p3_kernel(hp_ref, g_ref, wq_ref, k_ref, v_ref, o_ref):
    xn = _rmsnorm(hp_ref[...], g_ref[...]).astype(BF16)
    q = _dot(xn, wq_ref[...])
    for h in range(X_HEADS):
        sl = slice(h * X_HEAD_DIM, (h + 1) * X_HEAD_DIM)
        s = _dot_nt(q[:, sl].astype(BF16), k_ref[0, :, sl]) * (X_HEAD_DIM ** -0.5)
        e = jnp.exp(s - jnp.max(s, axis=1, keepdims=True))
        p = e * (1.0 / jnp.sum(e, axis=1, keepdims=True))
        o_ref[:, sl] = _dot(p.astype(BF16), v_ref[0, :, sl]).astype(BF16)


def _p3_call(hp, g, wq, kb, vb, seq_len):
    rows = hp.shape[0]
    tm = P3_TILE
    tiles_per_batch = seq_len // tm
    row = pl.BlockSpec((tm, D_MODEL), lambda i: (i, 0))
    mem = pl.BlockSpec((1, N_MEM, D_MODEL), lambda i: (i // tiles_per_batch, 0, 0))
    return pl.pallas_call(
        _p3_kernel,
        grid=(rows // tm,),
        in_specs=[row, _const_spec((1, D_MODEL)), _const_spec((D_MODEL, D_MODEL)), mem, mem],
        out_specs=row,
        out_shape=jax.ShapeDtypeStruct((rows, D_MODEL), BF16),
        compiler_params=_params(("arbitrary",)),
        name="p3_cross_attn",
    )(hp, g, wq, kb, vb)


def _cache_attention_row(q4, kc_ref, vc_ref, bl):
    qs = q4 * (X_HEAD_DIM ** -0.5)
    m_run = jnp.full((1, X_HEADS, 1), -jnp.inf, F32)
    l_run = jnp.zeros((1, X_HEADS, 1), F32)
    acc = jnp.zeros((X_HEADS, X_HEAD_DIM), F32)
    for c in range(N_MEM // MEM_CHUNK):
        blk = slice(c * MEM_CHUNK, (c + 1) * MEM_CHUNK)
        s = jnp.sum(kc_ref[bl, blk] * qs, axis=-1, keepdims=True)
        m_new = jnp.maximum(m_run, jnp.max(s, axis=0, keepdims=True))
        alpha = jnp.exp(m_run - m_new)
        e = jnp.exp(s - m_new)
        l_run = alpha * l_run + jnp.sum(e, axis=0, keepdims=True)
        acc = alpha[0] * acc + jnp.sum(e * vc_ref[bl, blk], axis=0)
        m_run = m_new
    return acc * (1.0 / l_run[0])


def _p4_kernel(side_rows, hp_ref, o_ref, wxo_ref, gf_ref, wgu_ref, wd_ref, gfin_ref, *rest):
    if side_rows:
        q4_ref, kc_ref, vc_ref, y_ref, os_ref, act_s = rest
        for bl in range(side_rows):
            os_ref[bl] = _cache_attention_row(q4_ref[bl], kc_ref, vc_ref, bl)
    else:
        y_ref, act_s = rest
    hp = hp_ref[...] + _dot(o_ref[...].astype(BF16), wxo_ref[...])
    xn = _rmsnorm(hp, gf_ref[...]).astype(BF16)
    for j in range(D_FF // FF_CHUNK):
        g = _dot(xn, wgu_ref[:, FF_CHUNK * j:FF_CHUNK * (j + 1)])
        u = _dot(xn, wgu_ref[:, D_FF + FF_CHUNK * j:D_FF + FF_CHUNK * (j + 1)])
        act_s[:, FF_CHUNK * j:FF_CHUNK * (j + 1)] = (g * jax.nn.sigmoid(g) * u).astype(BF16)
    hp = hp + _dot(act_s[...], wd_ref[...])
    y_ref[...] = _rmsnorm(hp, gfin_ref[...])


def _p4_call(hp, o, wxo, gf, wgu, wd, gfin, tm, side=None):
    rows = hp.shape[0]
    steps = rows // tm
    row = pl.BlockSpec((tm, D_MODEL), lambda i: (i, 0))
    in_specs = [row, row, _const_spec((D_MODEL, D_MODEL)), _const_spec((1, D_MODEL)),
                _const_spec((D_MODEL, 2 * D_FF)), _const_spec((D_FF, D_MODEL)),
                _const_spec((1, D_MODEL))]
    out_specs = [row]
    out_shape = [jax.ShapeDtypeStruct((rows, D_MODEL), F32)]
    args = [hp, o, wxo, gf, wgu, wd, gfin]
    side_rows = 0
    if side is not None:
        q4, kc, vc = side
        side_rows = q4.shape[0] // steps
        assert side_rows * steps == q4.shape[0]
        srow = pl.BlockSpec((side_rows, X_HEADS, X_HEAD_DIM), lambda i: (i, 0, 0))
        cache = pl.BlockSpec((side_rows, N_MEM, X_HEADS, X_HEAD_DIM), lambda i: (i, 0, 0, 0))
        in_specs += [srow, cache, cache]
        out_specs += [srow]
        out_shape += [jax.ShapeDtypeStruct(q4.shape, F32)]
        args += [q4, kc, vc]
    return pl.pallas_call(
        functools.partial(_p4_kernel, side_rows),
        grid=(steps,),
        in_specs=in_specs,
        out_specs=out_specs,
        out_shape=out_shape,
        scratch_shapes=[pltpu.VMEM((tm, D_FF), BF16)],
        compiler_params=_params(("arbitrary",)),
        name="p4_ffn_final",
    )(*args)


def _s1_kernel(x_ref, g_ref, w_ref, b_ref, wg_ref, bg_ref, cw_ref, st_ref,
               yconv_ref, sconv_ref, q_ref, k_ref, v_ref, so_ref, gate_ref, vt_ref, gatet_ref):
    xn = _rmsnorm(x_ref[...], g_ref[...]).astype(BF16)

    def seg(j):
        sl = slice(j * CONV_DIM, (j + 1) * CONV_DIM)
        return _dot(xn, w_ref[:, sl]) + b_ref[:, sl]

    u = seg(1) * seg(2)
    st0 = st_ref[:, 0:CONV_DIM]
    st1 = st_ref[:, CONV_DIM:2 * CONV_DIM]
    conv = cw_ref[0:1, :] * st0 + cw_ref[1:2, :] * st1 + cw_ref[2:3, :] * u
    yconv_ref[...] = seg(0) * conv
    sconv_ref[:, 0:CONV_DIM] = st1
    sconv_ref[:, CONV_DIM:2 * CONV_DIM] = u
    q_ref[...] = seg(3)
    k_ref[...] = seg(4) * (DQK ** -0.5)
    v = seg(5)
    v_ref[...] = v
    so_ref[...] = jax.nn.sigmoid(seg(6))
    n_gate = wg_ref.shape[0]
    gate_ref[...] = _gate_transform(_dot_nt(xn, wg_ref[...]) + b_ref[:, MAIN_DIM:MAIN_DIM + n_gate])
    for h in range(HEADS):
        sl = slice(h * DV, (h + 1) * DV)
        vt_ref[sl, :] = v[:, sl].T
    gatet_ref[...] = _gate_transform_rows(_dot_nt(wg_ref[...], xn) + bg_ref[...])


def _s1_call(x, g, w, b, wg, bg, cw, st):
    n = x.shape[0]
    n_gate = wg.shape[0]
    full = lambda *shape: pl.BlockSpec(shape, lambda i: (0,) * len(shape))
    ins = [x, g, w, b, wg, bg, cw, st]
    return pl.pallas_call(
        _s1_kernel,
        grid=(1,),
        in_specs=[full(*a.shape) for a in ins],
        out_specs=[full(n, CONV_DIM), full(n, 2 * CONV_DIM), full(n, MLSTM_DIM), full(n, MLSTM_DIM),
                   full(n, MLSTM_DIM), full(n, MLSTM_DIM), full(n, n_gate), full(MLSTM_DIM, n),
                   full(n_gate, n)],
        out_shape=[jax.ShapeDtypeStruct((n, CONV_DIM), F32),
                   jax.ShapeDtypeStruct((n, 2 * CONV_DIM), F32)]
        + [jax.ShapeDtypeStruct((n, MLSTM_DIM), F32)] * 4
        + [jax.ShapeDtypeStruct((n, n_gate), F32),
           jax.ShapeDtypeStruct((MLSTM_DIM, n), F32),
           jax.ShapeDtypeStruct((n_gate, n), F32)],
        compiler_params=_params(("arbitrary",)),
        name="s1_inproj_conv",
    )(*ins)


def _s3_kernel(cqt_ref, q_ref, k_ref, v_ref, so_ref, gate_ref, n_ref, m_ref, yconv_ref, x_ref,
               wout_ref, gmh_ref, gx_ref, wq_ref,
               hs_ref, qx_ref, nn_ref, mn_ref, y_s):
    n_rows = q_ref.shape[0]
    y_s[:, 0:CONV_DIM] = yconv_ref[...].astype(BF16)
    lane = lax.broadcasted_iota(jnp.int32, (n_rows, mn_ref.shape[1]), 1)
    m_out = jnp.zeros((n_rows, mn_ref.shape[1]), F32)
    for h in range(HEADS):
        sl = slice(h * DQK, (h + 1) * DQK)
        q = q_ref[:, sl]
        k = k_ref[:, sl]
        v = v_ref[:, sl]
        n_prev = n_ref[:, sl]
        cq = cqt_ref[sl, :].T
        ig = gate_ref[:, h:h + 1]
        lf = gate_ref[:, HEADS + h:HEADS + h + 1]
        m_prev = m_ref[:, h:h + 1]
        inter = lf + m_prev
        m_row = jnp.maximum(inter, ig)
        wgt = jnp.sum(q * k, axis=1, keepdims=True) * jnp.exp(ig - m_row)
        g = jnp.exp(inter - m_row)
        num = g * cq + wgt * v
        den = g * jnp.sum(n_prev * q, axis=1, keepdims=True) + wgt
        hh = num / jnp.maximum(jnp.abs(den), jnp.exp(-m_row))
        hh = hh * lax.rsqrt(jnp.mean(hh * hh, axis=1, keepdims=True) + EPS) * gmh_ref[:, sl]
        y_s[:, CONV_DIM + h * DV:CONV_DIM + (h + 1) * DV] = (so_ref[:, sl] * hh).astype(BF16)
        nn_ref[:, sl] = g * n_prev + jnp.exp(ig - m_row) * k
        m_out = jnp.where(lane == h, m_row, m_out)
    mn_ref[...] = m_out
    hs = x_ref[...] + _dot(y_s[...], wout_ref[...])
    hs_ref[...] = hs
    qx_ref[...] = _dot(_rmsnorm(hs, gx_ref[...]).astype(BF16), wq_ref[...])


def _s3_call(cqt, q, k, v, so, gates, nst, m, yconv, x, wout, gmh, gx, wq):
    n = q.shape[0]
    full = lambda *shape: pl.BlockSpec(shape, lambda i: (0,) * len(shape))
    ins = [cqt, q, k, v, so, gates, nst, m, yconv, x, wout, gmh, gx, wq]
    return pl.pallas_call(
        _s3_kernel,
        grid=(1,),
        in_specs=[full(*a.shape) for a in ins],
        out_specs=[full(n, D_MODEL), full(n, D_MODEL), full(n, MLSTM_DIM), full(n, 128)],
        out_shape=[jax.ShapeDtypeStruct((n, D_MODEL), F32), jax.ShapeDtypeStruct((n, D_MODEL), F32),
                   jax.ShapeDtypeStruct((n, MLSTM_DIM), F32), jax.ShapeDtypeStruct((n, 128), F32)],
        scratch_shapes=[pltpu.VMEM((n, D_MODEL), BF16)],
        compiler_params=_params(("arbitrary",)),
        name="s3_mlstm_finish",
    )(*ins)


def kernel(x_prompt, x_sample, mem_prompt, state_conv, state_mlstm_C, state_mlstm_n, state_mlstm_m,
           cache_mem_k, cache_mem_v, g_mix, w_in, b_in, conv_w, g_mh, w_out, g_cross, g_mem,
           w_xq, w_xkv, w_xo, g_ffn, w_gu, w_down, g_final):
    n_batch, seq_len, _ = x_prompt.shape
    n_dec = x_sample.shape[0]
    depth = w_in.shape[0]
    assert depth == 1 and x_sample.shape[1] == 1
    assert all(seq_len % t == 0 for t in (ROW_TILE, P1_TILE, P3_TILE))
    assert P1_TILE % MLSTM_CHUNK == 0 and n_batch % P2_SEQS == 0

    n_gate = 2 * HEADS
    w_in_b = w_in[0].astype(BF16)
    b_in_r = b_in[0].reshape(1, MAIN_DIM + n_gate)
    w_gate_r = w_in[0, :, MAIN_DIM:].T.astype(BF16)
    b_gate_r = b_in[0, MAIN_DIM:].reshape(n_gate, 1)
    w_out_b = w_out[0].astype(BF16)
    w_xq_b = w_xq[0].astype(BF16)
    w_xkv_b = w_xkv[0].astype(BF16)
    w_xo_b = w_xo[0].astype(BF16)
    w_gu_b = w_gu[0].astype(BF16)
    w_down_b = w_down[0].astype(BF16)
    g_mix_r = g_mix[0].reshape(1, D_MODEL)
    g_cross_r = g_cross[0].reshape(1, D_MODEL)
    g_mem_r = g_mem[0].reshape(1, D_MODEL)
    g_ffn_r = g_ffn[0].reshape(1, D_MODEL)
    g_final_r = g_final.reshape(1, D_MODEL)
    g_mh_r = g_mh[0].reshape(1, MLSTM_DIM)
    cw = conv_w[0]

    xs = x_sample.reshape(n_dec, D_MODEL)
    st = state_conv[0].reshape(n_dec, (CONV_W - 1) * CONV_DIM)
    m0 = state_mlstm_m[0]
    s_yconv, s_conv, sq, sk, sv, sso, sgates, svt, sgates_t = _s1_call(
        xs, g_mix_r, w_in_b, b_in_r, w_gate_r, b_gate_r, cw, st)

    xp = x_prompt.reshape(n_batch * seq_len, D_MODEL)
    yconv, q, k, v, so, gates, p_conv, s_c, cqt = _p1_call(
        xp, g_mix_r, w_in_b, b_in_r, w_gate_r, b_gate_r, cw, seq_len,
        side=(state_mlstm_C[0], sq, sk, svt, sgates, sgates_t, m0, m0.T))
    hs1, qx, s_n, s_m = _s3_call(cqt, sq, sk, sv, sso, sgates,
                                 state_mlstm_n[0].reshape(n_dec, MLSTM_DIM), m0, s_yconv, xs,
                                 w_out_b, g_mh_r, g_cross_r, w_xq_b)
    hp1, p_c, p_n, p_m = _p2_call(q, k, v, so, yconv, gates, xp, w_out_b, g_mh_r, n_batch, seq_len)
    hp1 = hp1.reshape(n_batch * seq_len, D_MODEL)
    pk, pv, pkb, pvb = _pm_call(mem_prompt.reshape(n_batch * N_MEM, D_MODEL), g_mem_r, w_xkv_b)
    o_p = _p3_call(hp1, g_cross_r, w_xq_b, pkb.reshape(n_batch, N_MEM, D_MODEL),
                   pvb.reshape(n_batch, N_MEM, D_MODEL), seq_len)
    y_p, o_s = _p4_call(hp1, o_p, w_xo_b, g_ffn_r, w_gu_b, w_down_b, g_final_r, ROW_TILE,
                        side=(qx.reshape(n_dec, X_HEADS, X_HEAD_DIM), cache_mem_k[0], cache_mem_v[0]))

    y_s, = _p4_call(hs1, o_s.reshape(n_dec, D_MODEL), w_xo_b, g_ffn_r, w_gu_b, w_down_b, g_final_r, n_dec)

    mem_shape = (1, n_batch, N_MEM, X_HEADS, X_HEAD_DIM)
    return (y_p.reshape(n_batch, seq_len, D_MODEL),
            y_s.reshape(n_dec, 1, D_MODEL),
            p_conv.reshape(1, n_batch, CONV_W - 1, CONV_DIM),
            p_c.reshape(1, n_batch, HEADS, DV, DQK),
            p_n.reshape(1, n_batch, HEADS, DQK),
            p_m[:, 0, :HEADS].reshape(1, n_batch, HEADS),
            pk.reshape(mem_shape),
            pv.reshape(mem_shape),
            s_conv.reshape(1, n_dec, CONV_W - 1, CONV_DIM),
            s_c.reshape(1, n_dec, HEADS, DV, DQK),
            s_n.reshape(1, n_dec, HEADS, DQK),
            s_m[:, :HEADS].reshape(1, n_dec, HEADS))
```

```python
import functools

import jax
import jax.numpy as jnp
from jax import lax
from jax.experimental import pallas as pl
from jax.experimental.pallas import tpu as pltpu

F32 = jnp.float32
BF16 = jnp.bfloat16

D_MODEL = 1024
CONV_DIM = 512
CONV_W = 3
MLSTM_DIM = 512
HEADS = 4
DQK = 128
DV = 128
N_MEM = 256
X_HEADS = 4
X_HEAD_DIM = 256
D_FF = 2816
MAIN_DIM = 3 * CONV_DIM + 4 * MLSTM_DIM
EPS = 1e-6

MLSTM_CHUNK = 256
P2_SEQS = 4
ROW_TILE = 512
P1_TILE = 1024
P3_TILE = 1024
FF_CHUNK = 256
PACK_ROWS = X_HEADS * (X_HEAD_DIM // 128)
MEM_CHUNK = 32
VMEM_LIMIT = 56 * 1024 * 1024


def _dot(a, b):
    return jnp.dot(a, b, preferred_element_type=F32)


def _dot_nt(a, b):
    return lax.dot_general(a, b, (((1,), (1,)), ((), ())), preferred_element_type=F32)


def _rmsnorm(x, g):
    return x * lax.rsqrt(jnp.mean(x * x, axis=-1, keepdims=True) + EPS) * g


def _const_spec(shape):
    zeros = (0,) * len(shape)
    return pl.BlockSpec(shape, lambda *_: zeros, pipeline_mode=pl.Buffered(1))


def _params(sem, flags=None):
    return pltpu.CompilerParams(dimension_semantics=sem, vmem_limit_bytes=VMEM_LIMIT, flags=flags)


def _gate_transform(gt):
    lane = lax.broadcasted_iota(jnp.int32, gt.shape, 1)
    return jnp.where(lane < HEADS, gt, jax.nn.log_sigmoid(gt))


def _gate_transform_rows(gt):
    sub = lax.broadcasted_iota(jnp.int32, gt.shape, 0)
    return jnp.where(sub < HEADS, gt, jax.nn.log_sigmoid(gt))


def _memory_update_rows(i, c_ref, q_ref, k_ref, vt_ref, gate_ref, gatet_ref, m_ref, mt_ref, cn_ref, cqt_ref):
    n = vt_ref.shape[1]
    bb = c_ref.shape[0]

    @pl.when(i == 0)
    def _():
        cqt_ref[...] = jnp.zeros_like(cqt_ref)

    lane = lax.broadcasted_iota(jnp.int32, (DV, n), 1)
    for h in range(HEADS):
        sl = slice(h * DQK, (h + 1) * DQK)
        ig_c = gate_ref[:, h:h + 1]
        lf_c = gate_ref[:, HEADS + h:HEADS + h + 1]
        m_c = m_ref[:, h:h + 1]
        dec = jnp.broadcast_to(jnp.exp(lf_c + m_c - jnp.maximum(lf_c + m_c, ig_c)), (bb, DQK))
        ig_r = gatet_ref[h:h + 1, :]
        lf_r = gatet_ref[HEADS + h:HEADS + h + 1, :]
        m_r = mt_ref[h:h + 1, :]
        svt = vt_ref[sl, :] * jnp.exp(ig_r - jnp.maximum(lf_r + m_r, ig_r))
        q_t = q_ref[:, sl]
        k_t = k_ref[:, sl]
        cqt = cqt_ref[sl, :]
        for bl in range(bb):
            onehot = lane == i * bb + bl
            c = c_ref[bl, h]
            cq_col = jnp.sum(c * q_t[bl:bl + 1, :], axis=1, keepdims=True)
            sv_col = jnp.sum(jnp.where(onehot, svt, 0.0), axis=1, keepdims=True)
            cn_ref[bl, h] = dec[bl:bl + 1, :] * c + sv_col * k_t[bl:bl + 1, :]
            cqt = jnp.where(onehot, cq_col, cqt)
        cqt_ref[sl, :] = cqt


def _p1_kernel(tiles_per_batch, x_ref, g_ref, w_ref, b_ref, wg_ref, bg_ref, cw_ref,
               c_ref, sq_ref, sk_ref, svt_ref, sgate_ref, sgatet_ref, sm_ref, smt_ref,
               yconv_ref, q_ref, k_ref, v_ref, so_ref, gate_ref, pconv_ref, cn_ref, cqt_ref, ubuf):
    tm = x_ref.shape[0]
    i = pl.program_id(0)
    _memory_update_rows(i, c_ref, sq_ref, sk_ref, svt_ref, sgate_ref, sgatet_ref, sm_ref, smt_ref,
                        cn_ref, cqt_ref)
    xn = _rmsnorm(x_ref[...], g_ref[...]).astype(BF16)

    def seg(j):
        sl = slice(j * CONV_DIM, (j + 1) * CONV_DIM)
        return _dot_nt(xn, w_ref[sl, :]) + b_ref[:, sl]

    prev = ubuf[tm:tm + 8, :]
    ubuf[0:8, :] = jnp.where(i % tiles_per_batch == 0, jnp.zeros_like(prev), prev)
    ubuf[8:8 + tm, :] = seg(1) * seg(2)
    conv = (cw_ref[0:1, :] * ubuf[6:6 + tm, :] + cw_ref[1:2, :] * ubuf[7:7 + tm, :]
            + cw_ref[2:3, :] * ubuf[8:8 + tm, :])
    yconv_ref[...] = (seg(0) * conv).astype(BF16)
    pconv_ref[0] = ubuf[tm + 6:tm + 8, :]

    q_ref[...] = seg(3).astype(BF16)
    k_ref[...] = (seg(4) * (DQK ** -0.5)).astype(BF16)
    v_ref[...] = seg(5).astype(BF16)
    so_ref[...] = jax.nn.sigmoid(seg(6)).astype(BF16)
    gt = _gate_transform_rows(_dot_nt(wg_ref[...], xn) + bg_ref[...])
    n_gate = gt.shape[0]
    gate_ref[0, 0:n_gate, :] = gt
    L = MLSTM_CHUNK
    n_blk = tm // L
    hi = gt.astype(BF16).astype(F32)
    r1 = gt - hi
    mid = r1.astype(BF16).astype(F32)
    lo = r1 - mid
    terms = jnp.concatenate([t[:, j * L:(j + 1) * L] for t in (hi, mid, lo) for j in range(n_blk)], axis=0)
    tri = (lax.broadcasted_iota(jnp.int32, (L, L), 0) <= lax.broadcasted_iota(jnp.int32, (L, L), 1)).astype(BF16)
    parts = _dot(terms.astype(BF16), tri)
    for j in range(n_blk):
        rows = [parts[(t * n_blk + j) * n_gate:(t * n_blk + j + 1) * n_gate, :] for t in range(3)]
        gate_ref[0, n_gate:2 * n_gate, j * L:(j + 1) * L] = (rows[0] + rows[1]) + rows[2]


def _p1_call(x, g, w, b, wg, bg, cw, seq_len, side):
    rows = x.shape[0]
    tm = P1_TILE
    steps = rows // tm
    tiles_per_batch = seq_len // tm
    n_batch = rows // seq_len
    row = lambda width: pl.BlockSpec((tm, width), lambda i: (i, 0))
    c, sq, sk, svt, sgate, sgatet, sm, smt = side
    n = sq.shape[0]
    sr = n // steps
    assert sr * steps == n and sr % 8 == 0
    full = lambda a: pl.BlockSpec(a.shape, lambda i: (0,) * a.ndim)
    srow = lambda a: pl.BlockSpec((sr,) + a.shape[1:], lambda i: (i,) + (0,) * (a.ndim - 1))
    return pl.pallas_call(
        functools.partial(_p1_kernel, tiles_per_batch),
        grid=(steps,),
        in_specs=[row(D_MODEL), _const_spec((1, D_MODEL)), _const_spec(w.shape),
                  _const_spec(b.shape), _const_spec((2 * HEADS, D_MODEL)),
                  _const_spec((2 * HEADS, 1)), _const_spec((CONV_W, CONV_DIM)),
                  srow(c), srow(sq), srow(sk), full(svt), srow(sgate), full(sgatet), srow(sm), full(smt)],
        out_specs=[row(CONV_DIM), row(MLSTM_DIM), row(MLSTM_DIM), row(MLSTM_DIM), row(MLSTM_DIM),
                   pl.BlockSpec((1, 4 * HEADS, tm), lambda i: (i // tiles_per_batch, 0, i % tiles_per_batch)),
                   pl.BlockSpec((1, CONV_W - 1, CONV_DIM), lambda i: (i // tiles_per_batch, 0, 0)),
                   srow(c), full(svt)],
        out_shape=[jax.ShapeDtypeStruct((rows, CONV_DIM), BF16)]
        + [jax.ShapeDtypeStruct((rows, MLSTM_DIM), BF16)] * 4
        + [jax.ShapeDtypeStruct((n_batch, 4 * HEADS, seq_len), F32),
           jax.ShapeDtypeStruct((n_batch, CONV_W - 1, CONV_DIM), F32),
           jax.ShapeDtypeStruct(c.shape, F32), jax.ShapeDtypeStruct(svt.shape, F32)],
        scratch_shapes=[pltpu.VMEM((tm + 8, CONV_DIM), F32)],
        compiler_params=_params(("arbitrary",)),
        name="p1_inproj_conv",
    )(x, g, w, b, wg, bg, cw, c, sq, sk, svt, sgate, sgatet, sm, smt)


def _p2_kernel(q_ref, k_ref, v_ref, so_ref, yconv_ref, gate_ref, x_ref, wout_ref, gmh_ref,
               hp_ref, pc_ref, pn_ref, pm_ref, c_s, m_s, y_s):
    nb, L = q_ref.shape[0], q_ref.shape[1]
    c = pl.program_id(1)

    @pl.when(c == 0)
    def _():
        c_s[...] = jnp.zeros_like(c_s)
        m_s[...] = jnp.zeros_like(m_s)

    row = lax.broadcasted_iota(jnp.int32, (L, L), 0)
    col = lax.broadcasted_iota(jnp.int32, (L, L), 1)
    causal = row >= col

    for bi in range(nb):
        gt = gate_ref[bi]
        for h in range(HEADS):
            sl = slice(h * DQK, (h + 1) * DQK)
            q = q_ref[bi, :, sl]
            k = k_ref[bi, :, sl]
            v = v_ref[bi, :, sl]
            lf_r = gt[HEADS + h:HEADS + h + 1, :]
            a_r = gt[h:h + 1, :] - gt[3 * HEADS + h:3 * HEADS + h + 1, :]
            m_prev = jnp.max(m_s[bi, h:h + 1, :], axis=1, keepdims=True)
            c_prev = c_s[bi, h]

            m_c = jnp.maximum(m_prev, jnp.max(jnp.where(causal, a_r, -jnp.inf), axis=1, keepdims=True))
            b_c = jnp.sum(jnp.where(causal, lf_r, 0.0), axis=1, keepdims=True)
            w = _dot_nt(q, k) * jnp.exp(jnp.where(causal, a_r - m_c, -jnp.inf))
            g = jnp.exp(m_prev - m_c)
            qc = _dot_nt(q, c_prev.astype(BF16))
            num = g * qc[:, 0:DV] + _dot(w.astype(BF16), v)
            den = g * qc[:, DV:2 * DV] + jnp.sum(w, axis=1, keepdims=True)
            hh = num / jnp.maximum(jnp.abs(den), jnp.exp(-(b_c + m_c)))
            hh = hh * lax.rsqrt(jnp.mean(hh * hh, axis=1, keepdims=True) + EPS) * gmh_ref[:, sl]
            y_s[bi * L:(bi + 1) * L, h * DV:(h + 1) * DV] = (so_ref[bi, :, sl].astype(F32) * hh).astype(BF16)

            m_last = jnp.maximum(m_prev, jnp.max(a_r, axis=1, keepdims=True))
            b_last = jnp.sum(lf_r, axis=1, keepdims=True)
            s_r = jnp.exp(a_r - m_last)
            sv_t = jnp.concatenate([v.T.astype(F32) * s_r, jnp.broadcast_to(s_r, (DV, L))], axis=0)
            c_s[bi, h] = jnp.exp(m_prev - m_last) * c_prev + _dot(sv_t.astype(BF16), k)
            m_s[bi, h:h + 1, :] = jnp.broadcast_to(b_last + m_last, (1, m_s.shape[2]))

    for bi in range(nb):
        out = (_dot(yconv_ref[bi], wout_ref[0:CONV_DIM, :])
               + _dot(y_s[bi * L:(bi + 1) * L, :], wout_ref[CONV_DIM:CONV_DIM + MLSTM_DIM, :]))
        hp_ref[bi] = x_ref[bi] + out

    @pl.when(c == pl.num_programs(1) - 1)
    def _():
        lane = lax.broadcasted_iota(jnp.int32, (1, m_s.shape[2]), 1)
        for bi in range(nb):
            acc = jnp.zeros((1, m_s.shape[2]), F32)
            for h in range(HEADS):
                pc_ref[bi, h] = c_s[bi, h, 0:DV, :]
                pn_ref[bi, h:h + 1, :] = c_s[bi, h, DV:DV + 1, :]
                acc = jnp.where(lane == h, m_s[bi, h:h + 1, :], acc)
            pm_ref[bi] = acc


def _p2_call(q, k, v, so, yconv, gates, x, wout, gmh, n_batch, seq_len):
    L = MLSTM_CHUNK
    nb = P2_SEQS
    nc = seq_len // L
    seq = lambda width: pl.BlockSpec((nb, L, width), lambda b, c: (b, c, 0))
    as_seq = lambda a: a.reshape(n_batch, seq_len, a.shape[-1])
    return pl.pallas_call(
        _p2_kernel,
        grid=(n_batch // nb, nc),
        in_specs=[seq(MLSTM_DIM), seq(MLSTM_DIM), seq(MLSTM_DIM), seq(MLSTM_DIM), seq(CONV_DIM),
                  pl.BlockSpec((nb, 4 * HEADS, L), lambda b, c: (b, 0, c)), seq(D_MODEL),
                  _const_spec((D_MODEL, D_MODEL)), _const_spec((1, MLSTM_DIM))],
        out_specs=[seq(D_MODEL),
                   pl.BlockSpec((nb, HEADS, DV, DQK), lambda b, c: (b, 0, 0, 0)),
                   pl.BlockSpec((nb, HEADS, DQK), lambda b, c: (b, 0, 0)),
                   pl.BlockSpec((nb, 1, 128), lambda b, c: (b, 0, 0))],
        out_shape=[jax.ShapeDtypeStruct((n_batch, seq_len, D_MODEL), F32),
                   jax.ShapeDtypeStruct((n_batch, HEADS, DV, DQK), F32),
                   jax.ShapeDtypeStruct((n_batch, HEADS, DQK), F32),
                   jax.ShapeDtypeStruct((n_batch, 1, 128), F32)],
        scratch_shapes=[pltpu.VMEM((nb, HEADS, 2 * DV, DQK), F32), pltpu.VMEM((nb, 8, 128), F32),
                        pltpu.VMEM((nb * L, MLSTM_DIM), BF16)],
        compiler_params=_params(("arbitrary", "arbitrary")),
        name="p2_mlstm_outproj",
    )(as_seq(q), as_seq(k), as_seq(v), as_seq(so), as_seq(yconv), gates, as_seq(x), wout, gmh)


def _pm_kernel(mem_ref, g_ref, w_ref, k_ref, v_ref, kb_ref, vb_ref):
    xn = _rmsnorm(mem_ref[...], g_ref[...]).astype(BF16)
    kk = _dot(xn, w_ref[:, 0:D_MODEL])
    vv = _dot(xn, w_ref[:, D_MODEL:2 * D_MODEL])
    for h in range(X_HEADS):
        sl = slice(h * X_HEAD_DIM, (h + 1) * X_HEAD_DIM)
        k_ref[:, h, :] = kk[:, sl]
        v_ref[:, h, :] = vv[:, sl]
    kb_ref[...] = kk.astype(BF16)
    vb_ref[...] = vv.astype(BF16)


def _pm_call(mem, g, w):
    rows = mem.shape[0]
    tm = ROW_TILE
    row = pl.BlockSpec((tm, D_MODEL), lambda i: (i, 0))
    row4 = pl.BlockSpec((tm, X_HEADS, X_HEAD_DIM), lambda i: (i, 0, 0))
    return pl.pallas_call(
        _pm_kernel,
        grid=(rows // tm,),
        in_specs=[row, _const_spec((1, D_MODEL)), _const_spec((D_MODEL, 2 * D_MODEL))],
        out_specs=[row4, row4, row, row],
        out_shape=[jax.ShapeDtypeStruct((rows, X_HEADS, X_HEAD_DIM), F32)] * 2
        + [jax.ShapeDtypeStruct((rows, D_MODEL), BF16)] * 2,
        compiler_params=_params(("arbitrary",)),
        name="pm_mem_kv",
    )(mem, g, w)


def _p3_kernel(hp_ref, g_ref, wq_ref, k_ref, v_ref, o_ref):
    xn = _rmsnorm(hp_ref[...], g_ref[...]).astype(BF16)
    q = _dot(xn, wq_ref[...])
    for h in range(X_HEADS):
        sl = slice(h * X_HEAD_DIM, (h + 1) * X_HEAD_DIM)
        s = _dot_nt(q[:, sl].astype(BF16), k_ref[0, :, sl]) * (X_HEAD_DIM ** -0.5)
        e = jnp.exp(s - jnp.max(s, axis=1, keepdims=True))
        p = e * (1.0 / jnp.sum(e, axis=1, keepdims=True))
        o_ref[:, sl] = _dot(p.astype(BF16), v_ref[0, :, sl]).astype(BF16)


def _p3_call(hp, g, wq, kb, vb, seq_len):
    rows = hp.shape[0]
    tm = P3_TILE
    tiles_per_batch = seq_len // tm
    row = pl.BlockSpec((tm, D_MODEL), lambda i: (i, 0))
    mem = pl.BlockSpec((1, N_MEM, D_MODEL), lambda i: (i // tiles_per_batch, 0, 0))
    return pl.pallas_call(
        _p3_kernel,
        grid=(rows // tm,),
        in_specs=[row, _const_spec((1, D_MODEL)), _const_spec((D_MODEL, D_MODEL)), mem, mem],
        out_specs=row,
        out_shape=jax.ShapeDtypeStruct((rows, D_MODEL), BF16),
        compiler_params=_params(("arbitrary",)),
        name="p3_cross_attn",
    )(hp, g, wq, kb, vb)


def _cache_attention_row(q8, kc_ref, vc_ref, bl):
    qs = q8 * (X_HEAD_DIM ** -0.5)
    m_run = jnp.full((1, PACK_ROWS, 1), -jnp.inf, F32)
    l_run = jnp.zeros((1, PACK_ROWS, 1), F32)
    acc = jnp.zeros((PACK_ROWS, 128), F32)
    for c in range(N_MEM // MEM_CHUNK):
        blk = slice(c * MEM_CHUNK, (c + 1) * MEM_CHUNK)
        prod = kc_ref[bl, blk] * qs
        s = jnp.sum(prod + pltpu.roll(prod, X_HEADS, 1), axis=-1, keepdims=True)
        m_new = jnp.maximum(m_run, jnp.max(s, axis=0, keepdims=True))
        alpha = jnp.exp(m_run - m_new)
        e = jnp.exp(s - m_new)
        l_run = alpha * l_run + jnp.sum(e, axis=0, keepdims=True)
        acc = alpha[0] * acc + jnp.sum(e * vc_ref[bl, blk], axis=0)
        m_run = m_new
    return acc * (1.0 / l_run[0])


def _p4_kernel(side_rows, hp_ref, o_ref, wxo_ref, gf_ref, wgu_ref, wd_ref, gfin_ref, *rest):
    if side_rows:
        q4_ref, kc_ref, vc_ref, y_ref, os_ref, act_s = rest
        for bl in range(side_rows):
            os_ref[bl] = _cache_attention_row(q4_ref[bl], kc_ref, vc_ref, bl)
    else:
        y_ref, act_s = rest
    hp = hp_ref[...] + _dot(o_ref[...].astype(BF16), wxo_ref[...])
    xn = _rmsnorm(hp, gf_ref[...]).astype(BF16)
    for j in range(D_FF // FF_CHUNK):
        g = _dot(xn, wgu_ref[:, FF_CHUNK * j:FF_CHUNK * (j + 1)])
        u = _dot(xn, wgu_ref[:, D_FF + FF_CHUNK * j:D_FF + FF_CHUNK * (j + 1)])
        act_s[:, FF_CHUNK * j:FF_CHUNK * (j + 1)] = (g * jax.nn.sigmoid(g) * u).astype(BF16)
    hp = hp + _dot(act_s[...], wd_ref[...])
    y_ref[...] = _rmsnorm(hp, gfin_ref[...])


def _p4_call(hp, o, wxo, gf, wgu, wd, gfin, tm, side=None):
    rows = hp.shape[0]
    steps = rows // tm
    row = pl.BlockSpec((tm, D_MODEL), lambda i: (i, 0))
    in_specs = [row, row, _const_spec((D_MODEL, D_MODEL)), _const_spec((1, D_MODEL)),
                _const_spec((D_MODEL, 2 * D_FF)), _const_spec((D_FF, D_MODEL)),
                _const_spec((1, D_MODEL))]
    out_specs = [row]
    out_shape = [jax.ShapeDtypeStruct((rows, D_MODEL), F32)]
    args = [hp, o, wxo, gf, wgu, wd, gfin]
    side_rows = 0
    if side is not None:
        q4, kc, vc = side
        side_rows = q4.shape[0] // steps
        assert side_rows * steps == q4.shape[0]
        srow = pl.BlockSpec((side_rows, PACK_ROWS, 128), lambda i: (i, 0, 0))
        cache = pl.BlockSpec((side_rows, N_MEM, PACK_ROWS, 128), lambda i: (i, 0, 0, 0))
        in_specs += [srow, cache, cache]
        out_specs += [srow]
        out_shape += [jax.ShapeDtypeStruct(q4.shape, F32)]
        args += [q4, kc, vc]
    return pl.pallas_call(
        functools.partial(_p4_kernel, side_rows),
        grid=(steps,),
        in_specs=in_specs,
        out_specs=out_specs,
        out_shape=out_shape,
        scratch_shapes=[pltpu.VMEM((tm, D_FF), BF16)],
        compiler_params=_params(("arbitrary",)),
        name="p4_ffn_final",
    )(*args)


def _s1_kernel(x_ref, g_ref, w_ref, b_ref, wg_ref, bg_ref, cw_ref, st_ref,
               yconv_ref, sconv_ref, q_ref, k_ref, v_ref, so_ref, gate_ref, vt_ref, gatet_ref):
    xn = _rmsnorm(x_ref[...], g_ref[...]).astype(BF16)

    def seg(j):
        sl = slice(j * CONV_DIM, (j + 1) * CONV_DIM)
        return _dot_nt(xn, w_ref[sl, :]) + b_ref[:, sl]

    u = seg(1) * seg(2)
    st0 = st_ref[:, 0:CONV_DIM]
    st1 = st_ref[:, CONV_DIM:2 * CONV_DIM]
    conv = cw_ref[0:1, :] * st0 + cw_ref[1:2, :] * st1 + cw_ref[2:3, :] * u
    yconv_ref[...] = seg(0) * conv
    sconv_ref[:, 0:CONV_DIM] = st1
    sconv_ref[:, CONV_DIM:2 * CONV_DIM] = u
    q_ref[...] = seg(3)
    k_ref[...] = seg(4) * (DQK ** -0.5)
    v = seg(5)
    v_ref[...] = v
    so_ref[...] = jax.nn.sigmoid(seg(6))
    n_gate = wg_ref.shape[0]
    gate_ref[...] = _gate_transform(_dot_nt(xn, wg_ref[...]) + b_ref[:, MAIN_DIM:MAIN_DIM + n_gate])
    for h in range(HEADS):
        sl = slice(h * DV, (h + 1) * DV)
        vt_ref[sl, :] = v[:, sl].T
    gatet_ref[...] = _gate_transform_rows(_dot_nt(wg_ref[...], xn) + bg_ref[...])


def _s1_call(x, g, w, b, wg, bg, cw, st):
    n = x.shape[0]
    n_gate = wg.shape[0]
    full = lambda *shape: pl.BlockSpec(shape, lambda i: (0,) * len(shape))
    ins = [x, g, w, b, wg, bg, cw, st]
    return pl.pallas_call(
        _s1_kernel,
        grid=(1,),
        in_specs=[full(*a.shape) for a in ins],
        out_specs=[full(n, CONV_DIM), full(n, 2 * CONV_DIM), full(n, MLSTM_DIM), full(n, MLSTM_DIM),
                   full(n, MLSTM_DIM), full(n, MLSTM_DIM), full(n, n_gate), full(MLSTM_DIM, n),
                   full(n_gate, n)],
        out_shape=[jax.ShapeDtypeStruct((n, CONV_DIM), F32),
                   jax.ShapeDtypeStruct((n, 2 * CONV_DIM), F32)]
        + [jax.ShapeDtypeStruct((n, MLSTM_DIM), F32)] * 4
        + [jax.ShapeDtypeStruct((n, n_gate), F32),
           jax.ShapeDtypeStruct((MLSTM_DIM, n), F32),
           jax.ShapeDtypeStruct((n_gate, n), F32)],
        compiler_params=_params(("arbitrary",)),
        name="s1_inproj_conv",
    )(*ins)


def _s3_kernel(cqt_ref, q_ref, k_ref, v_ref, so_ref, gate_ref, n_ref, m_ref, yconv_ref, x_ref,
               wout_ref, gmh_ref, gx_ref, wq_ref,
               hs_ref, qx_ref, nn_ref, mn_ref, y_s):
    n_rows = q_ref.shape[0]
    y_s[:, 0:CONV_DIM] = yconv_ref[...].astype(BF16)
    lane = lax.broadcasted_iota(jnp.int32, (n_rows, mn_ref.shape[1]), 1)
    m_out = jnp.zeros((n_rows, mn_ref.shape[1]), F32)
    for h in range(HEADS):
        sl = slice(h * DQK, (h + 1) * DQK)
        q = q_ref[:, sl]
        k = k_ref[:, sl]
        v = v_ref[:, sl]
        n_prev = n_ref[:, sl]
        cq = cqt_ref[sl, :].T
        ig = gate_ref[:, h:h + 1]
        lf = gate_ref[:, HEADS + h:HEADS + h + 1]
        m_prev = m_ref[:, h:h + 1]
        inter = lf + m_prev
        m_row = jnp.maximum(inter, ig)
        wgt = jnp.sum(q * k, axis=1, keepdims=True) * jnp.exp(ig - m_row)
        g = jnp.exp(inter - m_row)
        num = g * cq + wgt * v
        den = g * jnp.sum(n_prev * q, axis=1, keepdims=True) + wgt
        hh = num / jnp.maximum(jnp.abs(den), jnp.exp(-m_row))
        hh = hh * lax.rsqrt(jnp.mean(hh * hh, axis=1, keepdims=True) + EPS) * gmh_ref[:, sl]
        y_s[:, CONV_DIM + h * DV:CONV_DIM + (h + 1) * DV] = (so_ref[:, sl] * hh).astype(BF16)
        nn_ref[:, sl] = g * n_prev + jnp.exp(ig - m_row) * k
        m_out = jnp.where(lane == h, m_row, m_out)
    mn_ref[...] = m_out
    hs = x_ref[...] + _dot(y_s[...], wout_ref[...])
    hs_ref[...] = hs
    qx_ref[...] = _dot(_rmsnorm(hs, gx_ref[...]).astype(BF16), wq_ref[...])


def _s3_call(cqt, q, k, v, so, gates, nst, m, yconv, x, wout, gmh, gx, wq):
    n = q.shape[0]
    full = lambda *shape: pl.BlockSpec(shape, lambda i: (0,) * len(shape))
    ins = [cqt, q, k, v, so, gates, nst, m, yconv, x, wout, gmh, gx, wq]
    return pl.pallas_call(
        _s3_kernel,
        grid=(1,),
        in_specs=[full(*a.shape) for a in ins],
        out_specs=[full(n, D_MODEL), full(n, D_MODEL), full(n, MLSTM_DIM), full(n, 128)],
        out_shape=[jax.ShapeDtypeStruct((n, D_MODEL), F32), jax.ShapeDtypeStruct((n, D_MODEL), F32),
                   jax.ShapeDtypeStruct((n, MLSTM_DIM), F32), jax.ShapeDtypeStruct((n, 128), F32)],
        scratch_shapes=[pltpu.VMEM((n, D_MODEL), BF16)],
        compiler_params=_params(("arbitrary",)),
        name="s3_mlstm_finish",
    )(*ins)


def _pack_heads(a):
    lead = a.shape[:-2]
    a = a.reshape(lead + (X_HEADS, X_HEAD_DIM // 128, 128))
    return jnp.swapaxes(a, -3, -2).reshape(lead + (PACK_ROWS, 128))


def _unpack_heads(a):
    lead = a.shape[:-2]
    a = a.reshape(lead + (X_HEAD_DIM // 128, X_HEADS, 128))
    return jnp.swapaxes(a, -3, -2).reshape(lead + (X_HEADS, X_HEAD_DIM))


def kernel(x_prompt, x_sample, mem_prompt, state_conv, state_mlstm_C, state_mlstm_n, state_mlstm_m,
           cache_mem_k, cache_mem_v, g_mix, w_in, b_in, conv_w, g_mh, w_out, g_cross, g_mem,
           w_xq, w_xkv, w_xo, g_ffn, w_gu, w_down, g_final):
    n_batch, seq_len, _ = x_prompt.shape
    n_dec = x_sample.shape[0]
    depth = w_in.shape[0]
    assert depth == 1 and x_sample.shape[1] == 1
    assert all(seq_len % t == 0 for t in (ROW_TILE, P1_TILE, P3_TILE))
    assert P1_TILE % MLSTM_CHUNK == 0 and n_batch % P2_SEQS == 0

    n_gate = 2 * HEADS
    w_in_b = w_in[0].T.astype(BF16)
    b_in_r = b_in[0].reshape(1, MAIN_DIM + n_gate)
    w_gate_r = w_in_b[MAIN_DIM:]
    b_gate_r = b_in[0, MAIN_DIM:].reshape(n_gate, 1)
    w_out_b = w_out[0].astype(BF16)
    w_xq_b = w_xq[0].astype(BF16)
    w_xkv_b = w_xkv[0].astype(BF16)
    w_xo_b = w_xo[0].astype(BF16)
    w_gu_b = w_gu[0].astype(BF16)
    w_down_b = w_down[0].astype(BF16)
    g_mix_r = g_mix[0].reshape(1, D_MODEL)
    g_cross_r = g_cross[0].reshape(1, D_MODEL)
    g_mem_r = g_mem[0].reshape(1, D_MODEL)
    g_ffn_r = g_ffn[0].reshape(1, D_MODEL)
    g_final_r = g_final.reshape(1, D_MODEL)
    g_mh_r = g_mh[0].reshape(1, MLSTM_DIM)
    cw = conv_w[0]

    xs = x_sample.reshape(n_dec, D_MODEL)
    st = state_conv[0].reshape(n_dec, (CONV_W - 1) * CONV_DIM)
    m0 = state_mlstm_m[0]
    s_yconv, s_conv, sq, sk, sv, sso, sgates, svt, sgates_t = _s1_call(
        xs, g_mix_r, w_in_b, b_in_r, w_gate_r, b_gate_r, cw, st)

    xp = x_prompt.reshape(n_batch * seq_len, D_MODEL)
    yconv, q, k, v, so, gates, p_conv, s_c, cqt = _p1_call(
        xp, g_mix_r, w_in_b, b_in_r, w_gate_r, b_gate_r, cw, seq_len,
        side=(state_mlstm_C[0], sq, sk, svt, sgates, sgates_t, m0, m0.T))
    hs1, qx, s_n, s_m = _s3_call(cqt, sq, sk, sv, sso, sgates,
                                 state_mlstm_n[0].reshape(n_dec, MLSTM_DIM), m0, s_yconv, xs,
                                 w_out_b, g_mh_r, g_cross_r, w_xq_b)
    hp1, p_c, p_n, p_m = _p2_call(q, k, v, so, yconv, gates, xp, w_out_b, g_mh_r, n_batch, seq_len)
    hp1 = hp1.reshape(n_batch * seq_len, D_MODEL)
    pk, pv, pkb, pvb = _pm_call(mem_prompt.reshape(n_batch * N_MEM, D_MODEL), g_mem_r, w_xkv_b)
    o_p = _p3_call(hp1, g_cross_r, w_xq_b, pkb.reshape(n_batch, N_MEM, D_MODEL),
                   pvb.reshape(n_batch, N_MEM, D_MODEL), seq_len)
    y_p, o_s = _p4_call(hp1, o_p, w_xo_b, g_ffn_r, w_gu_b, w_down_b, g_final_r, ROW_TILE,
                        side=(_pack_heads(qx.reshape(n_dec, X_HEADS, X_HEAD_DIM)),
                              _pack_heads(cache_mem_k[0]), _pack_heads(cache_mem_v[0])))
    o_s = _unpack_heads(o_s)

    y_s, = _p4_call(hs1, o_s.reshape(n_dec, D_MODEL), w_xo_b, g_ffn_r, w_gu_b, w_down_b, g_final_r, n_dec)

    mem_shape = (1, n_batch, N_MEM, X_HEADS, X_HEAD_DIM)
    return (y_p.reshape(n_batch, seq_len, D_MODEL),
            y_s.reshape(n_dec, 1, D_MODEL),
            p_conv.reshape(1, n_batch, CONV_W - 1, CONV_DIM),
            p_c.reshape(1, n_batch, HEADS, DV, DQK),
            p_n.reshape(1, n_batch, HEADS, DQK),
            p_m[:, 0, :HEADS].reshape(1, n_batch, HEADS),
            pk.reshape(mem_shape),
            pv.reshape(mem_shape),
            s_conv.reshape(1, n_dec, CONV_W - 1, CONV_DIM),
            s_c.reshape(1, n_dec, HEADS, DV, DQK),
            s_n.reshape(1, n_dec, HEADS, DQK),
            s_m[:, :HEADS].reshape(1, n_dec, HEADS))
```

```python
import functools

import jax
import jax.numpy as jnp
from jax import lax
from jax.experimental import pallas as pl
from jax.experimental.pallas import tpu as pltpu

F32 = jnp.float32
BF16 = jnp.bfloat16

D_MODEL = 1024
CONV_DIM = 512
CONV_W = 3
MLSTM_DIM = 512
HEADS = 4
DQK = 128
DV = 128
N_MEM = 256
X_HEADS = 4
X_HEAD_DIM = 256
D_FF = 2816
MAIN_DIM = 3 * CONV_DIM + 4 * MLSTM_DIM
EPS = 1e-6

MLSTM_CHUNK = 256
P2_SEQS = 4
ROW_TILE = 512
P1_TILE = 1024
P3_TILE = 1024
FF_CHUNK = 256
PACK_ROWS = X_HEADS * (X_HEAD_DIM // 128)
MEM_CHUNK = 32
VMEM_LIMIT = 56 * 1024 * 1024


def _dot(a, b):
    return jnp.dot(a, b, preferred_element_type=F32)


def _dot_nt(a, b):
    return lax.dot_general(a, b, (((1,), (1,)), ((), ())), preferred_element_type=F32)


def _rmsnorm(x, g):
    return x * lax.rsqrt(jnp.mean(x * x, axis=-1, keepdims=True) + EPS) * g


def _const_spec(shape):
    zeros = (0,) * len(shape)
    return pl.BlockSpec(shape, lambda *_: zeros, pipeline_mode=pl.Buffered(1))


def _params(sem, flags=None):
    return pltpu.CompilerParams(dimension_semantics=sem, vmem_limit_bytes=VMEM_LIMIT, flags=flags)


def _gate_transform(gt):
    lane = lax.broadcasted_iota(jnp.int32, gt.shape, 1)
    return jnp.where(lane < HEADS, gt, jax.nn.log_sigmoid(gt))


def _gate_transform_rows(gt):
    sub = lax.broadcasted_iota(jnp.int32, gt.shape, 0)
    return jnp.where(sub < HEADS, gt, jax.nn.log_sigmoid(gt))


def _memory_update_rows(i, c_ref, q_ref, k_ref, vt_ref, gate_ref, gatet_ref, m_ref, mt_ref, cn_ref, cqt_ref):
    n = vt_ref.shape[1]
    bb = c_ref.shape[0]

    @pl.when(i == 0)
    def _():
        cqt_ref[...] = jnp.zeros_like(cqt_ref)

    lane = lax.broadcasted_iota(jnp.int32, (DV, n), 1)
    for h in range(HEADS):
        sl = slice(h * DQK, (h + 1) * DQK)
        ig_c = gate_ref[:, h:h + 1]
        lf_c = gate_ref[:, HEADS + h:HEADS + h + 1]
        m_c = m_ref[:, h:h + 1]
        dec = jnp.broadcast_to(jnp.exp(lf_c + m_c - jnp.maximum(lf_c + m_c, ig_c)), (bb, DQK))
        ig_r = gatet_ref[h:h + 1, :]
        lf_r = gatet_ref[HEADS + h:HEADS + h + 1, :]
        m_r = mt_ref[h:h + 1, :]
        svt = vt_ref[sl, :] * jnp.exp(ig_r - jnp.maximum(lf_r + m_r, ig_r))
        q_t = q_ref[:, sl]
        k_t = k_ref[:, sl]
        cqt = cqt_ref[sl, :]
        for bl in range(bb):
            onehot = lane == i * bb + bl
            c = c_ref[bl, h]
            cq_col = jnp.sum(c * q_t[bl:bl + 1, :], axis=1, keepdims=True)
            sv_col = jnp.sum(jnp.where(onehot, svt, 0.0), axis=1, keepdims=True)
            cn_ref[bl, h] = dec[bl:bl + 1, :] * c + sv_col * k_t[bl:bl + 1, :]
            cqt = jnp.where(onehot, cq_col, cqt)
        cqt_ref[sl, :] = cqt


def _p1_kernel(tiles_per_batch, x_ref, g_ref, w_ref, b_ref, wg_ref, bg_ref, cw_ref,
               c_ref, sq_ref, sk_ref, svt_ref, sgate_ref, sgatet_ref, sm_ref, smt_ref,
               yconv_ref, q_ref, k_ref, v_ref, so_ref, gate_ref, pconv_ref, cn_ref, cqt_ref, ubuf):
    tm = x_ref.shape[0]
    i = pl.program_id(0)
    _memory_update_rows(i, c_ref, sq_ref, sk_ref, svt_ref, sgate_ref, sgatet_ref, sm_ref, smt_ref,
                        cn_ref, cqt_ref)
    xn = _rmsnorm(x_ref[...], g_ref[...]).astype(BF16)

    def seg(j):
        sl = slice(j * CONV_DIM, (j + 1) * CONV_DIM)
        return _dot_nt(xn, w_ref[sl, :].astype(BF16)) + b_ref[:, sl]

    prev = ubuf[tm:tm + 8, :]
    ubuf[0:8, :] = jnp.where(i % tiles_per_batch == 0, jnp.zeros_like(prev), prev)
    ubuf[8:8 + tm, :] = seg(1) * seg(2)
    conv = (cw_ref[0:1, :] * ubuf[6:6 + tm, :] + cw_ref[1:2, :] * ubuf[7:7 + tm, :]
            + cw_ref[2:3, :] * ubuf[8:8 + tm, :])
    yconv_ref[...] = (seg(0) * conv).astype(BF16)
    pconv_ref[0] = ubuf[tm + 6:tm + 8, :]

    q_ref[...] = seg(3).astype(BF16)
    k_ref[...] = (seg(4) * (DQK ** -0.5)).astype(BF16)
    v_ref[...] = seg(5).astype(BF16)
    so_ref[...] = jax.nn.sigmoid(seg(6)).astype(BF16)
    gt = _gate_transform_rows(_dot_nt(wg_ref[...], xn) + bg_ref[...])
    n_gate = gt.shape[0]
    gate_ref[0, 0:n_gate, :] = gt
    L = MLSTM_CHUNK
    n_blk = tm // L
    hi = gt.astype(BF16).astype(F32)
    r1 = gt - hi
    mid = r1.astype(BF16).astype(F32)
    lo = r1 - mid
    terms = jnp.concatenate([t[:, j * L:(j + 1) * L] for t in (hi, mid, lo) for j in range(n_blk)], axis=0)
    tri = (lax.broadcasted_iota(jnp.int32, (L, L), 0) <= lax.broadcasted_iota(jnp.int32, (L, L), 1)).astype(BF16)
    parts = _dot(terms.astype(BF16), tri)
    for j in range(n_blk):
        rows = [parts[(t * n_blk + j) * n_gate:(t * n_blk + j + 1) * n_gate, :] for t in range(3)]
        gate_ref[0, n_gate:2 * n_gate, j * L:(j + 1) * L] = (rows[0] + rows[1]) + rows[2]


def _p1_call(x, g, w, b, wg, bg, cw, seq_len, side):
    rows = x.shape[0]
    tm = P1_TILE
    steps = rows // tm
    tiles_per_batch = seq_len // tm
    n_batch = rows // seq_len
    row = lambda width: pl.BlockSpec((tm, width), lambda i: (i, 0))
    c, sq, sk, svt, sgate, sgatet, sm, smt = side
    n = sq.shape[0]
    sr = n // steps
    assert sr * steps == n and sr % 8 == 0
    full = lambda a: pl.BlockSpec(a.shape, lambda i: (0,) * a.ndim)
    srow = lambda a: pl.BlockSpec((sr,) + a.shape[1:], lambda i: (i,) + (0,) * (a.ndim - 1))
    return pl.pallas_call(
        functools.partial(_p1_kernel, tiles_per_batch),
        grid=(steps,),
        in_specs=[row(D_MODEL), _const_spec((1, D_MODEL)), _const_spec(w.shape),
                  _const_spec(b.shape), _const_spec((2 * HEADS, D_MODEL)),
                  _const_spec((2 * HEADS, 1)), _const_spec((CONV_W, CONV_DIM)),
                  srow(c), srow(sq), srow(sk), full(svt), srow(sgate), full(sgatet), srow(sm), full(smt)],
        out_specs=[row(CONV_DIM), row(MLSTM_DIM), row(MLSTM_DIM), row(MLSTM_DIM), row(MLSTM_DIM),
                   pl.BlockSpec((1, 4 * HEADS, tm), lambda i: (i // tiles_per_batch, 0, i % tiles_per_batch)),
                   pl.BlockSpec((1, CONV_W - 1, CONV_DIM), lambda i: (i // tiles_per_batch, 0, 0)),
                   srow(c), full(svt)],
        out_shape=[jax.ShapeDtypeStruct((rows, CONV_DIM), BF16)]
        + [jax.ShapeDtypeStruct((rows, MLSTM_DIM), BF16)] * 4
        + [jax.ShapeDtypeStruct((n_batch, 4 * HEADS, seq_len), F32),
           jax.ShapeDtypeStruct((n_batch, CONV_W - 1, CONV_DIM), F32),
           jax.ShapeDtypeStruct(c.shape, F32), jax.ShapeDtypeStruct(svt.shape, F32)],
        scratch_shapes=[pltpu.VMEM((tm + 8, CONV_DIM), F32)],
        compiler_params=_params(("arbitrary",)),
        name="p1_inproj_conv",
    )(x, g, w, b, wg, bg, cw, c, sq, sk, svt, sgate, sgatet, sm, smt)


def _p2_kernel(q_ref, k_ref, v_ref, so_ref, yconv_ref, gate_ref, x_ref, wout_ref, gmh_ref,
               hp_ref, pc_ref, pn_ref, pm_ref, c_s, m_s, y_s):
    nb, L = q_ref.shape[0], q_ref.shape[1]
    c = pl.program_id(1)

    @pl.when(c == 0)
    def _():
        c_s[...] = jnp.zeros_like(c_s)
        m_s[...] = jnp.zeros_like(m_s)

    row = lax.broadcasted_iota(jnp.int32, (L, L), 0)
    col = lax.broadcasted_iota(jnp.int32, (L, L), 1)
    causal = row >= col

    for bi in range(nb):
        gt = gate_ref[bi]
        for h in range(HEADS):
            sl = slice(h * DQK, (h + 1) * DQK)
            q = q_ref[bi, :, sl]
            k = k_ref[bi, :, sl]
            v = v_ref[bi, :, sl]
            lf_r = gt[HEADS + h:HEADS + h + 1, :]
            a_r = gt[h:h + 1, :] - gt[3 * HEADS + h:3 * HEADS + h + 1, :]
            m_prev = jnp.max(m_s[bi, h:h + 1, :], axis=1, keepdims=True)
            c_prev = c_s[bi, h]

            m_c = jnp.maximum(m_prev, jnp.max(jnp.where(causal, a_r, -jnp.inf), axis=1, keepdims=True))
            b_c = jnp.sum(jnp.where(causal, lf_r, 0.0), axis=1, keepdims=True)
            w = _dot_nt(q, k) * jnp.exp(jnp.where(causal, a_r - m_c, -jnp.inf))
            g = jnp.exp(m_prev - m_c)
            qc = _dot_nt(q, c_prev.astype(BF16))
            num = g * qc[:, 0:DV] + _dot(w.astype(BF16), v)
            den = g * qc[:, DV:2 * DV] + jnp.sum(w, axis=1, keepdims=True)
            hh = num / jnp.maximum(jnp.abs(den), jnp.exp(-(b_c + m_c)))
            hh = hh * lax.rsqrt(jnp.mean(hh * hh, axis=1, keepdims=True) + EPS) * gmh_ref[:, sl]
            y_s[bi * L:(bi + 1) * L, h * DV:(h + 1) * DV] = (so_ref[bi, :, sl].astype(F32) * hh).astype(BF16)

            m_last = jnp.maximum(m_prev, jnp.max(a_r, axis=1, keepdims=True))
            b_last = jnp.sum(lf_r, axis=1, keepdims=True)
            s_r = jnp.exp(a_r - m_last)
            sv_t = jnp.concatenate([v.T.astype(F32) * s_r, jnp.broadcast_to(s_r, (DV, L))], axis=0)
            c_s[bi, h] = jnp.exp(m_prev - m_last) * c_prev + _dot(sv_t.astype(BF16), k)
            m_s[bi, h:h + 1, :] = jnp.broadcast_to(b_last + m_last, (1, m_s.shape[2]))

    w_conv = wout_ref[0:CONV_DIM, :].astype(BF16)
    w_ml = wout_ref[CONV_DIM:CONV_DIM + MLSTM_DIM, :].astype(BF16)
    for bi in range(nb):
        out = _dot(yconv_ref[bi], w_conv) + _dot(y_s[bi * L:(bi + 1) * L, :], w_ml)
        hp_ref[bi] = x_ref[bi] + out

    @pl.when(c == pl.num_programs(1) - 1)
    def _():
        lane = lax.broadcasted_iota(jnp.int32, (1, m_s.shape[2]), 1)
        for bi in range(nb):
            acc = jnp.zeros((1, m_s.shape[2]), F32)
            for h in range(HEADS):
                pc_ref[bi, h] = c_s[bi, h, 0:DV, :]
                pn_ref[bi, h:h + 1, :] = c_s[bi, h, DV:DV + 1, :]
                acc = jnp.where(lane == h, m_s[bi, h:h + 1, :], acc)
            pm_ref[bi] = acc


def _p2_call(q, k, v, so, yconv, gates, x, wout, gmh, n_batch, seq_len):
    L = MLSTM_CHUNK
    nb = P2_SEQS
    nc = seq_len // L
    seq = lambda width: pl.BlockSpec((nb, L, width), lambda b, c: (b, c, 0))
    as_seq = lambda a: a.reshape(n_batch, seq_len, a.shape[-1])
    return pl.pallas_call(
        _p2_kernel,
        grid=(n_batch // nb, nc),
        in_specs=[seq(MLSTM_DIM), seq(MLSTM_DIM), seq(MLSTM_DIM), seq(MLSTM_DIM), seq(CONV_DIM),
                  pl.BlockSpec((nb, 4 * HEADS, L), lambda b, c: (b, 0, c)), seq(D_MODEL),
                  _const_spec((D_MODEL, D_MODEL)), _const_spec((1, MLSTM_DIM))],
        out_specs=[seq(D_MODEL),
                   pl.BlockSpec((nb, HEADS, DV, DQK), lambda b, c: (b, 0, 0, 0)),
                   pl.BlockSpec((nb, HEADS, DQK), lambda b, c: (b, 0, 0)),
                   pl.BlockSpec((nb, 1, 128), lambda b, c: (b, 0, 0))],
        out_shape=[jax.ShapeDtypeStruct((n_batch, seq_len, D_MODEL), F32),
                   jax.ShapeDtypeStruct((n_batch, HEADS, DV, DQK), F32),
                   jax.ShapeDtypeStruct((n_batch, HEADS, DQK), F32),
                   jax.ShapeDtypeStruct((n_batch, 1, 128), F32)],
        scratch_shapes=[pltpu.VMEM((nb, HEADS, 2 * DV, DQK), F32), pltpu.VMEM((nb, 8, 128), F32),
                        pltpu.VMEM((nb * L, MLSTM_DIM), BF16)],
        compiler_params=_params(("arbitrary", "arbitrary")),
        name="p2_mlstm_outproj",
    )(as_seq(q), as_seq(k), as_seq(v), as_seq(so), as_seq(yconv), gates, as_seq(x), wout, gmh)


def _pm_kernel(mem_ref, g_ref, w_ref, k_ref, v_ref, kb_ref, vb_ref):
    xn = _rmsnorm(mem_ref[...], g_ref[...]).astype(BF16)
    kk = _dot(xn, w_ref[:, 0:D_MODEL].astype(BF16))
    vv = _dot(xn, w_ref[:, D_MODEL:2 * D_MODEL].astype(BF16))
    for h in range(X_HEADS):
        sl = slice(h * X_HEAD_DIM, (h + 1) * X_HEAD_DIM)
        k_ref[:, h, :] = kk[:, sl]
        v_ref[:, h, :] = vv[:, sl]
    kb_ref[...] = kk.astype(BF16)
    vb_ref[...] = vv.astype(BF16)


def _pm_call(mem, g, w):
    rows = mem.shape[0]
    tm = ROW_TILE
    row = pl.BlockSpec((tm, D_MODEL), lambda i: (i, 0))
    row4 = pl.BlockSpec((tm, X_HEADS, X_HEAD_DIM), lambda i: (i, 0, 0))
    return pl.pallas_call(
        _pm_kernel,
        grid=(rows // tm,),
        in_specs=[row, _const_spec((1, D_MODEL)), _const_spec((D_MODEL, 2 * D_MODEL))],
        out_specs=[row4, row4, row, row],
        out_shape=[jax.ShapeDtypeStruct((rows, X_HEADS, X_HEAD_DIM), F32)] * 2
        + [jax.ShapeDtypeStruct((rows, D_MODEL), BF16)] * 2,
        compiler_params=_params(("arbitrary",)),
        name="pm_mem_kv",
    )(mem, g, w)


def _p3_kernel(hp_ref, g_ref, wq_ref, k_ref, v_ref, o_ref):
    xn = _rmsnorm(hp_ref[...], g_ref[...]).astype(BF16)
    q = _dot(xn, wq_ref[...].astype(BF16))
    for h in range(X_HEADS):
        sl = slice(h * X_HEAD_DIM, (h + 1) * X_HEAD_DIM)
        s = _dot_nt(q[:, sl].astype(BF16), k_ref[0, :, sl]) * (X_HEAD_DIM ** -0.5)
        e = jnp.exp(s - jnp.max(s, axis=1, keepdims=True))
        p = e * (1.0 / jnp.sum(e, axis=1, keepdims=True))
        o_ref[:, sl] = _dot(p.astype(BF16), v_ref[0, :, sl]).astype(BF16)


def _p3_call(hp, g, wq, kb, vb, seq_len):
    rows = hp.shape[0]
    tm = P3_TILE
    tiles_per_batch = seq_len // tm
    row = pl.BlockSpec((tm, D_MODEL), lambda i: (i, 0))
    mem = pl.BlockSpec((1, N_MEM, D_MODEL), lambda i: (i // tiles_per_batch, 0, 0))
    return pl.pallas_call(
        _p3_kernel,
        grid=(rows // tm,),
        in_specs=[row, _const_spec((1, D_MODEL)), _const_spec((D_MODEL, D_MODEL)), mem, mem],
        out_specs=row,
        out_shape=jax.ShapeDtypeStruct((rows, D_MODEL), BF16),
        compiler_params=_params(("arbitrary",)),
        name="p3_cross_attn",
    )(hp, g, wq, kb, vb)


def _cache_attention_row(q8, kc_ref, vc_ref, bl):
    qs = q8 * (X_HEAD_DIM ** -0.5)
    m_run = jnp.full((1, PACK_ROWS, 1), -jnp.inf, F32)
    l_run = jnp.zeros((1, PACK_ROWS, 1), F32)
    acc = jnp.zeros((PACK_ROWS, 128), F32)
    for c in range(N_MEM // MEM_CHUNK):
        blk = slice(c * MEM_CHUNK, (c + 1) * MEM_CHUNK)
        prod = kc_ref[bl, blk] * qs
        s = jnp.sum(prod + pltpu.roll(prod, X_HEADS, 1), axis=-1, keepdims=True)
        m_new = jnp.maximum(m_run, jnp.max(s, axis=0, keepdims=True))
        alpha = jnp.exp(m_run - m_new)
        e = jnp.exp(s - m_new)
        l_run = alpha * l_run + jnp.sum(e, axis=0, keepdims=True)
        acc = alpha[0] * acc + jnp.sum(e * vc_ref[bl, blk], axis=0)
        m_run = m_new
    return acc * (1.0 / l_run[0])


def _p4_kernel(side_rows, hp_ref, o_ref, wxo_ref, gf_ref, wgu_ref, wd_ref, gfin_ref, *rest):
    if side_rows:
        q4_ref, kc_ref, vc_ref, y_ref, os_ref, act_s = rest
        for bl in range(side_rows):
            os_ref[bl] = _cache_attention_row(q4_ref[bl], kc_ref, vc_ref, bl)
    else:
        y_ref, act_s = rest
    hp = hp_ref[...] + _dot(o_ref[...].astype(BF16), wxo_ref[...])
    xn = _rmsnorm(hp, gf_ref[...]).astype(BF16)
    for j in range(D_FF // FF_CHUNK):
        g = _dot(xn, wgu_ref[:, FF_CHUNK * j:FF_CHUNK * (j + 1)])
        u = _dot(xn, wgu_ref[:, D_FF + FF_CHUNK * j:D_FF + FF_CHUNK * (j + 1)])
        act_s[:, FF_CHUNK * j:FF_CHUNK * (j + 1)] = (g * jax.nn.sigmoid(g) * u).astype(BF16)
    hp = hp + _dot(act_s[...], wd_ref[...])
    y_ref[...] = _rmsnorm(hp, gfin_ref[...])


def _p4_call(hp, o, wxo, gf, wgu, wd, gfin, tm, side=None):
    rows = hp.shape[0]
    steps = rows // tm
    row = pl.BlockSpec((tm, D_MODEL), lambda i: (i, 0))
    in_specs = [row, row, _const_spec((D_MODEL, D_MODEL)), _const_spec((1, D_MODEL)),
                _const_spec((D_MODEL, 2 * D_FF)), _const_spec((D_FF, D_MODEL)),
                _const_spec((1, D_MODEL))]
    out_specs = [row]
    out_shape = [jax.ShapeDtypeStruct((rows, D_MODEL), F32)]
    args = [hp, o, wxo, gf, wgu, wd, gfin]
    side_rows = 0
    if side is not None:
        q4, kc, vc = side
        side_rows = q4.shape[0] // steps
        assert side_rows * steps == q4.shape[0]
        srow = pl.BlockSpec((side_rows, PACK_ROWS, 128), lambda i: (i, 0, 0))
        cache = pl.BlockSpec((side_rows, N_MEM, PACK_ROWS, 128), lambda i: (i, 0, 0, 0))
        in_specs += [srow, cache, cache]
        out_specs += [srow]
        out_shape += [jax.ShapeDtypeStruct(q4.shape, F32)]
        args += [q4, kc, vc]
    return pl.pallas_call(
        functools.partial(_p4_kernel, side_rows),
        grid=(steps,),
        in_specs=in_specs,
        out_specs=out_specs,
        out_shape=out_shape,
        scratch_shapes=[pltpu.VMEM((tm, D_FF), BF16)],
        compiler_params=_params(("arbitrary",)),
        name="p4_ffn_final",
    )(*args)


def _s1_kernel(x_ref, g_ref, w_ref, b_ref, wg_ref, bg_ref, cw_ref, st_ref,
               yconv_ref, sconv_ref, q_ref, k_ref, v_ref, so_ref, gate_ref, vt_ref, gatet_ref):
    xn = _rmsnorm(x_ref[...], g_ref[...]).astype(BF16)

    def seg(j):
        sl = slice(j * CONV_DIM, (j + 1) * CONV_DIM)
        return _dot_nt(xn, w_ref[sl, :].astype(BF16)) + b_ref[:, sl]

    u = seg(1) * seg(2)
    st0 = st_ref[:, 0:CONV_DIM]
    st1 = st_ref[:, CONV_DIM:2 * CONV_DIM]
    conv = cw_ref[0:1, :] * st0 + cw_ref[1:2, :] * st1 + cw_ref[2:3, :] * u
    yconv_ref[...] = seg(0) * conv
    sconv_ref[:, 0:CONV_DIM] = st1
    sconv_ref[:, CONV_DIM:2 * CONV_DIM] = u
    q_ref[...] = seg(3)
    k_ref[...] = seg(4) * (DQK ** -0.5)
    v = seg(5)
    v_ref[...] = v
    so_ref[...] = jax.nn.sigmoid(seg(6))
    n_gate = wg_ref.shape[0]
    gate_ref[...] = _gate_transform(_dot_nt(xn, wg_ref[...]) + b_ref[:, MAIN_DIM:MAIN_DIM + n_gate])
    for h in range(HEADS):
        sl = slice(h * DV, (h + 1) * DV)
        vt_ref[sl, :] = v[:, sl].T
    gatet_ref[...] = _gate_transform_rows(_dot_nt(wg_ref[...], xn) + bg_ref[...])


def _s1_call(x, g, w, b, wg, bg, cw, st):
    n = x.shape[0]
    n_gate = wg.shape[0]
    full = lambda *shape: pl.BlockSpec(shape, lambda i: (0,) * len(shape))
    ins = [x, g, w, b, wg, bg, cw, st]
    return pl.pallas_call(
        _s1_kernel,
        grid=(1,),
        in_specs=[full(*a.shape) for a in ins],
        out_specs=[full(n, CONV_DIM), full(n, 2 * CONV_DIM), full(n, MLSTM_DIM), full(n, MLSTM_DIM),
                   full(n, MLSTM_DIM), full(n, MLSTM_DIM), full(n, n_gate), full(MLSTM_DIM, n),
                   full(n_gate, n)],
        out_shape=[jax.ShapeDtypeStruct((n, CONV_DIM), F32),
                   jax.ShapeDtypeStruct((n, 2 * CONV_DIM), F32)]
        + [jax.ShapeDtypeStruct((n, MLSTM_DIM), F32)] * 4
        + [jax.ShapeDtypeStruct((n, n_gate), F32),
           jax.ShapeDtypeStruct((MLSTM_DIM, n), F32),
           jax.ShapeDtypeStruct((n_gate, n), F32)],
        compiler_params=_params(("arbitrary",)),
        name="s1_inproj_conv",
    )(*ins)


def _s3_kernel(cqt_ref, q_ref, k_ref, v_ref, so_ref, gate_ref, n_ref, m_ref, yconv_ref, x_ref,
               wout_ref, gmh_ref, gx_ref, wq_ref,
               hs_ref, qx_ref, nn_ref, mn_ref, y_s):
    n_rows = q_ref.shape[0]
    y_s[:, 0:CONV_DIM] = yconv_ref[...].astype(BF16)
    lane = lax.broadcasted_iota(jnp.int32, (n_rows, mn_ref.shape[1]), 1)
    m_out = jnp.zeros((n_rows, mn_ref.shape[1]), F32)
    for h in range(HEADS):
        sl = slice(h * DQK, (h + 1) * DQK)
        q = q_ref[:, sl]
        k = k_ref[:, sl]
        v = v_ref[:, sl]
        n_prev = n_ref[:, sl]
        cq = cqt_ref[sl, :].T
        ig = gate_ref[:, h:h + 1]
        lf = gate_ref[:, HEADS + h:HEADS + h + 1]
        m_prev = m_ref[:, h:h + 1]
        inter = lf + m_prev
        m_row = jnp.maximum(inter, ig)
        wgt = jnp.sum(q * k, axis=1, keepdims=True) * jnp.exp(ig - m_row)
        g = jnp.exp(inter - m_row)
        num = g * cq + wgt * v
        den = g * jnp.sum(n_prev * q, axis=1, keepdims=True) + wgt
        hh = num / jnp.maximum(jnp.abs(den), jnp.exp(-m_row))
        hh = hh * lax.rsqrt(jnp.mean(hh * hh, axis=1, keepdims=True) + EPS) * gmh_ref[:, sl]
        y_s[:, CONV_DIM + h * DV:CONV_DIM + (h + 1) * DV] = (so_ref[:, sl] * hh).astype(BF16)
        nn_ref[:, sl] = g * n_prev + jnp.exp(ig - m_row) * k
        m_out = jnp.where(lane == h, m_row, m_out)
    mn_ref[...] = m_out
    hs = x_ref[...] + _dot(y_s[...], wout_ref[...].astype(BF16))
    hs_ref[...] = hs
    qx_ref[...] = _dot(_rmsnorm(hs, gx_ref[...]).astype(BF16), wq_ref[...].astype(BF16))


def _s3_call(cqt, q, k, v, so, gates, nst, m, yconv, x, wout, gmh, gx, wq):
    n = q.shape[0]
    full = lambda *shape: pl.BlockSpec(shape, lambda i: (0,) * len(shape))
    ins = [cqt, q, k, v, so, gates, nst, m, yconv, x, wout, gmh, gx, wq]
    return pl.pallas_call(
        _s3_kernel,
        grid=(1,),
        in_specs=[full(*a.shape) for a in ins],
        out_specs=[full(n, D_MODEL), full(n, D_MODEL), full(n, MLSTM_DIM), full(n, 128)],
        out_shape=[jax.ShapeDtypeStruct((n, D_MODEL), F32), jax.ShapeDtypeStruct((n, D_MODEL), F32),
                   jax.ShapeDtypeStruct((n, MLSTM_DIM), F32), jax.ShapeDtypeStruct((n, 128), F32)],
        scratch_shapes=[pltpu.VMEM((n, D_MODEL), BF16)],
        compiler_params=_params(("arbitrary",)),
        name="s3_mlstm_finish",
    )(*ins)


def _pack_heads(a):
    lead = a.shape[:-2]
    a = a.reshape(lead + (X_HEADS, X_HEAD_DIM // 128, 128))
    return jnp.swapaxes(a, -3, -2).reshape(lead + (PACK_ROWS, 128))


def _unpack_heads(a):
    lead = a.shape[:-2]
    a = a.reshape(lead + (X_HEAD_DIM // 128, X_HEADS, 128))
    return jnp.swapaxes(a, -3, -2).reshape(lead + (X_HEADS, X_HEAD_DIM))


def kernel(x_prompt, x_sample, mem_prompt, state_conv, state_mlstm_C, state_mlstm_n, state_mlstm_m,
           cache_mem_k, cache_mem_v, g_mix, w_in, b_in, conv_w, g_mh, w_out, g_cross, g_mem,
           w_xq, w_xkv, w_xo, g_ffn, w_gu, w_down, g_final):
    n_batch, seq_len, _ = x_prompt.shape
    n_dec = x_sample.shape[0]
    depth = w_in.shape[0]
    assert depth == 1 and x_sample.shape[1] == 1
    assert all(seq_len % t == 0 for t in (ROW_TILE, P1_TILE, P3_TILE))
    assert P1_TILE % MLSTM_CHUNK == 0 and n_batch % P2_SEQS == 0

    n_gate = 2 * HEADS
    w_in_b = w_in[0].T
    b_in_r = b_in[0].reshape(1, MAIN_DIM + n_gate)
    w_gate_r = w_in_b[MAIN_DIM:].astype(BF16)
    b_gate_r = b_in[0, MAIN_DIM:].reshape(n_gate, 1)
    w_out_b = w_out[0]
    w_xq_b = w_xq[0]
    w_xkv_b = w_xkv[0]
    w_xo_b = w_xo[0].astype(BF16)
    w_gu_b = w_gu[0].astype(BF16)
    w_down_b = w_down[0].astype(BF16)
    g_mix_r = g_mix[0].reshape(1, D_MODEL)
    g_cross_r = g_cross[0].reshape(1, D_MODEL)
    g_mem_r = g_mem[0].reshape(1, D_MODEL)
    g_ffn_r = g_ffn[0].reshape(1, D_MODEL)
    g_final_r = g_final.reshape(1, D_MODEL)
    g_mh_r = g_mh[0].reshape(1, MLSTM_DIM)
    cw = conv_w[0]

    xs = x_sample.reshape(n_dec, D_MODEL)
    st = state_conv[0].reshape(n_dec, (CONV_W - 1) * CONV_DIM)
    m0 = state_mlstm_m[0]
    s_yconv, s_conv, sq, sk, sv, sso, sgates, svt, sgates_t = _s1_call(
        xs, g_mix_r, w_in_b, b_in_r, w_gate_r, b_gate_r, cw, st)

    xp = x_prompt.reshape(n_batch * seq_len, D_MODEL)
    yconv, q, k, v, so, gates, p_conv, s_c, cqt = _p1_call(
        xp, g_mix_r, w_in_b, b_in_r, w_gate_r, b_gate_r, cw, seq_len,
        side=(state_mlstm_C[0], sq, sk, svt, sgates, sgates_t, m0, m0.T))
    hs1, qx, s_n, s_m = _s3_call(cqt, sq, sk, sv, sso, sgates,
                                 state_mlstm_n[0].reshape(n_dec, MLSTM_DIM), m0, s_yconv, xs,
                                 w_out_b, g_mh_r, g_cross_r, w_xq_b)
    hp1, p_c, p_n, p_m = _p2_call(q, k, v, so, yconv, gates, xp, w_out_b, g_mh_r, n_batch, seq_len)
    hp1 = hp1.reshape(n_batch * seq_len, D_MODEL)
    pk, pv, pkb, pvb = _pm_call(mem_prompt.reshape(n_batch * N_MEM, D_MODEL), g_mem_r, w_xkv_b)
    o_p = _p3_call(hp1, g_cross_r, w_xq_b, pkb.reshape(n_batch, N_MEM, D_MODEL),
                   pvb.reshape(n_batch, N_MEM, D_MODEL), seq_len)
    y_p, o_s = _p4_call(hp1, o_p, w_xo_b, g_ffn_r, w_gu_b, w_down_b, g_final_r, ROW_TILE,
                        side=(_pack_heads(qx.reshape(n_dec, X_HEADS, X_HEAD_DIM)),
                              _pack_heads(cache_mem_k[0]), _pack_heads(cache_mem_v[0])))
    o_s = _unpack_heads(o_s)

    y_s, = _p4_call(hs1, o_s.reshape(n_dec, D_MODEL), w_xo_b, g_ffn_r, w_gu_b, w_down_b, g_final_r, n_dec)

    mem_shape = (1, n_batch, N_MEM, X_HEADS, X_HEAD_DIM)
    return (y_p.reshape(n_batch, seq_len, D_MODEL),
            y_s.reshape(n_dec, 1, D_MODEL),
            p_conv.reshape(1, n_batch, CONV_W - 1, CONV_DIM),
            p_c.reshape(1, n_batch, HEADS, DV, DQK),
            p_n.reshape(1, n_batch, HEADS, DQK),
            p_m[:, 0, :HEADS].reshape(1, n_batch, HEADS),
            pk.reshape(mem_shape),
            pv.reshape(mem_shape),
            s_conv.reshape(1, n_dec, CONV_W - 1, CONV_DIM),
            s_c.reshape(1, n_dec, HEADS, DV, DQK),
            s_n.reshape(1, n_dec, HEADS, DQK),
            s_m[:, :HEADS].reshape(1, n_dec, HEADS))
```

```python
import functools

import jax
import jax.numpy as jnp
from jax import lax
from jax.experimental import pallas as pl
from jax.experimental.pallas import tpu as pltpu

F32 = jnp.float32
BF16 = jnp.bfloat16

D_MODEL = 1024
CONV_DIM = 512
CONV_W = 3
MLSTM_DIM = 512
HEADS = 4
DQK = 128
DV = 128
N_MEM = 256
X_HEADS = 4
X_HEAD_DIM = 256
D_FF = 2816
MAIN_DIM = 3 * CONV_DIM + 4 * MLSTM_DIM
EPS = 1e-6

MLSTM_CHUNK = 256
P2_SEQS = 4
ROW_TILE = 512
P1_TILE = 1024
P3_TILE = 1024
FF_CHUNK = 256
PACK_ROWS = X_HEADS * (X_HEAD_DIM // 128)
MEM_CHUNK = 32
VMEM_LIMIT = 56 * 1024 * 1024


def _dot(a, b):
    return jnp.dot(a, b, preferred_element_type=F32)


def _dot_nt(a, b):
    return lax.dot_general(a, b, (((1,), (1,)), ((), ())), preferred_element_type=F32)


def _rmsnorm(x, g):
    return x * lax.rsqrt(jnp.mean(x * x, axis=-1, keepdims=True) + EPS) * g


def _const_spec(shape):
    zeros = (0,) * len(shape)
    return pl.BlockSpec(shape, lambda *_: zeros, pipeline_mode=pl.Buffered(1))


def _params(sem):
    return pltpu.CompilerParams(dimension_semantics=sem, vmem_limit_bytes=VMEM_LIMIT)


def _gate_transform(gt):
    lane = lax.broadcasted_iota(jnp.int32, gt.shape, 1)
    return jnp.where(lane < HEADS, gt, jax.nn.log_sigmoid(gt))


def _gate_transform_rows(gt):
    sub = lax.broadcasted_iota(jnp.int32, gt.shape, 0)
    return jnp.where(sub < HEADS, gt, jax.nn.log_sigmoid(gt))


def _memory_update_rows(i, c_ref, q_ref, k_ref, vt_ref, gate_ref, gatet_ref, m_ref, mt_ref, cn_ref, cqt_ref):
    n = vt_ref.shape[1]
    bb = c_ref.shape[0]

    @pl.when(i == 0)
    def _():
        cqt_ref[...] = jnp.zeros_like(cqt_ref)

    lane = lax.broadcasted_iota(jnp.int32, (DV, n), 1)
    for h in range(HEADS):
        sl = slice(h * DQK, (h + 1) * DQK)
        ig_c = gate_ref[:, h:h + 1]
        lf_c = gate_ref[:, HEADS + h:HEADS + h + 1]
        m_c = m_ref[:, h:h + 1]
        dec = jnp.broadcast_to(jnp.exp(lf_c + m_c - jnp.maximum(lf_c + m_c, ig_c)), (bb, DQK))
        ig_r = gatet_ref[h:h + 1, :]
        lf_r = gatet_ref[HEADS + h:HEADS + h + 1, :]
        m_r = mt_ref[h:h + 1, :]
        svt = vt_ref[sl, :] * jnp.exp(ig_r - jnp.maximum(lf_r + m_r, ig_r))
        q_t = q_ref[:, sl]
        k_t = k_ref[:, sl]
        cqt = cqt_ref[sl, :]
        for bl in range(bb):
            onehot = lane == i * bb + bl
            c = c_ref[bl, h]
            cq_col = jnp.sum(c * q_t[bl:bl + 1, :], axis=1, keepdims=True)
            sv_col = jnp.sum(jnp.where(onehot, svt, 0.0), axis=1, keepdims=True)
            cn_ref[bl, h] = dec[bl:bl + 1, :] * c + sv_col * k_t[bl:bl + 1, :]
            cqt = jnp.where(onehot, cq_col, cqt)
        cqt_ref[sl, :] = cqt


def _p1_kernel(tiles_per_batch, x_ref, g_ref, w_ref, b_ref, wg_ref, bg_ref, cw_ref,
               c_ref, sq_ref, sk_ref, svt_ref, sgate_ref, sgatet_ref, sm_ref, smt_ref,
               yconv_ref, q_ref, k_ref, v_ref, so_ref, gate_ref, pconv_ref, cn_ref, cqt_ref, ubuf):
    tm = x_ref.shape[0]
    i = pl.program_id(0)
    _memory_update_rows(i, c_ref, sq_ref, sk_ref, svt_ref, sgate_ref, sgatet_ref, sm_ref, smt_ref,
                        cn_ref, cqt_ref)
    xn = _rmsnorm(x_ref[...], g_ref[...]).astype(BF16)

    def seg(j):
        sl = slice(j * CONV_DIM, (j + 1) * CONV_DIM)
        return _dot_nt(xn, w_ref[sl, :].astype(BF16)) + b_ref[:, sl]

    prev = ubuf[tm:tm + 8, :]
    ubuf[0:8, :] = jnp.where(i % tiles_per_batch == 0, jnp.zeros_like(prev), prev)
    ubuf[8:8 + tm, :] = seg(1) * seg(2)
    conv = (cw_ref[0:1, :] * ubuf[6:6 + tm, :] + cw_ref[1:2, :] * ubuf[7:7 + tm, :]
            + cw_ref[2:3, :] * ubuf[8:8 + tm, :])
    yconv_ref[...] = (seg(0) * conv).astype(BF16)
    pconv_ref[0] = ubuf[tm + 6:tm + 8, :]

    q_ref[...] = seg(3).astype(BF16)
    k_ref[...] = (seg(4) * (DQK ** -0.5)).astype(BF16)
    v_ref[...] = seg(5).astype(BF16)
    so_ref[...] = jax.nn.sigmoid(seg(6)).astype(BF16)
    gt = _gate_transform_rows(_dot_nt(wg_ref[...], xn) + bg_ref[...])
    n_gate = gt.shape[0]
    gate_ref[0, 0:n_gate, :] = gt
    L = MLSTM_CHUNK
    n_blk = tm // L
    hi = gt.astype(BF16).astype(F32)
    r1 = gt - hi
    mid = r1.astype(BF16).astype(F32)
    lo = r1 - mid
    terms = jnp.concatenate([t[:, j * L:(j + 1) * L] for t in (hi, mid, lo) for j in range(n_blk)], axis=0)
    tri = (lax.broadcasted_iota(jnp.int32, (L, L), 0) <= lax.broadcasted_iota(jnp.int32, (L, L), 1)).astype(BF16)
    parts = _dot(terms.astype(BF16), tri)
    for j in range(n_blk):
        rows = [parts[(t * n_blk + j) * n_gate:(t * n_blk + j + 1) * n_gate, :] for t in range(3)]
        gate_ref[0, n_gate:2 * n_gate, j * L:(j + 1) * L] = (rows[0] + rows[1]) + rows[2]


def _p1_call(x, g, w, b, wg, bg, cw, seq_len, side):
    rows = x.shape[0]
    tm = P1_TILE
    steps = rows // tm
    tiles_per_batch = seq_len // tm
    n_batch = rows // seq_len
    row = lambda width: pl.BlockSpec((tm, width), lambda i: (i, 0))
    c, sq, sk, svt, sgate, sgatet, sm, smt = side
    n = sq.shape[0]
    sr = n // steps
    assert sr * steps == n and sr % 8 == 0
    full = lambda a: pl.BlockSpec(a.shape, lambda i: (0,) * a.ndim)
    srow = lambda a: pl.BlockSpec((sr,) + a.shape[1:], lambda i: (i,) + (0,) * (a.ndim - 1))
    return pl.pallas_call(
        functools.partial(_p1_kernel, tiles_per_batch),
        grid=(steps,),
        in_specs=[row(D_MODEL), _const_spec((1, D_MODEL)), _const_spec(w.shape),
                  _const_spec(b.shape), _const_spec((2 * HEADS, D_MODEL)),
                  _const_spec((2 * HEADS, 1)), _const_spec((CONV_W, CONV_DIM)),
                  srow(c), srow(sq), srow(sk), full(svt), srow(sgate), full(sgatet), srow(sm), full(smt)],
        out_specs=[row(CONV_DIM), row(MLSTM_DIM), row(MLSTM_DIM), row(MLSTM_DIM), row(MLSTM_DIM),
                   pl.BlockSpec((1, 4 * HEADS, tm), lambda i: (i // tiles_per_batch, 0, i % tiles_per_batch)),
                   pl.BlockSpec((1, CONV_W - 1, CONV_DIM), lambda i: (i // tiles_per_batch, 0, 0)),
                   srow(c), full(svt)],
        out_shape=[jax.ShapeDtypeStruct((rows, CONV_DIM), BF16)]
        + [jax.ShapeDtypeStruct((rows, MLSTM_DIM), BF16)] * 4
        + [jax.ShapeDtypeStruct((n_batch, 4 * HEADS, seq_len), F32),
           jax.ShapeDtypeStruct((n_batch, CONV_W - 1, CONV_DIM), F32),
           jax.ShapeDtypeStruct(c.shape, F32), jax.ShapeDtypeStruct(svt.shape, F32)],
        scratch_shapes=[pltpu.VMEM((tm + 8, CONV_DIM), F32)],
        compiler_params=_params(("arbitrary",)),
        name="p1_inproj_conv",
    )(x, g, w, b, wg, bg, cw, c, sq, sk, svt, sgate, sgatet, sm, smt)


def _p2_kernel(q_ref, k_ref, v_ref, so_ref, yconv_ref, gate_ref, x_ref, wout_ref, gmh_ref,
               hp_ref, pc_ref, pn_ref, pm_ref, c_s, m_s, y_s):
    nb, L = q_ref.shape[0], q_ref.shape[1]
    c = pl.program_id(1)

    @pl.when(c == 0)
    def _():
        c_s[...] = jnp.zeros_like(c_s)
        m_s[...] = jnp.zeros_like(m_s)

    row = lax.broadcasted_iota(jnp.int32, (L, L), 0)
    col = lax.broadcasted_iota(jnp.int32, (L, L), 1)
    causal = row >= col

    for bi in range(nb):
        gt = gate_ref[bi]
        for h in range(HEADS):
            sl = slice(h * DQK, (h + 1) * DQK)
            q = q_ref[bi, :, sl]
            k = k_ref[bi, :, sl]
            v = v_ref[bi, :, sl]
            lf_r = gt[HEADS + h:HEADS + h + 1, :]
            a_r = gt[h:h + 1, :] - gt[3 * HEADS + h:3 * HEADS + h + 1, :]
            m_prev = jnp.max(m_s[bi, h:h + 1, :], axis=1, keepdims=True)
            c_prev = c_s[bi, h]

            m_c = jnp.maximum(m_prev, jnp.max(jnp.where(causal, a_r, -jnp.inf), axis=1, keepdims=True))
            b_c = jnp.sum(jnp.where(causal, lf_r, 0.0), axis=1, keepdims=True)
            w = _dot_nt(q, k) * jnp.exp(jnp.where(causal, a_r - m_c, -jnp.inf))
            g = jnp.exp(m_prev - m_c)
            qc = _dot_nt(q, c_prev.astype(BF16))
            num = g * qc[:, 0:DV] + _dot(w.astype(BF16), v)
            den = g * qc[:, DV:2 * DV] + jnp.sum(w, axis=1, keepdims=True)
            hh = num / jnp.maximum(jnp.abs(den), jnp.exp(-(b_c + m_c)))
            hh = hh * lax.rsqrt(jnp.mean(hh * hh, axis=1, keepdims=True) + EPS) * gmh_ref[:, sl]
            y_s[bi * L:(bi + 1) * L, h * DV:(h + 1) * DV] = (so_ref[bi, :, sl].astype(F32) * hh).astype(BF16)

            m_last = jnp.maximum(m_prev, jnp.max(a_r, axis=1, keepdims=True))
            b_last = jnp.sum(lf_r, axis=1, keepdims=True)
            s_r = jnp.exp(a_r - m_last)
            sv_t = jnp.concatenate([v.T.astype(F32) * s_r, jnp.broadcast_to(s_r, (DV, L))], axis=0)
            c_s[bi, h] = jnp.exp(m_prev - m_last) * c_prev + _dot(sv_t.astype(BF16), k)
            m_s[bi, h:h + 1, :] = jnp.broadcast_to(b_last + m_last, (1, m_s.shape[2]))

    w_conv = wout_ref[0:CONV_DIM, :].astype(BF16)
    w_ml = wout_ref[CONV_DIM:CONV_DIM + MLSTM_DIM, :].astype(BF16)
    for bi in range(nb):
        out = _dot(yconv_ref[bi], w_conv) + _dot(y_s[bi * L:(bi + 1) * L, :], w_ml)
        hp_ref[bi] = x_ref[bi] + out

    @pl.when(c == pl.num_programs(1) - 1)
    def _():
        lane = lax.broadcasted_iota(jnp.int32, (1, m_s.shape[2]), 1)
        for bi in range(nb):
            acc = jnp.zeros((1, m_s.shape[2]), F32)
            for h in range(HEADS):
                pc_ref[bi, h] = c_s[bi, h, 0:DV, :]
                pn_ref[bi, h:h + 1, :] = c_s[bi, h, DV:DV + 1, :]
                acc = jnp.where(lane == h, m_s[bi, h:h + 1, :], acc)
            pm_ref[bi] = acc


def _p2_call(q, k, v, so, yconv, gates, x, wout, gmh, n_batch, seq_len):
    L = MLSTM_CHUNK
    nb = P2_SEQS
    nc = seq_len // L
    seq = lambda width: pl.BlockSpec((nb, L, width), lambda b, c: (b, c, 0))
    as_seq = lambda a: a.reshape(n_batch, seq_len, a.shape[-1])
    return pl.pallas_call(
        _p2_kernel,
        grid=(n_batch // nb, nc),
        in_specs=[seq(MLSTM_DIM), seq(MLSTM_DIM), seq(MLSTM_DIM), seq(MLSTM_DIM), seq(CONV_DIM),
                  pl.BlockSpec((nb, 4 * HEADS, L), lambda b, c: (b, 0, c)), seq(D_MODEL),
                  _const_spec((D_MODEL, D_MODEL)), _const_spec((1, MLSTM_DIM))],
        out_specs=[seq(D_MODEL),
                   pl.BlockSpec((nb, HEADS, DV, DQK), lambda b, c: (b, 0, 0, 0)),
                   pl.BlockSpec((nb, HEADS, DQK), lambda b, c: (b, 0, 0)),
                   pl.BlockSpec((nb, 1, 128), lambda b, c: (b, 0, 0))],
        out_shape=[jax.ShapeDtypeStruct((n_batch, seq_len, D_MODEL), F32),
                   jax.ShapeDtypeStruct((n_batch, HEADS, DV, DQK), F32),
                   jax.ShapeDtypeStruct((n_batch, HEADS, DQK), F32),
                   jax.ShapeDtypeStruct((n_batch, 1, 128), F32)],
        scratch_shapes=[pltpu.VMEM((nb, HEADS, 2 * DV, DQK), F32), pltpu.VMEM((nb, 8, 128), F32),
                        pltpu.VMEM((nb * L, MLSTM_DIM), BF16)],
        compiler_params=_params(("arbitrary", "arbitrary")),
        name="p2_mlstm_outproj",
    )(as_seq(q), as_seq(k), as_seq(v), as_seq(so), as_seq(yconv), gates, as_seq(x), wout, gmh)


def _pm_kernel(mem_ref, g_ref, w_ref, k_ref, v_ref, kb_ref, vb_ref):
    xn = _rmsnorm(mem_ref[...], g_ref[...]).astype(BF16)
    kk = _dot(xn, w_ref[:, 0:D_MODEL].astype(BF16))
    vv = _dot(xn, w_ref[:, D_MODEL:2 * D_MODEL].astype(BF16))
    for h in range(X_HEADS):
        sl = slice(h * X_HEAD_DIM, (h + 1) * X_HEAD_DIM)
        k_ref[:, h, :] = kk[:, sl]
        v_ref[:, h, :] = vv[:, sl]
    kb_ref[...] = kk.astype(BF16)
    vb_ref[...] = vv.astype(BF16)


def _pm_call(mem, g, w):
    rows = mem.shape[0]
    tm = ROW_TILE
    row = pl.BlockSpec((tm, D_MODEL), lambda i: (i, 0))
    row4 = pl.BlockSpec((tm, X_HEADS, X_HEAD_DIM), lambda i: (i, 0, 0))
    return pl.pallas_call(
        _pm_kernel,
        grid=(rows // tm,),
        in_specs=[row, _const_spec((1, D_MODEL)), _const_spec((D_MODEL, 2 * D_MODEL))],
        out_specs=[row4, row4, row, row],
        out_shape=[jax.ShapeDtypeStruct((rows, X_HEADS, X_HEAD_DIM), F32)] * 2
        + [jax.ShapeDtypeStruct((rows, D_MODEL), BF16)] * 2,
        compiler_params=_params(("arbitrary",)),
        name="pm_mem_kv",
    )(mem, g, w)


def _p3_kernel(hp_ref, g_ref, wq_ref, k_ref, v_ref, o_ref):
    xn = _rmsnorm(hp_ref[...], g_ref[...]).astype(BF16)
    q = _dot(xn, wq_ref[...].astype(BF16))
    for h in range(X_HEADS):
        sl = slice(h * X_HEAD_DIM, (h + 1) * X_HEAD_DIM)
        s = _dot_nt(q[:, sl].astype(BF16), k_ref[0, :, sl]) * (X_HEAD_DIM ** -0.5)
        e = jnp.exp(s - jnp.max(s, axis=1, keepdims=True))
        p = e * (1.0 / jnp.sum(e, axis=1, keepdims=True))
        o_ref[:, sl] = _dot(p.astype(BF16), v_ref[0, :, sl]).astype(BF16)


def _p3_call(hp, g, wq, kb, vb, seq_len):
    rows = hp.shape[0]
    tm = P3_TILE
    tiles_per_batch = seq_len // tm
    row = pl.BlockSpec((tm, D_MODEL), lambda i: (i, 0))
    mem = pl.BlockSpec((1, N_MEM, D_MODEL), lambda i: (i // tiles_per_batch, 0, 0))
    return pl.pallas_call(
        _p3_kernel,
        grid=(rows // tm,),
        in_specs=[row, _const_spec((1, D_MODEL)), _const_spec((D_MODEL, D_MODEL)), mem, mem],
        out_specs=row,
        out_shape=jax.ShapeDtypeStruct((rows, D_MODEL), BF16),
        compiler_params=_params(("arbitrary",)),
        name="p3_cross_attn",
    )(hp, g, wq, kb, vb)


def _cache_attention_row(q8, kc_ref, vc_ref, bl):
    qs = q8 * (X_HEAD_DIM ** -0.5)
    m_run = jnp.full((1, PACK_ROWS, 1), -jnp.inf, F32)
    l_run = jnp.zeros((1, PACK_ROWS, 1), F32)
    acc = jnp.zeros((PACK_ROWS, 128), F32)
    for c in range(N_MEM // MEM_CHUNK):
        blk = slice(c * MEM_CHUNK, (c + 1) * MEM_CHUNK)
        prod = kc_ref[bl, blk] * qs
        s = jnp.sum(prod + pltpu.roll(prod, X_HEADS, 1), axis=-1, keepdims=True)
        m_new = jnp.maximum(m_run, jnp.max(s, axis=0, keepdims=True))
        alpha = jnp.exp(m_run - m_new)
        e = jnp.exp(s - m_new)
        l_run = alpha * l_run + jnp.sum(e, axis=0, keepdims=True)
        acc = alpha[0] * acc + jnp.sum(e * vc_ref[bl, blk], axis=0)
        m_run = m_new
    return acc * (1.0 / l_run[0])


def _p4_kernel(side_rows, hp_ref, o_ref, wxo_ref, gf_ref, wgu_ref, wd_ref, gfin_ref, *rest):
    if side_rows:
        q4_ref, kc_ref, vc_ref, y_ref, os_ref, act_s = rest
        for bl in range(side_rows):
            os_ref[bl] = _cache_attention_row(q4_ref[bl], kc_ref, vc_ref, bl)
    else:
        y_ref, act_s = rest
    if len(o_ref.shape) == 3:
        o = jnp.concatenate([o_ref[:, half * X_HEADS + h, :] for h in range(X_HEADS)
                             for half in range(X_HEAD_DIM // 128)], axis=1)
    else:
        o = o_ref[...]
    hp = hp_ref[...] + _dot(o.astype(BF16), wxo_ref[...])
    xn = _rmsnorm(hp, gf_ref[...]).astype(BF16)
    for j in range(D_FF // FF_CHUNK):
        g = _dot(xn, wgu_ref[:, FF_CHUNK * j:FF_CHUNK * (j + 1)])
        u = _dot(xn, wgu_ref[:, D_FF + FF_CHUNK * j:D_FF + FF_CHUNK * (j + 1)])
        act_s[:, FF_CHUNK * j:FF_CHUNK * (j + 1)] = (g * jax.nn.sigmoid(g) * u).astype(BF16)
    hp = hp + _dot(act_s[...], wd_ref[...])
    y = _rmsnorm(hp, gfin_ref[...])
    if len(y_ref.shape) == 3:
        y_ref[:, 0, :] = y
    else:
        y_ref[...] = y


def _p4_call(hp, o, wxo, gf, wgu, wd, gfin, tm, side=None, rows_3d=False):
    rows = hp.shape[0]
    steps = rows // tm
    row = pl.BlockSpec((tm, D_MODEL), lambda i: (i, 0))
    o_spec = row if o.ndim == 2 else pl.BlockSpec((tm,) + o.shape[1:], lambda i: (i, 0, 0))
    in_specs = [row, o_spec, _const_spec((D_MODEL, D_MODEL)), _const_spec((1, D_MODEL)),
                _const_spec((D_MODEL, 2 * D_FF)), _const_spec((D_FF, D_MODEL)),
                _const_spec((1, D_MODEL))]
    if rows_3d:
        out_specs = [pl.BlockSpec((tm, 1, D_MODEL), lambda i: (i, 0, 0))]
        out_shape = [jax.ShapeDtypeStruct((rows, 1, D_MODEL), F32)]
    else:
        out_specs = [row]
        out_shape = [jax.ShapeDtypeStruct((rows, D_MODEL), F32)]
    args = [hp, o, wxo, gf, wgu, wd, gfin]
    side_rows = 0
    if side is not None:
        q4, kc, vc = side
        side_rows = q4.shape[0] // steps
        assert side_rows * steps == q4.shape[0]
        srow = pl.BlockSpec((side_rows, PACK_ROWS, 128), lambda i: (i, 0, 0))
        cache = pl.BlockSpec((side_rows, N_MEM, PACK_ROWS, 128), lambda i: (i, 0, 0, 0))
        in_specs += [srow, cache, cache]
        out_specs += [srow]
        out_shape += [jax.ShapeDtypeStruct(q4.shape, F32)]
        args += [q4, kc, vc]
    return pl.pallas_call(
        functools.partial(_p4_kernel, side_rows),
        grid=(steps,),
        in_specs=in_specs,
        out_specs=out_specs,
        out_shape=out_shape,
        scratch_shapes=[pltpu.VMEM((tm, D_FF), BF16)],
        compiler_params=_params(("arbitrary",)),
        name="p4_ffn_final",
    )(*args)


def _s1_kernel(x_ref, g_ref, w_ref, b_ref, wg_ref, bg_ref, cw_ref, st_ref,
               yconv_ref, sconv_ref, q_ref, k_ref, v_ref, so_ref, gate_ref, vt_ref, gatet_ref):
    xn = _rmsnorm(x_ref[:, 0, :], g_ref[...]).astype(BF16)

    def seg(j):
        sl = slice(j * CONV_DIM, (j + 1) * CONV_DIM)
        return _dot_nt(xn, w_ref[sl, :].astype(BF16)) + b_ref[:, sl]

    u = seg(1) * seg(2)
    st0 = st_ref[:, 0, :]
    st1 = st_ref[:, 1, :]
    conv = cw_ref[0:1, :] * st0 + cw_ref[1:2, :] * st1 + cw_ref[2:3, :] * u
    yconv_ref[...] = seg(0) * conv
    sconv_ref[:, 0, :] = st1
    sconv_ref[:, 1, :] = u
    q_ref[...] = seg(3)
    k_ref[...] = seg(4) * (DQK ** -0.5)
    v = seg(5)
    v_ref[...] = v
    so_ref[...] = jax.nn.sigmoid(seg(6))
    n_gate = wg_ref.shape[0]
    gate_ref[...] = _gate_transform(_dot_nt(xn, wg_ref[...]) + b_ref[:, MAIN_DIM:MAIN_DIM + n_gate])
    for h in range(HEADS):
        sl = slice(h * DV, (h + 1) * DV)
        vt_ref[sl, :] = v[:, sl].T
    gatet_ref[...] = _gate_transform_rows(_dot_nt(wg_ref[...], xn) + bg_ref[...])


def _s1_call(x, g, w, b, wg, bg, cw, st):
    n = x.shape[0]
    n_gate = wg.shape[0]
    full = lambda *shape: pl.BlockSpec(shape, lambda i: (0,) * len(shape))
    ins = [x, g, w, b, wg, bg, cw, st]
    return pl.pallas_call(
        _s1_kernel,
        grid=(1,),
        in_specs=[full(*a.shape) for a in ins],
        out_specs=[full(n, CONV_DIM), full(*st.shape), full(n, MLSTM_DIM), full(n, MLSTM_DIM),
                   full(n, MLSTM_DIM), full(n, MLSTM_DIM), full(n, n_gate), full(MLSTM_DIM, n),
                   full(n_gate, n)],
        out_shape=[jax.ShapeDtypeStruct((n, CONV_DIM), F32),
                   jax.ShapeDtypeStruct(st.shape, F32)]
        + [jax.ShapeDtypeStruct((n, MLSTM_DIM), F32)] * 4
        + [jax.ShapeDtypeStruct((n, n_gate), F32),
           jax.ShapeDtypeStruct((MLSTM_DIM, n), F32),
           jax.ShapeDtypeStruct((n_gate, n), F32)],
        compiler_params=_params(("arbitrary",)),
        name="s1_inproj_conv",
    )(*ins)


def _s3_kernel(cqt_ref, q_ref, k_ref, v_ref, so_ref, gate_ref, n_ref, m_ref, yconv_ref, x_ref,
               wout_ref, gmh_ref, gx_ref, wq_ref,
               hs_ref, qx_ref, nn_ref, mn_ref, y_s):
    n_rows = q_ref.shape[0]
    y_s[:, 0:CONV_DIM] = yconv_ref[...].astype(BF16)
    lane = lax.broadcasted_iota(jnp.int32, (n_rows, mn_ref.shape[1]), 1)
    m_out = jnp.zeros((n_rows, mn_ref.shape[1]), F32)
    for h in range(HEADS):
        sl = slice(h * DQK, (h + 1) * DQK)
        q = q_ref[:, sl]
        k = k_ref[:, sl]
        v = v_ref[:, sl]
        n_prev = n_ref[:, h, :]
        cq = cqt_ref[sl, :].T
        ig = gate_ref[:, h:h + 1]
        lf = gate_ref[:, HEADS + h:HEADS + h + 1]
        m_prev = m_ref[:, h:h + 1]
        inter = lf + m_prev
        m_row = jnp.maximum(inter, ig)
        wgt = jnp.sum(q * k, axis=1, keepdims=True) * jnp.exp(ig - m_row)
        g = jnp.exp(inter - m_row)
        num = g * cq + wgt * v
        den = g * jnp.sum(n_prev * q, axis=1, keepdims=True) + wgt
        hh = num / jnp.maximum(jnp.abs(den), jnp.exp(-m_row))
        hh = hh * lax.rsqrt(jnp.mean(hh * hh, axis=1, keepdims=True) + EPS) * gmh_ref[:, sl]
        y_s[:, CONV_DIM + h * DV:CONV_DIM + (h + 1) * DV] = (so_ref[:, sl] * hh).astype(BF16)
        nn_ref[:, h, :] = g * n_prev + jnp.exp(ig - m_row) * k
        m_out = jnp.where(lane == h, m_row, m_out)
    mn_ref[...] = m_out
    hs = x_ref[:, 0, :] + _dot(y_s[...], wout_ref[...].astype(BF16))
    hs_ref[...] = hs
    qx = _dot(_rmsnorm(hs, gx_ref[...]).astype(BF16), wq_ref[...].astype(BF16))
    for h in range(X_HEADS):
        for half in range(X_HEAD_DIM // 128):
            lo = h * X_HEAD_DIM + half * 128
            qx_ref[:, half * X_HEADS + h, :] = qx[:, lo:lo + 128]


def _s3_call(cqt, q, k, v, so, gates, nst, m, yconv, x, wout, gmh, gx, wq):
    n = q.shape[0]
    full = lambda *shape: pl.BlockSpec(shape, lambda i: (0,) * len(shape))
    ins = [cqt, q, k, v, so, gates, nst, m, yconv, x, wout, gmh, gx, wq]
    return pl.pallas_call(
        _s3_kernel,
        grid=(1,),
        in_specs=[full(*a.shape) for a in ins],
        out_specs=[full(n, D_MODEL), full(n, PACK_ROWS, 128), full(*nst.shape), full(n, 128)],
        out_shape=[jax.ShapeDtypeStruct((n, D_MODEL), F32), jax.ShapeDtypeStruct((n, PACK_ROWS, 128), F32),
                   jax.ShapeDtypeStruct(nst.shape, F32), jax.ShapeDtypeStruct((n, 128), F32)],
        scratch_shapes=[pltpu.VMEM((n, D_MODEL), BF16)],
        compiler_params=_params(("arbitrary",)),
        name="s3_mlstm_finish",
    )(*ins)


def _pack_heads(a):
    lead = a.shape[:-2]
    a = a.reshape(lead + (X_HEADS, X_HEAD_DIM // 128, 128))
    return jnp.swapaxes(a, -3, -2).reshape(lead + (PACK_ROWS, 128))


def kernel(x_prompt, x_sample, mem_prompt, state_conv, state_mlstm_C, state_mlstm_n, state_mlstm_m,
           cache_mem_k, cache_mem_v, g_mix, w_in, b_in, conv_w, g_mh, w_out, g_cross, g_mem,
           w_xq, w_xkv, w_xo, g_ffn, w_gu, w_down, g_final):
    n_batch, seq_len, _ = x_prompt.shape
    n_dec = x_sample.shape[0]
    depth = w_in.shape[0]
    assert depth == 1 and x_sample.shape[1] == 1
    assert all(seq_len % t == 0 for t in (ROW_TILE, P1_TILE, P3_TILE))
    assert P1_TILE % MLSTM_CHUNK == 0 and n_batch % P2_SEQS == 0

    n_gate = 2 * HEADS
    w_in_b = w_in[0].T
    b_in_r = b_in[0].reshape(1, MAIN_DIM + n_gate)
    w_gate_r = w_in_b[MAIN_DIM:].astype(BF16)
    b_gate_r = b_in[0, MAIN_DIM:].reshape(n_gate, 1)
    w_out_b = w_out[0]
    w_xq_b = w_xq[0]
    w_xkv_b = w_xkv[0]
    w_xo_b = w_xo[0].astype(BF16)
    w_gu_b = w_gu[0].astype(BF16)
    w_down_b = w_down[0].astype(BF16)
    g_mix_r = g_mix[0].reshape(1, D_MODEL)
    g_cross_r = g_cross[0].reshape(1, D_MODEL)
    g_mem_r = g_mem[0].reshape(1, D_MODEL)
    g_ffn_r = g_ffn[0].reshape(1, D_MODEL)
    g_final_r = g_final.reshape(1, D_MODEL)
    g_mh_r = g_mh[0].reshape(1, MLSTM_DIM)
    cw = conv_w[0]

    m0 = state_mlstm_m[0]
    s_yconv, s_conv, sq, sk, sv, sso, sgates, svt, sgates_t = _s1_call(
        x_sample, g_mix_r, w_in_b, b_in_r, w_gate_r, b_gate_r, cw, state_conv[0])

    xp = x_prompt.reshape(n_batch * seq_len, D_MODEL)
    yconv, q, k, v, so, gates, p_conv, s_c, cqt = _p1_call(
        xp, g_mix_r, w_in_b, b_in_r, w_gate_r, b_gate_r, cw, seq_len,
        side=(state_mlstm_C[0], sq, sk, svt, sgates, sgates_t, m0, m0.T))
    hs1, qx, s_n, s_m = _s3_call(cqt, sq, sk, sv, sso, sgates,
                                 state_mlstm_n[0], m0, s_yconv, x_sample,
                                 w_out_b, g_mh_r, g_cross_r, w_xq_b)
    hp1, p_c, p_n, p_m = _p2_call(q, k, v, so, yconv, gates, xp, w_out_b, g_mh_r, n_batch, seq_len)
    hp1 = hp1.reshape(n_batch * seq_len, D_MODEL)
    pk, pv, pkb, pvb = _pm_call(mem_prompt.reshape(n_batch * N_MEM, D_MODEL), g_mem_r, w_xkv_b)
    o_p = _p3_call(hp1, g_cross_r, w_xq_b, pkb.reshape(n_batch, N_MEM, D_MODEL),
                   pvb.reshape(n_batch, N_MEM, D_MODEL), seq_len)
    y_p, o_s = _p4_call(hp1, o_p, w_xo_b, g_ffn_r, w_gu_b, w_down_b, g_final_r, ROW_TILE,
                        side=(qx, _pack_heads(cache_mem_k[0]), _pack_heads(cache_mem_v[0])))

    y_s, = _p4_call(hs1, o_s, w_xo_b, g_ffn_r, w_gu_b, w_down_b, g_final_r, n_dec, rows_3d=True)

    mem_shape = (1, n_batch, N_MEM, X_HEADS, X_HEAD_DIM)
    return (y_p.reshape(n_batch, seq_len, D_MODEL),
            y_s.reshape(n_dec, 1, D_MODEL),
            p_conv.reshape(1, n_batch, CONV_W - 1, CONV_DIM),
            p_c.reshape(1, n_batch, HEADS, DV, DQK),
            p_n.reshape(1, n_batch, HEADS, DQK),
            p_m[:, 0, :HEADS].reshape(1, n_batch, HEADS),
            pk.reshape(mem_shape),
            pv.reshape(mem_shape),
            s_conv.reshape(1, n_dec, CONV_W - 1, CONV_DIM),
            s_c.reshape(1, n_dec, HEADS, DV, DQK),
            s_n.reshape(1, n_dec, HEADS, DQK),
            s_m[:, :HEADS].reshape(1, n_dec, HEADS))
```

```python
import functools

import jax
import jax.numpy as jnp
from jax import lax
from jax.experimental import pallas as pl
from jax.experimental.pallas import tpu as pltpu

F32 = jnp.float32
BF16 = jnp.bfloat16

D_MODEL = 1024
CONV_DIM = 512
CONV_W = 3
MLSTM_DIM = 512
HEADS = 4
DQK = 128
DV = 128
N_MEM = 256
X_HEADS = 4
X_HEAD_DIM = 256
D_FF = 2816
MAIN_DIM = 3 * CONV_DIM + 4 * MLSTM_DIM
EPS = 1e-6

MLSTM_CHUNK = 256
P2_SEQS = 4
ROW_TILE = 512
P1_TILE = 1024
P3_TILE = 1024
FF_CHUNK = 256
PACK_ROWS = X_HEADS * (X_HEAD_DIM // 128)
MEM_CHUNK = 32
VMEM_LIMIT = 56 * 1024 * 1024


def _dot(a, b):
    return jnp.dot(a, b, preferred_element_type=F32)


def _dot_nt(a, b):
    return lax.dot_general(a, b, (((1,), (1,)), ((), ())), preferred_element_type=F32)


def _rmsnorm(x, g):
    return x * lax.rsqrt(jnp.mean(x * x, axis=-1, keepdims=True) + EPS) * g


def _const_spec(shape):
    zeros = (0,) * len(shape)
    return pl.BlockSpec(shape, lambda *_: zeros, pipeline_mode=pl.Buffered(1))


def _params(sem):
    return pltpu.CompilerParams(dimension_semantics=sem, vmem_limit_bytes=VMEM_LIMIT)


def _gate_transform(gt):
    lane = lax.broadcasted_iota(jnp.int32, gt.shape, 1)
    return jnp.where(lane < HEADS, gt, jax.nn.log_sigmoid(gt))


def _gate_transform_rows(gt):
    sub = lax.broadcasted_iota(jnp.int32, gt.shape, 0)
    return jnp.where(sub < HEADS, gt, jax.nn.log_sigmoid(gt))


def _memory_update_rows(i, c_ref, q_ref, k_ref, vt_ref, gate_ref, gatet_ref, m_ref, mt_ref, cn_ref, cqt_ref):
    n = vt_ref.shape[1]
    bb = c_ref.shape[0]

    @pl.when(i == 0)
    def _():
        cqt_ref[...] = jnp.zeros_like(cqt_ref)

    lane = lax.broadcasted_iota(jnp.int32, (DV, n), 1)
    for h in range(HEADS):
        sl = slice(h * DQK, (h + 1) * DQK)
        ig_c = gate_ref[:, h:h + 1]
        lf_c = gate_ref[:, HEADS + h:HEADS + h + 1]
        m_c = m_ref[:, h:h + 1]
        dec = jnp.broadcast_to(jnp.exp(lf_c + m_c - jnp.maximum(lf_c + m_c, ig_c)), (bb, DQK))
        ig_r = gatet_ref[h:h + 1, :]
        lf_r = gatet_ref[HEADS + h:HEADS + h + 1, :]
        m_r = mt_ref[h:h + 1, :]
        svt = vt_ref[sl, :] * jnp.exp(ig_r - jnp.maximum(lf_r + m_r, ig_r))
        q_t = q_ref[:, sl]
        k_t = k_ref[:, sl]
        cqt = cqt_ref[sl, :]
        for bl in range(bb):
            onehot = lane == i * bb + bl
            c = c_ref[bl, h]
            cq_col = jnp.sum(c * q_t[bl:bl + 1, :], axis=1, keepdims=True)
            sv_col = jnp.sum(jnp.where(onehot, svt, 0.0), axis=1, keepdims=True)
            cn_ref[bl, h] = dec[bl:bl + 1, :] * c + sv_col * k_t[bl:bl + 1, :]
            cqt = jnp.where(onehot, cq_col, cqt)
        cqt_ref[sl, :] = cqt


def _p1_kernel(tiles_per_batch, x_ref, g_ref, w_ref, b_ref, wg_ref, bg_ref, cw_ref,
               c_ref, sq_ref, sk_ref, svt_ref, sgate_ref, sgatet_ref, sm_ref, smt_ref,
               yconv_ref, q_ref, k_ref, v_ref, so_ref, gate_ref, pconv_ref, cn_ref, cqt_ref, ubuf):
    tm = x_ref.shape[0]
    i = pl.program_id(0)
    _memory_update_rows(i, c_ref, sq_ref, sk_ref, svt_ref, sgate_ref, sgatet_ref, sm_ref, smt_ref,
                        cn_ref, cqt_ref)
    xn = _rmsnorm(x_ref[...], g_ref[...]).astype(BF16)

    def seg(j):
        sl = slice(j * CONV_DIM, (j + 1) * CONV_DIM)
        return _dot_nt(xn, w_ref[sl, :].astype(BF16)) + b_ref[:, sl]

    prev = ubuf[tm:tm + 8, :]
    ubuf[0:8, :] = jnp.where(i % tiles_per_batch == 0, jnp.zeros_like(prev), prev)
    ubuf[8:8 + tm, :] = seg(1) * seg(2)
    conv = (cw_ref[0:1, :] * ubuf[6:6 + tm, :] + cw_ref[1:2, :] * ubuf[7:7 + tm, :]
            + cw_ref[2:3, :] * ubuf[8:8 + tm, :])
    yconv_ref[...] = (seg(0) * conv).astype(BF16)
    pconv_ref[0] = ubuf[tm + 6:tm + 8, :]

    q_ref[...] = seg(3).astype(BF16)
    k_ref[...] = (seg(4) * (DQK ** -0.5)).astype(BF16)
    v_ref[...] = seg(5).astype(BF16)
    so_ref[...] = jax.nn.sigmoid(seg(6)).astype(BF16)
    gt = _gate_transform_rows(_dot_nt(wg_ref[...], xn) + bg_ref[...])
    n_gate = gt.shape[0]
    gate_ref[0, 0:n_gate, :] = gt
    L = MLSTM_CHUNK
    n_blk = tm // L
    hi = gt.astype(BF16).astype(F32)
    r1 = gt - hi
    mid = r1.astype(BF16).astype(F32)
    lo = r1 - mid
    terms = jnp.concatenate([t[:, j * L:(j + 1) * L] for t in (hi, mid, lo) for j in range(n_blk)], axis=0)
    tri = (lax.broadcasted_iota(jnp.int32, (L, L), 0) <= lax.broadcasted_iota(jnp.int32, (L, L), 1)).astype(BF16)
    parts = _dot(terms.astype(BF16), tri)
    for j in range(n_blk):
        rows = [parts[(t * n_blk + j) * n_gate:(t * n_blk + j + 1) * n_gate, :] for t in range(3)]
        gate_ref[0, n_gate:2 * n_gate, j * L:(j + 1) * L] = (rows[0] + rows[1]) + rows[2]


def _p1_call(x, g, w, b, wg, bg, cw, seq_len, side):
    rows = x.shape[0]
    tm = P1_TILE
    steps = rows // tm
    tiles_per_batch = seq_len // tm
    n_batch = rows // seq_len
    row = lambda width: pl.BlockSpec((tm, width), lambda i: (i, 0))
    c, sq, sk, svt, sgate, sgatet, sm, smt = side
    n = sq.shape[0]
    sr = n // steps
    assert sr * steps == n and sr % 8 == 0
    full = lambda a: pl.BlockSpec(a.shape, lambda i: (0,) * a.ndim)
    srow = lambda a: pl.BlockSpec((sr,) + a.shape[1:], lambda i: (i,) + (0,) * (a.ndim - 1))
    return pl.pallas_call(
        functools.partial(_p1_kernel, tiles_per_batch),
        grid=(steps,),
        in_specs=[row(D_MODEL), _const_spec((1, D_MODEL)), _const_spec(w.shape),
                  _const_spec(b.shape), _const_spec((2 * HEADS, D_MODEL)),
                  _const_spec((2 * HEADS, 1)), _const_spec((CONV_W, CONV_DIM)),
                  srow(c), srow(sq), srow(sk), full(svt), srow(sgate), full(sgatet), srow(sm), full(smt)],
        out_specs=[row(CONV_DIM), row(MLSTM_DIM), row(MLSTM_DIM), row(MLSTM_DIM), row(MLSTM_DIM),
                   pl.BlockSpec((1, 4 * HEADS, tm), lambda i: (i // tiles_per_batch, 0, i % tiles_per_batch)),
                   pl.BlockSpec((1, CONV_W - 1, CONV_DIM), lambda i: (i // tiles_per_batch, 0, 0)),
                   srow(c), full(svt)],
        out_shape=[jax.ShapeDtypeStruct((rows, CONV_DIM), BF16)]
        + [jax.ShapeDtypeStruct((rows, MLSTM_DIM), BF16)] * 4
        + [jax.ShapeDtypeStruct((n_batch, 4 * HEADS, seq_len), F32),
           jax.ShapeDtypeStruct((n_batch, CONV_W - 1, CONV_DIM), F32),
           jax.ShapeDtypeStruct(c.shape, F32), jax.ShapeDtypeStruct(svt.shape, F32)],
        scratch_shapes=[pltpu.VMEM((tm + 8, CONV_DIM), F32)],
        compiler_params=_params(("arbitrary",)),
        name="p1_inproj_conv",
    )(x, g, w, b, wg, bg, cw, c, sq, sk, svt, sgate, sgatet, sm, smt)


def _p2_kernel(q_ref, k_ref, v_ref, so_ref, yconv_ref, gate_ref, x_ref, wout_ref, gmh_ref,
               hp_ref, pc_ref, pn_ref, pm_ref, c_s, m_s, y_s):
    nb, L = q_ref.shape[0], q_ref.shape[1]
    c = pl.program_id(1)

    @pl.when(c == 0)
    def _():
        c_s[...] = jnp.zeros_like(c_s)
        m_s[...] = jnp.zeros_like(m_s)

    row = lax.broadcasted_iota(jnp.int32, (L, L), 0)
    col = lax.broadcasted_iota(jnp.int32, (L, L), 1)
    causal = row >= col

    for bi in range(nb):
        gt = gate_ref[bi]
        for h in range(HEADS):
            sl = slice(h * DQK, (h + 1) * DQK)
            q = q_ref[bi, :, sl]
            k = k_ref[bi, :, sl]
            v = v_ref[bi, :, sl]
            lf_r = gt[HEADS + h:HEADS + h + 1, :]
            a_r = gt[h:h + 1, :] - gt[3 * HEADS + h:3 * HEADS + h + 1, :]
            m_prev = jnp.max(m_s[bi, h:h + 1, :], axis=1, keepdims=True)
            c_prev = c_s[bi, h]

            m_c = jnp.maximum(m_prev, jnp.max(jnp.where(causal, a_r, -jnp.inf), axis=1, keepdims=True))
            b_c = jnp.sum(jnp.where(causal, lf_r, 0.0), axis=1, keepdims=True)
            w = _dot_nt(q, k) * jnp.exp(jnp.where(causal, a_r - m_c, -jnp.inf))
            g = jnp.exp(m_prev - m_c)
            qc = _dot_nt(q, c_prev.astype(BF16))
            num = g * qc[:, 0:DV] + _dot(w.astype(BF16), v)
            den = g * qc[:, DV:2 * DV] + jnp.sum(w, axis=1, keepdims=True)
            hh = num / jnp.maximum(jnp.abs(den), jnp.exp(-(b_c + m_c)))
            hh = hh * lax.rsqrt(jnp.mean(hh * hh, axis=1, keepdims=True) + EPS) * gmh_ref[:, sl]
            y_s[bi * L:(bi + 1) * L, h * DV:(h + 1) * DV] = (so_ref[bi, :, sl].astype(F32) * hh).astype(BF16)

            m_last = jnp.maximum(m_prev, jnp.max(a_r, axis=1, keepdims=True))
            b_last = jnp.sum(lf_r, axis=1, keepdims=True)
            s_r = jnp.exp(a_r - m_last)
            sv_t = jnp.concatenate([v.T.astype(F32) * s_r, jnp.broadcast_to(s_r, (DV, L))], axis=0)
            c_s[bi, h] = jnp.exp(m_prev - m_last) * c_prev + _dot(sv_t.astype(BF16), k)
            m_s[bi, h:h + 1, :] = jnp.broadcast_to(b_last + m_last, (1, m_s.shape[2]))

    w_conv = wout_ref[0:CONV_DIM, :].astype(BF16)
    w_ml = wout_ref[CONV_DIM:CONV_DIM + MLSTM_DIM, :].astype(BF16)
    for bi in range(nb):
        out = _dot(yconv_ref[bi], w_conv) + _dot(y_s[bi * L:(bi + 1) * L, :], w_ml)
        hp_ref[bi] = x_ref[bi] + out

    @pl.when(c == pl.num_programs(1) - 1)
    def _():
        lane = lax.broadcasted_iota(jnp.int32, (1, m_s.shape[2]), 1)
        for bi in range(nb):
            acc = jnp.zeros((1, m_s.shape[2]), F32)
            for h in range(HEADS):
                pc_ref[bi, h] = c_s[bi, h, 0:DV, :]
                pn_ref[bi, h:h + 1, :] = c_s[bi, h, DV:DV + 1, :]
                acc = jnp.where(lane == h, m_s[bi, h:h + 1, :], acc)
            pm_ref[bi] = acc


def _p2_call(q, k, v, so, yconv, gates, x, wout, gmh, n_batch, seq_len):
    L = MLSTM_CHUNK
    nb = P2_SEQS
    nc = seq_len // L
    seq = lambda width: pl.BlockSpec((nb, L, width), lambda b, c: (b, c, 0))
    as_seq = lambda a: a.reshape(n_batch, seq_len, a.shape[-1])
    return pl.pallas_call(
        _p2_kernel,
        grid=(n_batch // nb, nc),
        in_specs=[seq(MLSTM_DIM), seq(MLSTM_DIM), seq(MLSTM_DIM), seq(MLSTM_DIM), seq(CONV_DIM),
                  pl.BlockSpec((nb, 4 * HEADS, L), lambda b, c: (b, 0, c)), seq(D_MODEL),
                  _const_spec((D_MODEL, D_MODEL)), _const_spec((1, MLSTM_DIM))],
        out_specs=[seq(D_MODEL),
                   pl.BlockSpec((nb, HEADS, DV, DQK), lambda b, c: (b, 0, 0, 0)),
                   pl.BlockSpec((nb, HEADS, DQK), lambda b, c: (b, 0, 0)),
                   pl.BlockSpec((nb, 1, 128), lambda b, c: (b, 0, 0))],
        out_shape=[jax.ShapeDtypeStruct((n_batch, seq_len, D_MODEL), F32),
                   jax.ShapeDtypeStruct((n_batch, HEADS, DV, DQK), F32),
                   jax.ShapeDtypeStruct((n_batch, HEADS, DQK), F32),
                   jax.ShapeDtypeStruct((n_batch, 1, 128), F32)],
        scratch_shapes=[pltpu.VMEM((nb, HEADS, 2 * DV, DQK), F32), pltpu.VMEM((nb, 8, 128), F32),
                        pltpu.VMEM((nb * L, MLSTM_DIM), BF16)],
        compiler_params=_params(("arbitrary", "arbitrary")),
        name="p2_mlstm_outproj",
    )(as_seq(q), as_seq(k), as_seq(v), as_seq(so), as_seq(yconv), gates, as_seq(x), wout, gmh)


def _pm_kernel(mem_ref, g_ref, w_ref, k_ref, v_ref, kb_ref, vb_ref):
    xn = _rmsnorm(mem_ref[...], g_ref[...]).astype(BF16)
    kk = _dot(xn, w_ref[:, 0:D_MODEL].astype(BF16))
    vv = _dot(xn, w_ref[:, D_MODEL:2 * D_MODEL].astype(BF16))
    for h in range(X_HEADS):
        sl = slice(h * X_HEAD_DIM, (h + 1) * X_HEAD_DIM)
        k_ref[:, h, :] = kk[:, sl]
        v_ref[:, h, :] = vv[:, sl]
    kb_ref[...] = kk.astype(BF16)
    vb_ref[...] = vv.astype(BF16)


def _pm_call(mem, g, w):
    rows = mem.shape[0]
    tm = ROW_TILE
    row = pl.BlockSpec((tm, D_MODEL), lambda i: (i, 0))
    row4 = pl.BlockSpec((tm, X_HEADS, X_HEAD_DIM), lambda i: (i, 0, 0))
    return pl.pallas_call(
        _pm_kernel,
        grid=(rows // tm,),
        in_specs=[row, _const_spec((1, D_MODEL)), _const_spec((D_MODEL, 2 * D_MODEL))],
        out_specs=[row4, row4, row, row],
        out_shape=[jax.ShapeDtypeStruct((rows, X_HEADS, X_HEAD_DIM), F32)] * 2
        + [jax.ShapeDtypeStruct((rows, D_MODEL), BF16)] * 2,
        compiler_params=_params(("arbitrary",)),
        name="pm_mem_kv",
    )(mem, g, w)


def _p3_kernel(n_cast, hp_ref, g_ref, wq_ref, k_ref, v_ref, *rest):
    for src, dst in zip(rest[:n_cast], rest[n_cast + 1:]):
        dst[...] = src[...].astype(BF16)
    o_ref = rest[n_cast]
    xn = _rmsnorm(hp_ref[...], g_ref[...]).astype(BF16)
    q = _dot(xn, wq_ref[...].astype(BF16))
    for h in range(X_HEADS):
        sl = slice(h * X_HEAD_DIM, (h + 1) * X_HEAD_DIM)
        s = _dot_nt(q[:, sl].astype(BF16), k_ref[0, :, sl]) * (X_HEAD_DIM ** -0.5)
        e = jnp.exp(s - jnp.max(s, axis=1, keepdims=True))
        p = e * (1.0 / jnp.sum(e, axis=1, keepdims=True))
        o_ref[:, sl] = _dot(p.astype(BF16), v_ref[0, :, sl]).astype(BF16)


def _p3_call(hp, g, wq, kb, vb, seq_len, cast=()):
    rows = hp.shape[0]
    tm = P3_TILE
    steps = rows // tm
    tiles_per_batch = seq_len // tm
    row = pl.BlockSpec((tm, D_MODEL), lambda i: (i, 0))
    mem = pl.BlockSpec((1, N_MEM, D_MODEL), lambda i: (i // tiles_per_batch, 0, 0))
    slabs = []
    for wgt in cast:
        slab = wgt.shape[0] // steps
        assert slab * steps == wgt.shape[0] and slab % 16 == 0
        slabs.append(pl.BlockSpec((slab, wgt.shape[1]), lambda i: (i, 0)))
    return pl.pallas_call(
        functools.partial(_p3_kernel, len(cast)),
        grid=(steps,),
        in_specs=[row, _const_spec((1, D_MODEL)), _const_spec((D_MODEL, D_MODEL)), mem, mem] + slabs,
        out_specs=[row] + slabs,
        out_shape=[jax.ShapeDtypeStruct((rows, D_MODEL), BF16)]
        + [jax.ShapeDtypeStruct(wgt.shape, BF16) for wgt in cast],
        compiler_params=_params(("arbitrary",)),
        name="p3_cross_attn",
    )(hp, g, wq, kb, vb, *cast)


def _cache_attention_row(q8, kc_ref, vc_ref, bl):
    qs = q8 * (X_HEAD_DIM ** -0.5)
    m_run = jnp.full((1, PACK_ROWS, 1), -jnp.inf, F32)
    l_run = jnp.zeros((1, PACK_ROWS, 1), F32)
    acc = jnp.zeros((PACK_ROWS, 128), F32)
    for c in range(N_MEM // MEM_CHUNK):
        blk = slice(c * MEM_CHUNK, (c + 1) * MEM_CHUNK)
        prod = kc_ref[bl, blk] * qs
        s = jnp.sum(prod + pltpu.roll(prod, X_HEADS, 1), axis=-1, keepdims=True)
        m_new = jnp.maximum(m_run, jnp.max(s, axis=0, keepdims=True))
        alpha = jnp.exp(m_run - m_new)
        e = jnp.exp(s - m_new)
        l_run = alpha * l_run + jnp.sum(e, axis=0, keepdims=True)
        acc = alpha[0] * acc + jnp.sum(e * vc_ref[bl, blk], axis=0)
        m_run = m_new
    return acc * (1.0 / l_run[0])


def _p4_kernel(side_rows, hp_ref, o_ref, wxo_ref, gf_ref, wgu_ref, wd_ref, gfin_ref, *rest):
    if side_rows:
        q4_ref, kc_ref, vc_ref, y_ref, os_ref, act_s = rest
        for bl in range(side_rows):
            os_ref[bl] = _cache_attention_row(q4_ref[bl], kc_ref, vc_ref, bl)
    else:
        y_ref, act_s = rest
    if len(o_ref.shape) == 3:
        o = jnp.concatenate([o_ref[:, half * X_HEADS + h, :] for h in range(X_HEADS)
                             for half in range(X_HEAD_DIM // 128)], axis=1)
    else:
        o = o_ref[...]
    hp = hp_ref[...] + _dot(o.astype(BF16), wxo_ref[...])
    xn = _rmsnorm(hp, gf_ref[...]).astype(BF16)
    for j in range(D_FF // FF_CHUNK):
        g = _dot(xn, wgu_ref[:, FF_CHUNK * j:FF_CHUNK * (j + 1)])
        u = _dot(xn, wgu_ref[:, D_FF + FF_CHUNK * j:D_FF + FF_CHUNK * (j + 1)])
        act_s[:, FF_CHUNK * j:FF_CHUNK * (j + 1)] = (g * jax.nn.sigmoid(g) * u).astype(BF16)
    hp = hp + _dot(act_s[...], wd_ref[...])
    y = _rmsnorm(hp, gfin_ref[...])
    if len(y_ref.shape) == 3:
        y_ref[:, 0, :] = y
    else:
        y_ref[...] = y


def _p4_call(hp, o, wxo, gf, wgu, wd, gfin, tm, side=None, rows_3d=False):
    rows = hp.shape[0]
    steps = rows // tm
    row = pl.BlockSpec((tm, D_MODEL), lambda i: (i, 0))
    o_spec = row if o.ndim == 2 else pl.BlockSpec((tm,) + o.shape[1:], lambda i: (i, 0, 0))
    in_specs = [row, o_spec, _const_spec((D_MODEL, D_MODEL)), _const_spec((1, D_MODEL)),
                _const_spec((D_MODEL, 2 * D_FF)), _const_spec((D_FF, D_MODEL)),
                _const_spec((1, D_MODEL))]
    if rows_3d:
        out_specs = [pl.BlockSpec((tm, 1, D_MODEL), lambda i: (i, 0, 0))]
        out_shape = [jax.ShapeDtypeStruct((rows, 1, D_MODEL), F32)]
    else:
        out_specs = [row]
        out_shape = [jax.ShapeDtypeStruct((rows, D_MODEL), F32)]
    args = [hp, o, wxo, gf, wgu, wd, gfin]
    side_rows = 0
    if side is not None:
        q4, kc, vc = side
        side_rows = q4.shape[0] // steps
        assert side_rows * steps == q4.shape[0]
        srow = pl.BlockSpec((side_rows, PACK_ROWS, 128), lambda i: (i, 0, 0))
        cache = pl.BlockSpec((side_rows, N_MEM, PACK_ROWS, 128), lambda i: (i, 0, 0, 0))
        in_specs += [srow, cache, cache]
        out_specs += [srow]
        out_shape += [jax.ShapeDtypeStruct(q4.shape, F32)]
        args += [q4, kc, vc]
    return pl.pallas_call(
        functools.partial(_p4_kernel, side_rows),
        grid=(steps,),
        in_specs=in_specs,
        out_specs=out_specs,
        out_shape=out_shape,
        scratch_shapes=[pltpu.VMEM((tm, D_FF), BF16)],
        compiler_params=_params(("arbitrary",)),
        name="p4_ffn_final",
    )(*args)


def _s1_kernel(x_ref, g_ref, w_ref, b_ref, wg_ref, bg_ref, cw_ref, st_ref,
               yconv_ref, sconv_ref, q_ref, k_ref, v_ref, so_ref, gate_ref, vt_ref, gatet_ref):
    xn = _rmsnorm(x_ref[:, 0, :], g_ref[...]).astype(BF16)

    def seg(j):
        sl = slice(j * CONV_DIM, (j + 1) * CONV_DIM)
        return _dot_nt(xn, w_ref[sl, :].astype(BF16)) + b_ref[:, sl]

    u = seg(1) * seg(2)
    st0 = st_ref[:, 0, :]
    st1 = st_ref[:, 1, :]
    conv = cw_ref[0:1, :] * st0 + cw_ref[1:2, :] * st1 + cw_ref[2:3, :] * u
    yconv_ref[...] = seg(0) * conv
    sconv_ref[:, 0, :] = st1
    sconv_ref[:, 1, :] = u
    q_ref[...] = seg(3)
    k_ref[...] = seg(4) * (DQK ** -0.5)
    v = seg(5)
    v_ref[...] = v
    so_ref[...] = jax.nn.sigmoid(seg(6))
    n_gate = wg_ref.shape[0]
    gate_ref[...] = _gate_transform(_dot_nt(xn, wg_ref[...]) + b_ref[:, MAIN_DIM:MAIN_DIM + n_gate])
    for h in range(HEADS):
        sl = slice(h * DV, (h + 1) * DV)
        vt_ref[sl, :] = v[:, sl].T
    gatet_ref[...] = _gate_transform_rows(_dot_nt(wg_ref[...], xn) + bg_ref[...])


def _s1_call(x, g, w, b, wg, bg, cw, st):
    n = x.shape[0]
    n_gate = wg.shape[0]
    full = lambda *shape: pl.BlockSpec(shape, lambda i: (0,) * len(shape))
    ins = [x, g, w, b, wg, bg, cw, st]
    return pl.pallas_call(
        _s1_kernel,
        grid=(1,),
        in_specs=[full(*a.shape) for a in ins],
        out_specs=[full(n, CONV_DIM), full(*st.shape), full(n, MLSTM_DIM), full(n, MLSTM_DIM),
                   full(n, MLSTM_DIM), full(n, MLSTM_DIM), full(n, n_gate), full(MLSTM_DIM, n),
                   full(n_gate, n)],
        out_shape=[jax.ShapeDtypeStruct((n, CONV_DIM), F32),
                   jax.ShapeDtypeStruct(st.shape, F32)]
        + [jax.ShapeDtypeStruct((n, MLSTM_DIM), F32)] * 4
        + [jax.ShapeDtypeStruct((n, n_gate), F32),
           jax.ShapeDtypeStruct((MLSTM_DIM, n), F32),
           jax.ShapeDtypeStruct((n_gate, n), F32)],
        compiler_params=_params(("arbitrary",)),
        name="s1_inproj_conv",
    )(*ins)


def _s3_kernel(cqt_ref, q_ref, k_ref, v_ref, so_ref, gate_ref, n_ref, m_ref, yconv_ref, x_ref,
               wout_ref, gmh_ref, gx_ref, wq_ref,
               hs_ref, qx_ref, nn_ref, mn_ref, y_s):
    n_rows = q_ref.shape[0]
    y_s[:, 0:CONV_DIM] = yconv_ref[...].astype(BF16)
    lane = lax.broadcasted_iota(jnp.int32, (n_rows, mn_ref.shape[1]), 1)
    m_out = jnp.zeros((n_rows, mn_ref.shape[1]), F32)
    for h in range(HEADS):
        sl = slice(h * DQK, (h + 1) * DQK)
        q = q_ref[:, sl]
        k = k_ref[:, sl]
        v = v_ref[:, sl]
        n_prev = n_ref[:, h, :]
        cq = cqt_ref[sl, :].T
        ig = gate_ref[:, h:h + 1]
        lf = gate_ref[:, HEADS + h:HEADS + h + 1]
        m_prev = m_ref[:, h:h + 1]
        inter = lf + m_prev
        m_row = jnp.maximum(inter, ig)
        wgt = jnp.sum(q * k, axis=1, keepdims=True) * jnp.exp(ig - m_row)
        g = jnp.exp(inter - m_row)
        num = g * cq + wgt * v
        den = g * jnp.sum(n_prev * q, axis=1, keepdims=True) + wgt
        hh = num / jnp.maximum(jnp.abs(den), jnp.exp(-m_row))
        hh = hh * lax.rsqrt(jnp.mean(hh * hh, axis=1, keepdims=True) + EPS) * gmh_ref[:, sl]
        y_s[:, CONV_DIM + h * DV:CONV_DIM + (h + 1) * DV] = (so_ref[:, sl] * hh).astype(BF16)
        nn_ref[:, h, :] = g * n_prev + jnp.exp(ig - m_row) * k
        m_out = jnp.where(lane == h, m_row, m_out)
    mn_ref[...] = m_out
    hs = x_ref[:, 0, :] + _dot(y_s[...], wout_ref[...].astype(BF16))
    hs_ref[...] = hs
    qx = _dot(_rmsnorm(hs, gx_ref[...]).astype(BF16), wq_ref[...].astype(BF16))
    for h in range(X_HEADS):
        for half in range(X_HEAD_DIM // 128):
            lo = h * X_HEAD_DIM + half * 128
            qx_ref[:, half * X_HEADS + h, :] = qx[:, lo:lo + 128]


def _s3_call(cqt, q, k, v, so, gates, nst, m, yconv, x, wout, gmh, gx, wq):
    n = q.shape[0]
    full = lambda *shape: pl.BlockSpec(shape, lambda i: (0,) * len(shape))
    ins = [cqt, q, k, v, so, gates, nst, m, yconv, x, wout, gmh, gx, wq]
    return pl.pallas_call(
        _s3_kernel,
        grid=(1,),
        in_specs=[full(*a.shape) for a in ins],
        out_specs=[full(n, D_MODEL), full(n, PACK_ROWS, 128), full(*nst.shape), full(n, 128)],
        out_shape=[jax.ShapeDtypeStruct((n, D_MODEL), F32), jax.ShapeDtypeStruct((n, PACK_ROWS, 128), F32),
                   jax.ShapeDtypeStruct(nst.shape, F32), jax.ShapeDtypeStruct((n, 128), F32)],
        scratch_shapes=[pltpu.VMEM((n, D_MODEL), BF16)],
        compiler_params=_params(("arbitrary",)),
        name="s3_mlstm_finish",
    )(*ins)


def _pack_heads(a):
    lead = a.shape[:-2]
    a = a.reshape(lead + (X_HEADS, X_HEAD_DIM // 128, 128))
    return jnp.swapaxes(a, -3, -2).reshape(lead + (PACK_ROWS, 128))


def kernel(x_prompt, x_sample, mem_prompt, state_conv, state_mlstm_C, state_mlstm_n, state_mlstm_m,
           cache_mem_k, cache_mem_v, g_mix, w_in, b_in, conv_w, g_mh, w_out, g_cross, g_mem,
           w_xq, w_xkv, w_xo, g_ffn, w_gu, w_down, g_final):
    n_batch, seq_len, _ = x_prompt.shape
    n_dec = x_sample.shape[0]
    depth = w_in.shape[0]
    assert depth == 1 and x_sample.shape[1] == 1
    assert all(seq_len % t == 0 for t in (ROW_TILE, P1_TILE, P3_TILE))
    assert P1_TILE % MLSTM_CHUNK == 0 and n_batch % P2_SEQS == 0

    n_gate = 2 * HEADS
    w_in_b = w_in[0].T
    b_in_r = b_in[0].reshape(1, MAIN_DIM + n_gate)
    w_gate_r = w_in_b[MAIN_DIM:].astype(BF16)
    b_gate_r = b_in[0, MAIN_DIM:].reshape(n_gate, 1)
    w_out_b = w_out[0]
    w_xq_b = w_xq[0]
    w_xkv_b = w_xkv[0]
    g_mix_r = g_mix[0].reshape(1, D_MODEL)
    g_cross_r = g_cross[0].reshape(1, D_MODEL)
    g_mem_r = g_mem[0].reshape(1, D_MODEL)
    g_ffn_r = g_ffn[0].reshape(1, D_MODEL)
    g_final_r = g_final.reshape(1, D_MODEL)
    g_mh_r = g_mh[0].reshape(1, MLSTM_DIM)
    cw = conv_w[0]

    m0 = state_mlstm_m[0]
    s_yconv, s_conv, sq, sk, sv, sso, sgates, svt, sgates_t = _s1_call(
        x_sample, g_mix_r, w_in_b, b_in_r, w_gate_r, b_gate_r, cw, state_conv[0])

    xp = x_prompt.reshape(n_batch * seq_len, D_MODEL)
    yconv, q, k, v, so, gates, p_conv, s_c, cqt = _p1_call(
        xp, g_mix_r, w_in_b, b_in_r, w_gate_r, b_gate_r, cw, seq_len,
        side=(state_mlstm_C[0], sq, sk, svt, sgates, sgates_t, m0, m0.T))
    hs1, qx, s_n, s_m = _s3_call(cqt, sq, sk, sv, sso, sgates,
                                 state_mlstm_n[0], m0, s_yconv, x_sample,
                                 w_out_b, g_mh_r, g_cross_r, w_xq_b)
    hp1, p_c, p_n, p_m = _p2_call(q, k, v, so, yconv, gates, xp, w_out_b, g_mh_r, n_batch, seq_len)
    hp1 = hp1.reshape(n_batch * seq_len, D_MODEL)
    pk, pv, pkb, pvb = _pm_call(mem_prompt.reshape(n_batch * N_MEM, D_MODEL), g_mem_r, w_xkv_b)
    o_p, w_xo_b, w_gu_b, w_down_b = _p3_call(
        hp1, g_cross_r, w_xq_b, pkb.reshape(n_batch, N_MEM, D_MODEL), pvb.reshape(n_batch, N_MEM, D_MODEL),
        seq_len, cast=(w_xo[0], w_gu[0], w_down[0]))
    y_p, o_s = _p4_call(hp1, o_p, w_xo_b, g_ffn_r, w_gu_b, w_down_b, g_final_r, ROW_TILE,
                        side=(qx, _pack_heads(cache_mem_k[0]), _pack_heads(cache_mem_v[0])))

    y_s, = _p4_call(hs1, o_s, w_xo_b, g_ffn_r, w_gu_b, w_down_b, g_final_r, n_dec, rows_3d=True)

    mem_shape = (1, n_batch, N_MEM, X_HEADS, X_HEAD_DIM)
    return (y_p.reshape(n_batch, seq_len, D_MODEL),
            y_s.reshape(n_dec, 1, D_MODEL),
            p_conv.reshape(1, n_batch, CONV_W - 1, CONV_DIM),
            p_c.reshape(1, n_batch, HEADS, DV, DQK),
            p_n.reshape(1, n_batch, HEADS, DQK),
            p_m[:, 0, :HEADS].reshape(1, n_batch, HEADS),
            pk.reshape(mem_shape),
            pv.reshape(mem_shape),
            s_conv.reshape(1, n_dec, CONV_W - 1, CONV_DIM),
            s_c.reshape(1, n_dec, HEADS, DV, DQK),
            s_n.reshape(1, n_dec, HEADS, DQK),
            s_m[:, :HEADS].reshape(1, n_dec, HEADS))
```

```python
import functools

import jax
import jax.numpy as jnp
from jax import lax
from jax.experimental import pallas as pl
from jax.experimental.pallas import tpu as pltpu

F32 = jnp.float32
BF16 = jnp.bfloat16

D_MODEL = 1024
CONV_DIM = 512
CONV_W = 3
MLSTM_DIM = 512
HEADS = 4
DQK = 128
DV = 128
N_MEM = 256
X_HEADS = 4
X_HEAD_DIM = 256
D_FF = 2816
MAIN_DIM = 3 * CONV_DIM + 4 * MLSTM_DIM
EPS = 1e-6

MLSTM_CHUNK = 256
P2_SEQS = 4
ROW_TILE = 512
P1_TILE = 1024
P3_TILE = 1024
FF_CHUNK = 256
PACK_ROWS = X_HEADS * (X_HEAD_DIM // 128)
MEM_CHUNK = 32
VMEM_LIMIT = 56 * 1024 * 1024


def _dot(a, b):
    return jnp.dot(a, b, preferred_element_type=F32)


def _dot_nt(a, b):
    return lax.dot_general(a, b, (((1,), (1,)), ((), ())), preferred_element_type=F32)


def _rmsnorm(x, g):
    return x * lax.rsqrt(jnp.mean(x * x, axis=-1, keepdims=True) + EPS) * g


def _const_spec(shape):
    zeros = (0,) * len(shape)
    return pl.BlockSpec(shape, lambda *_: zeros, pipeline_mode=pl.Buffered(1))


def _params(sem):
    return pltpu.CompilerParams(dimension_semantics=sem, vmem_limit_bytes=VMEM_LIMIT)


def _gate_transform(gt):
    lane = lax.broadcasted_iota(jnp.int32, gt.shape, 1)
    return jnp.where(lane < HEADS, gt, jax.nn.log_sigmoid(gt))


def _gate_transform_rows(gt):
    sub = lax.broadcasted_iota(jnp.int32, gt.shape, 0)
    return jnp.where(sub < HEADS, gt, jax.nn.log_sigmoid(gt))


def _memory_update_rows(i, c_ref, q_ref, k_ref, vt_ref, gate_ref, gatet_ref, m_ref, mt_ref, cn_ref, cqt_ref):
    n = vt_ref.shape[1]
    bb = c_ref.shape[0]

    @pl.when(i == 0)
    def _():
        cqt_ref[...] = jnp.zeros_like(cqt_ref)

    lane = lax.broadcasted_iota(jnp.int32, (DV, n), 1)
    for h in range(HEADS):
        sl = slice(h * DQK, (h + 1) * DQK)
        ig_c = gate_ref[:, h:h + 1]
        lf_c = gate_ref[:, HEADS + h:HEADS + h + 1]
        m_c = m_ref[:, h:h + 1]
        dec = jnp.broadcast_to(jnp.exp(lf_c + m_c - jnp.maximum(lf_c + m_c, ig_c)), (bb, DQK))
        ig_r = gatet_ref[h:h + 1, :]
        lf_r = gatet_ref[HEADS + h:HEADS + h + 1, :]
        m_r = mt_ref[h:h + 1, :]
        svt = vt_ref[sl, :] * jnp.exp(ig_r - jnp.maximum(lf_r + m_r, ig_r))
        q_t = q_ref[:, sl]
        k_t = k_ref[:, sl]
        cqt = cqt_ref[sl, :]
        for bl in range(bb):
            onehot = lane == i * bb + bl
            c = c_ref[bl, h]
            cq_col = jnp.sum(c * q_t[bl:bl + 1, :], axis=1, keepdims=True)
            sv_col = jnp.sum(jnp.where(onehot, svt, 0.0), axis=1, keepdims=True)
            cn_ref[bl, h] = dec[bl:bl + 1, :] * c + sv_col * k_t[bl:bl + 1, :]
            cqt = jnp.where(onehot, cq_col, cqt)
        cqt_ref[sl, :] = cqt


def _p1_kernel(tiles_per_batch, x_ref, g_ref, w_ref, b_ref, wg_ref, bg_ref, cw_ref,
               c_ref, sq_ref, sk_ref, svt_ref, sgate_ref, sgatet_ref, sm_ref, smt_ref,
               yconv_ref, q_ref, k_ref, v_ref, so_ref, gate_ref, pconv_ref, cn_ref, cqt_ref, ubuf):
    tm = x_ref.shape[0]
    i = pl.program_id(0)
    _memory_update_rows(i, c_ref, sq_ref, sk_ref, svt_ref, sgate_ref, sgatet_ref, sm_ref, smt_ref,
                        cn_ref, cqt_ref)
    xn = _rmsnorm(x_ref[...], g_ref[...]).astype(BF16)

    def seg(j):
        sl = slice(j * CONV_DIM, (j + 1) * CONV_DIM)
        return _dot_nt(xn, w_ref[sl, :].astype(BF16)) + b_ref[:, sl]

    prev = ubuf[tm:tm + 8, :]
    ubuf[0:8, :] = jnp.where(i % tiles_per_batch == 0, jnp.zeros_like(prev), prev)
    ubuf[8:8 + tm, :] = seg(1) * seg(2)
    conv = (cw_ref[0:1, :] * ubuf[6:6 + tm, :] + cw_ref[1:2, :] * ubuf[7:7 + tm, :]
            + cw_ref[2:3, :] * ubuf[8:8 + tm, :])
    yconv_ref[...] = (seg(0) * conv).astype(BF16)
    pconv_ref[0] = ubuf[tm + 6:tm + 8, :]

    q_ref[...] = seg(3).astype(BF16)
    k_ref[...] = (seg(4) * (DQK ** -0.5)).astype(BF16)
    v_ref[...] = seg(5).astype(BF16)
    so_ref[...] = jax.nn.sigmoid(seg(6)).astype(BF16)
    gt = _gate_transform_rows(_dot_nt(wg_ref[...], xn) + bg_ref[...])
    n_gate = gt.shape[0]
    gate_ref[0, 0:n_gate, :] = gt
    L = MLSTM_CHUNK
    n_blk = tm // L
    hi = gt.astype(BF16).astype(F32)
    r1 = gt - hi
    mid = r1.astype(BF16).astype(F32)
    lo = r1 - mid
    terms = jnp.concatenate([t[:, j * L:(j + 1) * L] for t in (hi, mid, lo) for j in range(n_blk)], axis=0)
    tri = (lax.broadcasted_iota(jnp.int32, (L, L), 0) <= lax.broadcasted_iota(jnp.int32, (L, L), 1)).astype(BF16)
    parts = _dot(terms.astype(BF16), tri)
    for j in range(n_blk):
        rows = [parts[(t * n_blk + j) * n_gate:(t * n_blk + j + 1) * n_gate, :] for t in range(3)]
        gate_ref[0, n_gate:2 * n_gate, j * L:(j + 1) * L] = (rows[0] + rows[1]) + rows[2]


def _p1_call(x, g, w, b, wg, bg, cw, seq_len, side):
    rows = x.shape[0]
    tm = P1_TILE
    steps = rows // tm
    tiles_per_batch = seq_len // tm
    n_batch = rows // seq_len
    row = lambda width: pl.BlockSpec((tm, width), lambda i: (i, 0))
    c, sq, sk, svt, sgate, sgatet, sm, smt = side
    n = sq.shape[0]
    sr = n // steps
    assert sr * steps == n and sr % 8 == 0
    full = lambda a: pl.BlockSpec(a.shape, lambda i: (0,) * a.ndim)
    srow = lambda a: pl.BlockSpec((sr,) + a.shape[1:], lambda i: (i,) + (0,) * (a.ndim - 1))
    return pl.pallas_call(
        functools.partial(_p1_kernel, tiles_per_batch),
        grid=(steps,),
        in_specs=[row(D_MODEL), _const_spec((1, D_MODEL)), _const_spec(w.shape),
                  _const_spec(b.shape), _const_spec((2 * HEADS, D_MODEL)),
                  _const_spec((2 * HEADS, 1)), _const_spec((CONV_W, CONV_DIM)),
                  srow(c), srow(sq), srow(sk), full(svt), srow(sgate), full(sgatet), srow(sm), full(smt)],
        out_specs=[row(CONV_DIM), row(MLSTM_DIM), row(MLSTM_DIM), row(MLSTM_DIM), row(MLSTM_DIM),
                   pl.BlockSpec((1, 4 * HEADS, tm), lambda i: (i // tiles_per_batch, 0, i % tiles_per_batch)),
                   pl.BlockSpec((1, CONV_W - 1, CONV_DIM), lambda i: (i // tiles_per_batch, 0, 0)),
                   srow(c), full(svt)],
        out_shape=[jax.ShapeDtypeStruct((rows, CONV_DIM), BF16)]
        + [jax.ShapeDtypeStruct((rows, MLSTM_DIM), BF16)] * 4
        + [jax.ShapeDtypeStruct((n_batch, 4 * HEADS, seq_len), F32),
           jax.ShapeDtypeStruct((n_batch, CONV_W - 1, CONV_DIM), F32),
           jax.ShapeDtypeStruct(c.shape, F32), jax.ShapeDtypeStruct(svt.shape, F32)],
        scratch_shapes=[pltpu.VMEM((tm + 8, CONV_DIM), F32)],
        compiler_params=_params(("arbitrary",)),
        name="p1_inproj_conv",
    )(x, g, w, b, wg, bg, cw, c, sq, sk, svt, sgate, sgatet, sm, smt)


def _p2_kernel(q_ref, k_ref, v_ref, so_ref, yconv_ref, gate_ref, x_ref, wout_ref, gmh_ref,
               mem_ref, gmem_ref, wkv_ref,
               hp_ref, pc_ref, pn_ref, pm_ref, pk_ref, pv_ref, kb_ref, vb_ref, c_s, m_s, y_s):
    nb, L = q_ref.shape[0], q_ref.shape[1]
    c = pl.program_id(1)

    @pl.when(c == 0)
    def _():
        c_s[...] = jnp.zeros_like(c_s)
        m_s[...] = jnp.zeros_like(m_s)

    _pm_kernel(mem_ref, gmem_ref, wkv_ref, pk_ref, pv_ref, kb_ref, vb_ref)

    row = lax.broadcasted_iota(jnp.int32, (L, L), 0)
    col = lax.broadcasted_iota(jnp.int32, (L, L), 1)
    causal = row >= col

    for bi in range(nb):
        gt = gate_ref[bi]
        for h in range(HEADS):
            sl = slice(h * DQK, (h + 1) * DQK)
            q = q_ref[bi, :, sl]
            k = k_ref[bi, :, sl]
            v = v_ref[bi, :, sl]
            lf_r = gt[HEADS + h:HEADS + h + 1, :]
            a_r = gt[h:h + 1, :] - gt[3 * HEADS + h:3 * HEADS + h + 1, :]
            m_prev = jnp.max(m_s[bi, h:h + 1, :], axis=1, keepdims=True)
            c_prev = c_s[bi, h]

            m_c = jnp.maximum(m_prev, jnp.max(jnp.where(causal, a_r, -jnp.inf), axis=1, keepdims=True))
            b_c = jnp.sum(jnp.where(causal, lf_r, 0.0), axis=1, keepdims=True)
            w = _dot_nt(q, k) * jnp.exp(jnp.where(causal, a_r - m_c, -jnp.inf))
            g = jnp.exp(m_prev - m_c)
            qc = _dot_nt(q, c_prev.astype(BF16))
            num = g * qc[:, 0:DV] + _dot(w.astype(BF16), v)
            den = g * qc[:, DV:2 * DV] + jnp.sum(w, axis=1, keepdims=True)
            hh = num / jnp.maximum(jnp.abs(den), jnp.exp(-(b_c + m_c)))
            hh = hh * lax.rsqrt(jnp.mean(hh * hh, axis=1, keepdims=True) + EPS) * gmh_ref[:, sl]
            y_s[bi * L:(bi + 1) * L, h * DV:(h + 1) * DV] = (so_ref[bi, :, sl].astype(F32) * hh).astype(BF16)

            m_last = jnp.maximum(m_prev, jnp.max(a_r, axis=1, keepdims=True))
            b_last = jnp.sum(lf_r, axis=1, keepdims=True)
            s_r = jnp.exp(a_r - m_last)
            sv_t = jnp.concatenate([v.T.astype(F32) * s_r, jnp.broadcast_to(s_r, (DV, L))], axis=0)
            c_s[bi, h] = jnp.exp(m_prev - m_last) * c_prev + _dot(sv_t.astype(BF16), k)
            m_s[bi, h:h + 1, :] = jnp.broadcast_to(b_last + m_last, (1, m_s.shape[2]))

    w_conv = wout_ref[0:CONV_DIM, :].astype(BF16)
    w_ml = wout_ref[CONV_DIM:CONV_DIM + MLSTM_DIM, :].astype(BF16)
    for bi in range(nb):
        out = _dot(yconv_ref[bi], w_conv) + _dot(y_s[bi * L:(bi + 1) * L, :], w_ml)
        hp_ref[bi] = x_ref[bi] + out

    @pl.when(c == pl.num_programs(1) - 1)
    def _():
        lane = lax.broadcasted_iota(jnp.int32, (1, m_s.shape[2]), 1)
        for bi in range(nb):
            acc = jnp.zeros((1, m_s.shape[2]), F32)
            for h in range(HEADS):
                pc_ref[bi, h] = c_s[bi, h, 0:DV, :]
                pn_ref[bi, h:h + 1, :] = c_s[bi, h, DV:DV + 1, :]
                acc = jnp.where(lane == h, m_s[bi, h:h + 1, :], acc)
            pm_ref[bi] = acc


def _p2_call(q, k, v, so, yconv, gates, x, wout, gmh, n_batch, seq_len, mem, gmem, wkv):
    L = MLSTM_CHUNK
    nb = P2_SEQS
    nc = seq_len // L
    seq = lambda width: pl.BlockSpec((nb, L, width), lambda b, c: (b, c, 0))
    as_seq = lambda a: a.reshape(n_batch, seq_len, a.shape[-1])
    mrows = mem.shape[0]
    slab = mrows // ((n_batch // nb) * nc)
    assert slab * (n_batch // nb) * nc == mrows and slab % 16 == 0
    mrow = pl.BlockSpec((slab, D_MODEL), lambda b, c: (b * nc + c, 0))
    mrow4 = pl.BlockSpec((slab, X_HEADS, X_HEAD_DIM), lambda b, c: (b * nc + c, 0, 0))
    return pl.pallas_call(
        _p2_kernel,
        grid=(n_batch // nb, nc),
        in_specs=[seq(MLSTM_DIM), seq(MLSTM_DIM), seq(MLSTM_DIM), seq(MLSTM_DIM), seq(CONV_DIM),
                  pl.BlockSpec((nb, 4 * HEADS, L), lambda b, c: (b, 0, c)), seq(D_MODEL),
                  _const_spec((D_MODEL, D_MODEL)), _const_spec((1, MLSTM_DIM)),
                  mrow, _const_spec((1, D_MODEL)), _const_spec((D_MODEL, 2 * D_MODEL))],
        out_specs=[seq(D_MODEL),
                   pl.BlockSpec((nb, HEADS, DV, DQK), lambda b, c: (b, 0, 0, 0)),
                   pl.BlockSpec((nb, HEADS, DQK), lambda b, c: (b, 0, 0)),
                   pl.BlockSpec((nb, 1, 128), lambda b, c: (b, 0, 0)),
                   mrow4, mrow4, mrow, mrow],
        out_shape=[jax.ShapeDtypeStruct((n_batch, seq_len, D_MODEL), F32),
                   jax.ShapeDtypeStruct((n_batch, HEADS, DV, DQK), F32),
                   jax.ShapeDtypeStruct((n_batch, HEADS, DQK), F32),
                   jax.ShapeDtypeStruct((n_batch, 1, 128), F32)]
        + [jax.ShapeDtypeStruct((mrows, X_HEADS, X_HEAD_DIM), F32)] * 2
        + [jax.ShapeDtypeStruct((mrows, D_MODEL), BF16)] * 2,
        scratch_shapes=[pltpu.VMEM((nb, HEADS, 2 * DV, DQK), F32), pltpu.VMEM((nb, 8, 128), F32),
                        pltpu.VMEM((nb * L, MLSTM_DIM), BF16)],
        compiler_params=_params(("arbitrary", "arbitrary")),
        name="p2_mlstm_outproj",
    )(as_seq(q), as_seq(k), as_seq(v), as_seq(so), as_seq(yconv), gates, as_seq(x), wout, gmh, mem, gmem, wkv)


def _pm_kernel(mem_ref, g_ref, w_ref, k_ref, v_ref, kb_ref, vb_ref):
    xn = _rmsnorm(mem_ref[...], g_ref[...]).astype(BF16)
    kk = _dot(xn, w_ref[:, 0:D_MODEL].astype(BF16))
    vv = _dot(xn, w_ref[:, D_MODEL:2 * D_MODEL].astype(BF16))
    for h in range(X_HEADS):
        sl = slice(h * X_HEAD_DIM, (h + 1) * X_HEAD_DIM)
        k_ref[:, h, :] = kk[:, sl]
        v_ref[:, h, :] = vv[:, sl]
    kb_ref[...] = kk.astype(BF16)
    vb_ref[...] = vv.astype(BF16)


def _pm_call(mem, g, w):
    rows = mem.shape[0]
    tm = ROW_TILE
    row = pl.BlockSpec((tm, D_MODEL), lambda i: (i, 0))
    row4 = pl.BlockSpec((tm, X_HEADS, X_HEAD_DIM), lambda i: (i, 0, 0))
    return pl.pallas_call(
        _pm_kernel,
        grid=(rows // tm,),
        in_specs=[row, _const_spec((1, D_MODEL)), _const_spec((D_MODEL, 2 * D_MODEL))],
        out_specs=[row4, row4, row, row],
        out_shape=[jax.ShapeDtypeStruct((rows, X_HEADS, X_HEAD_DIM), F32)] * 2
        + [jax.ShapeDtypeStruct((rows, D_MODEL), BF16)] * 2,
        compiler_params=_params(("arbitrary",)),
        name="pm_mem_kv",
    )(mem, g, w)


def _p3_kernel(n_cast, hp_ref, g_ref, wq_ref, k_ref, v_ref, *rest):
    for src, dst in zip(rest[:n_cast], rest[n_cast + 1:]):
        dst[...] = src[...].astype(BF16)
    o_ref = rest[n_cast]
    xn = _rmsnorm(hp_ref[...], g_ref[...]).astype(BF16)
    q = _dot(xn, wq_ref[...].astype(BF16))
    for h in range(X_HEADS):
        sl = slice(h * X_HEAD_DIM, (h + 1) * X_HEAD_DIM)
        s = _dot_nt(q[:, sl].astype(BF16), k_ref[0, :, sl]) * (X_HEAD_DIM ** -0.5)
        e = jnp.exp(s - jnp.max(s, axis=1, keepdims=True))
        p = e * (1.0 / jnp.sum(e, axis=1, keepdims=True))
        o_ref[:, sl] = _dot(p.astype(BF16), v_ref[0, :, sl]).astype(BF16)


def _p3_call(hp, g, wq, kb, vb, seq_len, cast=()):
    rows = hp.shape[0]
    tm = P3_TILE
    steps = rows // tm
    tiles_per_batch = seq_len // tm
    row = pl.BlockSpec((tm, D_MODEL), lambda i: (i, 0))
    mem = pl.BlockSpec((1, N_MEM, D_MODEL), lambda i: (i // tiles_per_batch, 0, 0))
    slabs = []
    for wgt in cast:
        slab = wgt.shape[0] // steps
        assert slab * steps == wgt.shape[0] and slab % 16 == 0
        slabs.append(pl.BlockSpec((slab, wgt.shape[1]), lambda i: (i, 0)))
    return pl.pallas_call(
        functools.partial(_p3_kernel, len(cast)),
        grid=(steps,),
        in_specs=[row, _const_spec((1, D_MODEL)), _const_spec((D_MODEL, D_MODEL)), mem, mem] + slabs,
        out_specs=[row] + slabs,
        out_shape=[jax.ShapeDtypeStruct((rows, D_MODEL), BF16)]
        + [jax.ShapeDtypeStruct(wgt.shape, BF16) for wgt in cast],
        compiler_params=_params(("arbitrary",)),
        name="p3_cross_attn",
    )(hp, g, wq, kb, vb, *cast)


def _cache_attention_row(q8, kc_ref, vc_ref, bl):
    qs = q8 * (X_HEAD_DIM ** -0.5)
    m_run = jnp.full((1, PACK_ROWS, 1), -jnp.inf, F32)
    l_run = jnp.zeros((1, PACK_ROWS, 1), F32)
    acc = jnp.zeros((PACK_ROWS, 128), F32)
    for c in range(N_MEM // MEM_CHUNK):
        blk = slice(c * MEM_CHUNK, (c + 1) * MEM_CHUNK)
        prod = kc_ref[bl, blk] * qs
        s = jnp.sum(prod + pltpu.roll(prod, X_HEADS, 1), axis=-1, keepdims=True)
        m_new = jnp.maximum(m_run, jnp.max(s, axis=0, keepdims=True))
        alpha = jnp.exp(m_run - m_new)
        e = jnp.exp(s - m_new)
        l_run = alpha * l_run + jnp.sum(e, axis=0, keepdims=True)
        acc = alpha[0] * acc + jnp.sum(e * vc_ref[bl, blk], axis=0)
        m_run = m_new
    return acc * (1.0 / l_run[0])


def _p4_kernel(side_rows, hp_ref, o_ref, wxo_ref, gf_ref, wgu_ref, wd_ref, gfin_ref, *rest):
    if side_rows:
        q4_ref, kc_ref, vc_ref, y_ref, os_ref, act_s = rest
        for bl in range(side_rows):
            os_ref[bl] = _cache_attention_row(q4_ref[bl], kc_ref, vc_ref, bl)
    else:
        y_ref, act_s = rest
    if len(o_ref.shape) == 3:
        o = jnp.concatenate([o_ref[:, half * X_HEADS + h, :] for h in range(X_HEADS)
                             for half in range(X_HEAD_DIM // 128)], axis=1)
    else:
        o = o_ref[...]
    hp = hp_ref[...] + _dot(o.astype(BF16), wxo_ref[...])
    xn = _rmsnorm(hp, gf_ref[...]).astype(BF16)
    for j in range(D_FF // FF_CHUNK):
        g = _dot(xn, wgu_ref[:, FF_CHUNK * j:FF_CHUNK * (j + 1)])
        u = _dot(xn, wgu_ref[:, D_FF + FF_CHUNK * j:D_FF + FF_CHUNK * (j + 1)])
        act_s[:, FF_CHUNK * j:FF_CHUNK * (j + 1)] = (g * jax.nn.sigmoid(g) * u).astype(BF16)
    hp = hp + _dot(act_s[...], wd_ref[...])
    y = _rmsnorm(hp, gfin_ref[...])
    if len(y_ref.shape) == 3:
        y_ref[:, 0, :] = y
    else:
        y_ref[...] = y


def _p4_call(hp, o, wxo, gf, wgu, wd, gfin, tm, side=None, rows_3d=False):
    rows = hp.shape[0]
    steps = rows // tm
    row = pl.BlockSpec((tm, D_MODEL), lambda i: (i, 0))
    o_spec = row if o.ndim == 2 else pl.BlockSpec((tm,) + o.shape[1:], lambda i: (i, 0, 0))
    in_specs = [row, o_spec, _const_spec((D_MODEL, D_MODEL)), _const_spec((1, D_MODEL)),
                _const_spec((D_MODEL, 2 * D_FF)), _const_spec((D_FF, D_MODEL)),
                _const_spec((1, D_MODEL))]
    if rows_3d:
        out_specs = [pl.BlockSpec((tm, 1, D_MODEL), lambda i: (i, 0, 0))]
        out_shape = [jax.ShapeDtypeStruct((rows, 1, D_MODEL), F32)]
    else:
        out_specs = [row]
        out_shape = [jax.ShapeDtypeStruct((rows, D_MODEL), F32)]
    args = [hp, o, wxo, gf, wgu, wd, gfin]
    side_rows = 0
    if side is not None:
        q4, kc, vc = side
        side_rows = q4.shape[0] // steps
        assert side_rows * steps == q4.shape[0]
        srow = pl.BlockSpec((side_rows, PACK_ROWS, 128), lambda i: (i, 0, 0))
        cache = pl.BlockSpec((side_rows, N_MEM, PACK_ROWS, 128), lambda i: (i, 0, 0, 0))
        in_specs += [srow, cache, cache]
        out_specs += [srow]
        out_shape += [jax.ShapeDtypeStruct(q4.shape, F32)]
        args += [q4, kc, vc]
    return pl.pallas_call(
        functools.partial(_p4_kernel, side_rows),
        grid=(steps,),
        in_specs=in_specs,
        out_specs=out_specs,
        out_shape=out_shape,
        scratch_shapes=[pltpu.VMEM((tm, D_FF), BF16)],
        compiler_params=_params(("arbitrary",)),
        name="p4_ffn_final",
    )(*args)


def _s1_kernel(x_ref, g_ref, w_ref, b_ref, wg_ref, bg_ref, cw_ref, st_ref,
               yconv_ref, sconv_ref, q_ref, k_ref, v_ref, so_ref, gate_ref, vt_ref, gatet_ref):
    xn = _rmsnorm(x_ref[:, 0, :], g_ref[...]).astype(BF16)

    def seg(j):
        sl = slice(j * CONV_DIM, (j + 1) * CONV_DIM)
        return _dot_nt(xn, w_ref[sl, :].astype(BF16)) + b_ref[:, sl]

    u = seg(1) * seg(2)
    st0 = st_ref[:, 0, :]
    st1 = st_ref[:, 1, :]
    conv = cw_ref[0:1, :] * st0 + cw_ref[1:2, :] * st1 + cw_ref[2:3, :] * u
    yconv_ref[...] = seg(0) * conv
    sconv_ref[:, 0, :] = st1
    sconv_ref[:, 1, :] = u
    q_ref[...] = seg(3)
    k_ref[...] = seg(4) * (DQK ** -0.5)
    v = seg(5)
    v_ref[...] = v
    so_ref[...] = jax.nn.sigmoid(seg(6))
    n_gate = wg_ref.shape[0]
    gate_ref[...] = _gate_transform(_dot_nt(xn, wg_ref[...]) + b_ref[:, MAIN_DIM:MAIN_DIM + n_gate])
    for h in range(HEADS):
        sl = slice(h * DV, (h + 1) * DV)
        vt_ref[sl, :] = v[:, sl].T
    gatet_ref[...] = _gate_transform_rows(_dot_nt(wg_ref[...], xn) + bg_ref[...])


def _s1_call(x, g, w, b, wg, bg, cw, st):
    n = x.shape[0]
    n_gate = wg.shape[0]
    full = lambda *shape: pl.BlockSpec(shape, lambda i: (0,) * len(shape))
    ins = [x, g, w, b, wg, bg, cw, st]
    return pl.pallas_call(
        _s1_kernel,
        grid=(1,),
        in_specs=[full(*a.shape) for a in ins],
        out_specs=[full(n, CONV_DIM), full(*st.shape), full(n, MLSTM_DIM), full(n, MLSTM_DIM),
                   full(n, MLSTM_DIM), full(n, MLSTM_DIM), full(n, n_gate), full(MLSTM_DIM, n),
                   full(n_gate, n)],
        out_shape=[jax.ShapeDtypeStruct((n, CONV_DIM), F32),
                   jax.ShapeDtypeStruct(st.shape, F32)]
        + [jax.ShapeDtypeStruct((n, MLSTM_DIM), F32)] * 4
        + [jax.ShapeDtypeStruct((n, n_gate), F32),
           jax.ShapeDtypeStruct((MLSTM_DIM, n), F32),
           jax.ShapeDtypeStruct((n_gate, n), F32)],
        compiler_params=_params(("arbitrary",)),
        name="s1_inproj_conv",
    )(*ins)


def _s3_kernel(cqt_ref, q_ref, k_ref, v_ref, so_ref, gate_ref, n_ref, m_ref, yconv_ref, x_ref,
               wout_ref, gmh_ref, gx_ref, wq_ref,
               hs_ref, qx_ref, nn_ref, mn_ref, y_s):
    n_rows = q_ref.shape[0]
    y_s[:, 0:CONV_DIM] = yconv_ref[...].astype(BF16)
    lane = lax.broadcasted_iota(jnp.int32, (n_rows, mn_ref.shape[1]), 1)
    m_out = jnp.zeros((n_rows, mn_ref.shape[1]), F32)
    for h in range(HEADS):
        sl = slice(h * DQK, (h + 1) * DQK)
        q = q_ref[:, sl]
        k = k_ref[:, sl]
        v = v_ref[:, sl]
        n_prev = n_ref[:, h, :]
        cq = cqt_ref[sl, :].T
        ig = gate_ref[:, h:h + 1]
        lf = gate_ref[:, HEADS + h:HEADS + h + 1]
        m_prev = m_ref[:, h:h + 1]
        inter = lf + m_prev
        m_row = jnp.maximum(inter, ig)
        wgt = jnp.sum(q * k, axis=1, keepdims=True) * jnp.exp(ig - m_row)
        g = jnp.exp(inter - m_row)
        num = g * cq + wgt * v
        den = g * jnp.sum(n_prev * q, axis=1, keepdims=True) + wgt
        hh = num / jnp.maximum(jnp.abs(den), jnp.exp(-m_row))
        hh = hh * lax.rsqrt(jnp.mean(hh * hh, axis=1, keepdims=True) + EPS) * gmh_ref[:, sl]
        y_s[:, CONV_DIM + h * DV:CONV_DIM + (h + 1) * DV] = (so_ref[:, sl] * hh).astype(BF16)
        nn_ref[:, h, :] = g * n_prev + jnp.exp(ig - m_row) * k
        m_out = jnp.where(lane == h, m_row, m_out)
    mn_ref[...] = m_out
    hs = x_ref[:, 0, :] + _dot(y_s[...], wout_ref[...].astype(BF16))
    hs_ref[...] = hs
    qx = _dot(_rmsnorm(hs, gx_ref[...]).astype(BF16), wq_ref[...].astype(BF16))
    for h in range(X_HEADS):
        for half in range(X_HEAD_DIM // 128):
            lo = h * X_HEAD_DIM + half * 128
            qx_ref[:, half * X_HEADS + h, :] = qx[:, lo:lo + 128]


def _s3_call(cqt, q, k, v, so, gates, nst, m, yconv, x, wout, gmh, gx, wq):
    n = q.shape[0]
    full = lambda *shape: pl.BlockSpec(shape, lambda i: (0,) * len(shape))
    ins = [cqt, q, k, v, so, gates, nst, m, yconv, x, wout, gmh, gx, wq]
    return pl.pallas_call(
        _s3_kernel,
        grid=(1,),
        in_specs=[full(*a.shape) for a in ins],
        out_specs=[full(n, D_MODEL), full(n, PACK_ROWS, 128), full(*nst.shape), full(n, 128)],
        out_shape=[jax.ShapeDtypeStruct((n, D_MODEL), F32), jax.ShapeDtypeStruct((n, PACK_ROWS, 128), F32),
                   jax.ShapeDtypeStruct(nst.shape, F32), jax.ShapeDtypeStruct((n, 128), F32)],
        scratch_shapes=[pltpu.VMEM((n, D_MODEL), BF16)],
        compiler_params=_params(("arbitrary",)),
        name="s3_mlstm_finish",
    )(*ins)


def _pack_heads(a):
    lead = a.shape[:-2]
    a = a.reshape(lead + (X_HEADS, X_HEAD_DIM // 128, 128))
    return jnp.swapaxes(a, -3, -2).reshape(lead + (PACK_ROWS, 128))


def kernel(x_prompt, x_sample, mem_prompt, state_conv, state_mlstm_C, state_mlstm_n, state_mlstm_m,
           cache_mem_k, cache_mem_v, g_mix, w_in, b_in, conv_w, g_mh, w_out, g_cross, g_mem,
           w_xq, w_xkv, w_xo, g_ffn, w_gu, w_down, g_final):
    n_batch, seq_len, _ = x_prompt.shape
    n_dec = x_sample.shape[0]
    depth = w_in.shape[0]
    assert depth == 1 and x_sample.shape[1] == 1
    assert all(seq_len % t == 0 for t in (ROW_TILE, P1_TILE, P3_TILE))
    assert P1_TILE % MLSTM_CHUNK == 0 and n_batch % P2_SEQS == 0

    n_gate = 2 * HEADS
    w_in_b = w_in[0].T
    b_in_r = b_in[0].reshape(1, MAIN_DIM + n_gate)
    w_gate_r = w_in_b[MAIN_DIM:].astype(BF16)
    b_gate_r = b_in[0, MAIN_DIM:].reshape(n_gate, 1)
    w_out_b = w_out[0]
    w_xq_b = w_xq[0]
    w_xkv_b = w_xkv[0]
    g_mix_r = g_mix[0].reshape(1, D_MODEL)
    g_cross_r = g_cross[0].reshape(1, D_MODEL)
    g_mem_r = g_mem[0].reshape(1, D_MODEL)
    g_ffn_r = g_ffn[0].reshape(1, D_MODEL)
    g_final_r = g_final.reshape(1, D_MODEL)
    g_mh_r = g_mh[0].reshape(1, MLSTM_DIM)
    cw = conv_w[0]

    m0 = state_mlstm_m[0]
    s_yconv, s_conv, sq, sk, sv, sso, sgates, svt, sgates_t = _s1_call(
        x_sample, g_mix_r, w_in_b, b_in_r, w_gate_r, b_gate_r, cw, state_conv[0])

    xp = x_prompt.reshape(n_batch * seq_len, D_MODEL)
    yconv, q, k, v, so, gates, p_conv, s_c, cqt = _p1_call(
        xp, g_mix_r, w_in_b, b_in_r, w_gate_r, b_gate_r, cw, seq_len,
        side=(state_mlstm_C[0], sq, sk, svt, sgates, sgates_t, m0, m0.T))
    hs1, qx, s_n, s_m = _s3_call(cqt, sq, sk, sv, sso, sgates,
                                 state_mlstm_n[0], m0, s_yconv, x_sample,
                                 w_out_b, g_mh_r, g_cross_r, w_xq_b)
    hp1, p_c, p_n, p_m, pk, pv, pkb, pvb = _p2_call(
        q, k, v, so, yconv, gates, xp, w_out_b, g_mh_r, n_batch, seq_len,
        mem_prompt.reshape(n_batch * N_MEM, D_MODEL), g_mem_r, w_xkv_b)
    hp1 = hp1.reshape(n_batch * seq_len, D_MODEL)
    o_p, w_xo_b, w_gu_b, w_down_b = _p3_call(
        hp1, g_cross_r, w_xq_b, pkb.reshape(n_batch, N_MEM, D_MODEL), pvb.reshape(n_batch, N_MEM, D_MODEL),
        seq_len, cast=(w_xo[0], w_gu[0], w_down[0]))
    y_p, o_s = _p4_call(hp1, o_p, w_xo_b, g_ffn_r, w_gu_b, w_down_b, g_final_r, ROW_TILE,
                        side=(qx, _pack_heads(cache_mem_k[0]), _pack_heads(cache_mem_v[0])))

    y_s, = _p4_call(hs1, o_s, w_xo_b, g_ffn_r, w_gu_b, w_down_b, g_final_r, n_dec, rows_3d=True)

    mem_shape = (1, n_batch, N_MEM, X_HEADS, X_HEAD_DIM)
    return (y_p.reshape(n_batch, seq_len, D_MODEL),
            y_s.reshape(n_dec, 1, D_MODEL),
            p_conv.reshape(1, n_batch, CONV_W - 1, CONV_DIM),
            p_c.reshape(1, n_batch, HEADS, DV, DQK),
            p_n.reshape(1, n_batch, HEADS, DQK),
            p_m[:, 0, :HEADS].reshape(1, n_batch, HEADS),
            pk.reshape(mem_shape),
            pv.reshape(mem_shape),
            s_conv.reshape(1, n_dec, CONV_W - 1, CONV_DIM),
            s_c.reshape(1, n_dec, HEADS, DV, DQK),
            s_n.reshape(1, n_dec, HEADS, DQK),
            s_m[:, :HEADS].reshape(1, n_dec, HEADS))
```

```python
import functools

import jax
import jax.numpy as jnp
from jax import lax
from jax.experimental import pallas as pl
from jax.experimental.pallas import tpu as pltpu

F32 = jnp.float32
BF16 = jnp.bfloat16

D_MODEL = 1024
CONV_DIM = 512
CONV_W = 3
MLSTM_DIM = 512
HEADS = 4
DQK = 128
DV = 128
N_MEM = 256
X_HEADS = 4
X_HEAD_DIM = 256
D_FF = 2816
MAIN_DIM = 3 * CONV_DIM + 4 * MLSTM_DIM
EPS = 1e-6

LANES = 128
SUBLANES = 8
BF16_ROWS = 16

MLSTM_CHUNK = 256
P2_SEQS = 4
ROW_TILE = 512
P1_TILE = 1024
P3_TILE = 1024
FF_CHUNK = 256
PACK_ROWS = X_HEADS * (X_HEAD_DIM // LANES)
MEM_CHUNK = 32
VMEM_LIMIT = 56 * 1024 * 1024


def _dot(a, b):
    return jnp.dot(a, b, preferred_element_type=F32)


def _dot_nt(a, b):
    return lax.dot_general(a, b, (((1,), (1,)), ((), ())), preferred_element_type=F32)


def _rmsnorm(x, g):
    return x * lax.rsqrt(jnp.mean(x * x, axis=-1, keepdims=True) + EPS) * g


def _const_spec(shape):
    zeros = (0,) * len(shape)
    return pl.BlockSpec(shape, lambda *_: zeros, pipeline_mode=pl.Buffered(1))


def _params(sem):
    return pltpu.CompilerParams(dimension_semantics=sem, vmem_limit_bytes=VMEM_LIMIT)


def _gate_transform(gt):
    lane = lax.broadcasted_iota(jnp.int32, gt.shape, 1)
    return jnp.where(lane < HEADS, gt, jax.nn.log_sigmoid(gt))


def _gate_transform_rows(gt):
    sub = lax.broadcasted_iota(jnp.int32, gt.shape, 0)
    return jnp.where(sub < HEADS, gt, jax.nn.log_sigmoid(gt))


def _memory_update_rows(i, c_ref, q_ref, k_ref, vt_ref, gate_ref, gatet_ref, m_ref, mt_ref, cn_ref, cqt_ref):
    n = vt_ref.shape[1]
    bb = c_ref.shape[0]

    @pl.when(i == 0)
    def _():
        cqt_ref[...] = jnp.zeros_like(cqt_ref)

    lane = lax.broadcasted_iota(jnp.int32, (DV, n), 1)
    for h in range(HEADS):
        sl = slice(h * DQK, (h + 1) * DQK)
        ig_c = gate_ref[:, h:h + 1]
        lf_c = gate_ref[:, HEADS + h:HEADS + h + 1]
        m_c = m_ref[:, h:h + 1]
        dec = jnp.broadcast_to(jnp.exp(lf_c + m_c - jnp.maximum(lf_c + m_c, ig_c)), (bb, DQK))
        ig_r = gatet_ref[h:h + 1, :]
        lf_r = gatet_ref[HEADS + h:HEADS + h + 1, :]
        m_r = mt_ref[h:h + 1, :]
        svt = vt_ref[sl, :] * jnp.exp(ig_r - jnp.maximum(lf_r + m_r, ig_r))
        q_t = q_ref[:, sl]
        k_t = k_ref[:, sl]
        cqt = cqt_ref[sl, :]
        for bl in range(bb):
            onehot = lane == i * bb + bl
            c = c_ref[bl, h]
            cq_col = jnp.sum(c * q_t[bl:bl + 1, :], axis=1, keepdims=True)
            sv_col = jnp.sum(jnp.where(onehot, svt, 0.0), axis=1, keepdims=True)
            cn_ref[bl, h] = dec[bl:bl + 1, :] * c + sv_col * k_t[bl:bl + 1, :]
            cqt = jnp.where(onehot, cq_col, cqt)
        cqt_ref[sl, :] = cqt


def _p1_kernel(tiles_per_batch, x_ref, g_ref, w_ref, b_ref, wg_ref, bg_ref, cw_ref,
               c_ref, sq_ref, sk_ref, svt_ref, sgate_ref, sgatet_ref, sm_ref, smt_ref,
               yconv_ref, q_ref, k_ref, v_ref, so_ref, gate_ref, pconv_ref, cn_ref, cqt_ref, ubuf):
    tm = x_ref.shape[0]
    i = pl.program_id(0)
    _memory_update_rows(i, c_ref, sq_ref, sk_ref, svt_ref, sgate_ref, sgatet_ref, sm_ref, smt_ref,
                        cn_ref, cqt_ref)
    xn = _rmsnorm(x_ref[...], g_ref[...]).astype(BF16)

    def seg(j):
        sl = slice(j * CONV_DIM, (j + 1) * CONV_DIM)
        return _dot_nt(xn, w_ref[sl, :].astype(BF16)) + b_ref[:, sl]

    P = SUBLANES
    prev = ubuf[tm:tm + P, :]
    ubuf[0:P, :] = jnp.where(i % tiles_per_batch == 0, jnp.zeros_like(prev), prev)
    ubuf[P:P + tm, :] = seg(1) * seg(2)
    conv = sum(cw_ref[j:j + 1, :] * ubuf[P - (CONV_W - 1) + j:P - (CONV_W - 1) + j + tm, :] for j in range(CONV_W))
    yconv_ref[...] = (seg(0) * conv).astype(BF16)
    pconv_ref[0] = ubuf[tm + P - (CONV_W - 1):tm + P, :]

    q_ref[...] = seg(3).astype(BF16)
    k_ref[...] = (seg(4) * (DQK ** -0.5)).astype(BF16)
    v_ref[...] = seg(5).astype(BF16)
    so_ref[...] = jax.nn.sigmoid(seg(6)).astype(BF16)
    gt = _gate_transform_rows(_dot_nt(wg_ref[...], xn) + bg_ref[...])
    n_gate = gt.shape[0]
    gate_ref[0, 0:n_gate, :] = gt
    L = MLSTM_CHUNK
    n_blk = tm // L
    hi = gt.astype(BF16).astype(F32)
    r1 = gt - hi
    mid = r1.astype(BF16).astype(F32)
    lo = r1 - mid
    terms = jnp.concatenate([t[:, j * L:(j + 1) * L] for t in (hi, mid, lo) for j in range(n_blk)], axis=0)
    tri = (lax.broadcasted_iota(jnp.int32, (L, L), 0) <= lax.broadcasted_iota(jnp.int32, (L, L), 1)).astype(BF16)
    parts = _dot(terms.astype(BF16), tri)
    for j in range(n_blk):
        rows = [parts[(t * n_blk + j) * n_gate:(t * n_blk + j + 1) * n_gate, :] for t in range(3)]
        gate_ref[0, n_gate:2 * n_gate, j * L:(j + 1) * L] = (rows[0] + rows[1]) + rows[2]


def _p1_call(x, g, w, b, wg, bg, cw, seq_len, side):
    rows = x.shape[0]
    tm = P1_TILE
    steps = rows // tm
    tiles_per_batch = seq_len // tm
    n_batch = rows // seq_len
    row = lambda width: pl.BlockSpec((tm, width), lambda i: (i, 0))
    c, sq, sk, svt, sgate, sgatet, sm, smt = side
    n = sq.shape[0]
    sr = n // steps
    assert sr * steps == n and sr % SUBLANES == 0
    full = lambda a: pl.BlockSpec(a.shape, lambda i: (0,) * a.ndim)
    srow = lambda a: pl.BlockSpec((sr,) + a.shape[1:], lambda i: (i,) + (0,) * (a.ndim - 1))
    return pl.pallas_call(
        functools.partial(_p1_kernel, tiles_per_batch),
        grid=(steps,),
        in_specs=[row(D_MODEL), _const_spec((1, D_MODEL)), _const_spec(w.shape),
                  _const_spec(b.shape), _const_spec((2 * HEADS, D_MODEL)),
                  _const_spec((2 * HEADS, 1)), _const_spec((CONV_W, CONV_DIM)),
                  srow(c), srow(sq), srow(sk), full(svt), srow(sgate), full(sgatet), srow(sm), full(smt)],
        out_specs=[row(CONV_DIM), row(MLSTM_DIM), row(MLSTM_DIM), row(MLSTM_DIM), row(MLSTM_DIM),
                   pl.BlockSpec((1, 4 * HEADS, tm), lambda i: (i // tiles_per_batch, 0, i % tiles_per_batch)),
                   pl.BlockSpec((1, CONV_W - 1, CONV_DIM), lambda i: (i // tiles_per_batch, 0, 0)),
                   srow(c), full(svt)],
        out_shape=[jax.ShapeDtypeStruct((rows, CONV_DIM), BF16)]
        + [jax.ShapeDtypeStruct((rows, MLSTM_DIM), BF16)] * 4
        + [jax.ShapeDtypeStruct((n_batch, 4 * HEADS, seq_len), F32),
           jax.ShapeDtypeStruct((n_batch, CONV_W - 1, CONV_DIM), F32),
           jax.ShapeDtypeStruct(c.shape, F32), jax.ShapeDtypeStruct(svt.shape, F32)],
        scratch_shapes=[pltpu.VMEM((tm + SUBLANES, CONV_DIM), F32)],
        compiler_params=_params(("arbitrary",)),
        name="p1_inproj_conv",
    )(x, g, w, b, wg, bg, cw, c, sq, sk, svt, sgate, sgatet, sm, smt)


def _p2_kernel(q_ref, k_ref, v_ref, so_ref, yconv_ref, gate_ref, x_ref, wout_ref, gmh_ref,
               hp_ref, pc_ref, pn_ref, pm_ref, c_s, m_s, y_s):
    nb, L = q_ref.shape[0], q_ref.shape[1]
    c = pl.program_id(1)

    @pl.when(c == 0)
    def _():
        c_s[...] = jnp.zeros_like(c_s)
        m_s[...] = jnp.zeros_like(m_s)

    row = lax.broadcasted_iota(jnp.int32, (L, L), 0)
    col = lax.broadcasted_iota(jnp.int32, (L, L), 1)
    causal = row >= col

    for bi in range(nb):
        gt = gate_ref[bi]
        for h in range(HEADS):
            sl = slice(h * DQK, (h + 1) * DQK)
            q = q_ref[bi, :, sl]
            k = k_ref[bi, :, sl]
            v = v_ref[bi, :, sl]
            lf_r = gt[HEADS + h:HEADS + h + 1, :]
            a_r = gt[h:h + 1, :] - gt[3 * HEADS + h:3 * HEADS + h + 1, :]
            m_prev = jnp.max(m_s[bi, h:h + 1, :], axis=1, keepdims=True)
            c_prev = c_s[bi, h]

            m_c = jnp.maximum(m_prev, jnp.max(jnp.where(causal, a_r, -jnp.inf), axis=1, keepdims=True))
            b_c = jnp.sum(jnp.where(causal, lf_r, 0.0), axis=1, keepdims=True)
            w = _dot_nt(q, k) * jnp.exp(jnp.where(causal, a_r - m_c, -jnp.inf))
            g = jnp.exp(m_prev - m_c)
            qc = _dot_nt(q, c_prev.astype(BF16))
            num = g * qc[:, 0:DV] + _dot(w.astype(BF16), v)
            den = g * qc[:, DV:2 * DV] + jnp.sum(w, axis=1, keepdims=True)
            hh = num / jnp.maximum(jnp.abs(den), jnp.exp(-(b_c + m_c)))
            hh = hh * lax.rsqrt(jnp.mean(hh * hh, axis=1, keepdims=True) + EPS) * gmh_ref[:, sl]
            y_s[bi * L:(bi + 1) * L, h * DV:(h + 1) * DV] = (so_ref[bi, :, sl].astype(F32) * hh).astype(BF16)

            m_last = jnp.maximum(m_prev, jnp.max(a_r, axis=1, keepdims=True))
            b_last = jnp.sum(lf_r, axis=1, keepdims=True)
            s_r = jnp.exp(a_r - m_last)
            sv_t = jnp.concatenate([v.T.astype(F32) * s_r, jnp.broadcast_to(s_r, (DV, L))], axis=0)
            c_s[bi, h] = jnp.exp(m_prev - m_last) * c_prev + _dot(sv_t.astype(BF16), k)
            m_s[bi, h:h + 1, :] = jnp.broadcast_to(b_last + m_last, (1, m_s.shape[2]))

    w_conv = wout_ref[0:CONV_DIM, :].astype(BF16)
    w_ml = wout_ref[CONV_DIM:CONV_DIM + MLSTM_DIM, :].astype(BF16)
    for bi in range(nb):
        out = _dot(yconv_ref[bi], w_conv) + _dot(y_s[bi * L:(bi + 1) * L, :], w_ml)
        hp_ref[bi] = x_ref[bi] + out

    @pl.when(c == pl.num_programs(1) - 1)
    def _():
        lane = lax.broadcasted_iota(jnp.int32, (1, m_s.shape[2]), 1)
        for bi in range(nb):
            acc = jnp.zeros((1, m_s.shape[2]), F32)
            for h in range(HEADS):
                pc_ref[bi, h] = c_s[bi, h, 0:DV, :]
                pn_ref[bi, h:h + 1, :] = c_s[bi, h, DV:DV + 1, :]
                acc = jnp.where(lane == h, m_s[bi, h:h + 1, :], acc)
            pm_ref[bi] = acc


def _p2_call(q, k, v, so, yconv, gates, x, wout, gmh, n_batch, seq_len):
    L = MLSTM_CHUNK
    nb = P2_SEQS
    nc = seq_len // L
    seq = lambda width: pl.BlockSpec((nb, L, width), lambda b, c: (b, c, 0))
    as_seq = lambda a: a.reshape(n_batch, seq_len, a.shape[-1])
    return pl.pallas_call(
        _p2_kernel,
        grid=(n_batch // nb, nc),
        in_specs=[seq(MLSTM_DIM), seq(MLSTM_DIM), seq(MLSTM_DIM), seq(MLSTM_DIM), seq(CONV_DIM),
                  pl.BlockSpec((nb, 4 * HEADS, L), lambda b, c: (b, 0, c)), seq(D_MODEL),
                  _const_spec((D_MODEL, D_MODEL)), _const_spec((1, MLSTM_DIM))],
        out_specs=[seq(D_MODEL),
                   pl.BlockSpec((nb, HEADS, DV, DQK), lambda b, c: (b, 0, 0, 0)),
                   pl.BlockSpec((nb, HEADS, DQK), lambda b, c: (b, 0, 0)),
                   pl.BlockSpec((nb, 1, LANES), lambda b, c: (b, 0, 0))],
        out_shape=[jax.ShapeDtypeStruct((n_batch, seq_len, D_MODEL), F32),
                   jax.ShapeDtypeStruct((n_batch, HEADS, DV, DQK), F32),
                   jax.ShapeDtypeStruct((n_batch, HEADS, DQK), F32),
                   jax.ShapeDtypeStruct((n_batch, 1, LANES), F32)],
        scratch_shapes=[pltpu.VMEM((nb, HEADS, 2 * DV, DQK), F32), pltpu.VMEM((nb, SUBLANES, LANES), F32),
                        pltpu.VMEM((nb * L, MLSTM_DIM), BF16)],
        compiler_params=_params(("arbitrary", "arbitrary")),
        name="p2_mlstm_outproj",
    )(as_seq(q), as_seq(k), as_seq(v), as_seq(so), as_seq(yconv), gates, as_seq(x), wout, gmh)


def _pm_kernel(mem_ref, g_ref, w_ref, k_ref, v_ref, kb_ref, vb_ref):
    xn = _rmsnorm(mem_ref[...], g_ref[...]).astype(BF16)
    kk = _dot(xn, w_ref[:, 0:D_MODEL].astype(BF16))
    vv = _dot(xn, w_ref[:, D_MODEL:2 * D_MODEL].astype(BF16))
    for h in range(X_HEADS):
        sl = slice(h * X_HEAD_DIM, (h + 1) * X_HEAD_DIM)
        k_ref[:, h, :] = kk[:, sl]
        v_ref[:, h, :] = vv[:, sl]
    kb_ref[...] = kk.astype(BF16)
    vb_ref[...] = vv.astype(BF16)


def _pm_call(mem, g, w):
    rows = mem.shape[0]
    tm = ROW_TILE
    row = pl.BlockSpec((tm, D_MODEL), lambda i: (i, 0))
    row4 = pl.BlockSpec((tm, X_HEADS, X_HEAD_DIM), lambda i: (i, 0, 0))
    return pl.pallas_call(
        _pm_kernel,
        grid=(rows // tm,),
        in_specs=[row, _const_spec((1, D_MODEL)), _const_spec((D_MODEL, 2 * D_MODEL))],
        out_specs=[row4, row4, row, row],
        out_shape=[jax.ShapeDtypeStruct((rows, X_HEADS, X_HEAD_DIM), F32)] * 2
        + [jax.ShapeDtypeStruct((rows, D_MODEL), BF16)] * 2,
        compiler_params=_params(("arbitrary",)),
        name="pm_mem_kv",
    )(mem, g, w)


def _p3_kernel(n_cast, hp_ref, g_ref, wq_ref, k_ref, v_ref, *rest):
    for src, dst in zip(rest[:n_cast], rest[n_cast + 1:]):
        dst[...] = src[...].astype(BF16)
    o_ref = rest[n_cast]
    xn = _rmsnorm(hp_ref[...], g_ref[...]).astype(BF16)
    q = _dot(xn, wq_ref[...].astype(BF16))
    for h in range(X_HEADS):
        sl = slice(h * X_HEAD_DIM, (h + 1) * X_HEAD_DIM)
        s = _dot_nt(q[:, sl].astype(BF16), k_ref[0, :, sl]) * (X_HEAD_DIM ** -0.5)
        e = jnp.exp(s - jnp.max(s, axis=1, keepdims=True))
        p = e * (1.0 / jnp.sum(e, axis=1, keepdims=True))
        o_ref[:, sl] = _dot(p.astype(BF16), v_ref[0, :, sl]).astype(BF16)


def _p3_call(hp, g, wq, kb, vb, seq_len, cast=()):
    rows = hp.shape[0]
    tm = P3_TILE
    steps = rows // tm
    tiles_per_batch = seq_len // tm
    row = pl.BlockSpec((tm, D_MODEL), lambda i: (i, 0))
    mem = pl.BlockSpec((1, N_MEM, D_MODEL), lambda i: (i // tiles_per_batch, 0, 0))
    slabs = []
    for wgt in cast:
        slab = wgt.shape[0] // steps
        assert slab * steps == wgt.shape[0] and slab % BF16_ROWS == 0
        slabs.append(pl.BlockSpec((slab, wgt.shape[1]), lambda i: (i, 0)))
    return pl.pallas_call(
        functools.partial(_p3_kernel, len(cast)),
        grid=(steps,),
        in_specs=[row, _const_spec((1, D_MODEL)), _const_spec((D_MODEL, D_MODEL)), mem, mem] + slabs,
        out_specs=[row] + slabs,
        out_shape=[jax.ShapeDtypeStruct((rows, D_MODEL), BF16)]
        + [jax.ShapeDtypeStruct(wgt.shape, BF16) for wgt in cast],
        compiler_params=_params(("arbitrary",)),
        name="p3_cross_attn",
    )(hp, g, wq, kb, vb, *cast)


def _cache_attention_row(q8, kc_ref, vc_ref, bl):
    qs = q8 * (X_HEAD_DIM ** -0.5)
    m_run = jnp.full((1, PACK_ROWS, 1), -jnp.inf, F32)
    l_run = jnp.zeros((1, PACK_ROWS, 1), F32)
    acc = jnp.zeros((PACK_ROWS, LANES), F32)
    for c in range(N_MEM // MEM_CHUNK):
        blk = slice(c * MEM_CHUNK, (c + 1) * MEM_CHUNK)
        prod = kc_ref[bl, blk] * qs
        s = jnp.sum(prod + pltpu.roll(prod, X_HEADS, 1), axis=-1, keepdims=True)
        m_new = jnp.maximum(m_run, jnp.max(s, axis=0, keepdims=True))
        alpha = jnp.exp(m_run - m_new)
        e = jnp.exp(s - m_new)
        l_run = alpha * l_run + jnp.sum(e, axis=0, keepdims=True)
        acc = alpha[0] * acc + jnp.sum(e * vc_ref[bl, blk], axis=0)
        m_run = m_new
    return acc * (1.0 / l_run[0])


def _p4_kernel(hp_ref, o_ref, wxo_ref, gf_ref, wgu_ref, wd_ref, gfin_ref, q4_ref, kc_ref, vc_ref,
               y_ref, os_ref, act_s):
    for bl in range(q4_ref.shape[0]):
        os_ref[bl] = _cache_attention_row(q4_ref[bl], kc_ref, vc_ref, bl)
    hp = hp_ref[...] + _dot(o_ref[...], wxo_ref[...])
    xn = _rmsnorm(hp, gf_ref[...]).astype(BF16)
    for j in range(D_FF // FF_CHUNK):
        g = _dot(xn, wgu_ref[:, FF_CHUNK * j:FF_CHUNK * (j + 1)])
        u = _dot(xn, wgu_ref[:, D_FF + FF_CHUNK * j:D_FF + FF_CHUNK * (j + 1)])
        act_s[:, FF_CHUNK * j:FF_CHUNK * (j + 1)] = (g * jax.nn.sigmoid(g) * u).astype(BF16)
    hp = hp + _dot(act_s[...], wd_ref[...])
    y_ref[...] = _rmsnorm(hp, gfin_ref[...])


def _p4_call(hp, o, wxo, gf, wgu, wd, gfin, side):
    rows = hp.shape[0]
    tm = ROW_TILE
    steps = rows // tm
    q4, kc, vc = side
    side_rows = q4.shape[0] // steps
    assert side_rows * steps == q4.shape[0]
    row = pl.BlockSpec((tm, D_MODEL), lambda i: (i, 0))
    srow = pl.BlockSpec((side_rows, PACK_ROWS, LANES), lambda i: (i, 0, 0))
    cache = pl.BlockSpec((side_rows, N_MEM, PACK_ROWS, LANES), lambda i: (i, 0, 0, 0))
    return pl.pallas_call(
        _p4_kernel,
        grid=(steps,),
        in_specs=[row, row, _const_spec((D_MODEL, D_MODEL)), _const_spec((1, D_MODEL)),
                  _const_spec((D_MODEL, 2 * D_FF)), _const_spec((D_FF, D_MODEL)),
                  _const_spec((1, D_MODEL)), srow, cache, cache],
        out_specs=[row, srow],
        out_shape=[jax.ShapeDtypeStruct((rows, D_MODEL), F32), jax.ShapeDtypeStruct(q4.shape, F32)],
        scratch_shapes=[pltpu.VMEM((tm, D_FF), BF16)],
        compiler_params=_params(("arbitrary",)),
        name="p4_ffn_final",
    )(hp, o, wxo, gf, wgu, wd, gfin, q4, kc, vc)


def _s1_kernel(x_ref, g_ref, w_ref, b_ref, wg_ref, bg_ref, cw_ref, st_ref,
               yconv_ref, sconv_ref, q_ref, k_ref, v_ref, so_ref, gate_ref, vt_ref, gatet_ref):
    xn = _rmsnorm(x_ref[:, 0, :], g_ref[...]).astype(BF16)

    def seg(j):
        sl = slice(j * CONV_DIM, (j + 1) * CONV_DIM)
        return _dot_nt(xn, w_ref[sl, :].astype(BF16)) + b_ref[:, sl]

    u = seg(1) * seg(2)
    st0 = st_ref[:, 0, :]
    st1 = st_ref[:, 1, :]
    conv = cw_ref[0:1, :] * st0 + cw_ref[1:2, :] * st1 + cw_ref[2:3, :] * u
    yconv_ref[...] = seg(0) * conv
    sconv_ref[:, 0, :] = st1
    sconv_ref[:, 1, :] = u
    q_ref[...] = seg(3)
    k_ref[...] = seg(4) * (DQK ** -0.5)
    v = seg(5)
    v_ref[...] = v
    so_ref[...] = jax.nn.sigmoid(seg(6))
    n_gate = wg_ref.shape[0]
    gate_ref[...] = _gate_transform(_dot_nt(xn, wg_ref[...]) + b_ref[:, MAIN_DIM:MAIN_DIM + n_gate])
    for h in range(HEADS):
        sl = slice(h * DV, (h + 1) * DV)
        vt_ref[sl, :] = v[:, sl].T
    gatet_ref[...] = _gate_transform_rows(_dot_nt(wg_ref[...], xn) + bg_ref[...])


def _s1_call(x, g, w, b, wg, bg, cw, st):
    n = x.shape[0]
    n_gate = wg.shape[0]
    full = lambda *shape: pl.BlockSpec(shape, lambda i: (0,) * len(shape))
    ins = [x, g, w, b, wg, bg, cw, st]
    return pl.pallas_call(
        _s1_kernel,
        grid=(1,),
        in_specs=[full(*a.shape) for a in ins],
        out_specs=[full(n, CONV_DIM), full(*st.shape), full(n, MLSTM_DIM), full(n, MLSTM_DIM),
                   full(n, MLSTM_DIM), full(n, MLSTM_DIM), full(n, n_gate), full(MLSTM_DIM, n),
                   full(n_gate, n)],
        out_shape=[jax.ShapeDtypeStruct((n, CONV_DIM), F32),
                   jax.ShapeDtypeStruct(st.shape, F32)]
        + [jax.ShapeDtypeStruct((n, MLSTM_DIM), F32)] * 4
        + [jax.ShapeDtypeStruct((n, n_gate), F32),
           jax.ShapeDtypeStruct((MLSTM_DIM, n), F32),
           jax.ShapeDtypeStruct((n_gate, n), F32)],
        compiler_params=_params(("arbitrary",)),
        name="s1_inproj_conv",
    )(*ins)


def _s3_kernel(cqt_ref, q_ref, k_ref, v_ref, so_ref, gate_ref, n_ref, m_ref, yconv_ref, x_ref,
               wout_ref, gmh_ref, gx_ref, wq_ref,
               hs_ref, qx_ref, nn_ref, mn_ref, y_s):
    n_rows = q_ref.shape[0]
    y_s[:, 0:CONV_DIM] = yconv_ref[...].astype(BF16)
    lane = lax.broadcasted_iota(jnp.int32, (n_rows, mn_ref.shape[1]), 1)
    m_out = jnp.zeros((n_rows, mn_ref.shape[1]), F32)
    for h in range(HEADS):
        sl = slice(h * DQK, (h + 1) * DQK)
        q = q_ref[:, sl]
        k = k_ref[:, sl]
        v = v_ref[:, sl]
        n_prev = n_ref[:, h, :]
        cq = cqt_ref[sl, :].T
        ig = gate_ref[:, h:h + 1]
        lf = gate_ref[:, HEADS + h:HEADS + h + 1]
        m_prev = m_ref[:, h:h + 1]
        inter = lf + m_prev
        m_row = jnp.maximum(inter, ig)
        wgt = jnp.sum(q * k, axis=1, keepdims=True) * jnp.exp(ig - m_row)
        g = jnp.exp(inter - m_row)
        num = g * cq + wgt * v
        den = g * jnp.sum(n_prev * q, axis=1, keepdims=True) + wgt
        hh = num / jnp.maximum(jnp.abs(den), jnp.exp(-m_row))
        hh = hh * lax.rsqrt(jnp.mean(hh * hh, axis=1, keepdims=True) + EPS) * gmh_ref[:, sl]
        y_s[:, CONV_DIM + h * DV:CONV_DIM + (h + 1) * DV] = (so_ref[:, sl] * hh).astype(BF16)
        nn_ref[:, h, :] = g * n_prev + jnp.exp(ig - m_row) * k
        m_out = jnp.where(lane == h, m_row, m_out)
    mn_ref[...] = m_out
    hs = x_ref[:, 0, :] + _dot(y_s[...], wout_ref[...].astype(BF16))
    hs_ref[...] = hs
    qx = _dot(_rmsnorm(hs, gx_ref[...]).astype(BF16), wq_ref[...].astype(BF16))
    for h in range(X_HEADS):
        for half in range(X_HEAD_DIM // LANES):
            lo = h * X_HEAD_DIM + half * LANES
            qx_ref[:, half * X_HEADS + h, :] = qx[:, lo:lo + LANES]


def _s3_call(cqt, q, k, v, so, gates, nst, m, yconv, x, wout, gmh, gx, wq):
    n = q.shape[0]
    full = lambda *shape: pl.BlockSpec(shape, lambda i: (0,) * len(shape))
    ins = [cqt, q, k, v, so, gates, nst, m, yconv, x, wout, gmh, gx, wq]
    return pl.pallas_call(
        _s3_kernel,
        grid=(1,),
        in_specs=[full(*a.shape) for a in ins],
        out_specs=[full(n, D_MODEL), full(n, PACK_ROWS, LANES), full(*nst.shape), full(n, LANES)],
        out_shape=[jax.ShapeDtypeStruct((n, D_MODEL), F32), jax.ShapeDtypeStruct((n, PACK_ROWS, LANES), F32),
                   jax.ShapeDtypeStruct(nst.shape, F32), jax.ShapeDtypeStruct((n, LANES), F32)],
        scratch_shapes=[pltpu.VMEM((n, D_MODEL), BF16)],
        compiler_params=_params(("arbitrary",)),
        name="s3_mlstm_finish",
    )(*ins)


def _s5_kernel(hs_ref, o_ref, wxo_ref, gf_ref, wg_ref, wu_ref, wd_ref, gfin_ref, y_ref, xn_s, acc_s):
    j = pl.program_id(0)

    @pl.when(j == 0)
    def _():
        o = jnp.concatenate([o_ref[:, half * X_HEADS + h, :] for h in range(X_HEADS)
                             for half in range(X_HEAD_DIM // LANES)], axis=1)
        hp = hs_ref[...] + _dot(o.astype(BF16), wxo_ref[...])
        acc_s[...] = hp
        xn_s[...] = _rmsnorm(hp, gf_ref[...]).astype(BF16)

    xn = xn_s[...]
    g = _dot(xn, wg_ref[...])
    u = _dot(xn, wu_ref[...])
    acc_s[...] += _dot((g * jax.nn.sigmoid(g) * u).astype(BF16), wd_ref[...])

    @pl.when(j == pl.num_programs(0) - 1)
    def _():
        y_ref[:, 0, :] = _rmsnorm(acc_s[...], gfin_ref[...])


def _s5_call(hs, o, wxo, gf, wgu, wd, gfin):
    n = hs.shape[0]
    n_ff = D_FF // FF_CHUNK
    full = lambda *shape: pl.BlockSpec(shape, lambda j: (0,) * len(shape))
    return pl.pallas_call(
        _s5_kernel,
        grid=(n_ff,),
        in_specs=[full(n, D_MODEL), full(*o.shape), full(D_MODEL, D_MODEL), full(1, D_MODEL),
                  pl.BlockSpec((D_MODEL, FF_CHUNK), lambda j: (0, j)),
                  pl.BlockSpec((D_MODEL, FF_CHUNK), lambda j: (0, n_ff + j)),
                  pl.BlockSpec((FF_CHUNK, D_MODEL), lambda j: (j, 0)), full(1, D_MODEL)],
        out_specs=full(n, 1, D_MODEL),
        out_shape=jax.ShapeDtypeStruct((n, 1, D_MODEL), F32),
        scratch_shapes=[pltpu.VMEM((n, D_MODEL), BF16), pltpu.VMEM((n, D_MODEL), F32)],
        compiler_params=_params(("arbitrary",)),
        name="s5_ffn_final",
    )(hs, o, wxo, gf, wgu, wgu, wd, gfin)


def _pack_heads(a):
    lead = a.shape[:-2]
    a = a.reshape(lead + (X_HEADS, X_HEAD_DIM // LANES, LANES))
    return jnp.swapaxes(a, -3, -2).reshape(lead + (PACK_ROWS, LANES))


def kernel(x_prompt, x_sample, mem_prompt, state_conv, state_mlstm_C, state_mlstm_n, state_mlstm_m,
           cache_mem_k, cache_mem_v, g_mix, w_in, b_in, conv_w, g_mh, w_out, g_cross, g_mem,
           w_xq, w_xkv, w_xo, g_ffn, w_gu, w_down, g_final):
    n_batch, seq_len, _ = x_prompt.shape
    n_dec = x_sample.shape[0]
    depth = w_in.shape[0]
    assert depth == 1 and x_sample.shape[1] == 1
    assert all(seq_len % t == 0 for t in (ROW_TILE, P1_TILE, P3_TILE))
    assert P1_TILE % MLSTM_CHUNK == 0 and n_batch % P2_SEQS == 0

    n_gate = 2 * HEADS
    w_in_b = w_in[0].T
    b_in_r = b_in[0].reshape(1, MAIN_DIM + n_gate)
    w_gate_r = w_in_b[MAIN_DIM:].astype(BF16)
    b_gate_r = b_in[0, MAIN_DIM:].reshape(n_gate, 1)
    w_out_b = w_out[0]
    w_xq_b = w_xq[0]
    w_xkv_b = w_xkv[0]
    g_mix_r = g_mix[0].reshape(1, D_MODEL)
    g_cross_r = g_cross[0].reshape(1, D_MODEL)
    g_mem_r = g_mem[0].reshape(1, D_MODEL)
    g_ffn_r = g_ffn[0].reshape(1, D_MODEL)
    g_final_r = g_final.reshape(1, D_MODEL)
    g_mh_r = g_mh[0].reshape(1, MLSTM_DIM)
    cw = conv_w[0]

    m0 = state_mlstm_m[0]
    s_yconv, s_conv, sq, sk, sv, sso, sgates, svt, sgates_t = _s1_call(
        x_sample, g_mix_r, w_in_b, b_in_r, w_gate_r, b_gate_r, cw, state_conv[0])

    xp = x_prompt.reshape(n_batch * seq_len, D_MODEL)
    yconv, q, k, v, so, gates, p_conv, s_c, cqt = _p1_call(
        xp, g_mix_r, w_in_b, b_in_r, w_gate_r, b_gate_r, cw, seq_len,
        side=(state_mlstm_C[0], sq, sk, svt, sgates, sgates_t, m0, m0.T))
    hs1, qx, s_n, s_m = _s3_call(cqt, sq, sk, sv, sso, sgates,
                                 state_mlstm_n[0], m0, s_yconv, x_sample,
                                 w_out_b, g_mh_r, g_cross_r, w_xq_b)
    hp1, p_c, p_n, p_m = _p2_call(q, k, v, so, yconv, gates, xp, w_out_b, g_mh_r, n_batch, seq_len)
    hp1 = hp1.reshape(n_batch * seq_len, D_MODEL)
    pk, pv, pkb, pvb = _pm_call(mem_prompt.reshape(n_batch * N_MEM, D_MODEL), g_mem_r, w_xkv_b)
    o_p, w_xo_b, w_gu_b, w_down_b = _p3_call(
        hp1, g_cross_r, w_xq_b, pkb.reshape(n_batch, N_MEM, D_MODEL), pvb.reshape(n_batch, N_MEM, D_MODEL),
        seq_len, cast=(w_xo[0], w_gu[0], w_down[0]))
    y_p, o_s = _p4_call(hp1, o_p, w_xo_b, g_ffn_r, w_gu_b, w_down_b, g_final_r,
                        side=(qx, _pack_heads(cache_mem_k[0]), _pack_heads(cache_mem_v[0])))

    y_s = _s5_call(hs1, o_s, w_xo_b, g_ffn_r, w_gu_b, w_down_b, g_final_r)

    mem_shape = (1, n_batch, N_MEM, X_HEADS, X_HEAD_DIM)
    return (y_p.reshape(n_batch, seq_len, D_MODEL),
            y_s.reshape(n_dec, 1, D_MODEL),
            p_conv.reshape(1, n_batch, CONV_W - 1, CONV_DIM),
            p_c.reshape(1, n_batch, HEADS, DV, DQK),
            p_n.reshape(1, n_batch, HEADS, DQK),
            p_m[:, 0, :HEADS].reshape(1, n_batch, HEADS),
            pk.reshape(mem_shape),
            pv.reshape(mem_shape),
            s_conv.reshape(1, n_dec, CONV_W - 1, CONV_DIM),
            s_c.reshape(1, n_dec, HEADS, DV, DQK),
            s_n.reshape(1, n_dec, HEADS, DQK),
            s_m[:, :HEADS].reshape(1, n_dec, HEADS))
```

```python
import functools

import jax
import jax.numpy as jnp
from jax import lax
from jax.experimental import pallas as pl
from jax.experimental.pallas import tpu as pltpu

F32 = jnp.float32
BF16 = jnp.bfloat16

D_MODEL = 1024
CONV_DIM = 512
CONV_W = 3
MLSTM_DIM = 512
HEADS = 4
DQK = 128
DV = 128
N_MEM = 256
X_HEADS = 4
X_HEAD_DIM = 256
D_FF = 2816
MAIN_DIM = 3 * CONV_DIM + 4 * MLSTM_DIM
EPS = 1e-6

LANES = 128
SUBLANES = 8
BF16_ROWS = 16

MLSTM_CHUNK = 256
P2_SEQS = 4
ROW_TILE = 512
P1_TILE = 1024
P3_TILE = 1024
FF_CHUNK = 256
PACK_ROWS = X_HEADS * (X_HEAD_DIM // LANES)
MEM_CHUNK = 32
VMEM_LIMIT = 56 * 1024 * 1024


def _dot(a, b):
    return jnp.dot(a, b, preferred_element_type=F32)


def _dot_nt(a, b):
    return lax.dot_general(a, b, (((1,), (1,)), ((), ())), preferred_element_type=F32)


def _rmsnorm(x, g):
    return x * lax.rsqrt(jnp.mean(x * x, axis=-1, keepdims=True) + EPS) * g


def _const_spec(shape):
    zeros = (0,) * len(shape)
    return pl.BlockSpec(shape, lambda *_: zeros, pipeline_mode=pl.Buffered(1))


def _params(sem):
    return pltpu.CompilerParams(dimension_semantics=sem, vmem_limit_bytes=VMEM_LIMIT)


def _gate_transform(gt):
    lane = lax.broadcasted_iota(jnp.int32, gt.shape, 1)
    return jnp.where(lane < HEADS, gt, jax.nn.log_sigmoid(gt))


def _gate_transform_rows(gt):
    sub = lax.broadcasted_iota(jnp.int32, gt.shape, 0)
    return jnp.where(sub < HEADS, gt, jax.nn.log_sigmoid(gt))


def _memory_update_rows(i, c_ref, q_ref, k_ref, vt_ref, gate_ref, gatet_ref, m_ref, mt_ref, cn_ref, cqt_ref):
    n = vt_ref.shape[1]
    bb = c_ref.shape[0]

    @pl.when(i == 0)
    def _():
        cqt_ref[...] = jnp.zeros_like(cqt_ref)

    lane = lax.broadcasted_iota(jnp.int32, (DV, n), 1)
    for h in range(HEADS):
        sl = slice(h * DQK, (h + 1) * DQK)
        ig_c = gate_ref[:, h:h + 1]
        lf_c = gate_ref[:, HEADS + h:HEADS + h + 1]
        m_c = m_ref[:, h:h + 1]
        dec = jnp.broadcast_to(jnp.exp(lf_c + m_c - jnp.maximum(lf_c + m_c, ig_c)), (bb, DQK))
        ig_r = gatet_ref[h:h + 1, :]
        lf_r = gatet_ref[HEADS + h:HEADS + h + 1, :]
        m_r = mt_ref[h:h + 1, :]
        svt = vt_ref[sl, :] * jnp.exp(ig_r - jnp.maximum(lf_r + m_r, ig_r))
        q_t = q_ref[:, sl]
        k_t = k_ref[:, sl]
        cqt = cqt_ref[sl, :]
        for bl in range(bb):
            onehot = lane == i * bb + bl
            c = c_ref[bl, h]
            cq_col = jnp.sum(c * q_t[bl:bl + 1, :], axis=1, keepdims=True)
            sv_col = jnp.sum(jnp.where(onehot, svt, 0.0), axis=1, keepdims=True)
            cn_ref[bl, h] = dec[bl:bl + 1, :] * c + sv_col * k_t[bl:bl + 1, :]
            cqt = jnp.where(onehot, cq_col, cqt)
        cqt_ref[sl, :] = cqt


def _p1_kernel(tiles_per_batch, x_ref, g_ref, w_ref, b_ref, wg_ref, bg_ref, cw_ref,
               c_ref, sq_ref, sk_ref, svt_ref, sgate_ref, sgatet_ref, sm_ref, smt_ref,
               yconv_ref, q_ref, k_ref, v_ref, so_ref, gate_ref, pconv_ref, cn_ref, cqt_ref, ubuf):
    tm = x_ref.shape[0]
    i = pl.program_id(0)
    _memory_update_rows(i, c_ref, sq_ref, sk_ref, svt_ref, sgate_ref, sgatet_ref, sm_ref, smt_ref,
                        cn_ref, cqt_ref)
    xn = _rmsnorm(x_ref[...], g_ref[...]).astype(BF16)

    def seg(j):
        sl = slice(j * CONV_DIM, (j + 1) * CONV_DIM)
        return _dot_nt(xn, w_ref[sl, :].astype(BF16)) + b_ref[:, sl]

    P = SUBLANES
    prev = ubuf[tm:tm + P, :]
    ubuf[0:P, :] = jnp.where(i % tiles_per_batch == 0, jnp.zeros_like(prev), prev)
    ubuf[P:P + tm, :] = seg(1) * seg(2)
    conv = sum(cw_ref[j:j + 1, :] * ubuf[P - (CONV_W - 1) + j:P - (CONV_W - 1) + j + tm, :] for j in range(CONV_W))
    yconv_ref[...] = (seg(0) * conv).astype(BF16)
    pconv_ref[0] = ubuf[tm + P - (CONV_W - 1):tm + P, :]

    q_ref[...] = seg(3).astype(BF16)
    k_ref[...] = (seg(4) * (DQK ** -0.5)).astype(BF16)
    v_ref[...] = seg(5).astype(BF16)
    so_ref[...] = jax.nn.sigmoid(seg(6)).astype(BF16)
    gt = _gate_transform_rows(_dot_nt(wg_ref[...], xn) + bg_ref[...])
    n_gate = gt.shape[0]
    gate_ref[0, 0:n_gate, :] = gt
    L = MLSTM_CHUNK
    n_blk = tm // L
    hi = gt.astype(BF16).astype(F32)
    r1 = gt - hi
    mid = r1.astype(BF16).astype(F32)
    lo = r1 - mid
    terms = jnp.concatenate([t[:, j * L:(j + 1) * L] for t in (hi, mid, lo) for j in range(n_blk)], axis=0)
    tri = (lax.broadcasted_iota(jnp.int32, (L, L), 0) <= lax.broadcasted_iota(jnp.int32, (L, L), 1)).astype(BF16)
    parts = _dot(terms.astype(BF16), tri)
    for j in range(n_blk):
        rows = [parts[(t * n_blk + j) * n_gate:(t * n_blk + j + 1) * n_gate, :] for t in range(3)]
        gate_ref[0, n_gate:2 * n_gate, j * L:(j + 1) * L] = (rows[0] + rows[1]) + rows[2]


def _p1_call(x, g, w, b, wg, bg, cw, seq_len, side):
    rows = x.shape[0]
    tm = P1_TILE
    steps = rows // tm
    tiles_per_batch = seq_len // tm
    n_batch = rows // seq_len
    row = lambda width: pl.BlockSpec((tm, width), lambda i: (i, 0))
    c, sq, sk, svt, sgate, sgatet, sm, smt = side
    n = sq.shape[0]
    sr = n // steps
    assert sr * steps == n and sr % SUBLANES == 0
    full = lambda a: pl.BlockSpec(a.shape, lambda i: (0,) * a.ndim)
    srow = lambda a: pl.BlockSpec((sr,) + a.shape[1:], lambda i: (i,) + (0,) * (a.ndim - 1))
    return pl.pallas_call(
        functools.partial(_p1_kernel, tiles_per_batch),
        grid=(steps,),
        in_specs=[row(D_MODEL), _const_spec((1, D_MODEL)), _const_spec(w.shape),
                  _const_spec(b.shape), _const_spec((2 * HEADS, D_MODEL)),
                  _const_spec((2 * HEADS, 1)), _const_spec((CONV_W, CONV_DIM)),
                  srow(c), srow(sq), srow(sk), full(svt), srow(sgate), full(sgatet), srow(sm), full(smt)],
        out_specs=[row(CONV_DIM), row(MLSTM_DIM), row(MLSTM_DIM), row(MLSTM_DIM), row(MLSTM_DIM),
                   pl.BlockSpec((1, 4 * HEADS, tm), lambda i: (i // tiles_per_batch, 0, i % tiles_per_batch)),
                   pl.BlockSpec((1, CONV_W - 1, CONV_DIM), lambda i: (i // tiles_per_batch, 0, 0)),
                   srow(c), full(svt)],
        out_shape=[jax.ShapeDtypeStruct((rows, CONV_DIM), BF16)]
        + [jax.ShapeDtypeStruct((rows, MLSTM_DIM), BF16)] * 4
        + [jax.ShapeDtypeStruct((n_batch, 4 * HEADS, seq_len), F32),
           jax.ShapeDtypeStruct((n_batch, CONV_W - 1, CONV_DIM), F32),
           jax.ShapeDtypeStruct(c.shape, F32), jax.ShapeDtypeStruct(svt.shape, F32)],
        scratch_shapes=[pltpu.VMEM((tm + SUBLANES, CONV_DIM), F32)],
        compiler_params=_params(("arbitrary",)),
        name="p1_inproj_conv",
    )(x, g, w, b, wg, bg, cw, c, sq, sk, svt, sgate, sgatet, sm, smt)


def _p2_kernel(q_ref, k_ref, v_ref, so_ref, yconv_ref, gate_ref, x_ref, wout_ref, gmh_ref,
               hp_ref, pc_ref, pn_ref, pm_ref, c_s, m_s, y_s):
    nb, L = q_ref.shape[0], q_ref.shape[1]
    c = pl.program_id(1)

    @pl.when(c == 0)
    def _():
        c_s[...] = jnp.zeros_like(c_s)
        m_s[...] = jnp.zeros_like(m_s)

    row = lax.broadcasted_iota(jnp.int32, (L, L), 0)
    col = lax.broadcasted_iota(jnp.int32, (L, L), 1)
    causal = row >= col

    for bi in range(nb):
        gt = gate_ref[bi]
        for h in range(HEADS):
            sl = slice(h * DQK, (h + 1) * DQK)
            q = q_ref[bi, :, sl]
            k = k_ref[bi, :, sl]
            v = v_ref[bi, :, sl]
            lf_r = gt[HEADS + h:HEADS + h + 1, :]
            a_r = gt[h:h + 1, :] - gt[3 * HEADS + h:3 * HEADS + h + 1, :]
            m_prev = jnp.max(m_s[bi, h:h + 1, :], axis=1, keepdims=True)
            c_prev = c_s[bi, h]

            m_c = jnp.maximum(m_prev, jnp.max(jnp.where(causal, a_r, -jnp.inf), axis=1, keepdims=True))
            b_c = jnp.sum(jnp.where(causal, lf_r, 0.0), axis=1, keepdims=True)
            w = _dot_nt(q, k) * jnp.exp(jnp.where(causal, a_r - m_c, -jnp.inf))
            g = jnp.exp(m_prev - m_c)
            qc = _dot_nt(q, c_prev.astype(BF16))
            num = g * qc[:, 0:DV] + _dot(w.astype(BF16), v)
            den = g * qc[:, DV:2 * DV] + jnp.sum(w, axis=1, keepdims=True)
            hh = num / jnp.maximum(jnp.abs(den), jnp.exp(-(b_c + m_c)))
            hh = hh * lax.rsqrt(jnp.mean(hh * hh, axis=1, keepdims=True) + EPS) * gmh_ref[:, sl]
            y_s[bi * L:(bi + 1) * L, h * DV:(h + 1) * DV] = (so_ref[bi, :, sl].astype(F32) * hh).astype(BF16)

            m_last = jnp.maximum(m_prev, jnp.max(a_r, axis=1, keepdims=True))
            b_last = jnp.sum(lf_r, axis=1, keepdims=True)
            s_r = jnp.exp(a_r - m_last)
            sv_t = jnp.concatenate([v.T.astype(F32) * s_r, jnp.broadcast_to(s_r, (DV, L))], axis=0)
            c_s[bi, h] = jnp.exp(m_prev - m_last) * c_prev + _dot(sv_t.astype(BF16), k)
            m_s[bi, h:h + 1, :] = jnp.broadcast_to(b_last + m_last, (1, m_s.shape[2]))

    w_conv = wout_ref[0:CONV_DIM, :].astype(BF16)
    w_ml = wout_ref[CONV_DIM:CONV_DIM + MLSTM_DIM, :].astype(BF16)
    for bi in range(nb):
        out = _dot(yconv_ref[bi], w_conv) + _dot(y_s[bi * L:(bi + 1) * L, :], w_ml)
        hp_ref[bi] = x_ref[bi] + out

    @pl.when(c == pl.num_programs(1) - 1)
    def _():
        lane = lax.broadcasted_iota(jnp.int32, (1, m_s.shape[2]), 1)
        for bi in range(nb):
            acc = jnp.zeros((1, m_s.shape[2]), F32)
            for h in range(HEADS):
                pc_ref[bi, h] = c_s[bi, h, 0:DV, :]
                pn_ref[bi, h:h + 1, :] = c_s[bi, h, DV:DV + 1, :]
                acc = jnp.where(lane == h, m_s[bi, h:h + 1, :], acc)
            pm_ref[bi] = acc


def _p2_call(q, k, v, so, yconv, gates, x, wout, gmh, n_batch, seq_len):
    L = MLSTM_CHUNK
    nb = P2_SEQS
    nc = seq_len // L
    seq = lambda width: pl.BlockSpec((nb, L, width), lambda b, c: (b, c, 0))
    as_seq = lambda a: a.reshape(n_batch, seq_len, a.shape[-1])
    return pl.pallas_call(
        _p2_kernel,
        grid=(n_batch // nb, nc),
        in_specs=[seq(MLSTM_DIM), seq(MLSTM_DIM), seq(MLSTM_DIM), seq(MLSTM_DIM), seq(CONV_DIM),
                  pl.BlockSpec((nb, 4 * HEADS, L), lambda b, c: (b, 0, c)), seq(D_MODEL),
                  _const_spec((D_MODEL, D_MODEL)), _const_spec((1, MLSTM_DIM))],
        out_specs=[seq(D_MODEL),
                   pl.BlockSpec((nb, HEADS, DV, DQK), lambda b, c: (b, 0, 0, 0)),
                   pl.BlockSpec((nb, HEADS, DQK), lambda b, c: (b, 0, 0)),
                   pl.BlockSpec((nb, 1, LANES), lambda b, c: (b, 0, 0))],
        out_shape=[jax.ShapeDtypeStruct((n_batch, seq_len, D_MODEL), F32),
                   jax.ShapeDtypeStruct((n_batch, HEADS, DV, DQK), F32),
                   jax.ShapeDtypeStruct((n_batch, HEADS, DQK), F32),
                   jax.ShapeDtypeStruct((n_batch, 1, LANES), F32)],
        scratch_shapes=[pltpu.VMEM((nb, HEADS, 2 * DV, DQK), F32), pltpu.VMEM((nb, SUBLANES, LANES), F32),
                        pltpu.VMEM((nb * L, MLSTM_DIM), BF16)],
        compiler_params=_params(("arbitrary", "arbitrary")),
        name="p2_mlstm_outproj",
    )(as_seq(q), as_seq(k), as_seq(v), as_seq(so), as_seq(yconv), gates, as_seq(x), wout, gmh)


def _pm_kernel(mem_ref, g_ref, w_ref, k_ref, v_ref, kb_ref, vb_ref):
    xn = _rmsnorm(mem_ref[...], g_ref[...]).astype(BF16)
    kk = _dot(xn, w_ref[:, 0:D_MODEL].astype(BF16))
    vv = _dot(xn, w_ref[:, D_MODEL:2 * D_MODEL].astype(BF16))
    for h in range(X_HEADS):
        sl = slice(h * X_HEAD_DIM, (h + 1) * X_HEAD_DIM)
        k_ref[:, h, :] = kk[:, sl]
        v_ref[:, h, :] = vv[:, sl]
    kb_ref[...] = kk.astype(BF16)
    vb_ref[...] = vv.astype(BF16)


def _pm_call(mem, g, w):
    rows = mem.shape[0]
    tm = ROW_TILE
    row = pl.BlockSpec((tm, D_MODEL), lambda i: (i, 0))
    row4 = pl.BlockSpec((tm, X_HEADS, X_HEAD_DIM), lambda i: (i, 0, 0))
    return pl.pallas_call(
        _pm_kernel,
        grid=(rows // tm,),
        in_specs=[row, _const_spec((1, D_MODEL)), _const_spec((D_MODEL, 2 * D_MODEL))],
        out_specs=[row4, row4, row, row],
        out_shape=[jax.ShapeDtypeStruct((rows, X_HEADS, X_HEAD_DIM), F32)] * 2
        + [jax.ShapeDtypeStruct((rows, D_MODEL), BF16)] * 2,
        compiler_params=_params(("arbitrary",)),
        name="pm_mem_kv",
    )(mem, g, w)


def _p3_kernel(n_cast, hp_ref, g_ref, wq_ref, k_ref, v_ref, *rest):
    for src, dst in zip(rest[:n_cast], rest[n_cast + 1:]):
        dst[...] = src[...].astype(BF16)
    o_ref = rest[n_cast]
    xn = _rmsnorm(hp_ref[...], g_ref[...]).astype(BF16)
    q = _dot(xn, wq_ref[...].astype(BF16))
    for h in range(X_HEADS):
        sl = slice(h * X_HEAD_DIM, (h + 1) * X_HEAD_DIM)
        s = _dot_nt(q[:, sl].astype(BF16), k_ref[0, :, sl]) * (X_HEAD_DIM ** -0.5)
        e = jnp.exp(s - jnp.max(s, axis=1, keepdims=True))
        p = e * (1.0 / jnp.sum(e, axis=1, keepdims=True))
        o_ref[:, sl] = _dot(p.astype(BF16), v_ref[0, :, sl]).astype(BF16)


def _p3_call(hp, g, wq, kb, vb, seq_len, cast=()):
    rows = hp.shape[0]
    tm = P3_TILE
    steps = rows // tm
    tiles_per_batch = seq_len // tm
    row = pl.BlockSpec((tm, D_MODEL), lambda i: (i, 0))
    mem = pl.BlockSpec((1, N_MEM, D_MODEL), lambda i: (i // tiles_per_batch, 0, 0))
    slabs = []
    for wgt in cast:
        slab = wgt.shape[0] // steps
        assert slab * steps == wgt.shape[0] and slab % BF16_ROWS == 0
        slabs.append(pl.BlockSpec((slab, wgt.shape[1]), lambda i: (i, 0)))
    return pl.pallas_call(
        functools.partial(_p3_kernel, len(cast)),
        grid=(steps,),
        in_specs=[row, _const_spec((1, D_MODEL)), _const_spec((D_MODEL, D_MODEL)), mem, mem] + slabs,
        out_specs=[row] + slabs,
        out_shape=[jax.ShapeDtypeStruct((rows, D_MODEL), BF16)]
        + [jax.ShapeDtypeStruct(wgt.shape, BF16) for wgt in cast],
        compiler_params=_params(("arbitrary",)),
        name="p3_cross_attn",
    )(hp, g, wq, kb, vb, *cast)


def _cache_attention_row(q8, kc_ref, vc_ref, bl):
    qs = q8 * (X_HEAD_DIM ** -0.5)
    m_run = jnp.full((1, PACK_ROWS, 1), -jnp.inf, F32)
    l_run = jnp.zeros((1, PACK_ROWS, 1), F32)
    acc = jnp.zeros((PACK_ROWS, LANES), F32)
    for c in range(N_MEM // MEM_CHUNK):
        blk = slice(c * MEM_CHUNK, (c + 1) * MEM_CHUNK)
        prod = kc_ref[bl, blk] * qs
        s = jnp.sum(prod + pltpu.roll(prod, X_HEADS, 1), axis=-1, keepdims=True)
        m_new = jnp.maximum(m_run, jnp.max(s, axis=0, keepdims=True))
        alpha = jnp.exp(m_run - m_new)
        e = jnp.exp(s - m_new)
        l_run = alpha * l_run + jnp.sum(e, axis=0, keepdims=True)
        acc = alpha[0] * acc + jnp.sum(e * vc_ref[bl, blk], axis=0)
        m_run = m_new
    return acc * (1.0 / l_run[0])


def _swiglu_final(hp, gf_ref, wgu_ref, wd_ref, gfin_ref, act_s):
    xn = _rmsnorm(hp, gf_ref[...]).astype(BF16)
    for j in range(D_FF // FF_CHUNK):
        g = _dot(xn, wgu_ref[:, FF_CHUNK * j:FF_CHUNK * (j + 1)])
        u = _dot(xn, wgu_ref[:, D_FF + FF_CHUNK * j:D_FF + FF_CHUNK * (j + 1)])
        act_s[:, FF_CHUNK * j:FF_CHUNK * (j + 1)] = (g * jax.nn.sigmoid(g) * u).astype(BF16)
    hp = hp + _dot(act_s[...], wd_ref[...])
    return _rmsnorm(hp, gfin_ref[...])


def _p4_kernel(hp_ref, o_ref, wxo_ref, gf_ref, wgu_ref, wd_ref, gfin_ref, q4_ref, kc_ref, vc_ref,
               y_ref, os_ref, act_s):
    for bl in range(q4_ref.shape[0]):
        os_ref[bl] = _cache_attention_row(q4_ref[bl], kc_ref, vc_ref, bl)
    hp = hp_ref[...] + _dot(o_ref[...], wxo_ref[...])
    y_ref[...] = _swiglu_final(hp, gf_ref, wgu_ref, wd_ref, gfin_ref, act_s)


def _p4_call(hp, o, wxo, gf, wgu, wd, gfin, side):
    rows = hp.shape[0]
    tm = ROW_TILE
    steps = rows // tm
    q4, kc, vc = side
    side_rows = q4.shape[0] // steps
    assert side_rows * steps == q4.shape[0]
    row = pl.BlockSpec((tm, D_MODEL), lambda i: (i, 0))
    srow = pl.BlockSpec((side_rows, PACK_ROWS, LANES), lambda i: (i, 0, 0))
    cache = pl.BlockSpec((side_rows, N_MEM, PACK_ROWS, LANES), lambda i: (i, 0, 0, 0))
    return pl.pallas_call(
        _p4_kernel,
        grid=(steps,),
        in_specs=[row, row, _const_spec((D_MODEL, D_MODEL)), _const_spec((1, D_MODEL)),
                  _const_spec((D_MODEL, 2 * D_FF)), _const_spec((D_FF, D_MODEL)),
                  _const_spec((1, D_MODEL)), srow, cache, cache],
        out_specs=[row, srow],
        out_shape=[jax.ShapeDtypeStruct((rows, D_MODEL), F32), jax.ShapeDtypeStruct(q4.shape, F32)],
        scratch_shapes=[pltpu.VMEM((tm, D_FF), BF16)],
        compiler_params=_params(("arbitrary",)),
        name="p4_ffn_final",
    )(hp, o, wxo, gf, wgu, wd, gfin, q4, kc, vc)


def _s1_kernel(x_ref, g_ref, w_ref, b_ref, wg_ref, bg_ref, cw_ref, st_ref,
               yconv_ref, sconv_ref, q_ref, k_ref, v_ref, so_ref, gate_ref, vt_ref, gatet_ref):
    xn = _rmsnorm(x_ref[:, 0, :], g_ref[...]).astype(BF16)

    def seg(j):
        sl = slice(j * CONV_DIM, (j + 1) * CONV_DIM)
        return _dot_nt(xn, w_ref[sl, :].astype(BF16)) + b_ref[:, sl]

    u = seg(1) * seg(2)
    st0 = st_ref[:, 0, :]
    st1 = st_ref[:, 1, :]
    conv = cw_ref[0:1, :] * st0 + cw_ref[1:2, :] * st1 + cw_ref[2:3, :] * u
    yconv_ref[...] = seg(0) * conv
    sconv_ref[:, 0, :] = st1
    sconv_ref[:, 1, :] = u
    q_ref[...] = seg(3)
    k_ref[...] = seg(4) * (DQK ** -0.5)
    v = seg(5)
    v_ref[...] = v
    so_ref[...] = jax.nn.sigmoid(seg(6))
    n_gate = wg_ref.shape[0]
    gate_ref[...] = _gate_transform(_dot_nt(xn, wg_ref[...]) + b_ref[:, MAIN_DIM:MAIN_DIM + n_gate])
    for h in range(HEADS):
        sl = slice(h * DV, (h + 1) * DV)
        vt_ref[sl, :] = v[:, sl].T
    gatet_ref[...] = _gate_transform_rows(_dot_nt(wg_ref[...], xn) + bg_ref[...])


def _s1_call(x, g, w, b, wg, bg, cw, st):
    n = x.shape[0]
    n_gate = wg.shape[0]
    full = lambda *shape: pl.BlockSpec(shape, lambda i: (0,) * len(shape))
    ins = [x, g, w, b, wg, bg, cw, st]
    return pl.pallas_call(
        _s1_kernel,
        grid=(1,),
        in_specs=[full(*a.shape) for a in ins],
        out_specs=[full(n, CONV_DIM), full(*st.shape), full(n, MLSTM_DIM), full(n, MLSTM_DIM),
                   full(n, MLSTM_DIM), full(n, MLSTM_DIM), full(n, n_gate), full(MLSTM_DIM, n),
                   full(n_gate, n)],
        out_shape=[jax.ShapeDtypeStruct((n, CONV_DIM), F32),
                   jax.ShapeDtypeStruct(st.shape, F32)]
        + [jax.ShapeDtypeStruct((n, MLSTM_DIM), F32)] * 4
        + [jax.ShapeDtypeStruct((n, n_gate), F32),
           jax.ShapeDtypeStruct((MLSTM_DIM, n), F32),
           jax.ShapeDtypeStruct((n_gate, n), F32)],
        compiler_params=_params(("arbitrary",)),
        name="s1_inproj_conv",
    )(*ins)


def _s3_kernel(cqt_ref, q_ref, k_ref, v_ref, so_ref, gate_ref, n_ref, m_ref, yconv_ref, x_ref,
               wout_ref, gmh_ref, gx_ref, wq_ref,
               hs_ref, qx_ref, nn_ref, mn_ref, y_s):
    n_rows = q_ref.shape[0]
    y_s[:, 0:CONV_DIM] = yconv_ref[...].astype(BF16)
    lane = lax.broadcasted_iota(jnp.int32, (n_rows, mn_ref.shape[1]), 1)
    m_out = jnp.zeros((n_rows, mn_ref.shape[1]), F32)
    for h in range(HEADS):
        sl = slice(h * DQK, (h + 1) * DQK)
        q = q_ref[:, sl]
        k = k_ref[:, sl]
        v = v_ref[:, sl]
        n_prev = n_ref[:, h, :]
        cq = cqt_ref[sl, :].T
        ig = gate_ref[:, h:h + 1]
        lf = gate_ref[:, HEADS + h:HEADS + h + 1]
        m_prev = m_ref[:, h:h + 1]
        inter = lf + m_prev
        m_row = jnp.maximum(inter, ig)
        wgt = jnp.sum(q * k, axis=1, keepdims=True) * jnp.exp(ig - m_row)
        g = jnp.exp(inter - m_row)
        num = g * cq + wgt * v
        den = g * jnp.sum(n_prev * q, axis=1, keepdims=True) + wgt
        hh = num / jnp.maximum(jnp.abs(den), jnp.exp(-m_row))
        hh = hh * lax.rsqrt(jnp.mean(hh * hh, axis=1, keepdims=True) + EPS) * gmh_ref[:, sl]
        y_s[:, CONV_DIM + h * DV:CONV_DIM + (h + 1) * DV] = (so_ref[:, sl] * hh).astype(BF16)
        nn_ref[:, h, :] = g * n_prev + jnp.exp(ig - m_row) * k
        m_out = jnp.where(lane == h, m_row, m_out)
    mn_ref[...] = m_out
    hs = x_ref[:, 0, :] + _dot(y_s[...], wout_ref[...].astype(BF16))
    hs_ref[...] = hs
    qx = _dot(_rmsnorm(hs, gx_ref[...]).astype(BF16), wq_ref[...].astype(BF16))
    for h in range(X_HEADS):
        for half in range(X_HEAD_DIM // LANES):
            lo = h * X_HEAD_DIM + half * LANES
            qx_ref[:, half * X_HEADS + h, :] = qx[:, lo:lo + LANES]


def _s3_call(cqt, q, k, v, so, gates, nst, m, yconv, x, wout, gmh, gx, wq):
    n = q.shape[0]
    full = lambda *shape: pl.BlockSpec(shape, lambda i: (0,) * len(shape))
    ins = [cqt, q, k, v, so, gates, nst, m, yconv, x, wout, gmh, gx, wq]
    return pl.pallas_call(
        _s3_kernel,
        grid=(1,),
        in_specs=[full(*a.shape) for a in ins],
        out_specs=[full(n, D_MODEL), full(n, PACK_ROWS, LANES), full(*nst.shape), full(n, LANES)],
        out_shape=[jax.ShapeDtypeStruct((n, D_MODEL), F32), jax.ShapeDtypeStruct((n, PACK_ROWS, LANES), F32),
                   jax.ShapeDtypeStruct(nst.shape, F32), jax.ShapeDtypeStruct((n, LANES), F32)],
        scratch_shapes=[pltpu.VMEM((n, D_MODEL), BF16)],
        compiler_params=_params(("arbitrary",)),
        name="s3_mlstm_finish",
    )(*ins)


def _s5_kernel(hs_ref, o_ref, wxo_ref, gf_ref, wgu_ref, wd_ref, gfin_ref, y_ref, act_s):
    o = jnp.concatenate([o_ref[:, half * X_HEADS + h, :] for h in range(X_HEADS)
                         for half in range(X_HEAD_DIM // LANES)], axis=1)
    hp = hs_ref[...] + _dot(o.astype(BF16), wxo_ref[...])
    y_ref[:, 0, :] = _swiglu_final(hp, gf_ref, wgu_ref, wd_ref, gfin_ref, act_s)


def _s5_call(hs, o, wxo, gf, wgu, wd, gfin):
    n = hs.shape[0]
    full = lambda a: pl.BlockSpec(a.shape, lambda i: (0,) * a.ndim)
    ins = [hs, o, wxo, gf, wgu, wd, gfin]
    return pl.pallas_call(
        _s5_kernel,
        grid=(1,),
        in_specs=[full(a) for a in ins],
        out_specs=pl.BlockSpec((n, 1, D_MODEL), lambda i: (0, 0, 0)),
        out_shape=jax.ShapeDtypeStruct((n, 1, D_MODEL), F32),
        scratch_shapes=[pltpu.VMEM((n, D_FF), BF16)],
        compiler_params=_params(("arbitrary",)),
        name="s5_ffn_final",
    )(*ins)


def _pack_heads(a):
    lead = a.shape[:-2]
    a = a.reshape(lead + (X_HEADS, X_HEAD_DIM // LANES, LANES))
    return jnp.swapaxes(a, -3, -2).reshape(lead + (PACK_ROWS, LANES))


def kernel(x_prompt, x_sample, mem_prompt, state_conv, state_mlstm_C, state_mlstm_n, state_mlstm_m,
           cache_mem_k, cache_mem_v, g_mix, w_in, b_in, conv_w, g_mh, w_out, g_cross, g_mem,
           w_xq, w_xkv, w_xo, g_ffn, w_gu, w_down, g_final):
    n_batch, seq_len, _ = x_prompt.shape
    n_dec = x_sample.shape[0]
    depth = w_in.shape[0]
    assert depth == 1 and x_sample.shape[1] == 1
    assert all(seq_len % t == 0 for t in (ROW_TILE, P1_TILE, P3_TILE))
    assert P1_TILE % MLSTM_CHUNK == 0 and n_batch % P2_SEQS == 0

    n_gate = 2 * HEADS
    w_in_b = w_in[0].T
    b_in_r = b_in[0].reshape(1, MAIN_DIM + n_gate)
    w_gate_r = w_in_b[MAIN_DIM:].astype(BF16)
    b_gate_r = b_in[0, MAIN_DIM:].reshape(n_gate, 1)
    w_out_b = w_out[0]
    w_xq_b = w_xq[0]
    w_xkv_b = w_xkv[0]
    g_mix_r = g_mix[0].reshape(1, D_MODEL)
    g_cross_r = g_cross[0].reshape(1, D_MODEL)
    g_mem_r = g_mem[0].reshape(1, D_MODEL)
    g_ffn_r = g_ffn[0].reshape(1, D_MODEL)
    g_final_r = g_final.reshape(1, D_MODEL)
    g_mh_r = g_mh[0].reshape(1, MLSTM_DIM)
    cw = conv_w[0]

    m0 = state_mlstm_m[0]
    s_yconv, s_conv, sq, sk, sv, sso, sgates, svt, sgates_t = _s1_call(
        x_sample, g_mix_r, w_in_b, b_in_r, w_gate_r, b_gate_r, cw, state_conv[0])

    xp = x_prompt.reshape(n_batch * seq_len, D_MODEL)
    yconv, q, k, v, so, gates, p_conv, s_c, cqt = _p1_call(
        xp, g_mix_r, w_in_b, b_in_r, w_gate_r, b_gate_r, cw, seq_len,
        side=(state_mlstm_C[0], sq, sk, svt, sgates, sgates_t, m0, m0.T))
    hs1, qx, s_n, s_m = _s3_call(cqt, sq, sk, sv, sso, sgates,
                                 state_mlstm_n[0], m0, s_yconv, x_sample,
                                 w_out_b, g_mh_r, g_cross_r, w_xq_b)
    hp1, p_c, p_n, p_m = _p2_call(q, k, v, so, yconv, gates, xp, w_out_b, g_mh_r, n_batch, seq_len)
    hp1 = hp1.reshape(n_batch * seq_len, D_MODEL)
    pk, pv, pkb, pvb = _pm_call(mem_prompt.reshape(n_batch * N_MEM, D_MODEL), g_mem_r, w_xkv_b)
    o_p, w_xo_b, w_gu_b, w_down_b = _p3_call(
        hp1, g_cross_r, w_xq_b, pkb.reshape(n_batch, N_MEM, D_MODEL), pvb.reshape(n_batch, N_MEM, D_MODEL),
        seq_len, cast=(w_xo[0], w_gu[0], w_down[0]))
    y_p, o_s = _p4_call(hp1, o_p, w_xo_b, g_ffn_r, w_gu_b, w_down_b, g_final_r,
                        side=(qx, _pack_heads(cache_mem_k[0]), _pack_heads(cache_mem_v[0])))

    y_s = _s5_call(hs1, o_s, w_xo_b, g_ffn_r, w_gu_b, w_down_b, g_final_r)

    mem_shape = (1, n_batch, N_MEM, X_HEADS, X_HEAD_DIM)
    return (y_p.reshape(n_batch, seq_len, D_MODEL),
            y_s.reshape(n_dec, 1, D_MODEL),
            p_conv.reshape(1, n_batch, CONV_W - 1, CONV_DIM),
            p_c.reshape(1, n_batch, HEADS, DV, DQK),
            p_n.reshape(1, n_batch, HEADS, DQK),
            p_m[:, 0, :HEADS].reshape(1, n_batch, HEADS),
            pk.reshape(mem_shape),
            pv.reshape(mem_shape),
            s_conv.reshape(1, n_dec, CONV_W - 1, CONV_DIM),
            s_c.reshape(1, n_dec, HEADS, DV, DQK),
            s_n.reshape(1, n_dec, HEADS, DQK),
            s_m[:, :HEADS].reshape(1, n_dec, HEADS))
```

```python
import functools

import jax
import jax.numpy as jnp
from jax import lax
from jax.experimental import pallas as pl
from jax.experimental.pallas import tpu as pltpu

F32 = jnp.float32
BF16 = jnp.bfloat16

D_MODEL = 1024
CONV_DIM = 512
CONV_W = 3
MLSTM_DIM = 512
HEADS = 4
DQK = 128
DV = 128
N_MEM = 256
X_HEADS = 4
X_HEAD_DIM = 256
D_FF = 2816
MAIN_DIM = 3 * CONV_DIM + 4 * MLSTM_DIM
EPS = 1e-6

LANES = 128
SUBLANES = 8
BF16_ROWS = 16

MLSTM_CHUNK = 256
P2_SEQS = 4
ROW_TILE = 512
P1_TILE = 1024
P3_TILE = 1024
FF_CHUNK = 256
PACK_ROWS = X_HEADS * (X_HEAD_DIM // LANES)
MEM_CHUNK = 32
VMEM_LIMIT = 56 * 1024 * 1024


def _dot(a, b):
    return jnp.dot(a, b, preferred_element_type=F32)


def _dot_nt(a, b):
    return lax.dot_general(a, b, (((1,), (1,)), ((), ())), preferred_element_type=F32)


def _rmsnorm(x, g):
    return x * lax.rsqrt(jnp.mean(x * x, axis=-1, keepdims=True) + EPS) * g


def _const_spec(shape):
    zeros = (0,) * len(shape)
    return pl.BlockSpec(shape, lambda *_: zeros, pipeline_mode=pl.Buffered(1))


def _params(sem):
    return pltpu.CompilerParams(dimension_semantics=sem, vmem_limit_bytes=VMEM_LIMIT)


def _gate_transform(gt):
    lane = lax.broadcasted_iota(jnp.int32, gt.shape, 1)
    return jnp.where(lane < HEADS, gt, jax.nn.log_sigmoid(gt))


def _gate_transform_rows(gt):
    sub = lax.broadcasted_iota(jnp.int32, gt.shape, 0)
    return jnp.where(sub < HEADS, gt, jax.nn.log_sigmoid(gt))


def _memory_update_rows(i, c_ref, q_ref, k_ref, vt_ref, gate_ref, gatet_ref, m_ref, mt_ref, cn_ref, cqt_ref):
    n = vt_ref.shape[1]
    bb = c_ref.shape[0]

    @pl.when(i == 0)
    def _():
        cqt_ref[...] = jnp.zeros_like(cqt_ref)

    lane = lax.broadcasted_iota(jnp.int32, (DV, n), 1)
    for h in range(HEADS):
        sl = slice(h * DQK, (h + 1) * DQK)
        ig_c = gate_ref[:, h:h + 1]
        lf_c = gate_ref[:, HEADS + h:HEADS + h + 1]
        m_c = m_ref[:, h:h + 1]
        dec = jnp.broadcast_to(jnp.exp(lf_c + m_c - jnp.maximum(lf_c + m_c, ig_c)), (bb, DQK))
        ig_r = gatet_ref[h:h + 1, :]
        lf_r = gatet_ref[HEADS + h:HEADS + h + 1, :]
        m_r = mt_ref[h:h + 1, :]
        svt = vt_ref[sl, :] * jnp.exp(ig_r - jnp.maximum(lf_r + m_r, ig_r))
        q_t = q_ref[:, sl]
        k_t = k_ref[:, sl]
        cqt = cqt_ref[sl, :]
        for bl in range(bb):
            onehot = lane == i * bb + bl
            c = c_ref[bl, h]
            cq_col = jnp.sum(c * q_t[bl:bl + 1, :], axis=1, keepdims=True)
            sv_col = jnp.sum(jnp.where(onehot, svt, 0.0), axis=1, keepdims=True)
            cn_ref[bl, h] = dec[bl:bl + 1, :] * c + sv_col * k_t[bl:bl + 1, :]
            cqt = jnp.where(onehot, cq_col, cqt)
        cqt_ref[sl, :] = cqt


def _cast_slabs(srcs, dsts):
    for src, dst in zip(srcs, dsts):
        dst[...] = src[...].astype(BF16)


def _p1_kernel(tiles_per_batch, n_cast, x_ref, g_ref, w_ref, b_ref, wg_ref, bg_ref, cw_ref,
               c_ref, sq_ref, sk_ref, svt_ref, sgate_ref, sgatet_ref, sm_ref, smt_ref, *rest):
    cast_in, rest = rest[:n_cast], rest[n_cast:]
    yconv_ref, q_ref, k_ref, v_ref, so_ref, gate_ref, pconv_ref, cn_ref, cqt_ref = rest[:9]
    cast_out, ubuf = rest[9:9 + n_cast], rest[9 + n_cast]
    _cast_slabs(cast_in, cast_out)
    tm = x_ref.shape[0]
    i = pl.program_id(0)
    _memory_update_rows(i, c_ref, sq_ref, sk_ref, svt_ref, sgate_ref, sgatet_ref, sm_ref, smt_ref,
                        cn_ref, cqt_ref)
    xn = _rmsnorm(x_ref[...], g_ref[...]).astype(BF16)

    def seg(j):
        sl = slice(j * CONV_DIM, (j + 1) * CONV_DIM)
        return _dot_nt(xn, w_ref[sl, :].astype(BF16)) + b_ref[:, sl]

    P = SUBLANES
    prev = ubuf[tm:tm + P, :]
    ubuf[0:P, :] = jnp.where(i % tiles_per_batch == 0, jnp.zeros_like(prev), prev)
    ubuf[P:P + tm, :] = seg(1) * seg(2)
    conv = sum(cw_ref[j:j + 1, :] * ubuf[P - (CONV_W - 1) + j:P - (CONV_W - 1) + j + tm, :] for j in range(CONV_W))
    yconv_ref[...] = (seg(0) * conv).astype(BF16)
    pconv_ref[0] = ubuf[tm + P - (CONV_W - 1):tm + P, :]

    q_ref[...] = seg(3).astype(BF16)
    k_ref[...] = (seg(4) * (DQK ** -0.5)).astype(BF16)
    v_ref[...] = seg(5).astype(BF16)
    so_ref[...] = jax.nn.sigmoid(seg(6)).astype(BF16)
    gt = _gate_transform_rows(_dot_nt(wg_ref[...], xn) + bg_ref[...])
    n_gate = gt.shape[0]
    gate_ref[0, 0:n_gate, :] = gt
    L = MLSTM_CHUNK
    n_blk = tm // L
    hi = gt.astype(BF16).astype(F32)
    r1 = gt - hi
    mid = r1.astype(BF16).astype(F32)
    lo = r1 - mid
    terms = jnp.concatenate([t[:, j * L:(j + 1) * L] for t in (hi, mid, lo) for j in range(n_blk)], axis=0)
    tri = (lax.broadcasted_iota(jnp.int32, (L, L), 0) <= lax.broadcasted_iota(jnp.int32, (L, L), 1)).astype(BF16)
    parts = _dot(terms.astype(BF16), tri)
    for j in range(n_blk):
        rows = [parts[(t * n_blk + j) * n_gate:(t * n_blk + j + 1) * n_gate, :] for t in range(3)]
        gate_ref[0, n_gate:2 * n_gate, j * L:(j + 1) * L] = (rows[0] + rows[1]) + rows[2]


def _slab_specs(weights, steps):
    specs = []
    for wgt in weights:
        slab = wgt.shape[0] // steps
        assert slab * steps == wgt.shape[0] and slab % BF16_ROWS == 0
        specs.append(pl.BlockSpec((slab, wgt.shape[1]), lambda i: (i, 0)))
    return specs


def _p1_call(x, g, w, b, wg, bg, cw, seq_len, side, cast=()):
    rows = x.shape[0]
    tm = P1_TILE
    steps = rows // tm
    slabs = _slab_specs(cast, steps)
    tiles_per_batch = seq_len // tm
    n_batch = rows // seq_len
    row = lambda width: pl.BlockSpec((tm, width), lambda i: (i, 0))
    c, sq, sk, svt, sgate, sgatet, sm, smt = side
    n = sq.shape[0]
    sr = n // steps
    assert sr * steps == n and sr % SUBLANES == 0
    full = lambda a: pl.BlockSpec(a.shape, lambda i: (0,) * a.ndim)
    srow = lambda a: pl.BlockSpec((sr,) + a.shape[1:], lambda i: (i,) + (0,) * (a.ndim - 1))
    return pl.pallas_call(
        functools.partial(_p1_kernel, tiles_per_batch, len(cast)),
        grid=(steps,),
        in_specs=[row(D_MODEL), _const_spec((1, D_MODEL)), _const_spec(w.shape),
                  _const_spec(b.shape), _const_spec((2 * HEADS, D_MODEL)),
                  _const_spec((2 * HEADS, 1)), _const_spec((CONV_W, CONV_DIM)),
                  srow(c), srow(sq), srow(sk), full(svt), srow(sgate), full(sgatet), srow(sm), full(smt)]
        + slabs,
        out_specs=[row(CONV_DIM), row(MLSTM_DIM), row(MLSTM_DIM), row(MLSTM_DIM), row(MLSTM_DIM),
                   pl.BlockSpec((1, 4 * HEADS, tm), lambda i: (i // tiles_per_batch, 0, i % tiles_per_batch)),
                   pl.BlockSpec((1, CONV_W - 1, CONV_DIM), lambda i: (i // tiles_per_batch, 0, 0)),
                   srow(c), full(svt)] + slabs,
        out_shape=[jax.ShapeDtypeStruct((rows, CONV_DIM), BF16)]
        + [jax.ShapeDtypeStruct((rows, MLSTM_DIM), BF16)] * 4
        + [jax.ShapeDtypeStruct((n_batch, 4 * HEADS, seq_len), F32),
           jax.ShapeDtypeStruct((n_batch, CONV_W - 1, CONV_DIM), F32),
           jax.ShapeDtypeStruct(c.shape, F32), jax.ShapeDtypeStruct(svt.shape, F32)]
        + [jax.ShapeDtypeStruct(wgt.shape, BF16) for wgt in cast],
        scratch_shapes=[pltpu.VMEM((tm + SUBLANES, CONV_DIM), F32)],
        compiler_params=_params(("arbitrary",)),
        name="p1_inproj_conv",
    )(x, g, w, b, wg, bg, cw, c, sq, sk, svt, sgate, sgatet, sm, smt, *cast)


def _p2_kernel(q_ref, k_ref, v_ref, so_ref, yconv_ref, gate_ref, x_ref, wout_ref, gmh_ref,
               hp_ref, pc_ref, pn_ref, pm_ref, c_s, m_s, y_s):
    nb, L = q_ref.shape[0], q_ref.shape[1]
    c = pl.program_id(1)

    @pl.when(c == 0)
    def _():
        c_s[...] = jnp.zeros_like(c_s)
        m_s[...] = jnp.zeros_like(m_s)

    row = lax.broadcasted_iota(jnp.int32, (L, L), 0)
    col = lax.broadcasted_iota(jnp.int32, (L, L), 1)
    causal = row >= col

    for bi in range(nb):
        gt = gate_ref[bi]
        for h in range(HEADS):
            sl = slice(h * DQK, (h + 1) * DQK)
            q = q_ref[bi, :, sl]
            k = k_ref[bi, :, sl]
            v = v_ref[bi, :, sl]
            lf_r = gt[HEADS + h:HEADS + h + 1, :]
            a_r = gt[h:h + 1, :] - gt[3 * HEADS + h:3 * HEADS + h + 1, :]
            m_prev = jnp.max(m_s[bi, h:h + 1, :], axis=1, keepdims=True)
            c_prev = c_s[bi, h]

            m_c = jnp.maximum(m_prev, jnp.max(jnp.where(causal, a_r, -jnp.inf), axis=1, keepdims=True))
            b_c = jnp.sum(jnp.where(causal, lf_r, 0.0), axis=1, keepdims=True)
            w = _dot_nt(q, k) * jnp.exp(jnp.where(causal, a_r - m_c, -jnp.inf))
            g = jnp.exp(m_prev - m_c)
            qc = _dot_nt(q, c_prev.astype(BF16))
            num = g * qc[:, 0:DV] + _dot(w.astype(BF16), v)
            den = g * qc[:, DV:2 * DV] + jnp.sum(w, axis=1, keepdims=True)
            hh = num / jnp.maximum(jnp.abs(den), jnp.exp(-(b_c + m_c)))
            hh = hh * lax.rsqrt(jnp.mean(hh * hh, axis=1, keepdims=True) + EPS) * gmh_ref[:, sl]
            y_s[bi * L:(bi + 1) * L, h * DV:(h + 1) * DV] = (so_ref[bi, :, sl].astype(F32) * hh).astype(BF16)

            m_last = jnp.maximum(m_prev, jnp.max(a_r, axis=1, keepdims=True))
            b_last = jnp.sum(lf_r, axis=1, keepdims=True)
            s_r = jnp.exp(a_r - m_last)
            sv_t = jnp.concatenate([v.T.astype(F32) * s_r, jnp.broadcast_to(s_r, (DV, L))], axis=0)
            c_s[bi, h] = jnp.exp(m_prev - m_last) * c_prev + _dot(sv_t.astype(BF16), k)
            m_s[bi, h:h + 1, :] = jnp.broadcast_to(b_last + m_last, (1, m_s.shape[2]))

    for bi in range(nb):
        out = (_dot(yconv_ref[bi], wout_ref[0:CONV_DIM, :])
               + _dot(y_s[bi * L:(bi + 1) * L, :], wout_ref[CONV_DIM:CONV_DIM + MLSTM_DIM, :]))
        hp_ref[bi] = x_ref[bi] + out

    @pl.when(c == pl.num_programs(1) - 1)
    def _():
        lane = lax.broadcasted_iota(jnp.int32, (1, m_s.shape[2]), 1)
        for bi in range(nb):
            acc = jnp.zeros((1, m_s.shape[2]), F32)
            for h in range(HEADS):
                pc_ref[bi, h] = c_s[bi, h, 0:DV, :]
                pn_ref[bi, h:h + 1, :] = c_s[bi, h, DV:DV + 1, :]
                acc = jnp.where(lane == h, m_s[bi, h:h + 1, :], acc)
            pm_ref[bi] = acc


def _p2_call(q, k, v, so, yconv, gates, x, wout, gmh, n_batch, seq_len):
    L = MLSTM_CHUNK
    nb = P2_SEQS
    nc = seq_len // L
    seq = lambda width: pl.BlockSpec((nb, L, width), lambda b, c: (b, c, 0))
    as_seq = lambda a: a.reshape(n_batch, seq_len, a.shape[-1])
    return pl.pallas_call(
        _p2_kernel,
        grid=(n_batch // nb, nc),
        in_specs=[seq(MLSTM_DIM), seq(MLSTM_DIM), seq(MLSTM_DIM), seq(MLSTM_DIM), seq(CONV_DIM),
                  pl.BlockSpec((nb, 4 * HEADS, L), lambda b, c: (b, 0, c)), seq(D_MODEL),
                  _const_spec((D_MODEL, D_MODEL)), _const_spec((1, MLSTM_DIM))],
        out_specs=[seq(D_MODEL),
                   pl.BlockSpec((nb, HEADS, DV, DQK), lambda b, c: (b, 0, 0, 0)),
                   pl.BlockSpec((nb, HEADS, DQK), lambda b, c: (b, 0, 0)),
                   pl.BlockSpec((nb, 1, LANES), lambda b, c: (b, 0, 0))],
        out_shape=[jax.ShapeDtypeStruct((n_batch, seq_len, D_MODEL), F32),
                   jax.ShapeDtypeStruct((n_batch, HEADS, DV, DQK), F32),
                   jax.ShapeDtypeStruct((n_batch, HEADS, DQK), F32),
                   jax.ShapeDtypeStruct((n_batch, 1, LANES), F32)],
        scratch_shapes=[pltpu.VMEM((nb, HEADS, 2 * DV, DQK), F32), pltpu.VMEM((nb, SUBLANES, LANES), F32),
                        pltpu.VMEM((nb * L, MLSTM_DIM), BF16)],
        compiler_params=_params(("arbitrary", "arbitrary")),
        name="p2_mlstm_outproj",
    )(as_seq(q), as_seq(k), as_seq(v), as_seq(so), as_seq(yconv), gates, as_seq(x), wout, gmh)


def _pm_kernel(mem_ref, g_ref, w_ref, k_ref, v_ref, kb_ref, vb_ref):
    xn = _rmsnorm(mem_ref[...], g_ref[...]).astype(BF16)
    kk = _dot(xn, w_ref[:, 0:D_MODEL])
    vv = _dot(xn, w_ref[:, D_MODEL:2 * D_MODEL])
    for h in range(X_HEADS):
        sl = slice(h * X_HEAD_DIM, (h + 1) * X_HEAD_DIM)
        k_ref[:, h, :] = kk[:, sl]
        v_ref[:, h, :] = vv[:, sl]
    kb_ref[...] = kk.astype(BF16)
    vb_ref[...] = vv.astype(BF16)


def _pm_call(mem, g, w):
    rows = mem.shape[0]
    tm = ROW_TILE
    row = pl.BlockSpec((tm, D_MODEL), lambda i: (i, 0))
    row4 = pl.BlockSpec((tm, X_HEADS, X_HEAD_DIM), lambda i: (i, 0, 0))
    return pl.pallas_call(
        _pm_kernel,
        grid=(rows // tm,),
        in_specs=[row, _const_spec((1, D_MODEL)), _const_spec((D_MODEL, 2 * D_MODEL))],
        out_specs=[row4, row4, row, row],
        out_shape=[jax.ShapeDtypeStruct((rows, X_HEADS, X_HEAD_DIM), F32)] * 2
        + [jax.ShapeDtypeStruct((rows, D_MODEL), BF16)] * 2,
        compiler_params=_params(("arbitrary",)),
        name="pm_mem_kv",
    )(mem, g, w)


def _p3_kernel(n_cast, hp_ref, g_ref, wq_ref, k_ref, v_ref, *rest):
    _cast_slabs(rest[:n_cast], rest[n_cast + 1:])
    o_ref = rest[n_cast]
    xn = _rmsnorm(hp_ref[...], g_ref[...]).astype(BF16)
    q = _dot(xn, wq_ref[...])
    for h in range(X_HEADS):
        sl = slice(h * X_HEAD_DIM, (h + 1) * X_HEAD_DIM)
        s = _dot_nt(q[:, sl].astype(BF16), k_ref[0, :, sl]) * (X_HEAD_DIM ** -0.5)
        e = jnp.exp(s - jnp.max(s, axis=1, keepdims=True))
        p = e * (1.0 / jnp.sum(e, axis=1, keepdims=True))
        o_ref[:, sl] = _dot(p.astype(BF16), v_ref[0, :, sl]).astype(BF16)


def _p3_call(hp, g, wq, kb, vb, seq_len, cast=()):
    rows = hp.shape[0]
    tm = P3_TILE
    steps = rows // tm
    tiles_per_batch = seq_len // tm
    row = pl.BlockSpec((tm, D_MODEL), lambda i: (i, 0))
    mem = pl.BlockSpec((1, N_MEM, D_MODEL), lambda i: (i // tiles_per_batch, 0, 0))
    slabs = _slab_specs(cast, steps)
    return pl.pallas_call(
        functools.partial(_p3_kernel, len(cast)),
        grid=(steps,),
        in_specs=[row, _const_spec((1, D_MODEL)), _const_spec((D_MODEL, D_MODEL)), mem, mem] + slabs,
        out_specs=[row] + slabs,
        out_shape=[jax.ShapeDtypeStruct((rows, D_MODEL), BF16)]
        + [jax.ShapeDtypeStruct(wgt.shape, BF16) for wgt in cast],
        compiler_params=_params(("arbitrary",)),
        name="p3_cross_attn",
    )(hp, g, wq, kb, vb, *cast)


def _cache_attention_row(q8, kc_ref, vc_ref, bl):
    qs = q8 * (X_HEAD_DIM ** -0.5)
    m_run = jnp.full((1, PACK_ROWS, 1), -jnp.inf, F32)
    l_run = jnp.zeros((1, PACK_ROWS, 1), F32)
    acc = jnp.zeros((PACK_ROWS, LANES), F32)
    for c in range(N_MEM // MEM_CHUNK):
        blk = slice(c * MEM_CHUNK, (c + 1) * MEM_CHUNK)
        prod = kc_ref[bl, blk] * qs
        s = jnp.sum(prod + pltpu.roll(prod, X_HEADS, 1), axis=-1, keepdims=True)
        m_new = jnp.maximum(m_run, jnp.max(s, axis=0, keepdims=True))
        alpha = jnp.exp(m_run - m_new)
        e = jnp.exp(s - m_new)
        l_run = alpha * l_run + jnp.sum(e, axis=0, keepdims=True)
        acc = alpha[0] * acc + jnp.sum(e * vc_ref[bl, blk], axis=0)
        m_run = m_new
    return acc * (1.0 / l_run[0])


def _swiglu_final(hp, gf_ref, wgu_ref, wd_ref, gfin_ref, act_s):
    xn = _rmsnorm(hp, gf_ref[...]).astype(BF16)
    for j in range(D_FF // FF_CHUNK):
        g = _dot(xn, wgu_ref[:, FF_CHUNK * j:FF_CHUNK * (j + 1)])
        u = _dot(xn, wgu_ref[:, D_FF + FF_CHUNK * j:D_FF + FF_CHUNK * (j + 1)])
        act_s[:, FF_CHUNK * j:FF_CHUNK * (j + 1)] = (g * jax.nn.sigmoid(g) * u).astype(BF16)
    hp = hp + _dot(act_s[...], wd_ref[...])
    return _rmsnorm(hp, gfin_ref[...])


def _p4_kernel(hp_ref, o_ref, wxo_ref, gf_ref, wgu_ref, wd_ref, gfin_ref, q4_ref, kc_ref, vc_ref,
               y_ref, os_ref, act_s):
    for bl in range(q4_ref.shape[0]):
        os_ref[bl] = _cache_attention_row(q4_ref[bl], kc_ref, vc_ref, bl)
    hp = hp_ref[...] + _dot(o_ref[...], wxo_ref[...])
    y_ref[...] = _swiglu_final(hp, gf_ref, wgu_ref, wd_ref, gfin_ref, act_s)


def _p4_call(hp, o, wxo, gf, wgu, wd, gfin, side):
    rows = hp.shape[0]
    tm = ROW_TILE
    steps = rows // tm
    q4, kc, vc = side
    side_rows = q4.shape[0] // steps
    assert side_rows * steps == q4.shape[0]
    row = pl.BlockSpec((tm, D_MODEL), lambda i: (i, 0))
    srow = pl.BlockSpec((side_rows, PACK_ROWS, LANES), lambda i: (i, 0, 0))
    cache = pl.BlockSpec((side_rows, N_MEM, PACK_ROWS, LANES), lambda i: (i, 0, 0, 0))
    return pl.pallas_call(
        _p4_kernel,
        grid=(steps,),
        in_specs=[row, row, _const_spec((D_MODEL, D_MODEL)), _const_spec((1, D_MODEL)),
                  _const_spec((D_MODEL, 2 * D_FF)), _const_spec((D_FF, D_MODEL)),
                  _const_spec((1, D_MODEL)), srow, cache, cache],
        out_specs=[row, srow],
        out_shape=[jax.ShapeDtypeStruct((rows, D_MODEL), F32), jax.ShapeDtypeStruct(q4.shape, F32)],
        scratch_shapes=[pltpu.VMEM((tm, D_FF), BF16)],
        compiler_params=_params(("arbitrary",)),
        name="p4_ffn_final",
    )(hp, o, wxo, gf, wgu, wd, gfin, q4, kc, vc)


def _s1_kernel(x_ref, g_ref, w_ref, b_ref, wg_ref, bg_ref, cw_ref, st_ref,
               yconv_ref, sconv_ref, q_ref, k_ref, v_ref, so_ref, gate_ref, vt_ref, gatet_ref):
    xn = _rmsnorm(x_ref[:, 0, :], g_ref[...]).astype(BF16)

    def seg(j):
        sl = slice(j * CONV_DIM, (j + 1) * CONV_DIM)
        return _dot_nt(xn, w_ref[sl, :].astype(BF16)) + b_ref[:, sl]

    u = seg(1) * seg(2)
    st0 = st_ref[:, 0, :]
    st1 = st_ref[:, 1, :]
    conv = cw_ref[0:1, :] * st0 + cw_ref[1:2, :] * st1 + cw_ref[2:3, :] * u
    yconv_ref[...] = seg(0) * conv
    sconv_ref[:, 0, :] = st1
    sconv_ref[:, 1, :] = u
    q_ref[...] = seg(3)
    k_ref[...] = seg(4) * (DQK ** -0.5)
    v = seg(5)
    v_ref[...] = v
    so_ref[...] = jax.nn.sigmoid(seg(6))
    n_gate = wg_ref.shape[0]
    gate_ref[...] = _gate_transform(_dot_nt(xn, wg_ref[...]) + b_ref[:, MAIN_DIM:MAIN_DIM + n_gate])
    for h in range(HEADS):
        sl = slice(h * DV, (h + 1) * DV)
        vt_ref[sl, :] = v[:, sl].T
    gatet_ref[...] = _gate_transform_rows(_dot_nt(wg_ref[...], xn) + bg_ref[...])


def _s1_call(x, g, w, b, wg, bg, cw, st):
    n = x.shape[0]
    n_gate = wg.shape[0]
    full = lambda *shape: pl.BlockSpec(shape, lambda i: (0,) * len(shape))
    ins = [x, g, w, b, wg, bg, cw, st]
    return pl.pallas_call(
        _s1_kernel,
        grid=(1,),
        in_specs=[full(*a.shape) for a in ins],
        out_specs=[full(n, CONV_DIM), full(*st.shape), full(n, MLSTM_DIM), full(n, MLSTM_DIM),
                   full(n, MLSTM_DIM), full(n, MLSTM_DIM), full(n, n_gate), full(MLSTM_DIM, n),
                   full(n_gate, n)],
        out_shape=[jax.ShapeDtypeStruct((n, CONV_DIM), F32),
                   jax.ShapeDtypeStruct(st.shape, F32)]
        + [jax.ShapeDtypeStruct((n, MLSTM_DIM), F32)] * 4
        + [jax.ShapeDtypeStruct((n, n_gate), F32),
           jax.ShapeDtypeStruct((MLSTM_DIM, n), F32),
           jax.ShapeDtypeStruct((n_gate, n), F32)],
        compiler_params=_params(("arbitrary",)),
        name="s1_inproj_conv",
    )(*ins)


def _s3_kernel(cqt_ref, q_ref, k_ref, v_ref, so_ref, gate_ref, n_ref, m_ref, yconv_ref, x_ref,
               wout_ref, gmh_ref, gx_ref, wq_ref,
               hs_ref, qx_ref, nn_ref, mn_ref, y_s):
    n_rows = q_ref.shape[0]
    y_s[:, 0:CONV_DIM] = yconv_ref[...].astype(BF16)
    lane = lax.broadcasted_iota(jnp.int32, (n_rows, mn_ref.shape[1]), 1)
    m_out = jnp.zeros((n_rows, mn_ref.shape[1]), F32)
    for h in range(HEADS):
        sl = slice(h * DQK, (h + 1) * DQK)
        q = q_ref[:, sl]
        k = k_ref[:, sl]
        v = v_ref[:, sl]
        n_prev = n_ref[:, h, :]
        cq = cqt_ref[sl, :].T
        ig = gate_ref[:, h:h + 1]
        lf = gate_ref[:, HEADS + h:HEADS + h + 1]
        m_prev = m_ref[:, h:h + 1]
        inter = lf + m_prev
        m_row = jnp.maximum(inter, ig)
        wgt = jnp.sum(q * k, axis=1, keepdims=True) * jnp.exp(ig - m_row)
        g = jnp.exp(inter - m_row)
        num = g * cq + wgt * v
        den = g * jnp.sum(n_prev * q, axis=1, keepdims=True) + wgt
        hh = num / jnp.maximum(jnp.abs(den), jnp.exp(-m_row))
        hh = hh * lax.rsqrt(jnp.mean(hh * hh, axis=1, keepdims=True) + EPS) * gmh_ref[:, sl]
        y_s[:, CONV_DIM + h * DV:CONV_DIM + (h + 1) * DV] = (so_ref[:, sl] * hh).astype(BF16)
        nn_ref[:, h, :] = g * n_prev + jnp.exp(ig - m_row) * k
        m_out = jnp.where(lane == h, m_row, m_out)
    mn_ref[...] = m_out
    hs = x_ref[:, 0, :] + _dot(y_s[...], wout_ref[...])
    hs_ref[...] = hs
    qx = _dot(_rmsnorm(hs, gx_ref[...]).astype(BF16), wq_ref[...])
    for h in range(X_HEADS):
        for half in range(X_HEAD_DIM // LANES):
            lo = h * X_HEAD_DIM + half * LANES
            qx_ref[:, half * X_HEADS + h, :] = qx[:, lo:lo + LANES]


def _s3_call(cqt, q, k, v, so, gates, nst, m, yconv, x, wout, gmh, gx, wq):
    n = q.shape[0]
    full = lambda *shape: pl.BlockSpec(shape, lambda i: (0,) * len(shape))
    ins = [cqt, q, k, v, so, gates, nst, m, yconv, x, wout, gmh, gx, wq]
    return pl.pallas_call(
        _s3_kernel,
        grid=(1,),
        in_specs=[full(*a.shape) for a in ins],
        out_specs=[full(n, D_MODEL), full(n, PACK_ROWS, LANES), full(*nst.shape), full(n, LANES)],
        out_shape=[jax.ShapeDtypeStruct((n, D_MODEL), F32), jax.ShapeDtypeStruct((n, PACK_ROWS, LANES), F32),
                   jax.ShapeDtypeStruct(nst.shape, F32), jax.ShapeDtypeStruct((n, LANES), F32)],
        scratch_shapes=[pltpu.VMEM((n, D_MODEL), BF16)],
        compiler_params=_params(("arbitrary",)),
        name="s3_mlstm_finish",
    )(*ins)


def _s5_kernel(hs_ref, o_ref, wxo_ref, gf_ref, wgu_ref, wd_ref, gfin_ref, y_ref, act_s):
    o = jnp.concatenate([o_ref[:, half * X_HEADS + h, :] for h in range(X_HEADS)
                         for half in range(X_HEAD_DIM // LANES)], axis=1)
    hp = hs_ref[...] + _dot(o.astype(BF16), wxo_ref[...])
    y_ref[:, 0, :] = _swiglu_final(hp, gf_ref, wgu_ref, wd_ref, gfin_ref, act_s)


def _s5_call(hs, o, wxo, gf, wgu, wd, gfin):
    n = hs.shape[0]
    full = lambda a: pl.BlockSpec(a.shape, lambda i: (0,) * a.ndim)
    ins = [hs, o, wxo, gf, wgu, wd, gfin]
    return pl.pallas_call(
        _s5_kernel,
        grid=(1,),
        in_specs=[full(a) for a in ins],
        out_specs=pl.BlockSpec((n, 1, D_MODEL), lambda i: (0, 0, 0)),
        out_shape=jax.ShapeDtypeStruct((n, 1, D_MODEL), F32),
        scratch_shapes=[pltpu.VMEM((n, D_FF), BF16)],
        compiler_params=_params(("arbitrary",)),
        name="s5_ffn_final",
    )(*ins)


def _pack_heads(a):
    lead = a.shape[:-2]
    a = a.reshape(lead + (X_HEADS, X_HEAD_DIM // LANES, LANES))
    return jnp.swapaxes(a, -3, -2).reshape(lead + (PACK_ROWS, LANES))


def kernel(x_prompt, x_sample, mem_prompt, state_conv, state_mlstm_C, state_mlstm_n, state_mlstm_m,
           cache_mem_k, cache_mem_v, g_mix, w_in, b_in, conv_w, g_mh, w_out, g_cross, g_mem,
           w_xq, w_xkv, w_xo, g_ffn, w_gu, w_down, g_final):
    n_batch, seq_len, _ = x_prompt.shape
    n_dec = x_sample.shape[0]
    depth = w_in.shape[0]
    assert depth == 1 and x_sample.shape[1] == 1
    assert all(seq_len % t == 0 for t in (ROW_TILE, P1_TILE, P3_TILE))
    assert P1_TILE % MLSTM_CHUNK == 0 and n_batch % P2_SEQS == 0

    n_gate = 2 * HEADS
    w_in_b = w_in[0].T
    b_in_r = b_in[0].reshape(1, MAIN_DIM + n_gate)
    w_gate_r = w_in_b[MAIN_DIM:].astype(BF16)
    b_gate_r = b_in[0, MAIN_DIM:].reshape(n_gate, 1)
    g_mix_r = g_mix[0].reshape(1, D_MODEL)
    g_cross_r = g_cross[0].reshape(1, D_MODEL)
    g_mem_r = g_mem[0].reshape(1, D_MODEL)
    g_ffn_r = g_ffn[0].reshape(1, D_MODEL)
    g_final_r = g_final.reshape(1, D_MODEL)
    g_mh_r = g_mh[0].reshape(1, MLSTM_DIM)
    cw = conv_w[0]

    m0 = state_mlstm_m[0]
    s_yconv, s_conv, sq, sk, sv, sso, sgates, svt, sgates_t = _s1_call(
        x_sample, g_mix_r, w_in_b, b_in_r, w_gate_r, b_gate_r, cw, state_conv[0])

    xp = x_prompt.reshape(n_batch * seq_len, D_MODEL)
    yconv, q, k, v, so, gates, p_conv, s_c, cqt, w_out_b, w_xq_b, w_xkv_b = _p1_call(
        xp, g_mix_r, w_in_b, b_in_r, w_gate_r, b_gate_r, cw, seq_len,
        side=(state_mlstm_C[0], sq, sk, svt, sgates, sgates_t, m0, m0.T),
        cast=(w_out[0], w_xq[0], w_xkv[0]))
    hs1, qx, s_n, s_m = _s3_call(cqt, sq, sk, sv, sso, sgates,
                                 state_mlstm_n[0], m0, s_yconv, x_sample,
                                 w_out_b, g_mh_r, g_cross_r, w_xq_b)
    hp1, p_c, p_n, p_m = _p2_call(q, k, v, so, yconv, gates, xp, w_out_b, g_mh_r, n_batch, seq_len)
    hp1 = hp1.reshape(n_batch * seq_len, D_MODEL)
    pk, pv, pkb, pvb = _pm_call(mem_prompt.reshape(n_batch * N_MEM, D_MODEL), g_mem_r, w_xkv_b)
    o_p, w_xo_b, w_gu_b, w_down_b = _p3_call(
        hp1, g_cross_r, w_xq_b, pkb.reshape(n_batch, N_MEM, D_MODEL), pvb.reshape(n_batch, N_MEM, D_MODEL),
        seq_len, cast=(w_xo[0], w_gu[0], w_down[0]))
    y_p, o_s = _p4_call(hp1, o_p, w_xo_b, g_ffn_r, w_gu_b, w_down_b, g_final_r,
                        side=(qx, _pack_heads(cache_mem_k[0]), _pack_heads(cache_mem_v[0])))

    y_s = _s5_call(hs1, o_s, w_xo_b, g_ffn_r, w_gu_b, w_down_b, g_final_r)

    mem_shape = (1, n_batch, N_MEM, X_HEADS, X_HEAD_DIM)
    return (y_p.reshape(n_batch, seq_len, D_MODEL),
            y_s.reshape(n_dec, 1, D_MODEL),
            p_conv.reshape(1, n_batch, CONV_W - 1, CONV_DIM),
            p_c.reshape(1, n_batch, HEADS, DV, DQK),
            p_n.reshape(1, n_batch, HEADS, DQK),
            p_m[:, 0, :HEADS].reshape(1, n_batch, HEADS),
            pk.reshape(mem_shape),
            pv.reshape(mem_shape),
            s_conv.reshape(1, n_dec, CONV_W - 1, CONV_DIM),
            s_c.reshape(1, n_dec, HEADS, DV, DQK),
            s_n.reshape(1, n_dec, HEADS, DQK),
            s_m[:, :HEADS].reshape(1, n_dec, HEADS))
```

```python
import functools

import jax
import jax.numpy as jnp
from jax import lax
from jax.experimental import pallas as pl
from jax.experimental.pallas import tpu as pltpu

F32 = jnp.float32
BF16 = jnp.bfloat16

D_MODEL = 1024
CONV_DIM = 512
CONV_W = 3
MLSTM_DIM = 512
HEADS = 4
DQK = 128
DV = 128
N_MEM = 256
X_HEADS = 4
X_HEAD_DIM = 256
D_FF = 2816
MAIN_DIM = 3 * CONV_DIM + 4 * MLSTM_DIM
EPS = 1e-6

LANES = 128
SUBLANES = 8
BF16_ROWS = 16

MLSTM_CHUNK = 256
P2_SEQS = 4
ROW_TILE = 512
P1_TILE = 1024
P3_TILE = 1024
FF_CHUNK = 256
PACK_ROWS = X_HEADS * (X_HEAD_DIM // LANES)
MEM_CHUNK = 32
VMEM_LIMIT = 56 * 1024 * 1024


def _dot(a, b):
    return jnp.dot(a, b, preferred_element_type=F32)


def _dot_nt(a, b):
    return lax.dot_general(a, b, (((1,), (1,)), ((), ())), preferred_element_type=F32)


def _rmsnorm(x, g):
    return x * lax.rsqrt(jnp.mean(x * x, axis=-1, keepdims=True) + EPS) * g


def _const_spec(shape):
    zeros = (0,) * len(shape)
    return pl.BlockSpec(shape, lambda *_: zeros, pipeline_mode=pl.Buffered(1))


def _params(sem):
    return pltpu.CompilerParams(dimension_semantics=sem, vmem_limit_bytes=VMEM_LIMIT)


def _gate_transform(gt):
    lane = lax.broadcasted_iota(jnp.int32, gt.shape, 1)
    return jnp.where(lane < HEADS, gt, jax.nn.log_sigmoid(gt))


def _gate_transform_rows(gt):
    sub = lax.broadcasted_iota(jnp.int32, gt.shape, 0)
    return jnp.where(sub < HEADS, gt, jax.nn.log_sigmoid(gt))


def _memory_update_rows(i, c_ref, q_ref, k_ref, vt_ref, gate_ref, gatet_ref, m_ref, mt_ref, cn_ref, cqt_ref):
    n = vt_ref.shape[1]
    bb = c_ref.shape[0]

    @pl.when(i == 0)
    def _():
        cqt_ref[...] = jnp.zeros_like(cqt_ref)

    lane = lax.broadcasted_iota(jnp.int32, (DV, n), 1)
    for h in range(HEADS):
        sl = slice(h * DQK, (h + 1) * DQK)
        ig_c = gate_ref[:, h:h + 1]
        lf_c = gate_ref[:, HEADS + h:HEADS + h + 1]
        m_c = m_ref[:, h:h + 1]
        dec = jnp.broadcast_to(jnp.exp(lf_c + m_c - jnp.maximum(lf_c + m_c, ig_c)), (bb, DQK))
        ig_r = gatet_ref[h:h + 1, :]
        lf_r = gatet_ref[HEADS + h:HEADS + h + 1, :]
        m_r = mt_ref[h:h + 1, :]
        svt = vt_ref[sl, :] * jnp.exp(ig_r - jnp.maximum(lf_r + m_r, ig_r))
        q_t = q_ref[:, sl]
        k_t = k_ref[:, sl]
        cqt = cqt_ref[sl, :]
        for bl in range(bb):
            onehot = lane == i * bb + bl
            c = c_ref[bl, h]
            cq_col = jnp.sum(c * q_t[bl:bl + 1, :], axis=1, keepdims=True)
            sv_col = jnp.sum(jnp.where(onehot, svt, 0.0), axis=1, keepdims=True)
            cn_ref[bl, h] = dec[bl:bl + 1, :] * c + sv_col * k_t[bl:bl + 1, :]
            cqt = jnp.where(onehot, cq_col, cqt)
        cqt_ref[sl, :] = cqt


def _cast_slabs(srcs, dsts):
    for src, dst in zip(srcs, dsts):
        dst[...] = src[...].astype(BF16)


def _p1_kernel(tiles_per_batch, n_cast, x_ref, g_ref, w_ref, b_ref, wg_ref, bg_ref, cw_ref,
               c_ref, sq_ref, sk_ref, svt_ref, sgate_ref, sgatet_ref, sm_ref, smt_ref, *rest):
    cast_in, rest = rest[:n_cast], rest[n_cast:]
    yconv_ref, q_ref, k_ref, v_ref, so_ref, gate_ref, pconv_ref, cn_ref, cqt_ref = rest[:9]
    cast_out, ubuf = rest[9:9 + n_cast], rest[9 + n_cast]
    _cast_slabs(cast_in, cast_out)
    tm = x_ref.shape[0]
    i = pl.program_id(0)
    _memory_update_rows(i, c_ref, sq_ref, sk_ref, svt_ref, sgate_ref, sgatet_ref, sm_ref, smt_ref,
                        cn_ref, cqt_ref)
    xn = _rmsnorm(x_ref[...], g_ref[...]).astype(BF16)

    def seg(j):
        sl = slice(j * CONV_DIM, (j + 1) * CONV_DIM)
        return _dot_nt(xn, w_ref[sl, :].astype(BF16)) + b_ref[:, sl]

    P = SUBLANES
    prev = ubuf[tm:tm + P, :]
    ubuf[0:P, :] = jnp.where(i % tiles_per_batch == 0, jnp.zeros_like(prev), prev)
    ubuf[P:P + tm, :] = seg(1) * seg(2)
    conv = sum(cw_ref[j:j + 1, :] * ubuf[P - (CONV_W - 1) + j:P - (CONV_W - 1) + j + tm, :] for j in range(CONV_W))
    yconv_ref[...] = (seg(0) * conv).astype(BF16)
    pconv_ref[0] = ubuf[tm + P - (CONV_W - 1):tm + P, :]

    q_ref[...] = seg(3).astype(BF16)
    k_ref[...] = (seg(4) * (DQK ** -0.5)).astype(BF16)
    v_ref[...] = seg(5).astype(BF16)
    so_ref[...] = jax.nn.sigmoid(seg(6)).astype(BF16)
    gt = _gate_transform_rows(_dot_nt(wg_ref[...], xn) + bg_ref[...])
    n_gate = gt.shape[0]
    gate_ref[0, 0:n_gate, :] = gt
    L = MLSTM_CHUNK
    n_blk = tm // L
    hi = gt.astype(BF16).astype(F32)
    r1 = gt - hi
    mid = r1.astype(BF16).astype(F32)
    lo = r1 - mid
    terms = jnp.concatenate([t[:, j * L:(j + 1) * L] for t in (hi, mid, lo) for j in range(n_blk)], axis=0)
    tri = (lax.broadcasted_iota(jnp.int32, (L, L), 0) <= lax.broadcasted_iota(jnp.int32, (L, L), 1)).astype(BF16)
    parts = _dot(terms.astype(BF16), tri)
    for j in range(n_blk):
        rows = [parts[(t * n_blk + j) * n_gate:(t * n_blk + j + 1) * n_gate, :] for t in range(3)]
        gate_ref[0, n_gate:2 * n_gate, j * L:(j + 1) * L] = (rows[0] + rows[1]) + rows[2]


def _slab_specs(weights, steps):
    specs = []
    for wgt in weights:
        slab = wgt.shape[0] // steps
        assert slab * steps == wgt.shape[0] and slab % BF16_ROWS == 0
        specs.append(pl.BlockSpec((slab, wgt.shape[1]), lambda i: (i, 0)))
    return specs


def _p1_call(x, g, w, b, wg, bg, cw, seq_len, side, cast=()):
    rows = x.shape[0]
    tm = P1_TILE
    steps = rows // tm
    slabs = _slab_specs(cast, steps)
    tiles_per_batch = seq_len // tm
    n_batch = rows // seq_len
    row = lambda width: pl.BlockSpec((tm, width), lambda i: (i, 0))
    c, sq, sk, svt, sgate, sgatet, sm, smt = side
    n = sq.shape[0]
    sr = n // steps
    assert sr * steps == n and sr % SUBLANES == 0
    full = lambda a: pl.BlockSpec(a.shape, lambda i: (0,) * a.ndim)
    srow = lambda a: pl.BlockSpec((sr,) + a.shape[1:], lambda i: (i,) + (0,) * (a.ndim - 1))
    return pl.pallas_call(
        functools.partial(_p1_kernel, tiles_per_batch, len(cast)),
        grid=(steps,),
        in_specs=[row(D_MODEL), _const_spec((1, D_MODEL)), _const_spec(w.shape),
                  _const_spec(b.shape), _const_spec((2 * HEADS, D_MODEL)),
                  _const_spec((2 * HEADS, 1)), _const_spec((CONV_W, CONV_DIM)),
                  srow(c), srow(sq), srow(sk), full(svt), srow(sgate), full(sgatet), srow(sm), full(smt)]
        + slabs,
        out_specs=[row(CONV_DIM), row(MLSTM_DIM), row(MLSTM_DIM), row(MLSTM_DIM), row(MLSTM_DIM),
                   pl.BlockSpec((1, 4 * HEADS, tm), lambda i: (i // tiles_per_batch, 0, i % tiles_per_batch)),
                   pl.BlockSpec((1, CONV_W - 1, CONV_DIM), lambda i: (i // tiles_per_batch, 0, 0)),
                   srow(c), full(svt)] + slabs,
        out_shape=[jax.ShapeDtypeStruct((rows, CONV_DIM), BF16)]
        + [jax.ShapeDtypeStruct((rows, MLSTM_DIM), BF16)] * 4
        + [jax.ShapeDtypeStruct((n_batch, 4 * HEADS, seq_len), F32),
           jax.ShapeDtypeStruct((n_batch, CONV_W - 1, CONV_DIM), F32),
           jax.ShapeDtypeStruct(c.shape, F32), jax.ShapeDtypeStruct(svt.shape, F32)]
        + [jax.ShapeDtypeStruct(wgt.shape, BF16) for wgt in cast],
        scratch_shapes=[pltpu.VMEM((tm + SUBLANES, CONV_DIM), F32)],
        compiler_params=_params(("arbitrary",)),
        name="p1_inproj_conv",
    )(x, g, w, b, wg, bg, cw, c, sq, sk, svt, sgate, sgatet, sm, smt, *cast)


def _p2_kernel(q_ref, k_ref, v_ref, so_ref, yconv_ref, gate_ref, x_ref, wout_ref, gmh_ref,
               hp_ref, pc_ref, pn_ref, pm_ref, c_s, m_s, y_s):
    nb, L = q_ref.shape[0], q_ref.shape[1]
    c = pl.program_id(1)

    @pl.when(c == 0)
    def _():
        c_s[...] = jnp.zeros_like(c_s)
        m_s[...] = jnp.zeros_like(m_s)

    row = lax.broadcasted_iota(jnp.int32, (L, L), 0)
    col = lax.broadcasted_iota(jnp.int32, (L, L), 1)
    causal = row >= col

    for bi in range(nb):
        gt = gate_ref[bi]
        for h in range(HEADS):
            sl = slice(h * DQK, (h + 1) * DQK)
            q = q_ref[bi, :, sl]
            k = k_ref[bi, :, sl]
            v = v_ref[bi, :, sl]
            lf_r = gt[HEADS + h:HEADS + h + 1, :]
            a_r = gt[h:h + 1, :] - gt[3 * HEADS + h:3 * HEADS + h + 1, :]
            m_prev = jnp.max(m_s[bi, h:h + 1, :], axis=1, keepdims=True)
            c_prev = c_s[bi, h]

            m_c = jnp.maximum(m_prev, jnp.max(jnp.where(causal, a_r, -jnp.inf), axis=1, keepdims=True))
            b_c = jnp.sum(jnp.where(causal, lf_r, 0.0), axis=1, keepdims=True)
            w = _dot_nt(q, k) * jnp.exp(jnp.where(causal, a_r - m_c, -jnp.inf))
            g = jnp.exp(m_prev - m_c)
            qc = _dot_nt(q, c_prev.astype(BF16))
            num = g * qc[:, 0:DV] + _dot(w.astype(BF16), v)
            den = g * qc[:, DV:2 * DV] + jnp.sum(w, axis=1, keepdims=True)
            hh = num / jnp.maximum(jnp.abs(den), jnp.exp(-(b_c + m_c)))
            hh = hh * lax.rsqrt(jnp.mean(hh * hh, axis=1, keepdims=True) + EPS) * gmh_ref[:, sl]
            y_s[bi * L:(bi + 1) * L, h * DV:(h + 1) * DV] = (so_ref[bi, :, sl].astype(F32) * hh).astype(BF16)

            m_last = jnp.maximum(m_prev, jnp.max(a_r, axis=1, keepdims=True))
            b_last = jnp.sum(lf_r, axis=1, keepdims=True)
            s_r = jnp.exp(a_r - m_last)
            sv_t = jnp.concatenate([v.T.astype(F32) * s_r, jnp.broadcast_to(s_r, (DV, L))], axis=0)
            c_s[bi, h] = jnp.exp(m_prev - m_last) * c_prev + _dot(sv_t.astype(BF16), k)
            m_s[bi, h:h + 1, :] = jnp.broadcast_to(b_last + m_last, (1, m_s.shape[2]))

    for bi in range(nb):
        out = (_dot(yconv_ref[bi], wout_ref[0:CONV_DIM, :])
               + _dot(y_s[bi * L:(bi + 1) * L, :], wout_ref[CONV_DIM:CONV_DIM + MLSTM_DIM, :]))
        hp_ref[bi] = x_ref[bi] + out

    @pl.when(c == pl.num_programs(1) - 1)
    def _():
        lane = lax.broadcasted_iota(jnp.int32, (1, m_s.shape[2]), 1)
        for bi in range(nb):
            acc = jnp.zeros((1, m_s.shape[2]), F32)
            for h in range(HEADS):
                pc_ref[bi, h] = c_s[bi, h, 0:DV, :]
                pn_ref[bi, h:h + 1, :] = c_s[bi, h, DV:DV + 1, :]
                acc = jnp.where(lane == h, m_s[bi, h:h + 1, :], acc)
            pm_ref[bi] = acc


def _p2_call(q, k, v, so, yconv, gates, x, wout, gmh, n_batch, seq_len):
    L = MLSTM_CHUNK
    nb = P2_SEQS
    nc = seq_len // L
    seq = lambda width: pl.BlockSpec((nb, L, width), lambda b, c: (b, c, 0))
    as_seq = lambda a: a.reshape(n_batch, seq_len, a.shape[-1])
    return pl.pallas_call(
        _p2_kernel,
        grid=(n_batch // nb, nc),
        in_specs=[seq(MLSTM_DIM), seq(MLSTM_DIM), seq(MLSTM_DIM), seq(MLSTM_DIM), seq(CONV_DIM),
                  pl.BlockSpec((nb, 4 * HEADS, L), lambda b, c: (b, 0, c)), seq(D_MODEL),
                  _const_spec((D_MODEL, D_MODEL)), _const_spec((1, MLSTM_DIM))],
        out_specs=[seq(D_MODEL),
                   pl.BlockSpec((nb, HEADS, DV, DQK), lambda b, c: (b, 0, 0, 0)),
                   pl.BlockSpec((nb, HEADS, DQK), lambda b, c: (b, 0, 0)),
                   pl.BlockSpec((nb, 1, LANES), lambda b, c: (b, 0, 0))],
        out_shape=[jax.ShapeDtypeStruct((n_batch, seq_len, D_MODEL), F32),
                   jax.ShapeDtypeStruct((n_batch, HEADS, DV, DQK), F32),
                   jax.ShapeDtypeStruct((n_batch, HEADS, DQK), F32),
                   jax.ShapeDtypeStruct((n_batch, 1, LANES), F32)],
        scratch_shapes=[pltpu.VMEM((nb, HEADS, 2 * DV, DQK), F32), pltpu.VMEM((nb, SUBLANES, LANES), F32),
                        pltpu.VMEM((nb * L, MLSTM_DIM), BF16)],
        compiler_params=_params(("arbitrary", "arbitrary")),
        name="p2_mlstm_outproj",
    )(as_seq(q), as_seq(k), as_seq(v), as_seq(so), as_seq(yconv), gates, as_seq(x), wout, gmh)


def _pm_kernel(mem_ref, g_ref, w_ref, k_ref, v_ref, kb_ref, vb_ref):
    xn = _rmsnorm(mem_ref[...], g_ref[...]).astype(BF16)
    kk = _dot(xn, w_ref[:, 0:D_MODEL])
    vv = _dot(xn, w_ref[:, D_MODEL:2 * D_MODEL])
    for h in range(X_HEADS):
        sl = slice(h * X_HEAD_DIM, (h + 1) * X_HEAD_DIM)
        k_ref[:, h, :] = kk[:, sl]
        v_ref[:, h, :] = vv[:, sl]
    kb_ref[...] = kk.astype(BF16)
    vb_ref[...] = vv.astype(BF16)


def _pm_call(mem, g, w):
    rows = mem.shape[0]
    tm = ROW_TILE
    row = pl.BlockSpec((tm, D_MODEL), lambda i: (i, 0))
    row4 = pl.BlockSpec((tm, X_HEADS, X_HEAD_DIM), lambda i: (i, 0, 0))
    return pl.pallas_call(
        _pm_kernel,
        grid=(rows // tm,),
        in_specs=[row, _const_spec((1, D_MODEL)), _const_spec((D_MODEL, 2 * D_MODEL))],
        out_specs=[row4, row4, row, row],
        out_shape=[jax.ShapeDtypeStruct((rows, X_HEADS, X_HEAD_DIM), F32)] * 2
        + [jax.ShapeDtypeStruct((rows, D_MODEL), BF16)] * 2,
        compiler_params=_params(("arbitrary",)),
        name="pm_mem_kv",
    )(mem, g, w)


def _p3_kernel(n_cast, hp_ref, g_ref, wq_ref, k_ref, v_ref, *rest):
    _cast_slabs(rest[:n_cast], rest[n_cast + 1:])
    o_ref = rest[n_cast]
    xn = _rmsnorm(hp_ref[...], g_ref[...]).astype(BF16)
    q = _dot(xn, wq_ref[...])
    for h in range(X_HEADS):
        sl = slice(h * X_HEAD_DIM, (h + 1) * X_HEAD_DIM)
        s = _dot_nt(q[:, sl].astype(BF16), k_ref[0, :, sl]) * (X_HEAD_DIM ** -0.5)
        e = jnp.exp(s - jnp.max(s, axis=1, keepdims=True))
        o_h = _dot(e.astype(BF16), v_ref[0, :, sl]) * (1.0 / jnp.sum(e, axis=1, keepdims=True))
        o_ref[:, sl] = o_h.astype(BF16)


def _p3_call(hp, g, wq, kb, vb, seq_len, cast=()):
    rows = hp.shape[0]
    tm = P3_TILE
    steps = rows // tm
    tiles_per_batch = seq_len // tm
    row = pl.BlockSpec((tm, D_MODEL), lambda i: (i, 0))
    mem = pl.BlockSpec((1, N_MEM, D_MODEL), lambda i: (i // tiles_per_batch, 0, 0))
    slabs = _slab_specs(cast, steps)
    return pl.pallas_call(
        functools.partial(_p3_kernel, len(cast)),
        grid=(steps,),
        in_specs=[row, _const_spec((1, D_MODEL)), _const_spec((D_MODEL, D_MODEL)), mem, mem] + slabs,
        out_specs=[row] + slabs,
        out_shape=[jax.ShapeDtypeStruct((rows, D_MODEL), BF16)]
        + [jax.ShapeDtypeStruct(wgt.shape, BF16) for wgt in cast],
        compiler_params=_params(("arbitrary",)),
        name="p3_cross_attn",
    )(hp, g, wq, kb, vb, *cast)


def _cache_attention_row(q8, kc_ref, vc_ref, bl):
    qs = q8 * (X_HEAD_DIM ** -0.5)
    m_run = jnp.full((1, PACK_ROWS, 1), -jnp.inf, F32)
    l_run = jnp.zeros((1, PACK_ROWS, 1), F32)
    acc = jnp.zeros((PACK_ROWS, LANES), F32)
    for c in range(N_MEM // MEM_CHUNK):
        blk = slice(c * MEM_CHUNK, (c + 1) * MEM_CHUNK)
        prod = kc_ref[bl, blk] * qs
        s = jnp.sum(prod + pltpu.roll(prod, X_HEADS, 1), axis=-1, keepdims=True)
        m_new = jnp.maximum(m_run, jnp.max(s, axis=0, keepdims=True))
        alpha = jnp.exp(m_run - m_new)
        e = jnp.exp(s - m_new)
        l_run = alpha * l_run + jnp.sum(e, axis=0, keepdims=True)
        acc = alpha[0] * acc + jnp.sum(e * vc_ref[bl, blk], axis=0)
        m_run = m_new
    return acc * (1.0 / l_run[0])


def _swiglu_final(hp, gf_ref, wgu_ref, wd_ref, gfin_ref, act_s):
    xn = _rmsnorm(hp, gf_ref[...]).astype(BF16)
    for j in range(D_FF // FF_CHUNK):
        g = _dot(xn, wgu_ref[:, FF_CHUNK * j:FF_CHUNK * (j + 1)])
        u = _dot(xn, wgu_ref[:, D_FF + FF_CHUNK * j:D_FF + FF_CHUNK * (j + 1)])
        act_s[:, FF_CHUNK * j:FF_CHUNK * (j + 1)] = (g * jax.nn.sigmoid(g) * u).astype(BF16)
    hp = hp + _dot(act_s[...], wd_ref[...])
    return _rmsnorm(hp, gfin_ref[...])


def _p4_kernel(hp_ref, o_ref, wxo_ref, gf_ref, wgu_ref, wd_ref, gfin_ref, q4_ref, kc_ref, vc_ref,
               y_ref, os_ref, act_s):
    for bl in range(q4_ref.shape[0]):
        os_ref[bl] = _cache_attention_row(q4_ref[bl], kc_ref, vc_ref, bl)
    hp = hp_ref[...] + _dot(o_ref[...], wxo_ref[...])
    y_ref[...] = _swiglu_final(hp, gf_ref, wgu_ref, wd_ref, gfin_ref, act_s)


def _p4_call(hp, o, wxo, gf, wgu, wd, gfin, side):
    rows = hp.shape[0]
    tm = ROW_TILE
    steps = rows // tm
    q4, kc, vc = side
    side_rows = q4.shape[0] // steps
    assert side_rows * steps == q4.shape[0]
    row = pl.BlockSpec((tm, D_MODEL), lambda i: (i, 0))
    srow = pl.BlockSpec((side_rows, PACK_ROWS, LANES), lambda i: (i, 0, 0))
    cache = pl.BlockSpec((side_rows, N_MEM, PACK_ROWS, LANES), lambda i: (i, 0, 0, 0))
    return pl.pallas_call(
        _p4_kernel,
        grid=(steps,),
        in_specs=[row, row, _const_spec((D_MODEL, D_MODEL)), _const_spec((1, D_MODEL)),
                  _const_spec((D_MODEL, 2 * D_FF)), _const_spec((D_FF, D_MODEL)),
                  _const_spec((1, D_MODEL)), srow, cache, cache],
        out_specs=[row, srow],
        out_shape=[jax.ShapeDtypeStruct((rows, D_MODEL), F32), jax.ShapeDtypeStruct(q4.shape, F32)],
        scratch_shapes=[pltpu.VMEM((tm, D_FF), BF16)],
        compiler_params=_params(("arbitrary",)),
        name="p4_ffn_final",
    )(hp, o, wxo, gf, wgu, wd, gfin, q4, kc, vc)


def _s1_kernel(x_ref, g_ref, w_ref, b_ref, wg_ref, bg_ref, cw_ref, st_ref,
               yconv_ref, sconv_ref, q_ref, k_ref, v_ref, so_ref, gate_ref, vt_ref, gatet_ref):
    xn = _rmsnorm(x_ref[:, 0, :], g_ref[...]).astype(BF16)

    def seg(j):
        sl = slice(j * CONV_DIM, (j + 1) * CONV_DIM)
        return _dot_nt(xn, w_ref[sl, :].astype(BF16)) + b_ref[:, sl]

    u = seg(1) * seg(2)
    st0 = st_ref[:, 0, :]
    st1 = st_ref[:, 1, :]
    conv = cw_ref[0:1, :] * st0 + cw_ref[1:2, :] * st1 + cw_ref[2:3, :] * u
    yconv_ref[...] = seg(0) * conv
    sconv_ref[:, 0, :] = st1
    sconv_ref[:, 1, :] = u
    q_ref[...] = seg(3)
    k_ref[...] = seg(4) * (DQK ** -0.5)
    v = seg(5)
    v_ref[...] = v
    so_ref[...] = jax.nn.sigmoid(seg(6))
    n_gate = wg_ref.shape[0]
    gate_ref[...] = _gate_transform(_dot_nt(xn, wg_ref[...]) + b_ref[:, MAIN_DIM:MAIN_DIM + n_gate])
    for h in range(HEADS):
        sl = slice(h * DV, (h + 1) * DV)
        vt_ref[sl, :] = v[:, sl].T
    gatet_ref[...] = _gate_transform_rows(_dot_nt(wg_ref[...], xn) + bg_ref[...])


def _s1_call(x, g, w, b, wg, bg, cw, st):
    n = x.shape[0]
    n_gate = wg.shape[0]
    full = lambda *shape: pl.BlockSpec(shape, lambda i: (0,) * len(shape))
    ins = [x, g, w, b, wg, bg, cw, st]
    return pl.pallas_call(
        _s1_kernel,
        grid=(1,),
        in_specs=[full(*a.shape) for a in ins],
        out_specs=[full(n, CONV_DIM), full(*st.shape), full(n, MLSTM_DIM), full(n, MLSTM_DIM),
                   full(n, MLSTM_DIM), full(n, MLSTM_DIM), full(n, n_gate), full(MLSTM_DIM, n),
                   full(n_gate, n)],
        out_shape=[jax.ShapeDtypeStruct((n, CONV_DIM), F32),
                   jax.ShapeDtypeStruct(st.shape, F32)]
        + [jax.ShapeDtypeStruct((n, MLSTM_DIM), F32)] * 4
        + [jax.ShapeDtypeStruct((n, n_gate), F32),
           jax.ShapeDtypeStruct((MLSTM_DIM, n), F32),
           jax.ShapeDtypeStruct((n_gate, n), F32)],
        compiler_params=_params(("arbitrary",)),
        name="s1_inproj_conv",
    )(*ins)


def _s3_kernel(cqt_ref, q_ref, k_ref, v_ref, so_ref, gate_ref, n_ref, m_ref, yconv_ref, x_ref,
               wout_ref, gmh_ref, gx_ref, wq_ref,
               hs_ref, qx_ref, nn_ref, mn_ref, y_s):
    n_rows = q_ref.shape[0]
    y_s[:, 0:CONV_DIM] = yconv_ref[...].astype(BF16)
    lane = lax.broadcasted_iota(jnp.int32, (n_rows, mn_ref.shape[1]), 1)
    m_out = jnp.zeros((n_rows, mn_ref.shape[1]), F32)
    for h in range(HEADS):
        sl = slice(h * DQK, (h + 1) * DQK)
        q = q_ref[:, sl]
        k = k_ref[:, sl]
        v = v_ref[:, sl]
        n_prev = n_ref[:, h, :]
        cq = cqt_ref[sl, :].T
        ig = gate_ref[:, h:h + 1]
        lf = gate_ref[:, HEADS + h:HEADS + h + 1]
        m_prev = m_ref[:, h:h + 1]
        inter = lf + m_prev
        m_row = jnp.maximum(inter, ig)
        wgt = jnp.sum(q * k, axis=1, keepdims=True) * jnp.exp(ig - m_row)
        g = jnp.exp(inter - m_row)
        num = g * cq + wgt * v
        den = g * jnp.sum(n_prev * q, axis=1, keepdims=True) + wgt
        hh = num / jnp.maximum(jnp.abs(den), jnp.exp(-m_row))
        hh = hh * lax.rsqrt(jnp.mean(hh * hh, axis=1, keepdims=True) + EPS) * gmh_ref[:, sl]
        y_s[:, CONV_DIM + h * DV:CONV_DIM + (h + 1) * DV] = (so_ref[:, sl] * hh).astype(BF16)
        nn_ref[:, h, :] = g * n_prev + jnp.exp(ig - m_row) * k
        m_out = jnp.where(lane == h, m_row, m_out)
    mn_ref[...] = m_out
    hs = x_ref[:, 0, :] + _dot(y_s[...], wout_ref[...])
    hs_ref[...] = hs
    qx = _dot(_rmsnorm(hs, gx_ref[...]).astype(BF16), wq_ref[...])
    for h in range(X_HEADS):
        for half in range(X_HEAD_DIM // LANES):
            lo = h * X_HEAD_DIM + half * LANES
            qx_ref[:, half * X_HEADS + h, :] = qx[:, lo:lo + LANES]


def _s3_call(cqt, q, k, v, so, gates, nst, m, yconv, x, wout, gmh, gx, wq):
    n = q.shape[0]
    full = lambda *shape: pl.BlockSpec(shape, lambda i: (0,) * len(shape))
    ins = [cqt, q, k, v, so, gates, nst, m, yconv, x, wout, gmh, gx, wq]
    return pl.pallas_call(
        _s3_kernel,
        grid=(1,),
        in_specs=[full(*a.shape) for a in ins],
        out_specs=[full(n, D_MODEL), full(n, PACK_ROWS, LANES), full(*nst.shape), full(n, LANES)],
        out_shape=[jax.ShapeDtypeStruct((n, D_MODEL), F32), jax.ShapeDtypeStruct((n, PACK_ROWS, LANES), F32),
                   jax.ShapeDtypeStruct(nst.shape, F32), jax.ShapeDtypeStruct((n, LANES), F32)],
        scratch_shapes=[pltpu.VMEM((n, D_MODEL), BF16)],
        compiler_params=_params(("arbitrary",)),
        name="s3_mlstm_finish",
    )(*ins)


def _s5_kernel(hs_ref, o_ref, wxo_ref, gf_ref, wgu_ref, wd_ref, gfin_ref, y_ref, act_s):
    o = jnp.concatenate([o_ref[:, half * X_HEADS + h, :] for h in range(X_HEADS)
                         for half in range(X_HEAD_DIM // LANES)], axis=1)
    hp = hs_ref[...] + _dot(o.astype(BF16), wxo_ref[...])
    y_ref[:, 0, :] = _swiglu_final(hp, gf_ref, wgu_ref, wd_ref, gfin_ref, act_s)


def _s5_call(hs, o, wxo, gf, wgu, wd, gfin):
    n = hs.shape[0]
    full = lambda a: pl.BlockSpec(a.shape, lambda i: (0,) * a.ndim)
    ins = [hs, o, wxo, gf, wgu, wd, gfin]
    return pl.pallas_call(
        _s5_kernel,
        grid=(1,),
        in_specs=[full(a) for a in ins],
        out_specs=pl.BlockSpec((n, 1, D_MODEL), lambda i: (0, 0, 0)),
        out_shape=jax.ShapeDtypeStruct((n, 1, D_MODEL), F32),
        scratch_shapes=[pltpu.VMEM((n, D_FF), BF16)],
        compiler_params=_params(("arbitrary",)),
        name="s5_ffn_final",
    )(*ins)


def _pack_heads(a):
    lead = a.shape[:-2]
    a = a.reshape(lead + (X_HEADS, X_HEAD_DIM // LANES, LANES))
    return jnp.swapaxes(a, -3, -2).reshape(lead + (PACK_ROWS, LANES))


def kernel(x_prompt, x_sample, mem_prompt, state_conv, state_mlstm_C, state_mlstm_n, state_mlstm_m,
           cache_mem_k, cache_mem_v, g_mix, w_in, b_in, conv_w, g_mh, w_out, g_cross, g_mem,
           w_xq, w_xkv, w_xo, g_ffn, w_gu, w_down, g_final):
    n_batch, seq_len, _ = x_prompt.shape
    n_dec = x_sample.shape[0]
    depth = w_in.shape[0]
    assert depth == 1 and x_sample.shape[1] == 1
    assert all(seq_len % t == 0 for t in (ROW_TILE, P1_TILE, P3_TILE))
    assert P1_TILE % MLSTM_CHUNK == 0 and n_batch % P2_SEQS == 0

    n_gate = 2 * HEADS
    w_in_b = w_in[0].T
    b_in_r = b_in[0].reshape(1, MAIN_DIM + n_gate)
    w_gate_r = w_in_b[MAIN_DIM:].astype(BF16)
    b_gate_r = b_in[0, MAIN_DIM:].reshape(n_gate, 1)
    g_mix_r = g_mix[0].reshape(1, D_MODEL)
    g_cross_r = g_cross[0].reshape(1, D_MODEL)
    g_mem_r = g_mem[0].reshape(1, D_MODEL)
    g_ffn_r = g_ffn[0].reshape(1, D_MODEL)
    g_final_r = g_final.reshape(1, D_MODEL)
    g_mh_r = g_mh[0].reshape(1, MLSTM_DIM)
    cw = conv_w[0]

    m0 = state_mlstm_m[0]
    s_yconv, s_conv, sq, sk, sv, sso, sgates, svt, sgates_t = _s1_call(
        x_sample, g_mix_r, w_in_b, b_in_r, w_gate_r, b_gate_r, cw, state_conv[0])

    xp = x_prompt.reshape(n_batch * seq_len, D_MODEL)
    yconv, q, k, v, so, gates, p_conv, s_c, cqt, w_out_b, w_xq_b, w_xkv_b = _p1_call(
        xp, g_mix_r, w_in_b, b_in_r, w_gate_r, b_gate_r, cw, seq_len,
        side=(state_mlstm_C[0], sq, sk, svt, sgates, sgates_t, m0, m0.T),
        cast=(w_out[0], w_xq[0], w_xkv[0]))
    hs1, qx, s_n, s_m = _s3_call(cqt, sq, sk, sv, sso, sgates,
                                 state_mlstm_n[0], m0, s_yconv, x_sample,
                                 w_out_b, g_mh_r, g_cross_r, w_xq_b)
    hp1, p_c, p_n, p_m = _p2_call(q, k, v, so, yconv, gates, xp, w_out_b, g_mh_r, n_batch, seq_len)
    hp1 = hp1.reshape(n_batch * seq_len, D_MODEL)
    pk, pv, pkb, pvb = _pm_call(mem_prompt.reshape(n_batch * N_MEM, D_MODEL), g_mem_r, w_xkv_b)
    o_p, w_xo_b, w_gu_b, w_down_b = _p3_call(
        hp1, g_cross_r, w_xq_b, pkb.reshape(n_batch, N_MEM, D_MODEL), pvb.reshape(n_batch, N_MEM, D_MODEL),
        seq_len, cast=(w_xo[0], w_gu[0], w_down[0]))
    y_p, o_s = _p4_call(hp1, o_p, w_xo_b, g_ffn_r, w_gu_b, w_down_b, g_final_r,
                        side=(qx, _pack_heads(cache_mem_k[0]), _pack_heads(cache_mem_v[0])))

    y_s = _s5_call(hs1, o_s, w_xo_b, g_ffn_r, w_gu_b, w_down_b, g_final_r)

    mem_shape = (1, n_batch, N_MEM, X_HEADS, X_HEAD_DIM)
    return (y_p.reshape(n_batch, seq_len, D_MODEL),
            y_s.reshape(n_dec, 1, D_MODEL),
            p_conv.reshape(1, n_batch, CONV_W - 1, CONV_DIM),
            p_c.reshape(1, n_batch, HEADS, DV, DQK),
            p_n.reshape(1, n_batch, HEADS, DQK),
            p_m[:, 0, :HEADS].reshape(1, n_batch, HEADS),
            pk.reshape(mem_shape),
            pv.reshape(mem_shape),
            s_conv.reshape(1, n_dec, CONV_W - 1, CONV_DIM),
            s_c.reshape(1, n_dec, HEADS, DV, DQK),
            s_n.reshape(1, n_dec, HEADS, DQK),
            s_m[:, :HEADS].reshape(1, n_dec, HEADS))
```

```python
import functools

import jax
import jax.numpy as jnp
from jax import lax
from jax.experimental import pallas as pl
from jax.experimental.pallas import tpu as pltpu

F32 = jnp.float32
BF16 = jnp.bfloat16

D_MODEL = 1024
CONV_DIM = 512
CONV_W = 3
MLSTM_DIM = 512
HEADS = 4
DQK = 128
DV = 128
N_MEM = 256
X_HEADS = 4
X_HEAD_DIM = 256
D_FF = 2816
MAIN_DIM = 3 * CONV_DIM + 4 * MLSTM_DIM
EPS = 1e-6

LANES = 128
SUBLANES = 8
BF16_ROWS = 16

MLSTM_CHUNK = 256
P2_SEQS = 4
ROW_TILE = 512
P1_TILE = 1024
P3_TILE = 1024
FF_CHUNK = 256
PACK_ROWS = X_HEADS * (X_HEAD_DIM // LANES)
MEM_CHUNK = 32
VMEM_LIMIT = 56 * 1024 * 1024


def _dot(a, b):
    return jnp.dot(a, b, preferred_element_type=F32)


def _dot_nt(a, b):
    return lax.dot_general(a, b, (((1,), (1,)), ((), ())), preferred_element_type=F32)


def _rmsnorm(x, g):
    return x * lax.rsqrt(jnp.mean(x * x, axis=-1, keepdims=True) + EPS) * g


def _const_spec(shape):
    zeros = (0,) * len(shape)
    return pl.BlockSpec(shape, lambda *_: zeros, pipeline_mode=pl.Buffered(1))


def _params(sem):
    return pltpu.CompilerParams(dimension_semantics=sem, vmem_limit_bytes=VMEM_LIMIT)


def _gate_transform(gt):
    lane = lax.broadcasted_iota(jnp.int32, gt.shape, 1)
    return jnp.where(lane < HEADS, gt, jax.nn.log_sigmoid(gt))


def _gate_transform_rows(gt):
    sub = lax.broadcasted_iota(jnp.int32, gt.shape, 0)
    return jnp.where(sub < HEADS, gt, jax.nn.log_sigmoid(gt))


def _memory_update_rows(i, c_ref, q_ref, k_ref, vt_ref, gate_ref, gatet_ref, m_ref, mt_ref, cn_ref, cqt_ref):
    n = vt_ref.shape[1]
    bb = c_ref.shape[0]

    @pl.when(i == 0)
    def _():
        cqt_ref[...] = jnp.zeros_like(cqt_ref)

    lane = lax.broadcasted_iota(jnp.int32, (DV, n), 1)
    for h in range(HEADS):
        sl = slice(h * DQK, (h + 1) * DQK)
        ig_c = gate_ref[:, h:h + 1]
        lf_c = gate_ref[:, HEADS + h:HEADS + h + 1]
        m_c = m_ref[:, h:h + 1]
        dec = jnp.broadcast_to(jnp.exp(lf_c + m_c - jnp.maximum(lf_c + m_c, ig_c)), (bb, DQK))
        ig_r = gatet_ref[h:h + 1, :]
        lf_r = gatet_ref[HEADS + h:HEADS + h + 1, :]
        m_r = mt_ref[h:h + 1, :]
        svt = vt_ref[sl, :] * jnp.exp(ig_r - jnp.maximum(lf_r + m_r, ig_r))
        q_t = q_ref[:, sl]
        k_t = k_ref[:, sl]
        cqt = cqt_ref[sl, :]
        for bl in range(bb):
            onehot = lane == i * bb + bl
            c = c_ref[bl, h]
            cq_col = jnp.sum(c * q_t[bl:bl + 1, :], axis=1, keepdims=True)
            sv_col = jnp.sum(jnp.where(onehot, svt, 0.0), axis=1, keepdims=True)
            cn_ref[bl, h] = dec[bl:bl + 1, :] * c + sv_col * k_t[bl:bl + 1, :]
            cqt = jnp.where(onehot, cq_col, cqt)
        cqt_ref[sl, :] = cqt


def _cast_slabs(srcs, dsts):
    for src, dst in zip(srcs, dsts):
        dst[...] = src[...].astype(BF16)


def _p1_kernel(tiles_per_batch, n_cast, x_ref, g_ref, w_ref, b_ref, wg_ref, bg_ref, cw_ref,
               c_ref, sq_ref, sk_ref, svt_ref, sgate_ref, sgatet_ref, sm_ref, smt_ref, *rest):
    cast_in, rest = rest[:n_cast], rest[n_cast:]
    yconv_ref, q_ref, k_ref, v_ref, so_ref, gate_ref, pconv_ref, cn_ref, cqt_ref = rest[:9]
    cast_out, ubuf = rest[9:9 + n_cast], rest[9 + n_cast]
    _cast_slabs(cast_in, cast_out)
    tm = x_ref.shape[0]
    i = pl.program_id(0)
    _memory_update_rows(i, c_ref, sq_ref, sk_ref, svt_ref, sgate_ref, sgatet_ref, sm_ref, smt_ref,
                        cn_ref, cqt_ref)
    xn = _rmsnorm(x_ref[...], g_ref[...]).astype(BF16)

    def seg(j):
        sl = slice(j * CONV_DIM, (j + 1) * CONV_DIM)
        return _dot_nt(xn, w_ref[sl, :].astype(BF16)) + b_ref[:, sl]

    P = SUBLANES
    prev = ubuf[tm:tm + P, :]
    ubuf[0:P, :] = jnp.where(i % tiles_per_batch == 0, jnp.zeros_like(prev), prev)
    ubuf[P:P + tm, :] = seg(1) * seg(2)
    conv = sum(cw_ref[j:j + 1, :] * ubuf[P - (CONV_W - 1) + j:P - (CONV_W - 1) + j + tm, :] for j in range(CONV_W))
    yconv_ref[...] = (seg(0) * conv).astype(BF16)
    pconv_ref[0] = ubuf[tm + P - (CONV_W - 1):tm + P, :]

    q_ref[...] = seg(3).astype(BF16)
    k_ref[...] = (seg(4) * (DQK ** -0.5)).astype(BF16)
    v_ref[...] = seg(5).astype(BF16)
    so_ref[...] = jax.nn.sigmoid(seg(6)).astype(BF16)
    gt = _gate_transform_rows(_dot_nt(wg_ref[...], xn) + bg_ref[...])
    n_gate = gt.shape[0]
    gate_ref[0, 0:n_gate, :] = gt
    L = MLSTM_CHUNK
    n_blk = tm // L
    hi = gt.astype(BF16).astype(F32)
    r1 = gt - hi
    mid = r1.astype(BF16).astype(F32)
    lo = r1 - mid
    terms = jnp.concatenate([t[:, j * L:(j + 1) * L] for t in (hi, mid, lo) for j in range(n_blk)], axis=0)
    tri = (lax.broadcasted_iota(jnp.int32, (L, L), 0) <= lax.broadcasted_iota(jnp.int32, (L, L), 1)).astype(BF16)
    parts = _dot(terms.astype(BF16), tri)
    for j in range(n_blk):
        rows = [parts[(t * n_blk + j) * n_gate:(t * n_blk + j + 1) * n_gate, :] for t in range(3)]
        gate_ref[0, n_gate:2 * n_gate, j * L:(j + 1) * L] = (rows[0] + rows[1]) + rows[2]


def _slab_specs(weights, steps):
    specs = []
    for wgt in weights:
        slab = wgt.shape[0] // steps
        assert slab * steps == wgt.shape[0] and slab % BF16_ROWS == 0
        specs.append(pl.BlockSpec((slab, wgt.shape[1]), lambda i: (i, 0)))
    return specs


def _p1_call(x, g, w, b, wg, bg, cw, seq_len, side, cast=()):
    rows = x.shape[0]
    tm = P1_TILE
    steps = rows // tm
    slabs = _slab_specs(cast, steps)
    tiles_per_batch = seq_len // tm
    n_batch = rows // seq_len
    row = lambda width: pl.BlockSpec((tm, width), lambda i: (i, 0))
    c, sq, sk, svt, sgate, sgatet, sm, smt = side
    n = sq.shape[0]
    sr = n // steps
    assert sr * steps == n and sr % SUBLANES == 0
    full = lambda a: pl.BlockSpec(a.shape, lambda i: (0,) * a.ndim)
    srow = lambda a: pl.BlockSpec((sr,) + a.shape[1:], lambda i: (i,) + (0,) * (a.ndim - 1))
    return pl.pallas_call(
        functools.partial(_p1_kernel, tiles_per_batch, len(cast)),
        grid=(steps,),
        in_specs=[row(D_MODEL), _const_spec((1, D_MODEL)), _const_spec(w.shape),
                  _const_spec(b.shape), _const_spec((2 * HEADS, D_MODEL)),
                  _const_spec((2 * HEADS, 1)), _const_spec((CONV_W, CONV_DIM)),
                  srow(c), srow(sq), srow(sk), full(svt), srow(sgate), full(sgatet), srow(sm), full(smt)]
        + slabs,
        out_specs=[row(CONV_DIM), row(MLSTM_DIM), row(MLSTM_DIM), row(MLSTM_DIM), row(MLSTM_DIM),
                   pl.BlockSpec((1, 4 * HEADS, tm), lambda i: (i // tiles_per_batch, 0, i % tiles_per_batch)),
                   pl.BlockSpec((1, CONV_W - 1, CONV_DIM), lambda i: (i // tiles_per_batch, 0, 0)),
                   srow(c), full(svt)] + slabs,
        out_shape=[jax.ShapeDtypeStruct((rows, CONV_DIM), BF16)]
        + [jax.ShapeDtypeStruct((rows, MLSTM_DIM), BF16)] * 4
        + [jax.ShapeDtypeStruct((n_batch, 4 * HEADS, seq_len), F32),
           jax.ShapeDtypeStruct((n_batch, CONV_W - 1, CONV_DIM), F32),
           jax.ShapeDtypeStruct(c.shape, F32), jax.ShapeDtypeStruct(svt.shape, F32)]
        + [jax.ShapeDtypeStruct(wgt.shape, BF16) for wgt in cast],
        scratch_shapes=[pltpu.VMEM((tm + SUBLANES, CONV_DIM), F32)],
        compiler_params=_params(("arbitrary",)),
        name="p1_inproj_conv",
    )(x, g, w, b, wg, bg, cw, c, sq, sk, svt, sgate, sgatet, sm, smt, *cast)


def _p2_kernel(q_ref, k_ref, v_ref, so_ref, yconv_ref, gate_ref, x_ref, wout_ref, gmh_ref,
               hp_ref, pc_ref, pn_ref, pm_ref, c_s, m_s, y_s):
    nb, L = q_ref.shape[0], q_ref.shape[1]
    c = pl.program_id(1)

    @pl.when(c == 0)
    def _():
        c_s[...] = jnp.zeros_like(c_s)
        m_s[...] = jnp.zeros_like(m_s)

    row = lax.broadcasted_iota(jnp.int32, (L, L), 0)
    col = lax.broadcasted_iota(jnp.int32, (L, L), 1)
    causal = row >= col

    for bi in range(nb):
        gt = gate_ref[bi]
        for h in range(HEADS):
            sl = slice(h * DQK, (h + 1) * DQK)
            q = q_ref[bi, :, sl]
            k = k_ref[bi, :, sl]
            v = v_ref[bi, :, sl]
            lf_r = gt[HEADS + h:HEADS + h + 1, :]
            a_r = gt[h:h + 1, :] - gt[3 * HEADS + h:3 * HEADS + h + 1, :]
            m_prev = jnp.max(m_s[bi, h:h + 1, :], axis=1, keepdims=True)
            c_prev = c_s[bi, h]

            m_c = jnp.maximum(m_prev, jnp.max(jnp.where(causal, a_r, -jnp.inf), axis=1, keepdims=True))
            b_c = jnp.sum(jnp.where(causal, lf_r, 0.0), axis=1, keepdims=True)
            w = _dot_nt(q, k) * jnp.exp(jnp.where(causal, a_r - m_c, -jnp.inf))
            g = jnp.exp(m_prev - m_c)
            qc = _dot_nt(q, c_prev.astype(BF16))
            num = g * qc[:, 0:DV] + _dot(w.astype(BF16), v)
            den = g * qc[:, DV:2 * DV] + jnp.sum(w, axis=1, keepdims=True)
            hh = num / jnp.maximum(jnp.abs(den), jnp.exp(-(b_c + m_c)))
            hh = hh * lax.rsqrt(jnp.mean(hh * hh, axis=1, keepdims=True) + EPS) * gmh_ref[:, sl]
            y_s[bi * L:(bi + 1) * L, h * DV:(h + 1) * DV] = (so_ref[bi, :, sl].astype(F32) * hh).astype(BF16)

            m_last = jnp.maximum(m_prev, jnp.max(a_r, axis=1, keepdims=True))
            b_last = jnp.sum(lf_r, axis=1, keepdims=True)
            s_r = jnp.exp(a_r - m_last)
            sv_t = jnp.concatenate([v.T.astype(F32) * s_r, jnp.broadcast_to(s_r, (DV, L))], axis=0)
            c_s[bi, h] = jnp.exp(m_prev - m_last) * c_prev + _dot(sv_t.astype(BF16), k)
            m_s[bi, h:h + 1, :] = jnp.broadcast_to(b_last + m_last, (1, m_s.shape[2]))

    for bi in range(nb):
        out = (_dot(yconv_ref[bi], wout_ref[0:CONV_DIM, :])
               + _dot(y_s[bi * L:(bi + 1) * L, :], wout_ref[CONV_DIM:CONV_DIM + MLSTM_DIM, :]))
        hp_ref[bi] = x_ref[bi] + out

    @pl.when(c == pl.num_programs(1) - 1)
    def _():
        lane = lax.broadcasted_iota(jnp.int32, (1, m_s.shape[2]), 1)
        for bi in range(nb):
            acc = jnp.zeros((1, m_s.shape[2]), F32)
            for h in range(HEADS):
                pc_ref[bi, h] = c_s[bi, h, 0:DV, :]
                pn_ref[bi, h:h + 1, :] = c_s[bi, h, DV:DV + 1, :]
                acc = jnp.where(lane == h, m_s[bi, h:h + 1, :], acc)
            pm_ref[bi] = acc


def _p2_call(q, k, v, so, yconv, gates, x, wout, gmh, n_batch, seq_len):
    L = MLSTM_CHUNK
    nb = P2_SEQS
    nc = seq_len // L
    seq = lambda width: pl.BlockSpec((nb, L, width), lambda b, c: (b, c, 0))
    as_seq = lambda a: a.reshape(n_batch, seq_len, a.shape[-1])
    return pl.pallas_call(
        _p2_kernel,
        grid=(n_batch // nb, nc),
        in_specs=[seq(MLSTM_DIM), seq(MLSTM_DIM), seq(MLSTM_DIM), seq(MLSTM_DIM), seq(CONV_DIM),
                  pl.BlockSpec((nb, 4 * HEADS, L), lambda b, c: (b, 0, c)), seq(D_MODEL),
                  _const_spec((D_MODEL, D_MODEL)), _const_spec((1, MLSTM_DIM))],
        out_specs=[seq(D_MODEL),
                   pl.BlockSpec((nb, HEADS, DV, DQK), lambda b, c: (b, 0, 0, 0)),
                   pl.BlockSpec((nb, HEADS, DQK), lambda b, c: (b, 0, 0)),
                   pl.BlockSpec((nb, 1, LANES), lambda b, c: (b, 0, 0))],
        out_shape=[jax.ShapeDtypeStruct((n_batch, seq_len, D_MODEL), F32),
                   jax.ShapeDtypeStruct((n_batch, HEADS, DV, DQK), F32),
                   jax.ShapeDtypeStruct((n_batch, HEADS, DQK), F32),
                   jax.ShapeDtypeStruct((n_batch, 1, LANES), F32)],
        scratch_shapes=[pltpu.VMEM((nb, HEADS, 2 * DV, DQK), F32), pltpu.VMEM((nb, SUBLANES, LANES), F32),
                        pltpu.VMEM((nb * L, MLSTM_DIM), BF16)],
        compiler_params=_params(("arbitrary", "arbitrary")),
        name="p2_mlstm_outproj",
    )(as_seq(q), as_seq(k), as_seq(v), as_seq(so), as_seq(yconv), gates, as_seq(x), wout, gmh)


def _pm_kernel(mem_ref, g_ref, w_ref, k_ref, v_ref, kb_ref, vb_ref):
    xn = _rmsnorm(mem_ref[...], g_ref[...]).astype(BF16)
    kk = _dot(xn, w_ref[:, 0:D_MODEL])
    vv = _dot(xn, w_ref[:, D_MODEL:2 * D_MODEL])
    for h in range(X_HEADS):
        sl = slice(h * X_HEAD_DIM, (h + 1) * X_HEAD_DIM)
        k_ref[:, h, :] = kk[:, sl]
        v_ref[:, h, :] = vv[:, sl]
    kb_ref[...] = kk.astype(BF16)
    vb_ref[...] = vv.astype(BF16)


def _pm_call(mem, g, w):
    rows = mem.shape[0]
    tm = ROW_TILE
    row = pl.BlockSpec((tm, D_MODEL), lambda i: (i, 0))
    row4 = pl.BlockSpec((tm, X_HEADS, X_HEAD_DIM), lambda i: (i, 0, 0))
    return pl.pallas_call(
        _pm_kernel,
        grid=(rows // tm,),
        in_specs=[row, _const_spec((1, D_MODEL)), _const_spec((D_MODEL, 2 * D_MODEL))],
        out_specs=[row4, row4, row, row],
        out_shape=[jax.ShapeDtypeStruct((rows, X_HEADS, X_HEAD_DIM), F32)] * 2
        + [jax.ShapeDtypeStruct((rows, D_MODEL), BF16)] * 2,
        compiler_params=_params(("arbitrary",)),
        name="pm_mem_kv",
    )(mem, g, w)


def _p3_kernel(n_cast, hp_ref, g_ref, wq_ref, k_ref, v_ref, *rest):
    _cast_slabs(rest[:n_cast], rest[n_cast + 1:])
    o_ref = rest[n_cast]
    xn = _rmsnorm(hp_ref[...], g_ref[...]).astype(BF16)
    q = _dot(xn, wq_ref[...])
    for h in range(X_HEADS):
        sl = slice(h * X_HEAD_DIM, (h + 1) * X_HEAD_DIM)
        s = _dot_nt(q[:, sl].astype(BF16), k_ref[0, :, sl]) * (X_HEAD_DIM ** -0.5)
        e = jnp.exp(s - jnp.max(s, axis=1, keepdims=True))
        o_h = _dot(e.astype(BF16), v_ref[0, :, sl]) * (1.0 / jnp.sum(e, axis=1, keepdims=True))
        o_ref[:, sl] = o_h.astype(BF16)


def _p3_call(hp, g, wq, kb, vb, seq_len, cast=()):
    rows = hp.shape[0]
    tm = P3_TILE
    steps = rows // tm
    tiles_per_batch = seq_len // tm
    row = pl.BlockSpec((tm, D_MODEL), lambda i: (i, 0))
    mem = pl.BlockSpec((1, N_MEM, D_MODEL), lambda i: (i // tiles_per_batch, 0, 0))
    slabs = _slab_specs(cast, steps)
    return pl.pallas_call(
        functools.partial(_p3_kernel, len(cast)),
        grid=(steps,),
        in_specs=[row, _const_spec((1, D_MODEL)), _const_spec((D_MODEL, D_MODEL)), mem, mem] + slabs,
        out_specs=[row] + slabs,
        out_shape=[jax.ShapeDtypeStruct((rows, D_MODEL), BF16)]
        + [jax.ShapeDtypeStruct(wgt.shape, BF16) for wgt in cast],
        compiler_params=_params(("arbitrary",)),
        name="p3_cross_attn",
    )(hp, g, wq, kb, vb, *cast)


def _cache_attention_row(q8, kc_ref, vc_ref, bl):
    qs = q8 * (X_HEAD_DIM ** -0.5)
    m_run = jnp.full((1, PACK_ROWS, 1), -jnp.inf, F32)
    l_run = jnp.zeros((1, PACK_ROWS, 1), F32)
    acc = jnp.zeros((PACK_ROWS, LANES), F32)
    for c in range(N_MEM // MEM_CHUNK):
        blk = slice(c * MEM_CHUNK, (c + 1) * MEM_CHUNK)
        prod = kc_ref[bl, blk] * qs
        s = jnp.sum(prod + pltpu.roll(prod, X_HEADS, 1), axis=-1, keepdims=True)
        m_new = jnp.maximum(m_run, jnp.max(s, axis=0, keepdims=True))
        alpha = jnp.exp(m_run - m_new)
        e = jnp.exp(s - m_new)
        l_run = alpha * l_run + jnp.sum(e, axis=0, keepdims=True)
        acc = alpha[0] * acc + jnp.sum(e * vc_ref[bl, blk], axis=0)
        m_run = m_new
    return acc * (1.0 / l_run[0])


def _swiglu_final(hp, gf_ref, wgu_ref, wd_ref, gfin_ref, act_s):
    xn = _rmsnorm(hp, gf_ref[...]).astype(BF16)
    for j in range(D_FF // FF_CHUNK):
        g = _dot(xn, wgu_ref[:, FF_CHUNK * j:FF_CHUNK * (j + 1)])
        u = _dot(xn, wgu_ref[:, D_FF + FF_CHUNK * j:D_FF + FF_CHUNK * (j + 1)])
        act_s[:, FF_CHUNK * j:FF_CHUNK * (j + 1)] = (g * jax.nn.sigmoid(g) * u).astype(BF16)
    hp = hp + _dot(act_s[...], wd_ref[...])
    return _rmsnorm(hp, gfin_ref[...])


def _p4_kernel(hp_ref, o_ref, wxo_ref, gf_ref, wgu_ref, wd_ref, gfin_ref, q4_ref, kc_ref, vc_ref, hs_ref,
               y_ref, ys_ref, act_s, os_s):
    i = pl.program_id(0)
    side_rows = q4_ref.shape[0]
    for bl in range(side_rows):
        os_s[i * side_rows + bl] = _cache_attention_row(q4_ref[bl], kc_ref, vc_ref, bl)
    hp = hp_ref[...] + _dot(o_ref[...], wxo_ref[...])
    y_ref[...] = _swiglu_final(hp, gf_ref, wgu_ref, wd_ref, gfin_ref, act_s)

    @pl.when(i == pl.num_programs(0) - 1)
    def _():
        n = hs_ref.shape[0]
        o = jnp.concatenate([os_s[:, half * X_HEADS + h, :] for h in range(X_HEADS)
                             for half in range(X_HEAD_DIM // LANES)], axis=1)
        hs = hs_ref[...] + _dot(o.astype(BF16), wxo_ref[...])
        ys_ref[:, 0, :] = _swiglu_final(hs, gf_ref, wgu_ref, wd_ref, gfin_ref, act_s.at[0:n])


def _p4_call(hp, o, wxo, gf, wgu, wd, gfin, side):
    rows = hp.shape[0]
    tm = ROW_TILE
    steps = rows // tm
    q4, kc, vc, hs = side
    n = hs.shape[0]
    side_rows = n // steps
    assert side_rows * steps == n and n <= tm
    row = pl.BlockSpec((tm, D_MODEL), lambda i: (i, 0))
    srow = pl.BlockSpec((side_rows, PACK_ROWS, LANES), lambda i: (i, 0, 0))
    cache = pl.BlockSpec((side_rows, N_MEM, PACK_ROWS, LANES), lambda i: (i, 0, 0, 0))
    return pl.pallas_call(
        _p4_kernel,
        grid=(steps,),
        in_specs=[row, row, _const_spec((D_MODEL, D_MODEL)), _const_spec((1, D_MODEL)),
                  _const_spec((D_MODEL, 2 * D_FF)), _const_spec((D_FF, D_MODEL)),
                  _const_spec((1, D_MODEL)), srow, cache, cache, _const_spec((n, D_MODEL))],
        out_specs=[row, pl.BlockSpec((n, 1, D_MODEL), lambda i: (0, 0, 0))],
        out_shape=[jax.ShapeDtypeStruct((rows, D_MODEL), F32), jax.ShapeDtypeStruct((n, 1, D_MODEL), F32)],
        scratch_shapes=[pltpu.VMEM((tm, D_FF), BF16), pltpu.VMEM((n, PACK_ROWS, LANES), F32)],
        compiler_params=_params(("arbitrary",)),
        name="p4_ffn_final",
    )(hp, o, wxo, gf, wgu, wd, gfin, q4, kc, vc, hs)


def _s1_kernel(x_ref, g_ref, w_ref, b_ref, wg_ref, bg_ref, cw_ref, st_ref,
               yconv_ref, sconv_ref, q_ref, k_ref, v_ref, so_ref, gate_ref, vt_ref, gatet_ref):
    xn = _rmsnorm(x_ref[:, 0, :], g_ref[...]).astype(BF16)

    def seg(j):
        sl = slice(j * CONV_DIM, (j + 1) * CONV_DIM)
        return _dot_nt(xn, w_ref[sl, :].astype(BF16)) + b_ref[:, sl]

    u = seg(1) * seg(2)
    st0 = st_ref[:, 0, :]
    st1 = st_ref[:, 1, :]
    conv = cw_ref[0:1, :] * st0 + cw_ref[1:2, :] * st1 + cw_ref[2:3, :] * u
    yconv_ref[...] = seg(0) * conv
    sconv_ref[:, 0, :] = st1
    sconv_ref[:, 1, :] = u
    q_ref[...] = seg(3)
    k_ref[...] = seg(4) * (DQK ** -0.5)
    v = seg(5)
    v_ref[...] = v
    so_ref[...] = jax.nn.sigmoid(seg(6))
    n_gate = wg_ref.shape[0]
    gate_ref[...] = _gate_transform(_dot_nt(xn, wg_ref[...]) + b_ref[:, MAIN_DIM:MAIN_DIM + n_gate])
    for h in range(HEADS):
        sl = slice(h * DV, (h + 1) * DV)
        vt_ref[sl, :] = v[:, sl].T
    gatet_ref[...] = _gate_transform_rows(_dot_nt(wg_ref[...], xn) + bg_ref[...])


def _s1_call(x, g, w, b, wg, bg, cw, st):
    n = x.shape[0]
    n_gate = wg.shape[0]
    full = lambda *shape: pl.BlockSpec(shape, lambda i: (0,) * len(shape))
    ins = [x, g, w, b, wg, bg, cw, st]
    return pl.pallas_call(
        _s1_kernel,
        grid=(1,),
        in_specs=[full(*a.shape) for a in ins],
        out_specs=[full(n, CONV_DIM), full(*st.shape), full(n, MLSTM_DIM), full(n, MLSTM_DIM),
                   full(n, MLSTM_DIM), full(n, MLSTM_DIM), full(n, n_gate), full(MLSTM_DIM, n),
                   full(n_gate, n)],
        out_shape=[jax.ShapeDtypeStruct((n, CONV_DIM), F32),
                   jax.ShapeDtypeStruct(st.shape, F32)]
        + [jax.ShapeDtypeStruct((n, MLSTM_DIM), F32)] * 4
        + [jax.ShapeDtypeStruct((n, n_gate), F32),
           jax.ShapeDtypeStruct((MLSTM_DIM, n), F32),
           jax.ShapeDtypeStruct((n_gate, n), F32)],
        compiler_params=_params(("arbitrary",)),
        name="s1_inproj_conv",
    )(*ins)


def _s3_kernel(cqt_ref, q_ref, k_ref, v_ref, so_ref, gate_ref, n_ref, m_ref, yconv_ref, x_ref,
               wout_ref, gmh_ref, gx_ref, wq_ref,
               hs_ref, qx_ref, nn_ref, mn_ref, y_s):
    n_rows = q_ref.shape[0]
    y_s[:, 0:CONV_DIM] = yconv_ref[...].astype(BF16)
    lane = lax.broadcasted_iota(jnp.int32, (n_rows, mn_ref.shape[1]), 1)
    m_out = jnp.zeros((n_rows, mn_ref.shape[1]), F32)
    for h in range(HEADS):
        sl = slice(h * DQK, (h + 1) * DQK)
        q = q_ref[:, sl]
        k = k_ref[:, sl]
        v = v_ref[:, sl]
        n_prev = n_ref[:, h, :]
        cq = cqt_ref[sl, :].T
        ig = gate_ref[:, h:h + 1]
        lf = gate_ref[:, HEADS + h:HEADS + h + 1]
        m_prev = m_ref[:, h:h + 1]
        inter = lf + m_prev
        m_row = jnp.maximum(inter, ig)
        wgt = jnp.sum(q * k, axis=1, keepdims=True) * jnp.exp(ig - m_row)
        g = jnp.exp(inter - m_row)
        num = g * cq + wgt * v
        den = g * jnp.sum(n_prev * q, axis=1, keepdims=True) + wgt
        hh = num / jnp.maximum(jnp.abs(den), jnp.exp(-m_row))
        hh = hh * lax.rsqrt(jnp.mean(hh * hh, axis=1, keepdims=True) + EPS) * gmh_ref[:, sl]
        y_s[:, CONV_DIM + h * DV:CONV_DIM + (h + 1) * DV] = (so_ref[:, sl] * hh).astype(BF16)
        nn_ref[:, h, :] = g * n_prev + jnp.exp(ig - m_row) * k
        m_out = jnp.where(lane == h, m_row, m_out)
    mn_ref[...] = m_out
    hs = x_ref[:, 0, :] + _dot(y_s[...], wout_ref[...])
    hs_ref[...] = hs
    qx = _dot(_rmsnorm(hs, gx_ref[...]).astype(BF16), wq_ref[...])
    for h in range(X_HEADS):
        for half in range(X_HEAD_DIM // LANES):
            lo = h * X_HEAD_DIM + half * LANES
            qx_ref[:, half * X_HEADS + h, :] = qx[:, lo:lo + LANES]


def _s3_call(cqt, q, k, v, so, gates, nst, m, yconv, x, wout, gmh, gx, wq):
    n = q.shape[0]
    full = lambda *shape: pl.BlockSpec(shape, lambda i: (0,) * len(shape))
    ins = [cqt, q, k, v, so, gates, nst, m, yconv, x, wout, gmh, gx, wq]
    return pl.pallas_call(
        _s3_kernel,
        grid=(1,),
        in_specs=[full(*a.shape) for a in ins],
        out_specs=[full(n, D_MODEL), full(n, PACK_ROWS, LANES), full(*nst.shape), full(n, LANES)],
        out_shape=[jax.ShapeDtypeStruct((n, D_MODEL), F32), jax.ShapeDtypeStruct((n, PACK_ROWS, LANES), F32),
                   jax.ShapeDtypeStruct(nst.shape, F32), jax.ShapeDtypeStruct((n, LANES), F32)],
        scratch_shapes=[pltpu.VMEM((n, D_MODEL), BF16)],
        compiler_params=_params(("arbitrary",)),
        name="s3_mlstm_finish",
    )(*ins)


def _pack_heads(a):
    lead = a.shape[:-2]
    a = a.reshape(lead + (X_HEADS, X_HEAD_DIM // LANES, LANES))
    return jnp.swapaxes(a, -3, -2).reshape(lead + (PACK_ROWS, LANES))


def kernel(x_prompt, x_sample, mem_prompt, state_conv, state_mlstm_C, state_mlstm_n, state_mlstm_m,
           cache_mem_k, cache_mem_v, g_mix, w_in, b_in, conv_w, g_mh, w_out, g_cross, g_mem,
           w_xq, w_xkv, w_xo, g_ffn, w_gu, w_down, g_final):
    n_batch, seq_len, _ = x_prompt.shape
    n_dec = x_sample.shape[0]
    depth = w_in.shape[0]
    assert depth == 1 and x_sample.shape[1] == 1
    assert all(seq_len % t == 0 for t in (ROW_TILE, P1_TILE, P3_TILE))
    assert P1_TILE % MLSTM_CHUNK == 0 and n_batch % P2_SEQS == 0

    n_gate = 2 * HEADS
    w_in_b = w_in[0].T
    b_in_r = b_in[0].reshape(1, MAIN_DIM + n_gate)
    w_gate_r = w_in_b[MAIN_DIM:].astype(BF16)
    b_gate_r = b_in[0, MAIN_DIM:].reshape(n_gate, 1)
    g_mix_r = g_mix[0].reshape(1, D_MODEL)
    g_cross_r = g_cross[0].reshape(1, D_MODEL)
    g_mem_r = g_mem[0].reshape(1, D_MODEL)
    g_ffn_r = g_ffn[0].reshape(1, D_MODEL)
    g_final_r = g_final.reshape(1, D_MODEL)
    g_mh_r = g_mh[0].reshape(1, MLSTM_DIM)
    cw = conv_w[0]

    m0 = state_mlstm_m[0]
    s_yconv, s_conv, sq, sk, sv, sso, sgates, svt, sgates_t = _s1_call(
        x_sample, g_mix_r, w_in_b, b_in_r, w_gate_r, b_gate_r, cw, state_conv[0])

    xp = x_prompt.reshape(n_batch * seq_len, D_MODEL)
    yconv, q, k, v, so, gates, p_conv, s_c, cqt, w_out_b, w_xq_b, w_xkv_b = _p1_call(
        xp, g_mix_r, w_in_b, b_in_r, w_gate_r, b_gate_r, cw, seq_len,
        side=(state_mlstm_C[0], sq, sk, svt, sgates, sgates_t, m0, m0.T),
        cast=(w_out[0], w_xq[0], w_xkv[0]))
    hs1, qx, s_n, s_m = _s3_call(cqt, sq, sk, sv, sso, sgates,
                                 state_mlstm_n[0], m0, s_yconv, x_sample,
                                 w_out_b, g_mh_r, g_cross_r, w_xq_b)
    hp1, p_c, p_n, p_m = _p2_call(q, k, v, so, yconv, gates, xp, w_out_b, g_mh_r, n_batch, seq_len)
    hp1 = hp1.reshape(n_batch * seq_len, D_MODEL)
    pk, pv, pkb, pvb = _pm_call(mem_prompt.reshape(n_batch * N_MEM, D_MODEL), g_mem_r, w_xkv_b)
    o_p, w_xo_b, w_gu_b, w_down_b = _p3_call(
        hp1, g_cross_r, w_xq_b, pkb.reshape(n_batch, N_MEM, D_MODEL), pvb.reshape(n_batch, N_MEM, D_MODEL),
        seq_len, cast=(w_xo[0], w_gu[0], w_down[0]))
    y_p, y_s = _p4_call(hp1, o_p, w_xo_b, g_ffn_r, w_gu_b, w_down_b, g_final_r,
                        side=(qx, _pack_heads(cache_mem_k[0]), _pack_heads(cache_mem_v[0]), hs1))

    mem_shape = (1, n_batch, N_MEM, X_HEADS, X_HEAD_DIM)
    return (y_p.reshape(n_batch, seq_len, D_MODEL),
            y_s.reshape(n_dec, 1, D_MODEL),
            p_conv.reshape(1, n_batch, CONV_W - 1, CONV_DIM),
            p_c.reshape(1, n_batch, HEADS, DV, DQK),
            p_n.reshape(1, n_batch, HEADS, DQK),
            p_m[:, 0, :HEADS].reshape(1, n_batch, HEADS),
            pk.reshape(mem_shape),
            pv.reshape(mem_shape),
            s_conv.reshape(1, n_dec, CONV_W - 1, CONV_DIM),
            s_c.reshape(1, n_dec, HEADS, DV, DQK),
            s_n.reshape(1, n_dec, HEADS, DQK),
            s_m[:, :HEADS].reshape(1, n_dec, HEADS))
```

```python
import functools

import jax
import jax.numpy as jnp
from jax import lax
from jax.experimental import pallas as pl
from jax.experimental.pallas import tpu as pltpu

F32 = jnp.float32
BF16 = jnp.bfloat16

D_MODEL = 1024
CONV_DIM = 512
CONV_W = 3
MLSTM_DIM = 512
HEADS = 4
DQK = 128
DV = 128
N_MEM = 256
X_HEADS = 4
X_HEAD_DIM = 256
D_FF = 2816
MAIN_DIM = 3 * CONV_DIM + 4 * MLSTM_DIM
EPS = 1e-6

LANES = 128
SUBLANES = 8
BF16_ROWS = 16

MLSTM_CHUNK = 256
P2_SEQS = 4
ROW_TILE = 512
P1_TILE = 1024
P3_TILE = 1024
FF_CHUNK = 256
PACK_ROWS = X_HEADS * (X_HEAD_DIM // LANES)
MEM_CHUNK = 32
VMEM_LIMIT = 56 * 1024 * 1024


def _dot(a, b):
    return jnp.dot(a, b, preferred_element_type=F32)


def _dot_nt(a, b):
    return lax.dot_general(a, b, (((1,), (1,)), ((), ())), preferred_element_type=F32)


def _rmsnorm(x, g):
    return x * lax.rsqrt(jnp.mean(x * x, axis=-1, keepdims=True) + EPS) * g


def _const_spec(shape):
    zeros = (0,) * len(shape)
    return pl.BlockSpec(shape, lambda *_: zeros, pipeline_mode=pl.Buffered(1))


def _params(sem):
    return pltpu.CompilerParams(dimension_semantics=sem, vmem_limit_bytes=VMEM_LIMIT)


def _gate_transform(gt):
    lane = lax.broadcasted_iota(jnp.int32, gt.shape, 1)
    return jnp.where(lane < HEADS, gt, jax.nn.log_sigmoid(gt))


def _gate_transform_rows(gt):
    sub = lax.broadcasted_iota(jnp.int32, gt.shape, 0)
    return jnp.where(sub < HEADS, gt, jax.nn.log_sigmoid(gt))


def _memory_update_rows(i, c_ref, q_ref, k_ref, vt_ref, gate_ref, gatet_ref, m_ref, mt_ref, cn_ref, cqt_ref):
    n = vt_ref.shape[1]
    bb = c_ref.shape[0]

    @pl.when(i == 0)
    def _():
        cqt_ref[...] = jnp.zeros_like(cqt_ref)

    lane = lax.broadcasted_iota(jnp.int32, (DV, n), 1)
    for h in range(HEADS):
        sl = slice(h * DQK, (h + 1) * DQK)
        ig_c = gate_ref[:, h:h + 1]
        lf_c = gate_ref[:, HEADS + h:HEADS + h + 1]
        m_c = m_ref[:, h:h + 1]
        dec = jnp.broadcast_to(jnp.exp(lf_c + m_c - jnp.maximum(lf_c + m_c, ig_c)), (bb, DQK))
        ig_r = gatet_ref[h:h + 1, :]
        lf_r = gatet_ref[HEADS + h:HEADS + h + 1, :]
        m_r = mt_ref[h:h + 1, :]
        svt = vt_ref[sl, :] * jnp.exp(ig_r - jnp.maximum(lf_r + m_r, ig_r))
        q_t = q_ref[:, sl]
        k_t = k_ref[:, sl]
        cqt = cqt_ref[sl, :]
        for bl in range(bb):
            onehot = lane == i * bb + bl
            c = c_ref[bl, h]
            cq_col = jnp.sum(c * q_t[bl:bl + 1, :], axis=1, keepdims=True)
            sv_col = jnp.sum(jnp.where(onehot, svt, 0.0), axis=1, keepdims=True)
            cn_ref[bl, h] = dec[bl:bl + 1, :] * c + sv_col * k_t[bl:bl + 1, :]
            cqt = jnp.where(onehot, cq_col, cqt)
        cqt_ref[sl, :] = cqt


def _cast_slabs(srcs, dsts):
    for src, dst in zip(srcs, dsts):
        dst[...] = src[...].astype(BF16)


def _p1_kernel(tiles_per_batch, n_cast, x_ref, g_ref, w_ref, b_ref, wg_ref, bg_ref, cw_ref,
               c_ref, sq_ref, sk_ref, svt_ref, sgate_ref, sgatet_ref, sm_ref, smt_ref, *rest):
    cast_in, rest = rest[:n_cast], rest[n_cast:]
    yconv_ref, q_ref, k_ref, v_ref, so_ref, gate_ref, pconv_ref, cn_ref, cqt_ref = rest[:9]
    cast_out, ubuf = rest[9:9 + n_cast], rest[9 + n_cast]
    _cast_slabs(cast_in, cast_out)
    tm = x_ref.shape[0]
    i = pl.program_id(0)
    _memory_update_rows(i, c_ref, sq_ref, sk_ref, svt_ref, sgate_ref, sgatet_ref, sm_ref, smt_ref,
                        cn_ref, cqt_ref)
    xn = _rmsnorm(x_ref[...], g_ref[...]).astype(BF16)

    def seg(j):
        sl = slice(j * CONV_DIM, (j + 1) * CONV_DIM)
        return _dot_nt(xn, w_ref[sl, :].astype(BF16)) + b_ref[:, sl]

    P = SUBLANES
    prev = ubuf[tm:tm + P, :]
    ubuf[0:P, :] = jnp.where(i % tiles_per_batch == 0, jnp.zeros_like(prev), prev)
    ubuf[P:P + tm, :] = seg(1) * seg(2)
    conv = sum(cw_ref[j:j + 1, :] * ubuf[P - (CONV_W - 1) + j:P - (CONV_W - 1) + j + tm, :] for j in range(CONV_W))
    yconv_ref[...] = (seg(0) * conv).astype(BF16)
    pconv_ref[0] = ubuf[tm + P - (CONV_W - 1):tm + P, :]

    q_ref[...] = seg(3).astype(BF16)
    k_ref[...] = (seg(4) * (DQK ** -0.5)).astype(BF16)
    v_ref[...] = seg(5).astype(BF16)
    so_ref[...] = jax.nn.sigmoid(seg(6)).astype(BF16)
    gt = _gate_transform_rows(_dot_nt(wg_ref[...], xn) + bg_ref[...])
    n_gate = gt.shape[0]
    gate_ref[0, 0:n_gate, :] = gt
    L = MLSTM_CHUNK
    n_blk = tm // L
    hi = gt.astype(BF16).astype(F32)
    r1 = gt - hi
    mid = r1.astype(BF16).astype(F32)
    lo = r1 - mid
    terms = jnp.concatenate([t[:, j * L:(j + 1) * L] for t in (hi, mid, lo) for j in range(n_blk)], axis=0)
    tri = (lax.broadcasted_iota(jnp.int32, (L, L), 0) <= lax.broadcasted_iota(jnp.int32, (L, L), 1)).astype(BF16)
    parts = _dot(terms.astype(BF16), tri)
    for j in range(n_blk):
        rows = [parts[(t * n_blk + j) * n_gate:(t * n_blk + j + 1) * n_gate, :] for t in range(3)]
        gate_ref[0, n_gate:2 * n_gate, j * L:(j + 1) * L] = (rows[0] + rows[1]) + rows[2]


def _slab_specs(weights, steps):
    specs = []
    for wgt in weights:
        slab = wgt.shape[0] // steps
        assert slab * steps == wgt.shape[0] and slab % BF16_ROWS == 0
        specs.append(pl.BlockSpec((slab, wgt.shape[1]), lambda i: (i, 0)))
    return specs


def _p1_call(x, g, w, b, wg, bg, cw, seq_len, side, cast=()):
    rows = x.shape[0]
    tm = P1_TILE
    steps = rows // tm
    slabs = _slab_specs(cast, steps)
    tiles_per_batch = seq_len // tm
    n_batch = rows // seq_len
    row = lambda width: pl.BlockSpec((tm, width), lambda i: (i, 0))
    c, sq, sk, svt, sgate, sgatet, sm, smt = side
    n = sq.shape[0]
    sr = n // steps
    assert sr * steps == n and sr % SUBLANES == 0
    full = lambda a: pl.BlockSpec(a.shape, lambda i: (0,) * a.ndim)
    srow = lambda a: pl.BlockSpec((sr,) + a.shape[1:], lambda i: (i,) + (0,) * (a.ndim - 1))
    return pl.pallas_call(
        functools.partial(_p1_kernel, tiles_per_batch, len(cast)),
        grid=(steps,),
        in_specs=[row(D_MODEL), _const_spec((1, D_MODEL)), _const_spec(w.shape),
                  _const_spec(b.shape), _const_spec((2 * HEADS, D_MODEL)),
                  _const_spec((2 * HEADS, 1)), _const_spec((CONV_W, CONV_DIM)),
                  srow(c), srow(sq), srow(sk), full(svt), srow(sgate), full(sgatet), srow(sm), full(smt)]
        + slabs,
        out_specs=[row(CONV_DIM), row(MLSTM_DIM), row(MLSTM_DIM), row(MLSTM_DIM), row(MLSTM_DIM),
                   pl.BlockSpec((1, 4 * HEADS, tm), lambda i: (i // tiles_per_batch, 0, i % tiles_per_batch)),
                   pl.BlockSpec((1, CONV_W - 1, CONV_DIM), lambda i: (i // tiles_per_batch, 0, 0)),
                   srow(c), full(svt)] + slabs,
        out_shape=[jax.ShapeDtypeStruct((rows, CONV_DIM), BF16)]
        + [jax.ShapeDtypeStruct((rows, MLSTM_DIM), BF16)] * 4
        + [jax.ShapeDtypeStruct((n_batch, 4 * HEADS, seq_len), F32),
           jax.ShapeDtypeStruct((n_batch, CONV_W - 1, CONV_DIM), F32),
           jax.ShapeDtypeStruct(c.shape, F32), jax.ShapeDtypeStruct(svt.shape, F32)]
        + [jax.ShapeDtypeStruct(wgt.shape, BF16) for wgt in cast],
        scratch_shapes=[pltpu.VMEM((tm + SUBLANES, CONV_DIM), F32)],
        compiler_params=_params(("arbitrary",)),
        name="p1_inproj_conv",
    )(x, g, w, b, wg, bg, cw, c, sq, sk, svt, sgate, sgatet, sm, smt, *cast)


def _p2_kernel(q_ref, k_ref, v_ref, so_ref, yconv_ref, gate_ref, x_ref, wout_ref, gmh_ref,
               hp_ref, pc_ref, pn_ref, pm_ref, c_s, m_s, y_s):
    nb, L = q_ref.shape[0], q_ref.shape[1]
    c = pl.program_id(1)

    @pl.when(c == 0)
    def _():
        c_s[...] = jnp.zeros_like(c_s)
        m_s[...] = jnp.zeros_like(m_s)

    row = lax.broadcasted_iota(jnp.int32, (L, L), 0)
    col = lax.broadcasted_iota(jnp.int32, (L, L), 1)
    causal = row >= col

    for bi in range(nb):
        gt = gate_ref[bi]
        for h in range(HEADS):
            sl = slice(h * DQK, (h + 1) * DQK)
            q = q_ref[bi, :, sl]
            k = k_ref[bi, :, sl]
            v = v_ref[bi, :, sl]
            lf_r = gt[HEADS + h:HEADS + h + 1, :]
            a_r = gt[h:h + 1, :] - gt[3 * HEADS + h:3 * HEADS + h + 1, :]
            m_prev = jnp.max(m_s[bi, h:h + 1, :], axis=1, keepdims=True)
            c_prev = c_s[bi, h]

            m_c = jnp.maximum(m_prev, jnp.max(jnp.where(causal, a_r, -jnp.inf), axis=1, keepdims=True))
            b_c = jnp.sum(jnp.where(causal, lf_r, 0.0), axis=1, keepdims=True)
            w = _dot_nt(q, k) * jnp.exp(jnp.where(causal, a_r - m_c, -jnp.inf))
            g = jnp.exp(m_prev - m_c)
            qc = _dot_nt(q, c_prev.astype(BF16))
            num = g * qc[:, 0:DV] + _dot(w.astype(BF16), v)
            den = g * qc[:, DV:2 * DV] + jnp.sum(w, axis=1, keepdims=True)
            hh = num / jnp.maximum(jnp.abs(den), jnp.exp(-(b_c + m_c)))
            hh = hh * lax.rsqrt(jnp.mean(hh * hh, axis=1, keepdims=True) + EPS) * gmh_ref[:, sl]
            y_s[bi * L:(bi + 1) * L, h * DV:(h + 1) * DV] = (so_ref[bi, :, sl].astype(F32) * hh).astype(BF16)

            m_last = jnp.maximum(m_prev, jnp.max(a_r, axis=1, keepdims=True))
            b_last = jnp.sum(lf_r, axis=1, keepdims=True)
            s_r = jnp.exp(a_r - m_last)
            sv_t = jnp.concatenate([v.T.astype(F32) * s_r, jnp.broadcast_to(s_r, (DV, L))], axis=0)
            c_s[bi, h] = jnp.exp(m_prev - m_last) * c_prev + _dot(sv_t.astype(BF16), k)
            m_s[bi, h:h + 1, :] = jnp.broadcast_to(b_last + m_last, (1, m_s.shape[2]))

    for bi in range(nb):
        out = (_dot(yconv_ref[bi], wout_ref[0:CONV_DIM, :])
               + _dot(y_s[bi * L:(bi + 1) * L, :], wout_ref[CONV_DIM:CONV_DIM + MLSTM_DIM, :]))
        hp_ref[bi] = x_ref[bi] + out

    @pl.when(c == pl.num_programs(1) - 1)
    def _():
        lane = lax.broadcasted_iota(jnp.int32, (1, m_s.shape[2]), 1)
        for bi in range(nb):
            acc = jnp.zeros((1, m_s.shape[2]), F32)
            for h in range(HEADS):
                pc_ref[bi, h] = c_s[bi, h, 0:DV, :]
                pn_ref[bi, h:h + 1, :] = c_s[bi, h, DV:DV + 1, :]
                acc = jnp.where(lane == h, m_s[bi, h:h + 1, :], acc)
            pm_ref[bi] = acc


def _p2_call(q, k, v, so, yconv, gates, x, wout, gmh, n_batch, seq_len):
    L = MLSTM_CHUNK
    nb = P2_SEQS
    nc = seq_len // L
    seq = lambda width: pl.BlockSpec((nb, L, width), lambda b, c: (b, c, 0))
    as_seq = lambda a: a.reshape(n_batch, seq_len, a.shape[-1])
    return pl.pallas_call(
        _p2_kernel,
        grid=(n_batch // nb, nc),
        in_specs=[seq(MLSTM_DIM), seq(MLSTM_DIM), seq(MLSTM_DIM), seq(MLSTM_DIM), seq(CONV_DIM),
                  pl.BlockSpec((nb, 4 * HEADS, L), lambda b, c: (b, 0, c)), seq(D_MODEL),
                  _const_spec((D_MODEL, D_MODEL)), _const_spec((1, MLSTM_DIM))],
        out_specs=[seq(D_MODEL),
                   pl.BlockSpec((nb, HEADS, DV, DQK), lambda b, c: (b, 0, 0, 0)),
                   pl.BlockSpec((nb, HEADS, DQK), lambda b, c: (b, 0, 0)),
                   pl.BlockSpec((nb, 1, LANES), lambda b, c: (b, 0, 0))],
        out_shape=[jax.ShapeDtypeStruct((n_batch, seq_len, D_MODEL), F32),
                   jax.ShapeDtypeStruct((n_batch, HEADS, DV, DQK), F32),
                   jax.ShapeDtypeStruct((n_batch, HEADS, DQK), F32),
                   jax.ShapeDtypeStruct((n_batch, 1, LANES), F32)],
        scratch_shapes=[pltpu.VMEM((nb, HEADS, 2 * DV, DQK), F32), pltpu.VMEM((nb, SUBLANES, LANES), F32),
                        pltpu.VMEM((nb * L, MLSTM_DIM), BF16)],
        compiler_params=_params(("arbitrary", "arbitrary")),
        name="p2_mlstm_outproj",
    )(as_seq(q), as_seq(k), as_seq(v), as_seq(so), as_seq(yconv), gates, as_seq(x), wout, gmh)


def _pm_kernel(mem_ref, g_ref, w_ref, k_ref, v_ref, kb_ref, vb_ref):
    xn = _rmsnorm(mem_ref[...], g_ref[...]).astype(BF16)
    kk = _dot(xn, w_ref[:, 0:D_MODEL])
    vv = _dot(xn, w_ref[:, D_MODEL:2 * D_MODEL])
    for h in range(X_HEADS):
        sl = slice(h * X_HEAD_DIM, (h + 1) * X_HEAD_DIM)
        k_ref[:, h, :] = kk[:, sl]
        v_ref[:, h, :] = vv[:, sl]
    kb_ref[...] = kk.astype(BF16)
    vb_ref[...] = vv.astype(BF16)


def _pm_call(mem, g, w):
    rows = mem.shape[0]
    tm = ROW_TILE
    row = pl.BlockSpec((tm, D_MODEL), lambda i: (i, 0))
    row4 = pl.BlockSpec((tm, X_HEADS, X_HEAD_DIM), lambda i: (i, 0, 0))
    return pl.pallas_call(
        _pm_kernel,
        grid=(rows // tm,),
        in_specs=[row, _const_spec((1, D_MODEL)), _const_spec((D_MODEL, 2 * D_MODEL))],
        out_specs=[row4, row4, row, row],
        out_shape=[jax.ShapeDtypeStruct((rows, X_HEADS, X_HEAD_DIM), F32)] * 2
        + [jax.ShapeDtypeStruct((rows, D_MODEL), BF16)] * 2,
        compiler_params=_params(("arbitrary",)),
        name="pm_mem_kv",
    )(mem, g, w)


def _p3_kernel(n_cast, n_side, hp_ref, g_ref, wq_ref, k_ref, v_ref, *rest):
    n_in = n_cast + n_side
    _cast_slabs(rest[:n_cast], rest[n_in + 1:n_in + 1 + n_cast])
    o_ref = rest[n_in]
    xn = _rmsnorm(hp_ref[...], g_ref[...]).astype(BF16)
    q = _dot(xn, wq_ref[...])
    for h in range(X_HEADS):
        sl = slice(h * X_HEAD_DIM, (h + 1) * X_HEAD_DIM)
        s = _dot_nt(q[:, sl].astype(BF16), k_ref[0, :, sl]) * (X_HEAD_DIM ** -0.5)
        e = jnp.exp(s - jnp.max(s, axis=1, keepdims=True))
        o_h = _dot(e.astype(BF16), v_ref[0, :, sl]) * (1.0 / jnp.sum(e, axis=1, keepdims=True))
        o_ref[:, sl] = o_h.astype(BF16)

    if n_side:
        @pl.when(pl.program_id(0) == pl.num_programs(0) - 1)
        def _():
            _sample_mlstm_finish(*rest[n_cast:n_in], g_ref, wq_ref, *rest[n_in + 1 + n_cast:])


def _p3_call(hp, g, wq, kb, vb, seq_len, cast=(), side=()):
    rows = hp.shape[0]
    tm = P3_TILE
    steps = rows // tm
    tiles_per_batch = seq_len // tm
    row = pl.BlockSpec((tm, D_MODEL), lambda i: (i, 0))
    mem = pl.BlockSpec((1, N_MEM, D_MODEL), lambda i: (i // tiles_per_batch, 0, 0))
    slabs = _slab_specs(cast, steps)
    side_out, side_scratch = [], []
    if side:
        n, nst = side[1].shape[0], side[6]
        side_out = [(n, D_MODEL), (n, PACK_ROWS, LANES), nst.shape, (n, LANES)]
        side_scratch = [pltpu.VMEM((n, D_MODEL), BF16)]
    full = lambda shape: pl.BlockSpec(shape, lambda i: (0,) * len(shape))
    return pl.pallas_call(
        functools.partial(_p3_kernel, len(cast), len(side)),
        grid=(steps,),
        in_specs=[row, _const_spec((1, D_MODEL)), _const_spec((D_MODEL, D_MODEL)), mem, mem] + slabs
        + [_const_spec(a.shape) for a in side],
        out_specs=[row] + slabs + [full(shape) for shape in side_out],
        out_shape=[jax.ShapeDtypeStruct((rows, D_MODEL), BF16)]
        + [jax.ShapeDtypeStruct(wgt.shape, BF16) for wgt in cast]
        + [jax.ShapeDtypeStruct(shape, F32) for shape in side_out],
        scratch_shapes=side_scratch,
        compiler_params=_params(("arbitrary",)),
        name="p3_cross_attn",
    )(hp, g, wq, kb, vb, *cast, *side)


def _cache_attention_row(q8, kc_ref, vc_ref, bl):
    qs = q8 * (X_HEAD_DIM ** -0.5)
    m_run = jnp.full((1, PACK_ROWS, 1), -jnp.inf, F32)
    l_run = jnp.zeros((1, PACK_ROWS, 1), F32)
    acc = jnp.zeros((PACK_ROWS, LANES), F32)
    for c in range(N_MEM // MEM_CHUNK):
        blk = slice(c * MEM_CHUNK, (c + 1) * MEM_CHUNK)
        prod = kc_ref[bl, blk] * qs
        s = jnp.sum(prod + pltpu.roll(prod, X_HEADS, 1), axis=-1, keepdims=True)
        m_new = jnp.maximum(m_run, jnp.max(s, axis=0, keepdims=True))
        alpha = jnp.exp(m_run - m_new)
        e = jnp.exp(s - m_new)
        l_run = alpha * l_run + jnp.sum(e, axis=0, keepdims=True)
        acc = alpha[0] * acc + jnp.sum(e * vc_ref[bl, blk], axis=0)
        m_run = m_new
    return acc * (1.0 / l_run[0])


def _swiglu_final(hp, gf_ref, wgu_ref, wd_ref, gfin_ref, act_s):
    xn = _rmsnorm(hp, gf_ref[...]).astype(BF16)
    for j in range(D_FF // FF_CHUNK):
        g = _dot(xn, wgu_ref[:, FF_CHUNK * j:FF_CHUNK * (j + 1)])
        u = _dot(xn, wgu_ref[:, D_FF + FF_CHUNK * j:D_FF + FF_CHUNK * (j + 1)])
        act_s[:, FF_CHUNK * j:FF_CHUNK * (j + 1)] = (g * jax.nn.sigmoid(g) * u).astype(BF16)
    hp = hp + _dot(act_s[...], wd_ref[...])
    return _rmsnorm(hp, gfin_ref[...])


def _p4_kernel(hp_ref, o_ref, wxo_ref, gf_ref, wgu_ref, wd_ref, gfin_ref, q4_ref, kc_ref, vc_ref, hs_ref,
               y_ref, ys_ref, act_s, os_s):
    i = pl.program_id(0)
    side_rows = q4_ref.shape[0]
    for bl in range(side_rows):
        os_s[i * side_rows + bl] = _cache_attention_row(q4_ref[bl], kc_ref, vc_ref, bl)
    hp = hp_ref[...] + _dot(o_ref[...], wxo_ref[...])
    y_ref[...] = _swiglu_final(hp, gf_ref, wgu_ref, wd_ref, gfin_ref, act_s)

    @pl.when(i == pl.num_programs(0) - 1)
    def _():
        n = hs_ref.shape[0]
        o = jnp.concatenate([os_s[:, half * X_HEADS + h, :] for h in range(X_HEADS)
                             for half in range(X_HEAD_DIM // LANES)], axis=1)
        hs = hs_ref[...] + _dot(o.astype(BF16), wxo_ref[...])
        ys_ref[:, 0, :] = _swiglu_final(hs, gf_ref, wgu_ref, wd_ref, gfin_ref, act_s.at[0:n])


def _p4_call(hp, o, wxo, gf, wgu, wd, gfin, side):
    rows = hp.shape[0]
    tm = ROW_TILE
    steps = rows // tm
    q4, kc, vc, hs = side
    n = hs.shape[0]
    side_rows = n // steps
    assert side_rows * steps == n and n <= tm
    row = pl.BlockSpec((tm, D_MODEL), lambda i: (i, 0))
    srow = pl.BlockSpec((side_rows, PACK_ROWS, LANES), lambda i: (i, 0, 0))
    cache = pl.BlockSpec((side_rows, N_MEM, PACK_ROWS, LANES), lambda i: (i, 0, 0, 0))
    return pl.pallas_call(
        _p4_kernel,
        grid=(steps,),
        in_specs=[row, row, _const_spec((D_MODEL, D_MODEL)), _const_spec((1, D_MODEL)),
                  _const_spec((D_MODEL, 2 * D_FF)), _const_spec((D_FF, D_MODEL)),
                  _const_spec((1, D_MODEL)), srow, cache, cache, _const_spec((n, D_MODEL))],
        out_specs=[row, pl.BlockSpec((n, 1, D_MODEL), lambda i: (0, 0, 0))],
        out_shape=[jax.ShapeDtypeStruct((rows, D_MODEL), F32), jax.ShapeDtypeStruct((n, 1, D_MODEL), F32)],
        scratch_shapes=[pltpu.VMEM((tm, D_FF), BF16), pltpu.VMEM((n, PACK_ROWS, LANES), F32)],
        compiler_params=_params(("arbitrary",)),
        name="p4_ffn_final",
    )(hp, o, wxo, gf, wgu, wd, gfin, q4, kc, vc, hs)


def _s1_kernel(x_ref, g_ref, w_ref, b_ref, wg_ref, bg_ref, cw_ref, st_ref,
               yconv_ref, sconv_ref, q_ref, k_ref, v_ref, so_ref, gate_ref, vt_ref, gatet_ref):
    xn = _rmsnorm(x_ref[:, 0, :], g_ref[...]).astype(BF16)

    def seg(j):
        sl = slice(j * CONV_DIM, (j + 1) * CONV_DIM)
        return _dot_nt(xn, w_ref[sl, :].astype(BF16)) + b_ref[:, sl]

    u = seg(1) * seg(2)
    st0 = st_ref[:, 0, :]
    st1 = st_ref[:, 1, :]
    conv = cw_ref[0:1, :] * st0 + cw_ref[1:2, :] * st1 + cw_ref[2:3, :] * u
    yconv_ref[...] = seg(0) * conv
    sconv_ref[:, 0, :] = st1
    sconv_ref[:, 1, :] = u
    q_ref[...] = seg(3)
    k_ref[...] = seg(4) * (DQK ** -0.5)
    v = seg(5)
    v_ref[...] = v
    so_ref[...] = jax.nn.sigmoid(seg(6))
    n_gate = wg_ref.shape[0]
    gate_ref[...] = _gate_transform(_dot_nt(xn, wg_ref[...]) + b_ref[:, MAIN_DIM:MAIN_DIM + n_gate])
    for h in range(HEADS):
        sl = slice(h * DV, (h + 1) * DV)
        vt_ref[sl, :] = v[:, sl].T
    gatet_ref[...] = _gate_transform_rows(_dot_nt(wg_ref[...], xn) + bg_ref[...])


def _s1_call(x, g, w, b, wg, bg, cw, st):
    n = x.shape[0]
    n_gate = wg.shape[0]
    full = lambda *shape: pl.BlockSpec(shape, lambda i: (0,) * len(shape))
    ins = [x, g, w, b, wg, bg, cw, st]
    return pl.pallas_call(
        _s1_kernel,
        grid=(1,),
        in_specs=[full(*a.shape) for a in ins],
        out_specs=[full(n, CONV_DIM), full(*st.shape), full(n, MLSTM_DIM), full(n, MLSTM_DIM),
                   full(n, MLSTM_DIM), full(n, MLSTM_DIM), full(n, n_gate), full(MLSTM_DIM, n),
                   full(n_gate, n)],
        out_shape=[jax.ShapeDtypeStruct((n, CONV_DIM), F32),
                   jax.ShapeDtypeStruct(st.shape, F32)]
        + [jax.ShapeDtypeStruct((n, MLSTM_DIM), F32)] * 4
        + [jax.ShapeDtypeStruct((n, n_gate), F32),
           jax.ShapeDtypeStruct((MLSTM_DIM, n), F32),
           jax.ShapeDtypeStruct((n_gate, n), F32)],
        compiler_params=_params(("arbitrary",)),
        name="s1_inproj_conv",
    )(*ins)


def _sample_mlstm_finish(cqt_ref, q_ref, k_ref, v_ref, so_ref, gate_ref, n_ref, m_ref, yconv_ref, x_ref,
                         wout_ref, gmh_ref, gx_ref, wq_ref,
                         hs_ref, qx_ref, nn_ref, mn_ref, y_s):
    n_rows = q_ref.shape[0]
    y_s[:, 0:CONV_DIM] = yconv_ref[...].astype(BF16)
    lane = lax.broadcasted_iota(jnp.int32, (n_rows, mn_ref.shape[1]), 1)
    m_out = jnp.zeros((n_rows, mn_ref.shape[1]), F32)
    for h in range(HEADS):
        sl = slice(h * DQK, (h + 1) * DQK)
        q = q_ref[:, sl]
        k = k_ref[:, sl]
        v = v_ref[:, sl]
        n_prev = n_ref[:, h, :]
        cq = cqt_ref[sl, :].T
        ig = gate_ref[:, h:h + 1]
        lf = gate_ref[:, HEADS + h:HEADS + h + 1]
        m_prev = m_ref[:, h:h + 1]
        inter = lf + m_prev
        m_row = jnp.maximum(inter, ig)
        wgt = jnp.sum(q * k, axis=1, keepdims=True) * jnp.exp(ig - m_row)
        g = jnp.exp(inter - m_row)
        num = g * cq + wgt * v
        den = g * jnp.sum(n_prev * q, axis=1, keepdims=True) + wgt
        hh = num / jnp.maximum(jnp.abs(den), jnp.exp(-m_row))
        hh = hh * lax.rsqrt(jnp.mean(hh * hh, axis=1, keepdims=True) + EPS) * gmh_ref[:, sl]
        y_s[:, CONV_DIM + h * DV:CONV_DIM + (h + 1) * DV] = (so_ref[:, sl] * hh).astype(BF16)
        nn_ref[:, h, :] = g * n_prev + jnp.exp(ig - m_row) * k
        m_out = jnp.where(lane == h, m_row, m_out)
    mn_ref[...] = m_out
    hs = x_ref[:, 0, :] + _dot(y_s[...], wout_ref[...])
    hs_ref[...] = hs
    qx = _dot(_rmsnorm(hs, gx_ref[...]).astype(BF16), wq_ref[...])
    for h in range(X_HEADS):
        for half in range(X_HEAD_DIM // LANES):
            lo = h * X_HEAD_DIM + half * LANES
            qx_ref[:, half * X_HEADS + h, :] = qx[:, lo:lo + LANES]


def _pack_heads(a):
    lead = a.shape[:-2]
    a = a.reshape(lead + (X_HEADS, X_HEAD_DIM // LANES, LANES))
    return jnp.swapaxes(a, -3, -2).reshape(lead + (PACK_ROWS, LANES))


def kernel(x_prompt, x_sample, mem_prompt, state_conv, state_mlstm_C, state_mlstm_n, state_mlstm_m,
           cache_mem_k, cache_mem_v, g_mix, w_in, b_in, conv_w, g_mh, w_out, g_cross, g_mem,
           w_xq, w_xkv, w_xo, g_ffn, w_gu, w_down, g_final):
    n_batch, seq_len, _ = x_prompt.shape
    n_dec = x_sample.shape[0]
    depth = w_in.shape[0]
    assert depth == 1 and x_sample.shape[1] == 1
    assert all(seq_len % t == 0 for t in (ROW_TILE, P1_TILE, P3_TILE))
    assert P1_TILE % MLSTM_CHUNK == 0 and n_batch % P2_SEQS == 0

    n_gate = 2 * HEADS
    w_in_b = w_in[0].T
    b_in_r = b_in[0].reshape(1, MAIN_DIM + n_gate)
    w_gate_r = w_in_b[MAIN_DIM:].astype(BF16)
    b_gate_r = b_in[0, MAIN_DIM:].reshape(n_gate, 1)
    g_mix_r = g_mix[0].reshape(1, D_MODEL)
    g_cross_r = g_cross[0].reshape(1, D_MODEL)
    g_mem_r = g_mem[0].reshape(1, D_MODEL)
    g_ffn_r = g_ffn[0].reshape(1, D_MODEL)
    g_final_r = g_final.reshape(1, D_MODEL)
    g_mh_r = g_mh[0].reshape(1, MLSTM_DIM)
    cw = conv_w[0]

    m0 = state_mlstm_m[0]
    s_yconv, s_conv, sq, sk, sv, sso, sgates, svt, sgates_t = _s1_call(
        x_sample, g_mix_r, w_in_b, b_in_r, w_gate_r, b_gate_r, cw, state_conv[0])

    xp = x_prompt.reshape(n_batch * seq_len, D_MODEL)
    yconv, q, k, v, so, gates, p_conv, s_c, cqt, w_out_b, w_xq_b, w_xkv_b = _p1_call(
        xp, g_mix_r, w_in_b, b_in_r, w_gate_r, b_gate_r, cw, seq_len,
        side=(state_mlstm_C[0], sq, sk, svt, sgates, sgates_t, m0, m0.T),
        cast=(w_out[0], w_xq[0], w_xkv[0]))
    hp1, p_c, p_n, p_m = _p2_call(q, k, v, so, yconv, gates, xp, w_out_b, g_mh_r, n_batch, seq_len)
    hp1 = hp1.reshape(n_batch * seq_len, D_MODEL)
    pk, pv, pkb, pvb = _pm_call(mem_prompt.reshape(n_batch * N_MEM, D_MODEL), g_mem_r, w_xkv_b)
    o_p, w_xo_b, w_gu_b, w_down_b, hs1, qx, s_n, s_m = _p3_call(
        hp1, g_cross_r, w_xq_b, pkb.reshape(n_batch, N_MEM, D_MODEL), pvb.reshape(n_batch, N_MEM, D_MODEL),
        seq_len, cast=(w_xo[0], w_gu[0], w_down[0]),
        side=(cqt, sq, sk, sv, sso, sgates, state_mlstm_n[0], m0, s_yconv, x_sample, w_out_b, g_mh_r))
    y_p, y_s = _p4_call(hp1, o_p, w_xo_b, g_ffn_r, w_gu_b, w_down_b, g_final_r,
                        side=(qx, _pack_heads(cache_mem_k[0]), _pack_heads(cache_mem_v[0]), hs1))

    mem_shape = (1, n_batch, N_MEM, X_HEADS, X_HEAD_DIM)
    return (y_p.reshape(n_batch, seq_len, D_MODEL),
            y_s.reshape(n_dec, 1, D_MODEL),
            p_conv.reshape(1, n_batch, CONV_W - 1, CONV_DIM),
            p_c.reshape(1, n_batch, HEADS, DV, DQK),
            p_n.reshape(1, n_batch, HEADS, DQK),
            p_m[:, 0, :HEADS].reshape(1, n_batch, HEADS),
            pk.reshape(mem_shape),
            pv.reshape(mem_shape),
            s_conv.reshape(1, n_dec, CONV_W - 1, CONV_DIM),
            s_c.reshape(1, n_dec, HEADS, DV, DQK),
            s_n.reshape(1, n_dec, HEADS, DQK),
            s_m[:, :HEADS].reshape(1, n_dec, HEADS))
```

```python
import functools

import jax
import jax.numpy as jnp
from jax import lax
from jax.experimental import pallas as pl
from jax.experimental.pallas import tpu as pltpu

F32 = jnp.float32
BF16 = jnp.bfloat16

D_MODEL = 1024
CONV_DIM = 512
CONV_W = 3
MLSTM_DIM = 512
HEADS = 4
DQK = 128
DV = 128
N_MEM = 256
X_HEADS = 4
X_HEAD_DIM = 256
D_FF = 2816
MAIN_DIM = 3 * CONV_DIM + 4 * MLSTM_DIM
EPS = 1e-6

LANES = 128
SUBLANES = 8
BF16_ROWS = 16

MLSTM_CHUNK = 256
P2_SEQS = 4
ROW_TILE = 512
P1_TILE = 1024
P3_TILE = 1024
FF_CHUNK = 256
PACK_ROWS = X_HEADS * (X_HEAD_DIM // LANES)
MEM_CHUNK = 32
VMEM_LIMIT = 56 * 1024 * 1024


def _dot(a, b):
    return jnp.dot(a, b, preferred_element_type=F32)


def _dot_nt(a, b):
    return lax.dot_general(a, b, (((1,), (1,)), ((), ())), preferred_element_type=F32)


def _rmsnorm(x, g):
    return x * lax.rsqrt(jnp.mean(x * x, axis=-1, keepdims=True) + EPS) * g


def _const_spec(shape):
    zeros = (0,) * len(shape)
    return pl.BlockSpec(shape, lambda *_: zeros, pipeline_mode=pl.Buffered(1))


def _params(sem):
    return pltpu.CompilerParams(dimension_semantics=sem, vmem_limit_bytes=VMEM_LIMIT)


def _gate_transform(gt):
    lane = lax.broadcasted_iota(jnp.int32, gt.shape, 1)
    return jnp.where(lane < HEADS, gt, jax.nn.log_sigmoid(gt))


def _gate_transform_rows(gt):
    sub = lax.broadcasted_iota(jnp.int32, gt.shape, 0)
    return jnp.where(sub < HEADS, gt, jax.nn.log_sigmoid(gt))


def _memory_update_rows(i, c_ref, q_ref, k_ref, vt_ref, gate_ref, gatet_ref, m_ref, mt_ref, cn_ref, cqt_ref):
    n = vt_ref.shape[1]
    bb = c_ref.shape[0]

    @pl.when(i == 0)
    def _():
        cqt_ref[...] = jnp.zeros_like(cqt_ref)

    lane = lax.broadcasted_iota(jnp.int32, (DV, n), 1)
    for h in range(HEADS):
        sl = slice(h * DQK, (h + 1) * DQK)
        ig_c = gate_ref[:, h:h + 1]
        lf_c = gate_ref[:, HEADS + h:HEADS + h + 1]
        m_c = m_ref[:, h:h + 1]
        dec = jnp.broadcast_to(jnp.exp(lf_c + m_c - jnp.maximum(lf_c + m_c, ig_c)), (bb, DQK))
        ig_r = gatet_ref[h:h + 1, :]
        lf_r = gatet_ref[HEADS + h:HEADS + h + 1, :]
        m_r = mt_ref[h:h + 1, :]
        svt = vt_ref[sl, :] * jnp.exp(ig_r - jnp.maximum(lf_r + m_r, ig_r))
        q_t = q_ref[:, sl]
        k_t = k_ref[:, sl]
        cqt = cqt_ref[sl, :]
        for bl in range(bb):
            onehot = lane == i * bb + bl
            c = c_ref[bl, h]
            cq_col = jnp.sum(c * q_t[bl:bl + 1, :], axis=1, keepdims=True)
            sv_col = jnp.sum(jnp.where(onehot, svt, 0.0), axis=1, keepdims=True)
            cn_ref[bl, h] = dec[bl:bl + 1, :] * c + sv_col * k_t[bl:bl + 1, :]
            cqt = jnp.where(onehot, cq_col, cqt)
        cqt_ref[sl, :] = cqt


def _cast_slabs(srcs, dsts):
    for src, dst in zip(srcs, dsts):
        dst[...] = src[...].astype(BF16)


def _p1_kernel(tiles_per_batch, n_cast, x_ref, g_ref, w_ref, b_ref, wg_ref, bg_ref, cw_ref,
               c_ref, sq_ref, sk_ref, svt_ref, sgate_ref, sgatet_ref, sm_ref, smt_ref, *rest):
    cast_in, rest = rest[:n_cast], rest[n_cast:]
    yconv_ref, q_ref, k_ref, v_ref, so_ref, gate_ref, pconv_ref, cn_ref, cqt_ref = rest[:9]
    cast_out, ubuf = rest[9:9 + n_cast], rest[9 + n_cast]
    _cast_slabs(cast_in, cast_out)
    tm = x_ref.shape[0]
    i = pl.program_id(0)
    _memory_update_rows(i, c_ref, sq_ref, sk_ref, svt_ref, sgate_ref, sgatet_ref, sm_ref, smt_ref,
                        cn_ref, cqt_ref)
    xn = _rmsnorm(x_ref[...], g_ref[...]).astype(BF16)

    def seg(j):
        sl = slice(j * CONV_DIM, (j + 1) * CONV_DIM)
        return _dot_nt(xn, w_ref[sl, :].astype(BF16)) + b_ref[:, sl]

    P = SUBLANES
    prev = ubuf[tm:tm + P, :]
    ubuf[0:P, :] = jnp.where(i % tiles_per_batch == 0, jnp.zeros_like(prev), prev)
    ubuf[P:P + tm, :] = seg(1) * seg(2)
    conv = sum(cw_ref[j:j + 1, :] * ubuf[P - (CONV_W - 1) + j:P - (CONV_W - 1) + j + tm, :] for j in range(CONV_W))
    yconv_ref[...] = (seg(0) * conv).astype(BF16)
    pconv_ref[0] = ubuf[tm + P - (CONV_W - 1):tm + P, :]

    q_ref[...] = seg(3).astype(BF16)
    k_ref[...] = (seg(4) * (DQK ** -0.5)).astype(BF16)
    v_ref[...] = seg(5).astype(BF16)
    so_ref[...] = jax.nn.sigmoid(seg(6)).astype(BF16)
    gt = _gate_transform_rows(_dot_nt(wg_ref[...], xn) + bg_ref[...])
    n_gate = gt.shape[0]
    gate_ref[0, 0:n_gate, :] = gt
    L = MLSTM_CHUNK
    n_blk = tm // L
    hi = gt.astype(BF16).astype(F32)
    r1 = gt - hi
    mid = r1.astype(BF16).astype(F32)
    lo = r1 - mid
    terms = jnp.concatenate([t[:, j * L:(j + 1) * L] for t in (hi, mid, lo) for j in range(n_blk)], axis=0)
    tri = (lax.broadcasted_iota(jnp.int32, (L, L), 0) <= lax.broadcasted_iota(jnp.int32, (L, L), 1)).astype(BF16)
    parts = _dot(terms.astype(BF16), tri)
    for j in range(n_blk):
        rows = [parts[(t * n_blk + j) * n_gate:(t * n_blk + j + 1) * n_gate, :] for t in range(3)]
        gate_ref[0, n_gate:2 * n_gate, j * L:(j + 1) * L] = (rows[0] + rows[1]) + rows[2]


def _slab_specs(weights, steps):
    specs = []
    for wgt in weights:
        slab = wgt.shape[0] // steps
        assert slab * steps == wgt.shape[0] and slab % BF16_ROWS == 0
        specs.append(pl.BlockSpec((slab, wgt.shape[1]), lambda i: (i, 0)))
    return specs


def _p1_call(x, g, w, b, wg, bg, cw, seq_len, side, cast=()):
    rows = x.shape[0]
    tm = P1_TILE
    steps = rows // tm
    slabs = _slab_specs(cast, steps)
    tiles_per_batch = seq_len // tm
    n_batch = rows // seq_len
    row = lambda width: pl.BlockSpec((tm, width), lambda i: (i, 0))
    c, sq, sk, svt, sgate, sgatet, sm, smt = side
    n = sq.shape[0]
    sr = n // steps
    assert sr * steps == n and sr % SUBLANES == 0
    full = lambda a: pl.BlockSpec(a.shape, lambda i: (0,) * a.ndim)
    srow = lambda a: pl.BlockSpec((sr,) + a.shape[1:], lambda i: (i,) + (0,) * (a.ndim - 1))
    return pl.pallas_call(
        functools.partial(_p1_kernel, tiles_per_batch, len(cast)),
        grid=(steps,),
        in_specs=[row(D_MODEL), _const_spec((1, D_MODEL)), _const_spec(w.shape),
                  _const_spec(b.shape), _const_spec((2 * HEADS, D_MODEL)),
                  _const_spec((2 * HEADS, 1)), _const_spec((CONV_W, CONV_DIM)),
                  srow(c), srow(sq), srow(sk), full(svt), srow(sgate), full(sgatet), srow(sm), full(smt)]
        + slabs,
        out_specs=[row(CONV_DIM), row(MLSTM_DIM), row(MLSTM_DIM), row(MLSTM_DIM), row(MLSTM_DIM),
                   pl.BlockSpec((1, 4 * HEADS, tm), lambda i: (i // tiles_per_batch, 0, i % tiles_per_batch)),
                   pl.BlockSpec((1, CONV_W - 1, CONV_DIM), lambda i: (i // tiles_per_batch, 0, 0)),
                   srow(c), full(svt)] + slabs,
        out_shape=[jax.ShapeDtypeStruct((rows, CONV_DIM), BF16)]
        + [jax.ShapeDtypeStruct((rows, MLSTM_DIM), BF16)] * 4
        + [jax.ShapeDtypeStruct((n_batch, 4 * HEADS, seq_len), F32),
           jax.ShapeDtypeStruct((n_batch, CONV_W - 1, CONV_DIM), F32),
           jax.ShapeDtypeStruct(c.shape, F32), jax.ShapeDtypeStruct(svt.shape, F32)]
        + [jax.ShapeDtypeStruct(wgt.shape, BF16) for wgt in cast],
        scratch_shapes=[pltpu.VMEM((tm + SUBLANES, CONV_DIM), F32)],
        compiler_params=_params(("arbitrary",)),
        name="p1_inproj_conv",
    )(x, g, w, b, wg, bg, cw, c, sq, sk, svt, sgate, sgatet, sm, smt, *cast)


def _p2_kernel(q_ref, k_ref, v_ref, so_ref, yconv_ref, gate_ref, x_ref, wout_ref, gmh_ref,
               hp_ref, pc_ref, pn_ref, pm_ref, c_s, m_s, y_s):
    nb, L = q_ref.shape[0], q_ref.shape[1]
    c = pl.program_id(1)

    @pl.when(c == 0)
    def _():
        c_s[...] = jnp.zeros_like(c_s)
        m_s[...] = jnp.zeros_like(m_s)

    row = lax.broadcasted_iota(jnp.int32, (L, L), 0)
    col = lax.broadcasted_iota(jnp.int32, (L, L), 1)
    causal = row >= col

    for bi in range(nb):
        gt = gate_ref[bi]
        for h in range(HEADS):
            sl = slice(h * DQK, (h + 1) * DQK)
            q = q_ref[bi, :, sl]
            k = k_ref[bi, :, sl]
            v = v_ref[bi, :, sl]
            lf_r = gt[HEADS + h:HEADS + h + 1, :]
            a_r = gt[h:h + 1, :] - gt[3 * HEADS + h:3 * HEADS + h + 1, :]
            m_prev = jnp.max(m_s[bi, h:h + 1, :], axis=1, keepdims=True)
            c_prev = c_s[bi, h]

            m_c = jnp.maximum(m_prev, jnp.max(jnp.where(causal, a_r, -jnp.inf), axis=1, keepdims=True))
            b_c = jnp.sum(jnp.where(causal, lf_r, 0.0), axis=1, keepdims=True)
            w = _dot_nt(q, k) * jnp.exp(jnp.where(causal, a_r - m_c, -jnp.inf))
            g = jnp.exp(m_prev - m_c)
            qc = _dot_nt(q, c_prev.astype(BF16))
            num = g * qc[:, 0:DV] + _dot(w.astype(BF16), v)
            den = g * qc[:, DV:2 * DV] + jnp.sum(w, axis=1, keepdims=True)
            hh = num / jnp.maximum(jnp.abs(den), jnp.exp(-(b_c + m_c)))
            hh = hh * lax.rsqrt(jnp.mean(hh * hh, axis=1, keepdims=True) + EPS) * gmh_ref[:, sl]
            y_s[bi * L:(bi + 1) * L, h * DV:(h + 1) * DV] = (so_ref[bi, :, sl].astype(F32) * hh).astype(BF16)

            m_last = jnp.maximum(m_prev, jnp.max(a_r, axis=1, keepdims=True))
            b_last = jnp.sum(lf_r, axis=1, keepdims=True)
            s_r = jnp.exp(a_r - m_last)
            sv_t = jnp.concatenate([v.T.astype(F32) * s_r, jnp.broadcast_to(s_r, (DV, L))], axis=0)
            c_s[bi, h] = jnp.exp(m_prev - m_last) * c_prev + _dot(sv_t.astype(BF16), k)
            m_s[bi, h:h + 1, :] = jnp.broadcast_to(b_last + m_last, (1, m_s.shape[2]))

    for bi in range(nb):
        out = (_dot(yconv_ref[bi], wout_ref[0:CONV_DIM, :])
               + _dot(y_s[bi * L:(bi + 1) * L, :], wout_ref[CONV_DIM:CONV_DIM + MLSTM_DIM, :]))
        hp_ref[bi] = x_ref[bi] + out

    @pl.when(c == pl.num_programs(1) - 1)
    def _():
        lane = lax.broadcasted_iota(jnp.int32, (1, m_s.shape[2]), 1)
        for bi in range(nb):
            acc = jnp.zeros((1, m_s.shape[2]), F32)
            for h in range(HEADS):
                pc_ref[bi, h] = c_s[bi, h, 0:DV, :]
                pn_ref[bi, h:h + 1, :] = c_s[bi, h, DV:DV + 1, :]
                acc = jnp.where(lane == h, m_s[bi, h:h + 1, :], acc)
            pm_ref[bi] = acc


def _p2_call(q, k, v, so, yconv, gates, x, wout, gmh, n_batch, seq_len):
    L = MLSTM_CHUNK
    nb = P2_SEQS
    nc = seq_len // L
    seq = lambda width: pl.BlockSpec((nb, L, width), lambda b, c: (b, c, 0))
    as_seq = lambda a: a.reshape(n_batch, seq_len, a.shape[-1])
    return pl.pallas_call(
        _p2_kernel,
        grid=(n_batch // nb, nc),
        in_specs=[seq(MLSTM_DIM), seq(MLSTM_DIM), seq(MLSTM_DIM), seq(MLSTM_DIM), seq(CONV_DIM),
                  pl.BlockSpec((nb, 4 * HEADS, L), lambda b, c: (b, 0, c)), seq(D_MODEL),
                  _const_spec((D_MODEL, D_MODEL)), _const_spec((1, MLSTM_DIM))],
        out_specs=[seq(D_MODEL),
                   pl.BlockSpec((nb, HEADS, DV, DQK), lambda b, c: (b, 0, 0, 0)),
                   pl.BlockSpec((nb, HEADS, DQK), lambda b, c: (b, 0, 0)),
                   pl.BlockSpec((nb, 1, LANES), lambda b, c: (b, 0, 0))],
        out_shape=[jax.ShapeDtypeStruct((n_batch, seq_len, D_MODEL), F32),
                   jax.ShapeDtypeStruct((n_batch, HEADS, DV, DQK), F32),
                   jax.ShapeDtypeStruct((n_batch, HEADS, DQK), F32),
                   jax.ShapeDtypeStruct((n_batch, 1, LANES), F32)],
        scratch_shapes=[pltpu.VMEM((nb, HEADS, 2 * DV, DQK), F32), pltpu.VMEM((nb, SUBLANES, LANES), F32),
                        pltpu.VMEM((nb * L, MLSTM_DIM), BF16)],
        compiler_params=_params(("arbitrary", "arbitrary")),
        name="p2_mlstm_outproj",
    )(as_seq(q), as_seq(k), as_seq(v), as_seq(so), as_seq(yconv), gates, as_seq(x), wout, gmh)


def _pm_kernel(mem_ref, g_ref, w_ref, k_ref, v_ref, kb_ref, vb_ref):
    xn = _rmsnorm(mem_ref[...], g_ref[...]).astype(BF16)
    kk = _dot(xn, w_ref[:, 0:D_MODEL])
    vv = _dot(xn, w_ref[:, D_MODEL:2 * D_MODEL])
    for h in range(X_HEADS):
        sl = slice(h * X_HEAD_DIM, (h + 1) * X_HEAD_DIM)
        k_ref[:, h, :] = kk[:, sl]
        v_ref[:, h, :] = vv[:, sl]
    kb_ref[...] = kk.astype(BF16)
    vb_ref[...] = vv.astype(BF16)


def _pm_call(mem, g, w):
    rows = mem.shape[0]
    tm = ROW_TILE
    row = pl.BlockSpec((tm, D_MODEL), lambda i: (i, 0))
    row4 = pl.BlockSpec((tm, X_HEADS, X_HEAD_DIM), lambda i: (i, 0, 0))
    return pl.pallas_call(
        _pm_kernel,
        grid=(rows // tm,),
        in_specs=[row, _const_spec((1, D_MODEL)), _const_spec((D_MODEL, 2 * D_MODEL))],
        out_specs=[row4, row4, row, row],
        out_shape=[jax.ShapeDtypeStruct((rows, X_HEADS, X_HEAD_DIM), F32)] * 2
        + [jax.ShapeDtypeStruct((rows, D_MODEL), BF16)] * 2,
        compiler_params=_params(("arbitrary",)),
        name="pm_mem_kv",
    )(mem, g, w)


def _p3_kernel(n_cast, hp_ref, g_ref, wq_ref, k_ref, v_ref, *rest):
    _cast_slabs(rest[:n_cast], rest[n_cast + 1:])
    o_ref = rest[n_cast]
    xn = _rmsnorm(hp_ref[...], g_ref[...]).astype(BF16)
    q = _dot(xn, wq_ref[...])
    for h in range(X_HEADS):
        sl = slice(h * X_HEAD_DIM, (h + 1) * X_HEAD_DIM)
        s = _dot_nt(q[:, sl].astype(BF16), k_ref[0, :, sl]) * (X_HEAD_DIM ** -0.5)
        e = jnp.exp(s - jnp.max(s, axis=1, keepdims=True))
        o_h = _dot(e.astype(BF16), v_ref[0, :, sl]) * (1.0 / jnp.sum(e, axis=1, keepdims=True))
        o_ref[:, sl] = o_h.astype(BF16)


def _p3_call(hp, g, wq, kb, vb, seq_len, cast=()):
    rows = hp.shape[0]
    tm = P3_TILE
    steps = rows // tm
    tiles_per_batch = seq_len // tm
    row = pl.BlockSpec((tm, D_MODEL), lambda i: (i, 0))
    mem = pl.BlockSpec((1, N_MEM, D_MODEL), lambda i: (i // tiles_per_batch, 0, 0))
    slabs = _slab_specs(cast, steps)
    return pl.pallas_call(
        functools.partial(_p3_kernel, len(cast)),
        grid=(steps,),
        in_specs=[row, _const_spec((1, D_MODEL)), _const_spec((D_MODEL, D_MODEL)), mem, mem] + slabs,
        out_specs=[row] + slabs,
        out_shape=[jax.ShapeDtypeStruct((rows, D_MODEL), BF16)]
        + [jax.ShapeDtypeStruct(wgt.shape, BF16) for wgt in cast],
        compiler_params=_params(("arbitrary",)),
        name="p3_cross_attn",
    )(hp, g, wq, kb, vb, *cast)


def _cache_attention_row(q8, kc_ref, vc_ref, bl):
    qs = q8 * (X_HEAD_DIM ** -0.5)
    m_run = jnp.full((1, PACK_ROWS, 1), -jnp.inf, F32)
    l_run = jnp.zeros((1, PACK_ROWS, 1), F32)
    acc = jnp.zeros((PACK_ROWS, LANES), F32)
    for c in range(N_MEM // MEM_CHUNK):
        blk = slice(c * MEM_CHUNK, (c + 1) * MEM_CHUNK)
        prod = kc_ref[bl, blk] * qs
        s = jnp.sum(prod + pltpu.roll(prod, X_HEADS, 1), axis=-1, keepdims=True)
        m_new = jnp.maximum(m_run, jnp.max(s, axis=0, keepdims=True))
        alpha = jnp.exp(m_run - m_new)
        e = jnp.exp(s - m_new)
        l_run = alpha * l_run + jnp.sum(e, axis=0, keepdims=True)
        acc = alpha[0] * acc + jnp.sum(e * vc_ref[bl, blk], axis=0)
        m_run = m_new
    return acc * (1.0 / l_run[0])


def _swiglu_final(hp, gf_ref, wgu_ref, wd_ref, gfin_ref, act_s):
    xn = _rmsnorm(hp, gf_ref[...]).astype(BF16)
    for j in range(D_FF // FF_CHUNK):
        g = _dot(xn, wgu_ref[:, FF_CHUNK * j:FF_CHUNK * (j + 1)])
        u = _dot(xn, wgu_ref[:, D_FF + FF_CHUNK * j:D_FF + FF_CHUNK * (j + 1)])
        act_s[:, FF_CHUNK * j:FF_CHUNK * (j + 1)] = (g * jax.nn.sigmoid(g) * u).astype(BF16)
    hp = hp + _dot(act_s[...], wd_ref[...])
    return _rmsnorm(hp, gfin_ref[...])


def _p4_kernel(hp_ref, o_ref, wxo_ref, gf_ref, wgu_ref, wd_ref, gfin_ref, q4_ref, kc_ref, vc_ref, hs_ref,
               y_ref, ys_ref, act_s, os_s):
    i = pl.program_id(0)
    side_rows = q4_ref.shape[0]
    for bl in range(side_rows):
        os_s[i * side_rows + bl] = _cache_attention_row(q4_ref[bl], kc_ref, vc_ref, bl)
    hp = hp_ref[...] + _dot(o_ref[...], wxo_ref[...])
    y_ref[...] = _swiglu_final(hp, gf_ref, wgu_ref, wd_ref, gfin_ref, act_s)

    @pl.when(i == pl.num_programs(0) - 1)
    def _():
        n = hs_ref.shape[0]
        o = jnp.concatenate([os_s[:, half * X_HEADS + h, :] for h in range(X_HEADS)
                             for half in range(X_HEAD_DIM // LANES)], axis=1)
        hs = hs_ref[...] + _dot(o.astype(BF16), wxo_ref[...])
        ys_ref[:, 0, :] = _swiglu_final(hs, gf_ref, wgu_ref, wd_ref, gfin_ref, act_s.at[0:n])


def _p4_call(hp, o, wxo, gf, wgu, wd, gfin, side):
    rows = hp.shape[0]
    tm = ROW_TILE
    steps = rows // tm
    q4, kc, vc, hs = side
    n = hs.shape[0]
    side_rows = n // steps
    assert side_rows * steps == n and n <= tm
    row = pl.BlockSpec((tm, D_MODEL), lambda i: (i, 0))
    srow = pl.BlockSpec((side_rows, PACK_ROWS, LANES), lambda i: (i, 0, 0))
    cache = pl.BlockSpec((side_rows, N_MEM, PACK_ROWS, LANES), lambda i: (i, 0, 0, 0))
    return pl.pallas_call(
        _p4_kernel,
        grid=(steps,),
        in_specs=[row, row, _const_spec((D_MODEL, D_MODEL)), _const_spec((1, D_MODEL)),
                  _const_spec((D_MODEL, 2 * D_FF)), _const_spec((D_FF, D_MODEL)),
                  _const_spec((1, D_MODEL)), srow, cache, cache, _const_spec((n, D_MODEL))],
        out_specs=[row, pl.BlockSpec((n, 1, D_MODEL), lambda i: (0, 0, 0))],
        out_shape=[jax.ShapeDtypeStruct((rows, D_MODEL), F32), jax.ShapeDtypeStruct((n, 1, D_MODEL), F32)],
        scratch_shapes=[pltpu.VMEM((tm, D_FF), BF16), pltpu.VMEM((n, PACK_ROWS, LANES), F32)],
        compiler_params=_params(("arbitrary",)),
        name="p4_ffn_final",
    )(hp, o, wxo, gf, wgu, wd, gfin, q4, kc, vc, hs)


def _s1_kernel(x_ref, g_ref, w_ref, b_ref, wg_ref, bg_ref, cw_ref, st_ref,
               yconv_ref, sconv_ref, q_ref, k_ref, v_ref, so_ref, gate_ref, vt_ref, gatet_ref, wbuf, sem):
    order = (1, 2, 0, 3, 4, 5, 6)
    assert sorted(order) == list(range(wbuf.shape[0]))
    copies = {j: pltpu.make_async_copy(w_ref.at[pl.ds(j * CONV_DIM, CONV_DIM), :], wbuf.at[j], sem.at[j])
              for j in order}
    for j in order:
        copies[j].start()
    xn = _rmsnorm(x_ref[:, 0, :], g_ref[...]).astype(BF16)

    def seg(j):
        sl = slice(j * CONV_DIM, (j + 1) * CONV_DIM)
        copies[j].wait()
        return _dot_nt(xn, wbuf[j].astype(BF16)) + b_ref[:, sl]

    u = seg(1) * seg(2)
    st0 = st_ref[:, 0, :]
    st1 = st_ref[:, 1, :]
    conv = cw_ref[0:1, :] * st0 + cw_ref[1:2, :] * st1 + cw_ref[2:3, :] * u
    yconv_ref[...] = seg(0) * conv
    sconv_ref[:, 0, :] = st1
    sconv_ref[:, 1, :] = u
    q_ref[...] = seg(3)
    k_ref[...] = seg(4) * (DQK ** -0.5)
    v = seg(5)
    v_ref[...] = v
    so_ref[...] = jax.nn.sigmoid(seg(6))
    n_gate = wg_ref.shape[0]
    gate_ref[...] = _gate_transform(_dot_nt(xn, wg_ref[...]) + b_ref[:, MAIN_DIM:MAIN_DIM + n_gate])
    for h in range(HEADS):
        sl = slice(h * DV, (h + 1) * DV)
        vt_ref[sl, :] = v[:, sl].T
    gatet_ref[...] = _gate_transform_rows(_dot_nt(wg_ref[...], xn) + bg_ref[...])


def _s1_call(x, g, w, b, wg, bg, cw, st):
    n = x.shape[0]
    n_gate = wg.shape[0]
    full = lambda *shape: pl.BlockSpec(shape, lambda i: (0,) * len(shape))
    ins = [x, g, w, b, wg, bg, cw, st]
    n_seg = MAIN_DIM // CONV_DIM
    return pl.pallas_call(
        _s1_kernel,
        grid=(1,),
        in_specs=[pl.BlockSpec(memory_space=pl.ANY) if a is w else full(*a.shape) for a in ins],
        out_specs=[full(n, CONV_DIM), full(*st.shape), full(n, MLSTM_DIM), full(n, MLSTM_DIM),
                   full(n, MLSTM_DIM), full(n, MLSTM_DIM), full(n, n_gate), full(MLSTM_DIM, n),
                   full(n_gate, n)],
        out_shape=[jax.ShapeDtypeStruct((n, CONV_DIM), F32),
                   jax.ShapeDtypeStruct(st.shape, F32)]
        + [jax.ShapeDtypeStruct((n, MLSTM_DIM), F32)] * 4
        + [jax.ShapeDtypeStruct((n, n_gate), F32),
           jax.ShapeDtypeStruct((MLSTM_DIM, n), F32),
           jax.ShapeDtypeStruct((n_gate, n), F32)],
        scratch_shapes=[pltpu.VMEM((n_seg, CONV_DIM, D_MODEL), F32), pltpu.SemaphoreType.DMA((n_seg,))],
        compiler_params=_params(("arbitrary",)),
        name="s1_inproj_conv",
    )(*ins)


def _s3_kernel(cqt_ref, q_ref, k_ref, v_ref, so_ref, gate_ref, n_ref, m_ref, yconv_ref, x_ref,
               wout_ref, gmh_ref, gx_ref, wq_ref,
               hs_ref, qx_ref, nn_ref, mn_ref, y_s):
    n_rows = q_ref.shape[0]
    y_s[:, 0:CONV_DIM] = yconv_ref[...].astype(BF16)
    lane = lax.broadcasted_iota(jnp.int32, (n_rows, mn_ref.shape[1]), 1)
    m_out = jnp.zeros((n_rows, mn_ref.shape[1]), F32)
    for h in range(HEADS):
        sl = slice(h * DQK, (h + 1) * DQK)
        q = q_ref[:, sl]
        k = k_ref[:, sl]
        v = v_ref[:, sl]
        n_prev = n_ref[:, h, :]
        cq = cqt_ref[sl, :].T
        ig = gate_ref[:, h:h + 1]
        lf = gate_ref[:, HEADS + h:HEADS + h + 1]
        m_prev = m_ref[:, h:h + 1]
        inter = lf + m_prev
        m_row = jnp.maximum(inter, ig)
        wgt = jnp.sum(q * k, axis=1, keepdims=True) * jnp.exp(ig - m_row)
        g = jnp.exp(inter - m_row)
        num = g * cq + wgt * v
        den = g * jnp.sum(n_prev * q, axis=1, keepdims=True) + wgt
        hh = num / jnp.maximum(jnp.abs(den), jnp.exp(-m_row))
        hh = hh * lax.rsqrt(jnp.mean(hh * hh, axis=1, keepdims=True) + EPS) * gmh_ref[:, sl]
        y_s[:, CONV_DIM + h * DV:CONV_DIM + (h + 1) * DV] = (so_ref[:, sl] * hh).astype(BF16)
        nn_ref[:, h, :] = g * n_prev + jnp.exp(ig - m_row) * k
        m_out = jnp.where(lane == h, m_row, m_out)
    mn_ref[...] = m_out
    hs = x_ref[:, 0, :] + _dot(y_s[...], wout_ref[...])
    hs_ref[...] = hs
    qx = _dot(_rmsnorm(hs, gx_ref[...]).astype(BF16), wq_ref[...])
    for h in range(X_HEADS):
        for half in range(X_HEAD_DIM // LANES):
            lo = h * X_HEAD_DIM + half * LANES
            qx_ref[:, half * X_HEADS + h, :] = qx[:, lo:lo + LANES]


def _s3_call(cqt, q, k, v, so, gates, nst, m, yconv, x, wout, gmh, gx, wq):
    n = q.shape[0]
    full = lambda *shape: pl.BlockSpec(shape, lambda i: (0,) * len(shape))
    ins = [cqt, q, k, v, so, gates, nst, m, yconv, x, wout, gmh, gx, wq]
    return pl.pallas_call(
        _s3_kernel,
        grid=(1,),
        in_specs=[full(*a.shape) for a in ins],
        out_specs=[full(n, D_MODEL), full(n, PACK_ROWS, LANES), full(*nst.shape), full(n, LANES)],
        out_shape=[jax.ShapeDtypeStruct((n, D_MODEL), F32), jax.ShapeDtypeStruct((n, PACK_ROWS, LANES), F32),
                   jax.ShapeDtypeStruct(nst.shape, F32), jax.ShapeDtypeStruct((n, LANES), F32)],
        scratch_shapes=[pltpu.VMEM((n, D_MODEL), BF16)],
        compiler_params=_params(("arbitrary",)),
        name="s3_mlstm_finish",
    )(*ins)


def _pack_heads(a):
    lead = a.shape[:-2]
    a = a.reshape(lead + (X_HEADS, X_HEAD_DIM // LANES, LANES))
    return jnp.swapaxes(a, -3, -2).reshape(lead + (PACK_ROWS, LANES))


def kernel(x_prompt, x_sample, mem_prompt, state_conv, state_mlstm_C, state_mlstm_n, state_mlstm_m,
           cache_mem_k, cache_mem_v, g_mix, w_in, b_in, conv_w, g_mh, w_out, g_cross, g_mem,
           w_xq, w_xkv, w_xo, g_ffn, w_gu, w_down, g_final):
    n_batch, seq_len, _ = x_prompt.shape
    n_dec = x_sample.shape[0]
    depth = w_in.shape[0]
    assert depth == 1 and x_sample.shape[1] == 1
    assert all(seq_len % t == 0 for t in (ROW_TILE, P1_TILE, P3_TILE))
    assert P1_TILE % MLSTM_CHUNK == 0 and n_batch % P2_SEQS == 0

    n_gate = 2 * HEADS
    w_in_b = w_in[0].T
    b_in_r = b_in[0].reshape(1, MAIN_DIM + n_gate)
    w_gate_r = w_in_b[MAIN_DIM:].astype(BF16)
    b_gate_r = b_in[0, MAIN_DIM:].reshape(n_gate, 1)
    g_mix_r = g_mix[0].reshape(1, D_MODEL)
    g_cross_r = g_cross[0].reshape(1, D_MODEL)
    g_mem_r = g_mem[0].reshape(1, D_MODEL)
    g_ffn_r = g_ffn[0].reshape(1, D_MODEL)
    g_final_r = g_final.reshape(1, D_MODEL)
    g_mh_r = g_mh[0].reshape(1, MLSTM_DIM)
    cw = conv_w[0]

    m0 = state_mlstm_m[0]
    s_yconv, s_conv, sq, sk, sv, sso, sgates, svt, sgates_t = _s1_call(
        x_sample, g_mix_r, w_in_b, b_in_r, w_gate_r, b_gate_r, cw, state_conv[0])

    xp = x_prompt.reshape(n_batch * seq_len, D_MODEL)
    yconv, q, k, v, so, gates, p_conv, s_c, cqt, w_out_b, w_xq_b, w_xkv_b = _p1_call(
        xp, g_mix_r, w_in_b, b_in_r, w_gate_r, b_gate_r, cw, seq_len,
        side=(state_mlstm_C[0], sq, sk, svt, sgates, sgates_t, m0, m0.T),
        cast=(w_out[0], w_xq[0], w_xkv[0]))
    hs1, qx, s_n, s_m = _s3_call(cqt, sq, sk, sv, sso, sgates,
                                 state_mlstm_n[0], m0, s_yconv, x_sample,
                                 w_out_b, g_mh_r, g_cross_r, w_xq_b)
    hp1, p_c, p_n, p_m = _p2_call(q, k, v, so, yconv, gates, xp, w_out_b, g_mh_r, n_batch, seq_len)
    hp1 = hp1.reshape(n_batch * seq_len, D_MODEL)
    pk, pv, pkb, pvb = _pm_call(mem_prompt.reshape(n_batch * N_MEM, D_MODEL), g_mem_r, w_xkv_b)
    o_p, w_xo_b, w_gu_b, w_down_b = _p3_call(
        hp1, g_cross_r, w_xq_b, pkb.reshape(n_batch, N_MEM, D_MODEL), pvb.reshape(n_batch, N_MEM, D_MODEL),
        seq_len, cast=(w_xo[0], w_gu[0], w_down[0]))
    y_p, y_s = _p4_call(hp1, o_p, w_xo_b, g_ffn_r, w_gu_b, w_down_b, g_final_r,
                        side=(qx, _pack_heads(cache_mem_k[0]), _pack_heads(cache_mem_v[0]), hs1))

    mem_shape = (1, n_batch, N_MEM, X_HEADS, X_HEAD_DIM)
    return (y_p.reshape(n_batch, seq_len, D_MODEL),
            y_s.reshape(n_dec, 1, D_MODEL),
            p_conv.reshape(1, n_batch, CONV_W - 1, CONV_DIM),
            p_c.reshape(1, n_batch, HEADS, DV, DQK),
            p_n.reshape(1, n_batch, HEADS, DQK),
            p_m[:, 0, :HEADS].reshape(1, n_batch, HEADS),
            pk.reshape(mem_shape),
            pv.reshape(mem_shape),
            s_conv.reshape(1, n_dec, CONV_W - 1, CONV_DIM),
            s_c.reshape(1, n_dec, HEADS, DV, DQK),
            s_n.reshape(1, n_dec, HEADS, DQK),
            s_m[:, :HEADS].reshape(1, n_dec, HEADS))
```

```python
import functools

import jax
import jax.numpy as jnp
from jax import lax
from jax.experimental import pallas as pl
from jax.experimental.pallas import tpu as pltpu

F32 = jnp.float32
BF16 = jnp.bfloat16

D_MODEL = 1024
CONV_DIM = 512
CONV_W = 3
MLSTM_DIM = 512
HEADS = 4
DQK = 128
DV = 128
N_MEM = 256
X_HEADS = 4
X_HEAD_DIM = 256
D_FF = 2816
MAIN_DIM = 3 * CONV_DIM + 4 * MLSTM_DIM
EPS = 1e-6

LANES = 128
SUBLANES = 8
BF16_ROWS = 16

MLSTM_CHUNK = 256
P2_SEQS = 4
ROW_TILE = 512
P1_TILE = 1024
P3_TILE = 1024
FF_CHUNK = 256
PACK_ROWS = X_HEADS * (X_HEAD_DIM // LANES)
MEM_CHUNK = 32
VMEM_LIMIT = 56 * 1024 * 1024


def _dot(a, b):
    return jnp.dot(a, b, preferred_element_type=F32)


def _dot_nt(a, b):
    return lax.dot_general(a, b, (((1,), (1,)), ((), ())), preferred_element_type=F32)


def _rmsnorm(x, g):
    return x * lax.rsqrt(jnp.mean(x * x, axis=-1, keepdims=True) + EPS) * g


def _const_spec(shape):
    zeros = (0,) * len(shape)
    return pl.BlockSpec(shape, lambda *_: zeros, pipeline_mode=pl.Buffered(1))


def _params(sem):
    return pltpu.CompilerParams(dimension_semantics=sem, vmem_limit_bytes=VMEM_LIMIT)


def _gate_transform(gt):
    lane = lax.broadcasted_iota(jnp.int32, gt.shape, 1)
    return jnp.where(lane < HEADS, gt, jax.nn.log_sigmoid(gt))


def _gate_transform_rows(gt):
    sub = lax.broadcasted_iota(jnp.int32, gt.shape, 0)
    return jnp.where(sub < HEADS, gt, jax.nn.log_sigmoid(gt))


def _memory_update_rows(i, c_ref, q_ref, k_ref, vt_ref, gate_ref, gatet_ref, m_ref, mt_ref, cn_ref, cqt_ref):
    n = vt_ref.shape[1]
    bb = c_ref.shape[0]

    @pl.when(i == 0)
    def _():
        cqt_ref[...] = jnp.zeros_like(cqt_ref)

    lane = lax.broadcasted_iota(jnp.int32, (DV, n), 1)
    for h in range(HEADS):
        sl = slice(h * DQK, (h + 1) * DQK)
        ig_c = gate_ref[:, h:h + 1]
        lf_c = gate_ref[:, HEADS + h:HEADS + h + 1]
        m_c = m_ref[:, h:h + 1]
        dec = jnp.broadcast_to(jnp.exp(lf_c + m_c - jnp.maximum(lf_c + m_c, ig_c)), (bb, DQK))
        ig_r = gatet_ref[h:h + 1, :]
        lf_r = gatet_ref[HEADS + h:HEADS + h + 1, :]
        m_r = mt_ref[h:h + 1, :]
        svt = vt_ref[sl, :] * jnp.exp(ig_r - jnp.maximum(lf_r + m_r, ig_r))
        q_t = q_ref[:, sl]
        k_t = k_ref[:, sl]
        cqt = cqt_ref[sl, :]
        for bl in range(bb):
            onehot = lane == i * bb + bl
            c = c_ref[bl, h]
            cq_col = jnp.sum(c * q_t[bl:bl + 1, :], axis=1, keepdims=True)
            sv_col = jnp.sum(jnp.where(onehot, svt, 0.0), axis=1, keepdims=True)
            cn_ref[bl, h] = dec[bl:bl + 1, :] * c + sv_col * k_t[bl:bl + 1, :]
            cqt = jnp.where(onehot, cq_col, cqt)
        cqt_ref[sl, :] = cqt


def _cast_slabs(srcs, dsts):
    for src, dst in zip(srcs, dsts):
        dst[...] = src[...].astype(BF16)


def _p1_kernel(tiles_per_batch, n_cast, x_ref, g_ref, w_ref, b_ref, wg_ref, bg_ref, cw_ref,
               c_ref, sq_ref, sk_ref, svt_ref, sgate_ref, sgatet_ref, sm_ref, smt_ref, *rest):
    cast_in, rest = rest[:n_cast], rest[n_cast:]
    yconv_ref, q_ref, k_ref, v_ref, so_ref, gate_ref, pconv_ref, cn_ref, cqt_ref = rest[:9]
    cast_out, ubuf = rest[9:9 + n_cast], rest[9 + n_cast]
    _cast_slabs(cast_in, cast_out)
    tm = x_ref.shape[0]
    i = pl.program_id(0)
    _memory_update_rows(i, c_ref, sq_ref, sk_ref, svt_ref, sgate_ref, sgatet_ref, sm_ref, smt_ref,
                        cn_ref, cqt_ref)
    xn = _rmsnorm(x_ref[...], g_ref[...]).astype(BF16)

    def seg(j):
        sl = slice(j * CONV_DIM, (j + 1) * CONV_DIM)
        return _dot_nt(xn, w_ref[sl, :].astype(BF16)) + b_ref[:, sl]

    P = SUBLANES
    prev = ubuf[tm:tm + P, :]
    ubuf[0:P, :] = jnp.where(i % tiles_per_batch == 0, jnp.zeros_like(prev), prev)
    ubuf[P:P + tm, :] = seg(1) * seg(2)
    conv = sum(cw_ref[j:j + 1, :] * ubuf[P - (CONV_W - 1) + j:P - (CONV_W - 1) + j + tm, :] for j in range(CONV_W))
    yconv_ref[...] = (seg(0) * conv).astype(BF16)
    pconv_ref[0] = ubuf[tm + P - (CONV_W - 1):tm + P, :]

    q_ref[...] = seg(3).astype(BF16)
    k_ref[...] = (seg(4) * (DQK ** -0.5)).astype(BF16)
    v_ref[...] = seg(5).astype(BF16)
    so_ref[...] = jax.nn.sigmoid(seg(6)).astype(BF16)
    gt = _gate_transform_rows(_dot_nt(wg_ref[...], xn) + bg_ref[...])
    n_gate = gt.shape[0]
    gate_ref[0, 0:n_gate, :] = gt
    L = MLSTM_CHUNK
    n_blk = tm // L
    hi = gt.astype(BF16).astype(F32)
    r1 = gt - hi
    mid = r1.astype(BF16).astype(F32)
    lo = r1 - mid
    terms = jnp.concatenate([t[:, j * L:(j + 1) * L] for t in (hi, mid, lo) for j in range(n_blk)], axis=0)
    tri = (lax.broadcasted_iota(jnp.int32, (L, L), 0) <= lax.broadcasted_iota(jnp.int32, (L, L), 1)).astype(BF16)
    parts = _dot(terms.astype(BF16), tri)
    for j in range(n_blk):
        rows = [parts[(t * n_blk + j) * n_gate:(t * n_blk + j + 1) * n_gate, :] for t in range(3)]
        gate_ref[0, n_gate:2 * n_gate, j * L:(j + 1) * L] = (rows[0] + rows[1]) + rows[2]


def _slab_specs(weights, steps):
    specs = []
    for wgt in weights:
        slab = wgt.shape[0] // steps
        assert slab * steps == wgt.shape[0] and slab % BF16_ROWS == 0
        specs.append(pl.BlockSpec((slab, wgt.shape[1]), lambda i: (i, 0)))
    return specs


def _p1_call(x, g, w, b, wg, bg, cw, seq_len, side, cast=()):
    rows = x.shape[0]
    tm = P1_TILE
    steps = rows // tm
    slabs = _slab_specs(cast, steps)
    tiles_per_batch = seq_len // tm
    n_batch = rows // seq_len
    row = lambda width: pl.BlockSpec((tm, width), lambda i: (i, 0))
    c, sq, sk, svt, sgate, sgatet, sm, smt = side
    n = sq.shape[0]
    sr = n // steps
    assert sr * steps == n and sr % SUBLANES == 0
    full = lambda a: pl.BlockSpec(a.shape, lambda i: (0,) * a.ndim)
    srow = lambda a: pl.BlockSpec((sr,) + a.shape[1:], lambda i: (i,) + (0,) * (a.ndim - 1))
    return pl.pallas_call(
        functools.partial(_p1_kernel, tiles_per_batch, len(cast)),
        grid=(steps,),
        in_specs=[row(D_MODEL), _const_spec((1, D_MODEL)), _const_spec(w.shape),
                  _const_spec(b.shape), _const_spec((2 * HEADS, D_MODEL)),
                  _const_spec((2 * HEADS, 1)), _const_spec((CONV_W, CONV_DIM)),
                  srow(c), srow(sq), srow(sk), full(svt), srow(sgate), full(sgatet), srow(sm), full(smt)]
        + slabs,
        out_specs=[row(CONV_DIM), row(MLSTM_DIM), row(MLSTM_DIM), row(MLSTM_DIM), row(MLSTM_DIM),
                   pl.BlockSpec((1, 4 * HEADS, tm), lambda i: (i // tiles_per_batch, 0, i % tiles_per_batch)),
                   pl.BlockSpec((1, CONV_W - 1, CONV_DIM), lambda i: (i // tiles_per_batch, 0, 0)),
                   srow(c), full(svt)] + slabs,
        out_shape=[jax.ShapeDtypeStruct((rows, CONV_DIM), BF16)]
        + [jax.ShapeDtypeStruct((rows, MLSTM_DIM), BF16)] * 4
        + [jax.ShapeDtypeStruct((n_batch, 4 * HEADS, seq_len), F32),
           jax.ShapeDtypeStruct((n_batch, CONV_W - 1, CONV_DIM), F32),
           jax.ShapeDtypeStruct(c.shape, F32), jax.ShapeDtypeStruct(svt.shape, F32)]
        + [jax.ShapeDtypeStruct(wgt.shape, BF16) for wgt in cast],
        scratch_shapes=[pltpu.VMEM((tm + SUBLANES, CONV_DIM), F32)],
        compiler_params=_params(("arbitrary",)),
        name="p1_inproj_conv",
    )(x, g, w, b, wg, bg, cw, c, sq, sk, svt, sgate, sgatet, sm, smt, *cast)


def _p2_kernel(q_ref, k_ref, v_ref, so_ref, yconv_ref, gate_ref, x_ref, wout_ref, gmh_ref,
               hp_ref, pc_ref, pn_ref, pm_ref, c_s, m_s, y_s):
    nb, L = q_ref.shape[0], q_ref.shape[1]
    c = pl.program_id(1)

    @pl.when(c == 0)
    def _():
        c_s[...] = jnp.zeros_like(c_s)
        m_s[...] = jnp.zeros_like(m_s)

    row = lax.broadcasted_iota(jnp.int32, (L, L), 0)
    col = lax.broadcasted_iota(jnp.int32, (L, L), 1)
    causal = row >= col

    for bi in range(nb):
        gt = gate_ref[bi]
        for h in range(HEADS):
            sl = slice(h * DQK, (h + 1) * DQK)
            q = q_ref[bi, :, sl]
            k = k_ref[bi, :, sl]
            v = v_ref[bi, :, sl]
            lf_r = gt[HEADS + h:HEADS + h + 1, :]
            a_r = gt[h:h + 1, :] - gt[3 * HEADS + h:3 * HEADS + h + 1, :]
            m_prev = jnp.max(m_s[bi, h:h + 1, :], axis=1, keepdims=True)
            c_prev = c_s[bi, h]

            m_c = jnp.maximum(m_prev, jnp.max(jnp.where(causal, a_r, -jnp.inf), axis=1, keepdims=True))
            b_c = jnp.sum(jnp.where(causal, lf_r, 0.0), axis=1, keepdims=True)
            w = _dot_nt(q, k) * jnp.exp(jnp.where(causal, a_r - m_c, -jnp.inf))
            g = jnp.exp(m_prev - m_c)
            qc = _dot_nt(q, c_prev.astype(BF16))
            num = g * qc[:, 0:DV] + _dot(w.astype(BF16), v)
            den = g * qc[:, DV:2 * DV] + jnp.sum(w, axis=1, keepdims=True)
            hh = num / jnp.maximum(jnp.abs(den), jnp.exp(-(b_c + m_c)))
            hh = hh * lax.rsqrt(jnp.mean(hh * hh, axis=1, keepdims=True) + EPS) * gmh_ref[:, sl]
            y_s[bi * L:(bi + 1) * L, h * DV:(h + 1) * DV] = (so_ref[bi, :, sl].astype(F32) * hh).astype(BF16)

            m_last = jnp.maximum(m_prev, jnp.max(a_r, axis=1, keepdims=True))
            b_last = jnp.sum(lf_r, axis=1, keepdims=True)
            s_r = jnp.exp(a_r - m_last)
            sv_t = jnp.concatenate([v.T.astype(F32) * s_r, jnp.broadcast_to(s_r, (DV, L))], axis=0)
            c_s[bi, h] = jnp.exp(m_prev - m_last) * c_prev + _dot(sv_t.astype(BF16), k)
            m_s[bi, h:h + 1, :] = jnp.broadcast_to(b_last + m_last, (1, m_s.shape[2]))

    for bi in range(nb):
        out = (_dot(yconv_ref[bi], wout_ref[0:CONV_DIM, :])
               + _dot(y_s[bi * L:(bi + 1) * L, :], wout_ref[CONV_DIM:CONV_DIM + MLSTM_DIM, :]))
        hp_ref[bi] = x_ref[bi] + out

    @pl.when(c == pl.num_programs(1) - 1)
    def _():
        lane = lax.broadcasted_iota(jnp.int32, (1, m_s.shape[2]), 1)
        for bi in range(nb):
            acc = jnp.zeros((1, m_s.shape[2]), F32)
            for h in range(HEADS):
                pc_ref[bi, h] = c_s[bi, h, 0:DV, :]
                pn_ref[bi, h:h + 1, :] = c_s[bi, h, DV:DV + 1, :]
                acc = jnp.where(lane == h, m_s[bi, h:h + 1, :], acc)
            pm_ref[bi] = acc


def _p2_call(q, k, v, so, yconv, gates, x, wout, gmh, n_batch, seq_len):
    L = MLSTM_CHUNK
    nb = P2_SEQS
    nc = seq_len // L
    seq = lambda width: pl.BlockSpec((nb, L, width), lambda b, c: (b, c, 0))
    as_seq = lambda a: a.reshape(n_batch, seq_len, a.shape[-1])
    return pl.pallas_call(
        _p2_kernel,
        grid=(n_batch // nb, nc),
        in_specs=[seq(MLSTM_DIM), seq(MLSTM_DIM), seq(MLSTM_DIM), seq(MLSTM_DIM), seq(CONV_DIM),
                  pl.BlockSpec((nb, 4 * HEADS, L), lambda b, c: (b, 0, c)), seq(D_MODEL),
                  _const_spec((D_MODEL, D_MODEL)), _const_spec((1, MLSTM_DIM))],
        out_specs=[seq(D_MODEL),
                   pl.BlockSpec((nb, HEADS, DV, DQK), lambda b, c: (b, 0, 0, 0)),
                   pl.BlockSpec((nb, HEADS, DQK), lambda b, c: (b, 0, 0)),
                   pl.BlockSpec((nb, 1, LANES), lambda b, c: (b, 0, 0))],
        out_shape=[jax.ShapeDtypeStruct((n_batch, seq_len, D_MODEL), F32),
                   jax.ShapeDtypeStruct((n_batch, HEADS, DV, DQK), F32),
                   jax.ShapeDtypeStruct((n_batch, HEADS, DQK), F32),
                   jax.ShapeDtypeStruct((n_batch, 1, LANES), F32)],
        scratch_shapes=[pltpu.VMEM((nb, HEADS, 2 * DV, DQK), F32), pltpu.VMEM((nb, SUBLANES, LANES), F32),
                        pltpu.VMEM((nb * L, MLSTM_DIM), BF16)],
        compiler_params=_params(("arbitrary", "arbitrary")),
        name="p2_mlstm_outproj",
    )(as_seq(q), as_seq(k), as_seq(v), as_seq(so), as_seq(yconv), gates, as_seq(x), wout, gmh)


def _pm_kernel(mem_ref, g_ref, w_ref, k_ref, v_ref, kb_ref, vb_ref):
    xn = _rmsnorm(mem_ref[...], g_ref[...]).astype(BF16)
    kk = _dot(xn, w_ref[:, 0:D_MODEL])
    vv = _dot(xn, w_ref[:, D_MODEL:2 * D_MODEL])
    for h in range(X_HEADS):
        sl = slice(h * X_HEAD_DIM, (h + 1) * X_HEAD_DIM)
        k_ref[:, h, :] = kk[:, sl]
        v_ref[:, h, :] = vv[:, sl]
    kb_ref[...] = kk.astype(BF16)
    vb_ref[...] = vv.astype(BF16)


def _pm_call(mem, g, w):
    rows = mem.shape[0]
    tm = ROW_TILE
    row = pl.BlockSpec((tm, D_MODEL), lambda i: (i, 0))
    row4 = pl.BlockSpec((tm, X_HEADS, X_HEAD_DIM), lambda i: (i, 0, 0))
    return pl.pallas_call(
        _pm_kernel,
        grid=(rows // tm,),
        in_specs=[row, _const_spec((1, D_MODEL)), _const_spec((D_MODEL, 2 * D_MODEL))],
        out_specs=[row4, row4, row, row],
        out_shape=[jax.ShapeDtypeStruct((rows, X_HEADS, X_HEAD_DIM), F32)] * 2
        + [jax.ShapeDtypeStruct((rows, D_MODEL), BF16)] * 2,
        compiler_params=_params(("arbitrary",)),
        name="pm_mem_kv",
    )(mem, g, w)


def _p3_kernel(n_cast, hp_ref, g_ref, wq_ref, k_ref, v_ref, *rest):
    _cast_slabs(rest[:n_cast], rest[n_cast + 1:])
    o_ref = rest[n_cast]
    xn = _rmsnorm(hp_ref[...], g_ref[...]).astype(BF16)
    q = _dot(xn, wq_ref[...])
    for h in range(X_HEADS):
        sl = slice(h * X_HEAD_DIM, (h + 1) * X_HEAD_DIM)
        s = _dot_nt(q[:, sl].astype(BF16), k_ref[0, :, sl]) * (X_HEAD_DIM ** -0.5)
        e = jnp.exp(s - jnp.max(s, axis=1, keepdims=True))
        o_h = _dot(e.astype(BF16), v_ref[0, :, sl]) * (1.0 / jnp.sum(e, axis=1, keepdims=True))
        o_ref[:, sl] = o_h.astype(BF16)


def _p3_call(hp, g, wq, kb, vb, seq_len, cast=()):
    rows = hp.shape[0]
    tm = P3_TILE
    steps = rows // tm
    tiles_per_batch = seq_len // tm
    row = pl.BlockSpec((tm, D_MODEL), lambda i: (i, 0))
    mem = pl.BlockSpec((1, N_MEM, D_MODEL), lambda i: (i // tiles_per_batch, 0, 0))
    slabs = _slab_specs(cast, steps)
    return pl.pallas_call(
        functools.partial(_p3_kernel, len(cast)),
        grid=(steps,),
        in_specs=[row, _const_spec((1, D_MODEL)), _const_spec((D_MODEL, D_MODEL)), mem, mem] + slabs,
        out_specs=[row] + slabs,
        out_shape=[jax.ShapeDtypeStruct((rows, D_MODEL), BF16)]
        + [jax.ShapeDtypeStruct(wgt.shape, BF16) for wgt in cast],
        compiler_params=_params(("arbitrary",)),
        name="p3_cross_attn",
    )(hp, g, wq, kb, vb, *cast)


def _cache_attention_row(q8, kc_ref, vc_ref, bl):
    qs = q8 * (X_HEAD_DIM ** -0.5)
    m_run = jnp.full((1, PACK_ROWS, 1), -jnp.inf, F32)
    l_run = jnp.zeros((1, PACK_ROWS, 1), F32)
    acc = jnp.zeros((PACK_ROWS, LANES), F32)
    for c in range(N_MEM // MEM_CHUNK):
        blk = slice(c * MEM_CHUNK, (c + 1) * MEM_CHUNK)
        prod = kc_ref[bl, blk] * qs
        s = jnp.sum(prod + pltpu.roll(prod, X_HEADS, 1), axis=-1, keepdims=True)
        m_new = jnp.maximum(m_run, jnp.max(s, axis=0, keepdims=True))
        alpha = jnp.exp(m_run - m_new)
        e = jnp.exp(s - m_new)
        l_run = alpha * l_run + jnp.sum(e, axis=0, keepdims=True)
        acc = alpha[0] * acc + jnp.sum(e * vc_ref[bl, blk], axis=0)
        m_run = m_new
    return acc * (1.0 / l_run[0])


def _swiglu_final(hp, gf_ref, wgu_ref, wd_ref, gfin_ref, act_s):
    xn = _rmsnorm(hp, gf_ref[...]).astype(BF16)
    for j in range(D_FF // FF_CHUNK):
        g = _dot(xn, wgu_ref[:, FF_CHUNK * j:FF_CHUNK * (j + 1)])
        u = _dot(xn, wgu_ref[:, D_FF + FF_CHUNK * j:D_FF + FF_CHUNK * (j + 1)])
        act_s[:, FF_CHUNK * j:FF_CHUNK * (j + 1)] = (g * jax.nn.sigmoid(g) * u).astype(BF16)
    hp = hp + _dot(act_s[...], wd_ref[...])
    return _rmsnorm(hp, gfin_ref[...])


def _p4_kernel(hp_ref, o_ref, wxo_ref, gf_ref, wgu_ref, wd_ref, gfin_ref, q4_ref, kc_ref, vc_ref, hs_ref,
               y_ref, ys_ref, act_s, os_s):
    i = pl.program_id(0)
    side_rows = q4_ref.shape[0]
    for bl in range(side_rows):
        os_s[i * side_rows + bl] = _cache_attention_row(q4_ref[bl], kc_ref, vc_ref, bl)
    hp = hp_ref[...] + _dot(o_ref[...], wxo_ref[...])
    y_ref[...] = _swiglu_final(hp, gf_ref, wgu_ref, wd_ref, gfin_ref, act_s)

    @pl.when(i == pl.num_programs(0) - 1)
    def _():
        n = hs_ref.shape[0]
        o = jnp.concatenate([os_s[:, half * X_HEADS + h, :] for h in range(X_HEADS)
                             for half in range(X_HEAD_DIM // LANES)], axis=1)
        hs = hs_ref[...] + _dot(o.astype(BF16), wxo_ref[...])
        ys_ref[:, 0, :] = _swiglu_final(hs, gf_ref, wgu_ref, wd_ref, gfin_ref, act_s.at[0:n])


def _p4_call(hp, o, wxo, gf, wgu, wd, gfin, side):
    rows = hp.shape[0]
    tm = ROW_TILE
    steps = rows // tm
    q4, kc, vc, hs = side
    n = hs.shape[0]
    side_rows = n // steps
    assert side_rows * steps == n and n <= tm
    row = pl.BlockSpec((tm, D_MODEL), lambda i: (i, 0))
    srow = pl.BlockSpec((side_rows, PACK_ROWS, LANES), lambda i: (i, 0, 0))
    cache = pl.BlockSpec((side_rows, N_MEM, PACK_ROWS, LANES), lambda i: (i, 0, 0, 0))
    return pl.pallas_call(
        _p4_kernel,
        grid=(steps,),
        in_specs=[row, row, _const_spec((D_MODEL, D_MODEL)), _const_spec((1, D_MODEL)),
                  _const_spec((D_MODEL, 2 * D_FF)), _const_spec((D_FF, D_MODEL)),
                  _const_spec((1, D_MODEL)), srow, cache, cache, _const_spec((n, D_MODEL))],
        out_specs=[row, pl.BlockSpec((n, 1, D_MODEL), lambda i: (0, 0, 0))],
        out_shape=[jax.ShapeDtypeStruct((rows, D_MODEL), F32), jax.ShapeDtypeStruct((n, 1, D_MODEL), F32)],
        scratch_shapes=[pltpu.VMEM((tm, D_FF), BF16), pltpu.VMEM((n, PACK_ROWS, LANES), F32)],
        compiler_params=_params(("arbitrary",)),
        name="p4_ffn_final",
    )(hp, o, wxo, gf, wgu, wd, gfin, q4, kc, vc, hs)


def _s1_kernel(x_ref, g_ref, w_ref, b_ref, wg_ref, bg_ref, cw_ref, st_ref,
               yconv_ref, sconv_ref, q_ref, k_ref, v_ref, so_ref, gate_ref, vt_ref, gatet_ref, wbuf, sem):
    order = (1, 2, 0, 3, 4, 5, 6)
    assert sorted(order) == list(range(wbuf.shape[0]))
    copies = {j: pltpu.make_async_copy(w_ref.at[pl.ds(j * CONV_DIM, CONV_DIM), :], wbuf.at[j], sem.at[j])
              for j in order}
    for j in order:
        copies[j].start()
    xn = _rmsnorm(x_ref[:, 0, :], g_ref[...]).astype(BF16)

    def seg(j):
        sl = slice(j * CONV_DIM, (j + 1) * CONV_DIM)
        copies[j].wait()
        return _dot_nt(xn, wbuf[j].astype(BF16)) + b_ref[:, sl]

    u = seg(1) * seg(2)
    st0 = st_ref[:, 0, :]
    st1 = st_ref[:, 1, :]
    conv = cw_ref[0:1, :] * st0 + cw_ref[1:2, :] * st1 + cw_ref[2:3, :] * u
    yconv_ref[...] = seg(0) * conv
    sconv_ref[:, 0, :] = st1
    sconv_ref[:, 1, :] = u
    q_ref[...] = seg(3)
    k_ref[...] = seg(4) * (DQK ** -0.5)
    v = seg(5)
    v_ref[...] = v
    so_ref[...] = jax.nn.sigmoid(seg(6))
    n_gate = wg_ref.shape[0]
    gate_ref[...] = _gate_transform(_dot_nt(xn, wg_ref[...]) + b_ref[:, MAIN_DIM:MAIN_DIM + n_gate])
    for h in range(HEADS):
        sl = slice(h * DV, (h + 1) * DV)
        vt_ref[sl, :] = v[:, sl].T
    gatet_ref[...] = _gate_transform_rows(_dot_nt(wg_ref[...], xn) + bg_ref[...])


def _s1_call(x, g, w, b, wg, bg, cw, st):
    n = x.shape[0]
    n_gate = wg.shape[0]
    full = lambda *shape: pl.BlockSpec(shape, lambda i: (0,) * len(shape))
    ins = [x, g, w, b, wg, bg, cw, st]
    n_seg = MAIN_DIM // CONV_DIM
    return pl.pallas_call(
        _s1_kernel,
        grid=(1,),
        in_specs=[pl.BlockSpec(memory_space=pl.ANY) if a is w else full(*a.shape) for a in ins],
        out_specs=[full(n, CONV_DIM), full(*st.shape), full(n, MLSTM_DIM), full(n, MLSTM_DIM),
                   full(n, MLSTM_DIM), full(n, MLSTM_DIM), full(n, n_gate), full(MLSTM_DIM, n),
                   full(n_gate, n)],
        out_shape=[jax.ShapeDtypeStruct((n, CONV_DIM), F32),
                   jax.ShapeDtypeStruct(st.shape, F32)]
        + [jax.ShapeDtypeStruct((n, MLSTM_DIM), F32)] * 4
        + [jax.ShapeDtypeStruct((n, n_gate), F32),
           jax.ShapeDtypeStruct((MLSTM_DIM, n), F32),
           jax.ShapeDtypeStruct((n_gate, n), F32)],
        scratch_shapes=[pltpu.VMEM((n_seg, CONV_DIM, D_MODEL), F32), pltpu.SemaphoreType.DMA((n_seg,))],
        compiler_params=_params(("arbitrary",)),
        name="s1_inproj_conv",
    )(*ins)


def _s3_kernel(cqt_ref, q_ref, k_ref, v_ref, so_ref, gate_ref, n_ref, m_ref, yconv_ref, x_ref,
               wout_ref, gmh_ref, gx_ref, wq_ref,
               hs_ref, qx_ref, nn_ref, mn_ref, y_s, wout_s, wq_s, sem):
    wout_copy = pltpu.make_async_copy(wout_ref, wout_s, sem.at[0])
    wq_copy = pltpu.make_async_copy(wq_ref, wq_s, sem.at[1])
    wout_copy.start()
    wq_copy.start()
    n_rows = q_ref.shape[0]
    y_s[:, 0:CONV_DIM] = yconv_ref[...].astype(BF16)
    lane = lax.broadcasted_iota(jnp.int32, (n_rows, mn_ref.shape[1]), 1)
    m_out = jnp.zeros((n_rows, mn_ref.shape[1]), F32)
    for h in range(HEADS):
        sl = slice(h * DQK, (h + 1) * DQK)
        q = q_ref[:, sl]
        k = k_ref[:, sl]
        v = v_ref[:, sl]
        n_prev = n_ref[:, h, :]
        cq = cqt_ref[sl, :].T
        ig = gate_ref[:, h:h + 1]
        lf = gate_ref[:, HEADS + h:HEADS + h + 1]
        m_prev = m_ref[:, h:h + 1]
        inter = lf + m_prev
        m_row = jnp.maximum(inter, ig)
        wgt = jnp.sum(q * k, axis=1, keepdims=True) * jnp.exp(ig - m_row)
        g = jnp.exp(inter - m_row)
        num = g * cq + wgt * v
        den = g * jnp.sum(n_prev * q, axis=1, keepdims=True) + wgt
        hh = num / jnp.maximum(jnp.abs(den), jnp.exp(-m_row))
        hh = hh * lax.rsqrt(jnp.mean(hh * hh, axis=1, keepdims=True) + EPS) * gmh_ref[:, sl]
        y_s[:, CONV_DIM + h * DV:CONV_DIM + (h + 1) * DV] = (so_ref[:, sl] * hh).astype(BF16)
        nn_ref[:, h, :] = g * n_prev + jnp.exp(ig - m_row) * k
        m_out = jnp.where(lane == h, m_row, m_out)
    mn_ref[...] = m_out
    wout_copy.wait()
    hs = x_ref[:, 0, :] + _dot(y_s[...], wout_s[...])
    hs_ref[...] = hs
    wq_copy.wait()
    qx = _dot(_rmsnorm(hs, gx_ref[...]).astype(BF16), wq_s[...])
    for h in range(X_HEADS):
        for half in range(X_HEAD_DIM // LANES):
            lo = h * X_HEAD_DIM + half * LANES
            qx_ref[:, half * X_HEADS + h, :] = qx[:, lo:lo + LANES]


def _s3_call(cqt, q, k, v, so, gates, nst, m, yconv, x, wout, gmh, gx, wq):
    n = q.shape[0]
    full = lambda *shape: pl.BlockSpec(shape, lambda i: (0,) * len(shape))
    ins = [cqt, q, k, v, so, gates, nst, m, yconv, x, wout, gmh, gx, wq]
    return pl.pallas_call(
        _s3_kernel,
        grid=(1,),
        in_specs=[pl.BlockSpec(memory_space=pl.ANY) if a is wout or a is wq else full(*a.shape) for a in ins],
        out_specs=[full(n, D_MODEL), full(n, PACK_ROWS, LANES), full(*nst.shape), full(n, LANES)],
        out_shape=[jax.ShapeDtypeStruct((n, D_MODEL), F32), jax.ShapeDtypeStruct((n, PACK_ROWS, LANES), F32),
                   jax.ShapeDtypeStruct(nst.shape, F32), jax.ShapeDtypeStruct((n, LANES), F32)],
        scratch_shapes=[pltpu.VMEM((n, D_MODEL), BF16), pltpu.VMEM(wout.shape, BF16), pltpu.VMEM(wq.shape, BF16),
                        pltpu.SemaphoreType.DMA((2,))],
        compiler_params=_params(("arbitrary",)),
        name="s3_mlstm_finish",
    )(*ins)


def _pack_heads(a):
    lead = a.shape[:-2]
    a = a.reshape(lead + (X_HEADS, X_HEAD_DIM // LANES, LANES))
    return jnp.swapaxes(a, -3, -2).reshape(lead + (PACK_ROWS, LANES))


def kernel(x_prompt, x_sample, mem_prompt, state_conv, state_mlstm_C, state_mlstm_n, state_mlstm_m,
           cache_mem_k, cache_mem_v, g_mix, w_in, b_in, conv_w, g_mh, w_out, g_cross, g_mem,
           w_xq, w_xkv, w_xo, g_ffn, w_gu, w_down, g_final):
    n_batch, seq_len, _ = x_prompt.shape
    n_dec = x_sample.shape[0]
    depth = w_in.shape[0]
    assert depth == 1 and x_sample.shape[1] == 1
    assert all(seq_len % t == 0 for t in (ROW_TILE, P1_TILE, P3_TILE))
    assert P1_TILE % MLSTM_CHUNK == 0 and n_batch % P2_SEQS == 0

    n_gate = 2 * HEADS
    w_in_b = w_in[0].T
    b_in_r = b_in[0].reshape(1, MAIN_DIM + n_gate)
    w_gate_r = w_in_b[MAIN_DIM:].astype(BF16)
    b_gate_r = b_in[0, MAIN_DIM:].reshape(n_gate, 1)
    g_mix_r = g_mix[0].reshape(1, D_MODEL)
    g_cross_r = g_cross[0].reshape(1, D_MODEL)
    g_mem_r = g_mem[0].reshape(1, D_MODEL)
    g_ffn_r = g_ffn[0].reshape(1, D_MODEL)
    g_final_r = g_final.reshape(1, D_MODEL)
    g_mh_r = g_mh[0].reshape(1, MLSTM_DIM)
    cw = conv_w[0]

    m0 = state_mlstm_m[0]
    s_yconv, s_conv, sq, sk, sv, sso, sgates, svt, sgates_t = _s1_call(
        x_sample, g_mix_r, w_in_b, b_in_r, w_gate_r, b_gate_r, cw, state_conv[0])

    xp = x_prompt.reshape(n_batch * seq_len, D_MODEL)
    yconv, q, k, v, so, gates, p_conv, s_c, cqt, w_out_b, w_xq_b, w_xkv_b = _p1_call(
        xp, g_mix_r, w_in_b, b_in_r, w_gate_r, b_gate_r, cw, seq_len,
        side=(state_mlstm_C[0], sq, sk, svt, sgates, sgates_t, m0, m0.T),
        cast=(w_out[0], w_xq[0], w_xkv[0]))
    hs1, qx, s_n, s_m = _s3_call(cqt, sq, sk, sv, sso, sgates,
                                 state_mlstm_n[0], m0, s_yconv, x_sample,
                                 w_out_b, g_mh_r, g_cross_r, w_xq_b)
    hp1, p_c, p_n, p_m = _p2_call(q, k, v, so, yconv, gates, xp, w_out_b, g_mh_r, n_batch, seq_len)
    hp1 = hp1.reshape(n_batch * seq_len, D_MODEL)
    pk, pv, pkb, pvb = _pm_call(mem_prompt.reshape(n_batch * N_MEM, D_MODEL), g_mem_r, w_xkv_b)
    o_p, w_xo_b, w_gu_b, w_down_b = _p3_call(
        hp1, g_cross_r, w_xq_b, pkb.reshape(n_batch, N_MEM, D_MODEL), pvb.reshape(n_batch, N_MEM, D_MODEL),
        seq_len, cast=(w_xo[0], w_gu[0], w_down[0]))
    y_p, y_s = _p4_call(hp1, o_p, w_xo_b, g_ffn_r, w_gu_b, w_down_b, g_final_r,
                        side=(qx, _pack_heads(cache_mem_k[0]), _pack_heads(cache_mem_v[0]), hs1))

    mem_shape = (1, n_batch, N_MEM, X_HEADS, X_HEAD_DIM)
    return (y_p.reshape(n_batch, seq_len, D_MODEL),
            y_s.reshape(n_dec, 1, D_MODEL),
            p_conv.reshape(1, n_batch, CONV_W - 1, CONV_DIM),
            p_c.reshape(1, n_batch, HEADS, DV, DQK),
            p_n.reshape(1, n_batch, HEADS, DQK),
            p_m[:, 0, :HEADS].reshape(1, n_batch, HEADS),
            pk.reshape(mem_shape),
            pv.reshape(mem_shape),
            s_conv.reshape(1, n_dec, CONV_W - 1, CONV_DIM),
            s_c.reshape(1, n_dec, HEADS, DV, DQK),
            s_n.reshape(1, n_dec, HEADS, DQK),
            s_m[:, :HEADS].reshape(1, n_dec, HEADS))
```
